```python
import jax
import jax.numpy as jnp
from jax import lax
import numpy as np

D_MODEL = 1024
BATCH = 8
SEQ = 2048
DEPTH = 1

MEM_LEN = 256
GRID_W = 64
EPS = 1e-6
D_MIX = D_MODEL
ATT_HEADS = 8
ATT_KV_HEADS = 2
HEAD_DIM = 64
D_ATT = ATT_HEADS * HEAD_DIM
D_KV = ATT_KV_HEADS * HEAD_DIM
Q_BLOCK = 128
ROPE_THETA = 10000.0
D_SSM = D_MIX - D_ATT
SSM_GROUP = 16
SSM_GROUPS = D_SSM // SSM_GROUP
SSM_STATE = 64
DT_MIN = 1e-3
DT_MAX = 1e-1
D_IN = D_ATT + 2 * D_KV + D_SSM
XA_HEADS = 4
XA_HEAD_DIM = D_MODEL // XA_HEADS
N_EXPERT_GROUPS = 4
EXPERTS_PER_GROUP = 4
N_EXPERTS = N_EXPERT_GROUPS * EXPERTS_PER_GROUP
TOP_K = 2
D_EXPERT = D_MODEL // 2

kernel_name = "hybrid_attn_s5_xattn_hmoe_encoder"


def rmsnorm(x, g):
    xf = x.astype(jnp.float32)
    y = xf * lax.rsqrt(jnp.mean(xf * xf, axis=-1, keepdims=True) + EPS)
    return y.astype(x.dtype) * g


def rope_1d(x, pos):
    half = x.shape[-1] // 2
    freqs = ROPE_THETA ** (-jnp.arange(half, dtype=jnp.float32) / half)
    ang = pos.astype(jnp.float32)[:, None] * freqs[None, :]
    cos = jnp.cos(ang)[:, None, :].astype(x.dtype)
    sin = jnp.sin(ang)[:, None, :].astype(x.dtype)
    x1, x2 = x[..., :half], x[..., half:]
    return jnp.concatenate([x1 * cos - x2 * sin, x2 * cos + x1 * sin], axis=-1)


def axial_rope(x, row, col):
    half = HEAD_DIM // 2
    return jnp.concatenate([rope_1d(x[..., :half], row), rope_1d(x[..., half:], col)], axis=-1)


def gqa_blocked(q, k, v):
    B, S, H, Dh = q.shape
    KV = k.shape[2]
    G = H // KV
    nb = S // Q_BLOCK
    qb = q.reshape(B, nb, Q_BLOCK, KV, G, Dh).transpose(1, 0, 2, 3, 4, 5)
    scale = Dh ** -0.5

    def one_block(q_blk):
        s = jnp.einsum('bqkgd,bskd->bkgqs', q_blk, k).astype(jnp.float32) * scale
        p = jax.nn.softmax(s, axis=-1).astype(v.dtype)
        return jnp.einsum('bkgqs,bskd->bqkgd', p, v)

    o = lax.map(one_block, qb)
    return o.transpose(1, 0, 2, 3, 4, 5).reshape(B, S, H * Dh)


def _complex_affine_combine(e1, e2):
    a1r, a1i, b1r, b1i = e1
    a2r, a2i, b2r, b2i = e2
    return (a2r * a1r - a2i * a1i,
            a2r * a1i + a2i * a1r,
            a2r * b1r - a2i * b1i + b2r,
            a2r * b1i + a2i * b1r + b2i)


def s5_direction(u, lam_re, lam_im, log_step, b_re, b_im, c_re, c_im, reverse):
    f32 = jnp.float32
    lr = jnp.minimum(lam_re.astype(f32), -1e-4)
    li = lam_im.astype(f32)
    dt = jnp.exp(log_step.astype(f32))[:, None]
    mag = jnp.exp(lr * dt)
    ab_r = mag * jnp.cos(li * dt)
    ab_i = mag * jnp.sin(li * dt)
    den = lr * lr + li * li
    nr = ab_r - 1.0
    f_r = (nr * lr + ab_i * li) / den
    f_i = (ab_i * lr - nr * li) / den
    br = b_re.astype(f32)
    bi = b_im.astype(f32)
    bb_r = f_r[..., None] * br - f_i[..., None] * bi
    bb_i = f_r[..., None] * bi + f_i[..., None] * br
    bu_r = jnp.einsum('bsgc,gpc->bsgp', u, bb_r)
    bu_i = jnp.einsum('bsgc,gpc->bsgp', u, bb_i)
    a_r = jnp.broadcast_to(ab_r, bu_r.shape)
    a_i = jnp.broadcast_to(ab_i, bu_i.shape)
    _, _, s_r, s_i = lax.associative_scan(
        _complex_affine_combine, (a_r, a_i, bu_r, bu_i), axis=1, reverse=reverse)
    return (jnp.einsum('bsgp,gcp->bsgc', s_r, c_re.astype(f32))
            - jnp.einsum('bsgp,gcp->bsgc', s_i, c_im.astype(f32)))


def s5_mixer(u, lam_re, lam_im, log_step, b_re, b_im, c_re, c_im, ssm_d, glu_w, glu_b):
    B, S, _ = u.shape
    uf = u.astype(jnp.float32)
    ug = uf.reshape(B, S, SSM_GROUPS, SSM_GROUP)
    y = (s5_direction(ug, lam_re[0], lam_im[0], log_step[0], b_re[0], b_im[0], c_re[0], c_im[0], False)
         + s5_direction(ug, lam_re[1], lam_im[1], log_step[1], b_re[1], b_im[1], c_re[1], c_im[1], True))
    y = (y.reshape(B, S, D_SSM) + ssm_d.astype(jnp.float32) * uf).astype(u.dtype)
    z = jax.nn.gelu(y)
    return z * jax.nn.sigmoid(z @ glu_w + glu_b)


def hybrid_mixer(x, norm_g, w_in, q_norm_g, k_norm_g, lam_re, lam_im, log_step, b_re, b_im,
                 c_re, c_im, ssm_d, glu_w, glu_b, att_out_g, ssm_out_g, w_out, row, col):
    B, S, _ = x.shape
    h = rmsnorm(x, norm_g)
    proj = h @ w_in
    q, k, v, u = jnp.split(proj, [D_ATT, D_ATT + D_KV, D_ATT + 2 * D_KV], axis=-1)
    q = rmsnorm(q.reshape(B, S, ATT_HEADS, HEAD_DIM), q_norm_g)
    k = rmsnorm(k.reshape(B, S, ATT_KV_HEADS, HEAD_DIM), k_norm_g)
    v = v.reshape(B, S, ATT_KV_HEADS, HEAD_DIM)
    q = axial_rope(q, row, col)
    k = axial_rope(k, row, col)
    y_att = gqa_blocked(q, k, v)
    y_ssm = s5_mixer(u, lam_re, lam_im, log_step, b_re, b_im, c_re, c_im, ssm_d, glu_w, glu_b)
    y = jnp.concatenate([rmsnorm(y_att, att_out_g), rmsnorm(y_ssm, ssm_out_g)], axis=-1)
    return y @ w_out


def memory_cross_attn(x, mem, xa_norm_g, mem_norm_g, wq, wk, wv, wo):
    B, S, _ = x.shape
    M = mem.shape[1]
    hq = rmsnorm(x, xa_norm_g)
    hm = rmsnorm(mem, mem_norm_g)
    q = (hq @ wq).reshape(B, S, XA_HEADS, XA_HEAD_DIM)
    k = (hm @ wk).reshape(B, M, XA_HEADS, XA_HEAD_DIM)
    v = (hm @ wv).reshape(B, M, XA_HEADS, XA_HEAD_DIM)
    s = jnp.einsum('bqhd,bmhd->bhqm', q, k).astype(jnp.float32) * (XA_HEAD_DIM ** -0.5)
    p = jax.nn.softmax(s, axis=-1).astype(v.dtype)
    o = jnp.einsum('bhqm,bmhd->bqhd', p, v).reshape(B, S, D_MODEL)
    return o @ wo


def hier_moe(x, norm_g, grp_w, grp_b, exp_w, exp_b, w_gate, w_up, w_down):
    B, S, D = x.shape
    T = B * S
    h = rmsnorm(x, norm_g).reshape(T, D)
    grp_prob = jax.nn.softmax((h @ grp_w).astype(jnp.float32) + grp_b, axis=-1)
    g_w, g_sel = lax.top_k(grp_prob, 1)
    exp_logits = ((h @ exp_w).astype(jnp.float32) + exp_b).reshape(T, N_EXPERT_GROUPS, EXPERTS_PER_GROUP)
    idx = jnp.broadcast_to(g_sel[:, :, None], (T, 1, EXPERTS_PER_GROUP))
    in_grp = jnp.take_along_axis(exp_logits, idx, axis=1)[:, 0]
    top_v, top_i = lax.top_k(in_grp, TOP_K)
    w = jax.nn.softmax(top_v, axis=-1) * g_w
    e_idx = g_sel * EXPERTS_PER_GROUP + top_i
    gates = jnp.sum(jax.nn.one_hot(e_idx, N_EXPERTS, dtype=jnp.float32) * w[..., None], axis=1)
    gates = gates.astype(h.dtype)
    y = jnp.zeros_like(h)
    for e in range(N_EXPERTS):
        a = jax.nn.silu(h @ w_gate[e]) * (h @ w_up[e])
        y = y + (a * gates[:, e:e + 1]) @ w_down[e]
    return y.reshape(B, S, D)


def setup_inputs(seed: int = 0) -> dict:
    key = jax.random.key(seed)
    ks = jax.random.split(key, 34)
    f32 = jnp.float32
    L = DEPTH

    def nrm(k, shape, scale):
        return jax.random.normal(k, shape, f32) * scale

    def gain(k, shape):
        return 1.0 + nrm(k, shape, 0.02)

    n_idx = jnp.arange(SSM_STATE, dtype=f32)
    lam_im0 = jnp.broadcast_to(jnp.pi * n_idx, (L, 2, SSM_GROUPS, SSM_STATE))
    return {
        'x': nrm(ks[0], (BATCH, SEQ, D_MODEL), 1.0),
        'mem': nrm(ks[1], (BATCH, MEM_LEN, D_MODEL), 1.0),
        'mix_norm_g': gain(ks[2], (L, D_MODEL)),
        'w_in': nrm(ks[3], (L, D_MODEL, D_IN), D_MODEL ** -0.5),
        'q_norm_g': gain(ks[4], (L, HEAD_DIM)),
        'k_norm_g': gain(ks[5], (L, HEAD_DIM)),
        'ssm_lambda_re': -0.5 + nrm(ks[6], (L, 2, SSM_GROUPS, SSM_STATE), 0.01),
        'ssm_lambda_im': lam_im0 + nrm(ks[7], (L, 2, SSM_GROUPS, SSM_STATE), 0.01),
        'ssm_log_step': jax.random.uniform(ks[8], (L, 2, SSM_GROUPS), f32,
                                           minval=float(np.log(DT_MIN)), maxval=float(np.log(DT_MAX))),
        'ssm_b_re': nrm(ks[9], (L, 2, SSM_GROUPS, SSM_STATE, SSM_GROUP), (2 * SSM_GROUP) ** -0.5),
        'ssm_b_im': nrm(ks[10], (L, 2, SSM_GROUPS, SSM_STATE, SSM_GROUP), (2 * SSM_GROUP) ** -0.5),
        'ssm_c_re': nrm(ks[11], (L, 2, SSM_GROUPS, SSM_GROUP, SSM_STATE), (2 * SSM_STATE) ** -0.5),
        'ssm_c_im': nrm(ks[12], (L, 2, SSM_GROUPS, SSM_GROUP, SSM_STATE), (2 * SSM_STATE) ** -0.5),
        'ssm_d': nrm(ks[13], (L, D_SSM), 1.0),
        'ssm_glu_w': nrm(ks[14], (L, D_SSM, D_SSM), D_SSM ** -0.5),
        'ssm_glu_b': nrm(ks[15], (L, D_SSM), 0.01),
        'att_out_g': gain(ks[16], (L, D_ATT)),
        'ssm_out_g': gain(ks[17], (L, D_SSM)),
        'w_out': nrm(ks[18], (L, D_MIX, D_MODEL), D_MIX ** -0.5),
        'xa_norm_g': gain(ks[19], (L, D_MODEL)),
        'mem_norm_g': gain(ks[20], (L, D_MODEL)),
        'xa_wq': nrm(ks[21], (L, D_MODEL, D_MODEL), D_MODEL ** -0.5),
        'xa_wk': nrm(ks[22], (L, D_MODEL, D_MODEL), D_MODEL ** -0.5),
        'xa_wv': nrm(ks[23], (L, D_MODEL, D_MODEL), D_MODEL ** -0.5),
        'xa_wo': nrm(ks[24], (L, D_MODEL, D_MODEL), D_MODEL ** -0.5),
        'moe_norm_g': gain(ks[25], (L, D_MODEL)),
        'router_grp_w': nrm(ks[26], (L, D_MODEL, N_EXPERT_GROUPS), D_MODEL ** -0.5),
        'router_grp_b': nrm(ks[27], (L, N_EXPERT_GROUPS), 0.01),
        'router_exp_w': nrm(ks[28], (L, D_MODEL, N_EXPERTS), D_MODEL ** -0.5),
        'router_exp_b': nrm(ks[29], (L, N_EXPERTS), 0.01),
        'exp_w_gate': nrm(ks[30], (L, N_EXPERTS, D_MODEL, D_EXPERT), D_MODEL ** -0.5),
        'exp_w_up': nrm(ks[31], (L, N_EXPERTS, D_MODEL, D_EXPERT), D_MODEL ** -0.5),
        'exp_w_down': nrm(ks[32], (L, N_EXPERTS, D_EXPERT, D_MODEL), D_EXPERT ** -0.5),
        'final_norm_g': gain(ks[33], (D_MODEL,)),
    }


def reference(x, mem, mix_norm_g, w_in, q_norm_g, k_norm_g, ssm_lambda_re, ssm_lambda_im,
              ssm_log_step, ssm_b_re, ssm_b_im, ssm_c_re, ssm_c_im, ssm_d, ssm_glu_w, ssm_glu_b,
              att_out_g, ssm_out_g, w_out, xa_norm_g, mem_norm_g, xa_wq, xa_wk, xa_wv, xa_wo,
              moe_norm_g, router_grp_w, router_grp_b, router_exp_w, router_exp_b,
              exp_w_gate, exp_w_up, exp_w_down, final_norm_g):
    S = x.shape[1]
    rows = S // GRID_W
    row = jnp.repeat(jnp.arange(rows, dtype=jnp.int32), GRID_W)
    col = jnp.tile(jnp.arange(GRID_W, dtype=jnp.int32), rows)
    for l in range(DEPTH):
        x = x + hybrid_mixer(x, mix_norm_g[l], w_in[l], q_norm_g[l], k_norm_g[l],
                             ssm_lambda_re[l], ssm_lambda_im[l], ssm_log_step[l],
                             ssm_b_re[l], ssm_b_im[l], ssm_c_re[l], ssm_c_im[l],
                             ssm_d[l], ssm_glu_w[l], ssm_glu_b[l],
                             att_out_g[l], ssm_out_g[l], w_out[l], row, col)
        x = x + memory_cross_attn(x, mem, xa_norm_g[l], mem_norm_g[l],
                                  xa_wq[l], xa_wk[l], xa_wv[l], xa_wo[l])
        x = x + hier_moe(x, moe_norm_g[l], router_grp_w[l], router_grp_b[l],
                         router_exp_w[l], router_exp_b[l],
                         exp_w_gate[l], exp_w_up[l], exp_w_down[l])
    return rmsnorm(x, final_norm_g)
```

```python
import functools

import numpy as np
import jax
import jax.numpy as jnp
from jax import lax
from jax.experimental import pallas as pl
from jax.experimental.pallas import tpu as pltpu

F32 = jnp.float32
BF16 = jnp.bfloat16

EPS = 1e-6
GRID_W = 64
ROPE_THETA = 10000.0
ATT_HEADS = 8
ATT_KV_HEADS = 2
HEAD_DIM = 64
D_ATT = ATT_HEADS * HEAD_DIM
D_KV = ATT_KV_HEADS * HEAD_DIM
SSM_GROUP = 16
SSM_STATE = 64
XA_HEADS = 4
N_EXPERT_GROUPS = 4
EXPERTS_PER_GROUP = 4
N_EXPERTS = N_EXPERT_GROUPS * EXPERTS_PER_GROUP
PAIRS_PER_GROUP = 6
N_CLASSES = N_EXPERT_GROUPS * PAIRS_PER_GROUP

LANES = 128
SUBLANES = 8
CHUNK = 16
VMEM_LIMIT = 56 * 1024 * 1024

META_CLS, META_RANK, META_WA, META_WB = 0, 1, 2, 3


def _cparams(*sem):
    return pltpu.CompilerParams(dimension_semantics=sem, vmem_limit_bytes=VMEM_LIMIT)


def _rms(x):
    return x * lax.rsqrt(jnp.mean(x * x, axis=-1, keepdims=True) + EPS)


def _dot(a, b):
    return jnp.dot(a, b, preferred_element_type=F32)


def _dot_nt(a, b):
    return lax.dot_general(a, b, (((1,), (1,)), ((), ())), preferred_element_type=F32)


def _inproj_body(x_ref, g_ref, w_ref, qkg_ref, ones_ref, cos_ref, sin_ref,
                 q_ref, k_ref, v_ref, u_ref):
    n_qk = q_ref.shape[1] + k_ref.shape[1]
    h = _rms(x_ref[...]) * g_ref[...]
    proj = _dot(h.astype(BF16), w_ref[...])
    qk = proj[:, :n_qk]
    ssq = _dot((qk * qk).astype(BF16), ones_ref[...])
    qk = qk * lax.rsqrt(ssq * (1.0 / HEAD_DIM) + EPS) * qkg_ref[...]
    cos = cos_ref[...]
    sin = sin_ref[...]
    lane = lax.broadcasted_iota(jnp.int32, cos.shape, 1)
    first = (lane % 32) < 16
    nq = q_ref.shape[1] // LANES
    for c in range(n_qk // LANES):
        xc = qk[:, c * LANES:(c + 1) * LANES]
        partner = jnp.where(first, pltpu.roll(xc, LANES - 16, 1), pltpu.roll(xc, 16, 1))
        rot = xc * cos + partner * sin
        if c < nq:
            q_ref[:, c * LANES:(c + 1) * LANES] = (rot * (HEAD_DIM ** -0.5)).astype(BF16)
        else:
            k_ref[:, (c - nq) * LANES:(c - nq + 1) * LANES] = rot.astype(BF16)
    nv = n_qk + v_ref.shape[1]
    v_ref[...] = proj[:, n_qk:nv].astype(BF16)
    u_ref[...] = proj[:, nv:].astype(BF16)


def _rope_tables(S):
    t = jnp.arange(S, dtype=jnp.int32)
    freqs = ROPE_THETA ** (-jnp.arange(16, dtype=F32) / 16)
    halves = []
    for pos in (t // GRID_W, t % GRID_W):
        ang = pos.astype(F32)[:, None] * freqs[None, :]
        halves.append((jnp.cos(ang), jnp.sin(ang)))
    cos_h = jnp.concatenate([c for c, _ in halves for _ in range(2)], axis=1)
    sin_h = jnp.concatenate([sg * s for _, s in halves for sg in (-1.0, 1.0)], axis=1)
    return jnp.concatenate([cos_h, cos_h], axis=1), jnp.concatenate([sin_h, sin_h], axis=1)


def _inproj(x2d, S, mix_g, w_in, q_g, k_g):
    T, D = x2d.shape
    tm = 512
    d_ssm = w_in.shape[1] - D_ATT - 2 * D_KV
    wq = w_in[:, :D_ATT]
    wk = w_in[:, D_ATT:D_ATT + D_KV]
    wv = w_in[:, D_ATT + D_KV:D_ATT + 2 * D_KV]
    wu = w_in[:, D_ATT + 2 * D_KV:]
    dup = lambda w: jnp.concatenate([w[:, :64], w[:, :64], w[:, 64:], w[:, 64:]], axis=1)
    w_ext = jnp.concatenate([wq, dup(wk), dup(wv), wu], axis=1).astype(BF16)
    n_qk = D_ATT + 2 * D_KV
    qkg = jnp.concatenate([jnp.tile(q_g, ATT_HEADS), jnp.tile(k_g, 2 * ATT_KV_HEADS)])[None, :]
    hid = np.arange(n_qk) // HEAD_DIM
    ones = jnp.asarray((hid[:, None] == hid[None, :]).astype(np.float32), dtype=BF16)
    cos_t, sin_t = _rope_tables(S)
    sb = S // tm
    const = lambda i: (0, 0)
    return pl.pallas_call(
        _inproj_body,
        grid=(T // tm,),
        in_specs=[
            pl.BlockSpec((tm, D), lambda i: (i, 0)),
            pl.BlockSpec((1, D), const),
            pl.BlockSpec(w_ext.shape, const),
            pl.BlockSpec((1, n_qk), const),
            pl.BlockSpec((n_qk, n_qk), const),
            pl.BlockSpec((tm, LANES), lambda i: (i % sb, 0)),
            pl.BlockSpec((tm, LANES), lambda i: (i % sb, 0)),
        ],
        out_specs=[
            pl.BlockSpec((tm, D_ATT), lambda i: (i, 0)),
            pl.BlockSpec((tm, 2 * D_KV), lambda i: (i, 0)),
            pl.BlockSpec((tm, 2 * D_KV), lambda i: (i, 0)),
            pl.BlockSpec((tm, d_ssm), lambda i: (i, 0)),
        ],
        out_shape=[
            jax.ShapeDtypeStruct((T, D_ATT), BF16),
            jax.ShapeDtypeStruct((T, 2 * D_KV), BF16),
            jax.ShapeDtypeStruct((T, 2 * D_KV), BF16),
            jax.ShapeDtypeStruct((T, d_ssm), BF16),
        ],
        compiler_params=_cparams("arbitrary"),
        name="inproj",
    )(x2d, mix_g[None, :], w_ext, qkg, ones, cos_t, sin_t)


def _attn_body(q_ref, k_ref, v_ref, o_ref, klo, khi, vlo, vhi):
    @pl.when(pl.program_id(2) == 0)
    def _():
        lo = lax.broadcasted_iota(jnp.int32, k_ref.shape, 1) < HEAD_DIM
        kd = k_ref[...]
        vd = v_ref[...]
        zero = jnp.zeros_like(kd)
        klo[...] = jnp.where(lo, kd, zero)
        khi[...] = jnp.where(lo, zero, kd)
        vlo[...] = jnp.where(lo, vd, zero)
        vhi[...] = jnp.where(lo, zero, vd)

    for p in range(2):
        qp = q_ref[:, p * LANES:(p + 1) * LANES]
        acc = None
        for km, vm in ((klo, vlo), (khi, vhi)):
            s = _dot_nt(qp, km[...])
            e = jnp.exp(s - jnp.max(s, axis=-1, keepdims=True))
            inv = 1.0 / jnp.sum(e, axis=-1, keepdims=True)
            o = _dot(e.astype(BF16), vm[...]) * inv
            acc = o if acc is None else acc + o
        o_ref[:, p * LANES:(p + 1) * LANES] = acc.astype(BF16)


def _attention(q, kd, vd, B, S):
    tq = 256
    q3 = q.reshape(B, S, D_ATT)
    k3 = kd.reshape(B, S, 2 * D_KV)
    v3 = vd.reshape(B, S, 2 * D_KV)
    gw = D_ATT // ATT_KV_HEADS
    out = pl.pallas_call(
        _attn_body,
        grid=(B, ATT_KV_HEADS, S // tq),
        in_specs=[
            pl.BlockSpec((None, tq, gw), lambda b, j, i: (b, i, j)),
            pl.BlockSpec((None, S, LANES), lambda b, j, i: (b, 0, j)),
            pl.BlockSpec((None, S, LANES), lambda b, j, i: (b, 0, j)),
        ],
        out_specs=pl.BlockSpec((None, tq, gw), lambda b, j, i: (b, i, j)),
        out_shape=jax.ShapeDtypeStruct((B, S, D_ATT), BF16),
        scratch_shapes=[pltpu.VMEM((S, LANES), BF16)] * 4,
        compiler_params=_cparams("arbitrary", "arbitrary", "arbitrary"),
        name="attn",
    )(q3, k3, v3)
    return out.reshape(B * S, D_ATT)


def _s5_operators(lam_re, lam_im, log_step, b_re, b_im, c_re, c_im):
    hp = lax.Precision.HIGHEST
    lr = jnp.minimum(lam_re.astype(F32), -1e-4)
    li = lam_im.astype(F32)
    dt = jnp.exp(log_step.astype(F32))[..., None]
    mag = jnp.exp(lr * dt)
    ab_r = mag * jnp.cos(li * dt)
    ab_i = mag * jnp.sin(li * dt)
    den = lr * lr + li * li
    nr = ab_r - 1.0
    f_r = (nr * lr + ab_i * li) / den
    f_i = (ab_i * lr - nr * li) / den
    br = b_re.astype(F32)
    bi = b_im.astype(F32)
    bb_r = f_r[..., None] * br - f_i[..., None] * bi
    bb_i = f_r[..., None] * bi + f_i[..., None] * br
    pr = [jnp.ones_like(ab_r)]
    pi = [jnp.zeros_like(ab_i)]
    for _ in range(CHUNK):
        pr.append(pr[-1] * ab_r - pi[-1] * ab_i)
        pi.append(pr[-2] * ab_i + pi[-1] * ab_r)
    pw_r = jnp.stack(pr, axis=2)
    pw_i = jnp.stack(pi, axis=2)
    AB_r = pw_r[..., None] * bb_r[:, :, None] - pw_i[..., None] * bb_i[:, :, None]
    AB_i = pw_r[..., None] * bb_i[:, :, None] + pw_i[..., None] * bb_r[:, :, None]
    cr = c_re.astype(F32)
    ci = c_im.astype(F32)
    K = (jnp.einsum('dgcp,dgnpi->dgnci', cr, AB_r, precision=hp)
         - jnp.einsum('dgcp,dgnpi->dgnci', ci, AB_i, precision=hp))
    CA_r = cr[:, :, None] * pw_r[:, :, :, None] - ci[:, :, None] * pw_i[:, :, :, None]
    CA_i = cr[:, :, None] * pw_i[:, :, :, None] + ci[:, :, None] * pw_r[:, :, :, None]
    G = lr.shape[1]
    C = SSM_GROUP
    L = CHUNK
    Kcat = jnp.concatenate([K[1][:, 1:L][:, ::-1], (K[0][:, 0] + K[1][:, 0])[:, None], K[0][:, 1:L]],
                           axis=1)
    jj, tt = np.meshgrid(np.arange(L), np.arange(L), indexing='ij')
    M = Kcat[:, tt - jj + (L - 1)]
    M = M.transpose(0, 1, 4, 2, 3).reshape(G, L * C, L * C)
    rev = np.arange(L - 1, -1, -1)
    ef_r = AB_r[0][:, rev].transpose(0, 1, 3, 2).reshape(G, L * C, SSM_STATE)
    ef_i = AB_i[0][:, rev].transpose(0, 1, 3, 2).reshape(G, L * C, SSM_STATE)
    eb_r = AB_r[1][:, :L].transpose(0, 1, 3, 2).reshape(G, L * C, SSM_STATE)
    eb_i = AB_i[1][:, :L].transpose(0, 1, 3, 2).reshape(G, L * C, SSM_STATE)
    E = jnp.concatenate([ef_r, eb_r, ef_i, eb_i], axis=2)
    W1 = jnp.concatenate([M, E], axis=2).astype(BF16)
    df_r = CA_r[0][:, 1:L + 1].transpose(0, 3, 1, 2).reshape(G, SSM_STATE, L * C)
    df_i = CA_i[0][:, 1:L + 1].transpose(0, 3, 1, 2).reshape(G, SSM_STATE, L * C)
    db_r = CA_r[1][:, 1:L + 1][:, ::-1].transpose(0, 3, 1, 2).reshape(G, SSM_STATE, L * C)
    db_i = CA_i[1][:, 1:L + 1][:, ::-1].transpose(0, 3, 1, 2).reshape(G, SSM_STATE, L * C)
    zero = jnp.zeros_like(df_r)
    DA = jnp.concatenate([df_r, zero, -df_i, zero], axis=1).astype(BF16)
    DB = jnp.concatenate([zero, db_r, zero, -db_i], axis=1).astype(BF16)
    a16 = jnp.stack([jnp.concatenate([pw_r[0][:, L], pw_r[1][:, L]], axis=1),
                     jnp.concatenate([pw_i[0][:, L], pw_i[1][:, L]], axis=1)], axis=1)
    return W1, DA, DB, a16


def _s5_body(u_ref, w1_ref, da_ref, db_ref, a_ref, d_ref, z_ref, y_s, x_s, ha_s, hb_s, *, nb):
    R = u_ref.shape[0]
    nchunk = R // nb
    half = x_s.shape[1] // 2
    u = u_ref[...]
    xy = _dot(u, w1_ref[...])
    y_s[...] = xy[:, :2 * half]
    x_s[...] = xy[:, 2 * half:]
    ar = jnp.broadcast_to(a_ref[0:1, :], (nb, half))
    ai = jnp.broadcast_to(a_ref[1:2, :], (nb, half))
    fwd = lax.broadcasted_iota(jnp.int32, (nb, half), 1) < SSM_STATE

    def step(c, carry):
        sr, si = carry
        rf = pl.multiple_of(c * nb, nb)
        rb = pl.multiple_of((nchunk - 1 - c) * nb, nb)
        ha_s[pl.ds(rf, nb), :] = jnp.concatenate([sr, si], axis=1)
        hb_s[pl.ds(rb, nb), :] = jnp.concatenate([sr, si], axis=1)
        xf = x_s[pl.ds(rf, nb), :]
        xb = x_s[pl.ds(rb, nb), :]
        xr = jnp.where(fwd, xf[:, :half], xb[:, :half])
        xi = jnp.where(fwd, xf[:, half:], xb[:, half:])
        return sr * ar - si * ai + xr, sr * ai + si * ar + xi

    zero = jnp.zeros((nb, half), F32)
    lax.fori_loop(0, nchunk, step, (zero, zero))
    y = (y_s[...] + _dot(ha_s[...].astype(BF16), da_ref[...])
         + _dot(hb_s[...].astype(BF16), db_ref[...]) + d_ref[...] * u.astype(F32))
    z_ref[...] = jax.nn.gelu(y).astype(BF16)


def _s5(u, B, S, lam_re, lam_im, log_step, b_re, b_im, c_re, c_im, ssm_d):
    T, d_ssm = u.shape
    G = d_ssm // SSM_GROUP
    nchunk = S // CHUNK
    R = nchunk * B
    W = CHUNK * SSM_GROUP
    W1, DA, DB, a16 = _s5_operators(lam_re, lam_im, log_step, b_re, b_im, c_re, c_im)
    dvec = jnp.tile(ssm_d.astype(F32).reshape(G, 1, SSM_GROUP), (1, CHUNK, 1)).reshape(G, 1, W)
    ug = u.reshape(B, nchunk, CHUNK, G, SSM_GROUP).transpose(3, 1, 0, 2, 4).reshape(G, R, W)
    blk = lambda *s: pl.BlockSpec((None,) + s, lambda g: (g,) + (0,) * len(s))
    z = pl.pallas_call(
        functools.partial(_s5_body, nb=B),
        grid=(G,),
        in_specs=[blk(R, W), blk(W, 2 * W), blk(W, W), blk(W, W), blk(2, W // 2), blk(1, W)],
        out_specs=blk(R, W),
        out_shape=jax.ShapeDtypeStruct((G, R, W), BF16),
        scratch_shapes=[pltpu.VMEM((R, W), F32)] * 4,
        compiler_params=_cparams("arbitrary"),
        name="s5",
    )(ug, W1, DA, DB, a16, dvec)
    return z.reshape(G, nchunk, B, CHUNK, SSM_GROUP).transpose(2, 1, 3, 0, 4).reshape(T, d_ssm)


def _memkv_body(m_ref, g_ref, wk_ref, wv_ref, k_ref, v_ref):
    h = (_rms(m_ref[...]) * g_ref[...]).astype(BF16)
    k_ref[...] = _dot(h, wk_ref[...]).astype(BF16)
    v_ref[...] = _dot(h, wv_ref[...]).astype(BF16)


def _memkv(mem, g, wk, wv):
    B, M, D = mem.shape
    const = lambda b: (0, 0)
    row = pl.BlockSpec((None, M, D), lambda b: (b, 0, 0))
    return pl.pallas_call(
        _memkv_body,
        grid=(B,),
        in_specs=[row, pl.BlockSpec((1, D), const), pl.BlockSpec((D, D), const),
                  pl.BlockSpec((D, D), const)],
        out_specs=[row, row],
        out_shape=[jax.ShapeDtypeStruct((B, M, D), BF16)] * 2,
        compiler_params=_cparams("arbitrary"),
        name="memkv",
    )(mem, g[None, :], wk.astype(BF16), wv.astype(BF16))


def _mid_body(x_ref, ya_ref, z_ref, gw_ref, gb_ref, ag_ref, sg_ref, wo_ref, xg_ref,
              wq_ref, xwo_ref, k_ref, v_ref, o_ref):
    D = x_ref.shape[1]
    z = z_ref[...]
    s = z.astype(F32) * jax.nn.sigmoid(_dot(z, gw_ref[...]) + gb_ref[...])
    ya = _rms(ya_ref[...].astype(F32)) * ag_ref[...]
    ys = _rms(s) * sg_ref[...]
    y = jnp.concatenate([ya, ys], axis=1).astype(BF16)
    x1 = x_ref[...] + _dot(y, wo_ref[...])
    hq = (_rms(x1) * xg_ref[...]).astype(BF16)
    dh = D // XA_HEADS
    q = (_dot(hq, wq_ref[...]) * (dh ** -0.5)).astype(BF16)
    outs = []
    for h in range(XA_HEADS):
        sl = slice(h * dh, (h + 1) * dh)
        sc = _dot_nt(q[:, sl], k_ref[:, sl])
        e = jnp.exp(sc - jnp.max(sc, axis=-1, keepdims=True))
        inv = 1.0 / jnp.sum(e, axis=-1, keepdims=True)
        outs.append((_dot(e.astype(BF16), v_ref[:, sl]) * inv).astype(BF16))
    o = jnp.concatenate(outs, axis=1)
    o_ref[:, :D] = x1 + _dot(o, xwo_ref[...])
    o_ref[:, D:] = jnp.zeros((o_ref.shape[0], o_ref.shape[1] - D), F32)


def _mid(x2d, S, y_att, z, glu_w, glu_b, att_g, ssm_g, w_out, xa_g, wq, wo, kmem, vmem):
    T, D = x2d.shape
    M = kmem.shape[1]
    tm = 256
    sb = S // tm
    const = lambda i: (0, 0)
    tok = lambda w: pl.BlockSpec((tm, w), lambda i: (i, 0))
    full = lambda a: pl.BlockSpec(a.shape, const)
    vec = lambda a: a.astype(F32)[None, :]
    mem = pl.BlockSpec((None, M, D), lambda i: (i // sb, 0, 0))
    args = [x2d, y_att, z, glu_w.astype(BF16), vec(glu_b), vec(att_g), vec(ssm_g), w_out.astype(BF16),
            vec(xa_g), wq.astype(BF16), wo.astype(BF16), kmem, vmem]
    specs = [tok(D), tok(y_att.shape[1]), tok(z.shape[1])] + [full(a) for a in args[3:11]] + [mem, mem]
    return pl.pallas_call(
        _mid_body,
        grid=(T // tm,),
        in_specs=specs,
        out_specs=pl.BlockSpec((tm, D + LANES), lambda i: (i, 0)),
        out_shape=jax.ShapeDtypeStruct((T, D + LANES), F32),
        compiler_params=_cparams("arbitrary"),
        name="mid",
    )(*args)


def _split_bf16(a):
    hi = a.astype(BF16)
    return hi, (a - hi.astype(F32)).astype(BF16)


def _route_body(x_ref, g_ref, whi_ref, wlo_ref, b_ref, meta_ref, cnt_ref, carry):
    i = pl.program_id(0)
    tm, D = x_ref.shape[0], g_ref.shape[1]

    @pl.when(i == 0)
    def _():
        carry[...] = jnp.zeros_like(carry)

    h = _rms(x_ref[:, :D]) * g_ref[...]
    hhi, hlo = _split_bf16(h)
    logits = (_dot(hhi, whi_ref[...]) + _dot(hlo, whi_ref[...]) + _dot(hhi, wlo_ref[...])
              + b_ref[...])
    lane = lax.broadcasted_iota(jnp.int32, logits.shape, 1)
    ninf = jnp.float32(-jnp.inf)
    big = jnp.int32(LANES)
    rmax = lambda v: jnp.max(v, axis=-1, keepdims=True)
    rmin = lambda v: jnp.min(v, axis=-1, keepdims=True)
    is_grp = lane < N_EXPERT_GROUPS
    gl = jnp.where(is_grp, logits, ninf)
    gmax = rmax(gl)
    g_sel = rmin(jnp.where(gl == gmax, lane, big))
    g_w = 1.0 / jnp.sum(jnp.where(is_grp, jnp.exp(gl - gmax), 0.0), axis=-1, keepdims=True)
    e_id = lane - N_EXPERT_GROUPS
    in_grp = (e_id >= 0) & (e_id < N_EXPERTS) & ((e_id >> 2) == g_sel)
    el = jnp.where(in_grp, logits, ninf)
    v1 = rmax(el)
    i1 = rmin(jnp.where(el == v1, lane, big))
    el2 = jnp.where(lane == i1, ninf, el)
    v2 = rmax(el2)
    i2 = rmin(jnp.where(el2 == v2, lane, big))
    t = jnp.exp(v2 - v1)
    w1 = g_w / (1.0 + t)
    w2 = g_w * t / (1.0 + t)
    first_low = i1 < i2
    ia = (jnp.minimum(i1, i2) - N_EXPERT_GROUPS) & 3
    ib = (jnp.maximum(i1, i2) - N_EXPERT_GROUPS) & 3
    wa = jnp.where(first_low, w1, w2)
    wb = jnp.where(first_low, w2, w1)
    cls = g_sel * PAIRS_PER_GROUP + ((ia * (7 - ia)) >> 1) + ib - ia - 1
    onehot = (lane == cls).astype(F32)
    r = lax.broadcasted_iota(jnp.int32, (tm, tm), 0)
    c = lax.broadcasted_iota(jnp.int32, (tm, tm), 1)
    before = _dot((c < r).astype(BF16), onehot.astype(BF16))
    rank = jnp.sum(onehot * (before + carry[...]), axis=-1, keepdims=True)
    carry[...] = carry[...] + jnp.sum(onehot, axis=0, keepdims=True)
    cnt_ref[...] = carry[...]
    meta = jnp.where(lane == META_CLS, cls.astype(F32),
                     jnp.where(lane == META_RANK, rank,
                               jnp.where(lane == META_WA, wa,
                                         jnp.where(lane == META_WB, wb, 0.0))))
    meta_ref[...] = meta


def _route(x2a, D, moe_g, grp_w, grp_b, exp_w, exp_b):
    T = x2a.shape[0]
    tm = 256
    pad = LANES - N_EXPERT_GROUPS - N_EXPERTS
    w = jnp.concatenate([grp_w, exp_w, jnp.zeros((D, pad), F32)], axis=1).astype(F32)
    b = jnp.concatenate([grp_b, exp_b, jnp.zeros((pad,), F32)]).astype(F32)[None, :]
    whi, wlo = _split_bf16(w)
    const = lambda i: (0, 0)
    ncol = D // LANES
    return pl.pallas_call(
        _route_body,
        grid=(T // tm,),
        in_specs=[pl.BlockSpec((tm, D + LANES), lambda i: (i, 0)),
                  pl.BlockSpec((1, D), const), pl.BlockSpec((D, LANES), const),
                  pl.BlockSpec((D, LANES), const), pl.BlockSpec((1, LANES), const)],
        out_specs=[pl.BlockSpec((tm, LANES), lambda i: (i, ncol)),
                   pl.BlockSpec((1, LANES), const)],
        out_shape=[jax.ShapeDtypeStruct(x2a.shape, F32), jax.ShapeDtypeStruct((1, LANES), F32)],
        scratch_shapes=[pltpu.VMEM((1, LANES), F32)],
        input_output_aliases={0: 0},
        compiler_params=_cparams("arbitrary"),
        name="route",
    )(x2a, moe_g.astype(F32)[None, :], whi, wlo, b)


def _row_copy_body(pos_ref, src_ref, *rest, rows, scatter):
    dst_ref, sem = rest[-2:]
    base = pl.program_id(0) * rows

    def copy(r):
        p = pos_ref[r]
        if scatter:
            return pltpu.make_async_copy(src_ref.at[pl.ds(base + r, 1)], dst_ref.at[pl.ds(p, 1)], sem)
        return pltpu.make_async_copy(src_ref.at[pl.ds(p, 1)], dst_ref.at[pl.ds(base + r, 1)], sem)

    def start(r, _):
        copy(r).start()
        return 0

    def wait(r, _):
        copy(r).wait()
        return 0

    lax.fori_loop(0, rows, start, 0, unroll=8)
    lax.fori_loop(0, rows, wait, 0, unroll=8)


def _row_copy(pos, src, *, dst_init=None, dst_rows=None, name):
    T = pos.shape[0]
    rows = 512
    scatter = dst_init is not None
    hbm = pl.BlockSpec(memory_space=pl.ANY)
    out_rows = dst_init.shape[0] if scatter else dst_rows
    return pl.pallas_call(
        functools.partial(_row_copy_body, rows=rows, scatter=scatter),
        grid=(T // rows,),
        in_specs=[pl.BlockSpec((rows,), lambda i: (i,), memory_space=pltpu.SMEM), hbm]
        + ([hbm] if scatter else []),
        out_specs=hbm,
        out_shape=jax.ShapeDtypeStruct((out_rows, src.shape[1]), src.dtype),
        scratch_shapes=[pltpu.SemaphoreType.DMA(())],
        input_output_aliases={2: 0} if scatter else {},
        compiler_params=pltpu.CompilerParams(dimension_semantics=("arbitrary",)),
        name=name,
    )(*((pos, src, dst_init) if scatter else (pos, src)))


def _ffn_body(ea_ref, eb_ref, nt_ref, xs_ref, g_ref, fg_ref, wga, wua, wda, wgb, wub, wdb, o_ref):
    del ea_ref, eb_ref
    D = g_ref.shape[1]
    valid = pl.program_id(0) < nt_ref[0]

    @pl.when(valid)
    def _():
        x = xs_ref[:, :D]
        meta = xs_ref[:, D:]
        h = (_rms(x) * g_ref[...]).astype(BF16)
        y = x
        for wg, wu, wd, lane in ((wga, wua, wda, META_WA), (wgb, wub, wdb, META_WB)):
            a = jax.nn.silu(_dot(h, wg[...])) * _dot(h, wu[...]) * meta[:, lane:lane + 1]
            y = y + _dot(a.astype(BF16), wd[...])
        o_ref[...] = _rms(y) * fg_ref[...]

    @pl.when(jnp.logical_not(valid))
    def _():
        o_ref[...] = jnp.zeros_like(o_ref)


def _ffn(xs, D, tile_ea, tile_eb, n_tiles, moe_g, final_g, w_gate, w_up, w_down, tm):
    P = xs.shape[0]
    nt_max = P // tm
    De = w_gate.shape[2]
    wg = w_gate.astype(BF16)
    wu = w_up.astype(BF16)
    wd = w_down.astype(BF16)
    const = lambda i, ea, eb, nt: (0, 0)
    sel_a = lambda i, ea, eb, nt: (ea[i], 0, 0)
    sel_b = lambda i, ea, eb, nt: (eb[i], 0, 0)
    up = lambda sel: pl.BlockSpec((None, D, De), sel)
    down = lambda sel: pl.BlockSpec((None, De, D), sel)
    grid_spec = pltpu.PrefetchScalarGridSpec(
        num_scalar_prefetch=3,
        grid=(nt_max,),
        in_specs=[pl.BlockSpec((tm, D + LANES), lambda i, ea, eb, nt: (jnp.minimum(i, nt[0] - 1), 0)),
                  pl.BlockSpec((1, D), const), pl.BlockSpec((1, D), const),
                  up(sel_a), up(sel_a), down(sel_a), up(sel_b), up(sel_b), down(sel_b)],
        out_specs=pl.BlockSpec((tm, D), lambda i, ea, eb, nt: (i, 0)),
    )
    return pl.pallas_call(
        _ffn_body,
        grid_spec=grid_spec,
        out_shape=jax.ShapeDtypeStruct((P, D), F32),
        compiler_params=_cparams("arbitrary"),
        name="ffn",
    )(tile_ea, tile_eb, n_tiles, xs, moe_g.astype(F32)[None, :], final_g.astype(F32)[None, :],
      wg, wu, wd, wg, wu, wd)


def _moe(x2a, D, moe_g, grp_w, grp_b, exp_w, exp_b, w_gate, w_up, w_down, final_g):
    T = x2a.shape[0]
    tm = 256
    x2a, counts = _route(x2a, D, moe_g, grp_w, grp_b, exp_w, exp_b)
    cnt = counts[0, :N_CLASSES].astype(jnp.int32)
    ntile_c = (cnt + tm - 1) // tm
    tile_end = jnp.cumsum(ntile_c)
    off = (tile_end - ntile_c) * tm
    cls = x2a[:, D + META_CLS].astype(jnp.int32)
    rank = x2a[:, D + META_RANK].astype(jnp.int32)
    pos = off[cls] + rank
    nt_max = T // tm + N_CLASSES
    n_tiles = tile_end[-1:]
    tile_cls = jnp.minimum(jnp.searchsorted(tile_end, jnp.arange(nt_max, dtype=jnp.int32), side='right'),
                           N_CLASSES - 1).astype(jnp.int32)
    tile_cls = jnp.where(jnp.arange(nt_max) < n_tiles[0], tile_cls, tile_cls[jnp.maximum(n_tiles[0] - 1, 0)])
    pa = jnp.asarray([0, 0, 0, 1, 1, 2], jnp.int32)
    pb = jnp.asarray([1, 2, 3, 2, 3, 3], jnp.int32)
    grp = tile_cls // PAIRS_PER_GROUP
    tile_ea = grp * EXPERTS_PER_GROUP + pa[tile_cls % PAIRS_PER_GROUP]
    tile_eb = grp * EXPERTS_PER_GROUP + pb[tile_cls % PAIRS_PER_GROUP]
    xs = _row_copy(pos, x2a, dst_init=jnp.zeros((nt_max * tm, D + LANES), F32), name="dispatch")
    ys = _ffn(xs, D, tile_ea, tile_eb, n_tiles.astype(jnp.int32), moe_g, final_g, w_gate, w_up, w_down, tm)
    return _row_copy(pos, ys, dst_rows=T, name="unsort")


def kernel(x, mem, mix_norm_g, w_in, q_norm_g, k_norm_g, ssm_lambda_re, ssm_lambda_im, ssm_log_step, ssm_b_re, ssm_b_im, ssm_c_re, ssm_c_im, ssm_d, ssm_glu_w, ssm_glu_b, att_out_g, ssm_out_g, w_out, xa_norm_g, mem_norm_g, xa_wq, xa_wk, xa_wv, xa_wo, moe_norm_g, router_grp_w, router_grp_b, router_exp_w, router_exp_b, exp_w_gate, exp_w_up, exp_w_down, final_norm_g):
    B, S, D = x.shape
    assert mix_norm_g.shape[0] == 1, "single layer"
    x2d = x.reshape(B * S, D)
    q, kd, vd, u = _inproj(x2d, S, mix_norm_g[0], w_in[0], q_norm_g[0], k_norm_g[0])
    y_att = _attention(q, kd, vd, B, S)
    z = _s5(u, B, S, ssm_lambda_re[0], ssm_lambda_im[0], ssm_log_step[0], ssm_b_re[0], ssm_b_im[0],
            ssm_c_re[0], ssm_c_im[0], ssm_d[0])
    kmem, vmem = _memkv(mem, mem_norm_g[0], xa_wk[0], xa_wv[0])
    x2a = _mid(x2d, S, y_att, z, ssm_glu_w[0], ssm_glu_b[0], att_out_g[0], ssm_out_g[0], w_out[0],
               xa_norm_g[0], xa_wq[0], xa_wo[0], kmem, vmem)
    out = _moe(x2a, D, moe_norm_g[0], router_grp_w[0], router_grp_b[0], router_exp_w[0],
               router_exp_b[0], exp_w_gate[0], exp_w_up[0], exp_w_down[0], final_norm_g)
    return out.reshape(B, S, D)
```

```python
import functools

import numpy as np
import jax
import jax.numpy as jnp
from jax import lax
from jax.experimental import pallas as pl
from jax.experimental.pallas import tpu as pltpu

F32 = jnp.float32
BF16 = jnp.bfloat16

EPS = 1e-6
GRID_W = 64
ROPE_THETA = 10000.0
ATT_HEADS = 8
ATT_KV_HEADS = 2
HEAD_DIM = 64
D_ATT = ATT_HEADS * HEAD_DIM
D_KV = ATT_KV_HEADS * HEAD_DIM
SSM_GROUP = 16
SSM_STATE = 64
XA_HEADS = 4
N_EXPERT_GROUPS = 4
EXPERTS_PER_GROUP = 4
N_EXPERTS = N_EXPERT_GROUPS * EXPERTS_PER_GROUP
PAIRS_PER_GROUP = 6
N_CLASSES = N_EXPERT_GROUPS * PAIRS_PER_GROUP

LANES = 128
SUBLANES = 8
CHUNK = 16
VMEM_LIMIT = 56 * 1024 * 1024

META_CLS, META_RANK, META_WA, META_WB = 0, 1, 2, 3


def _cparams(*sem):
    return pltpu.CompilerParams(dimension_semantics=sem, vmem_limit_bytes=VMEM_LIMIT)


def _rms(x):
    return x * lax.rsqrt(jnp.mean(x * x, axis=-1, keepdims=True) + EPS)


def _dot(a, b):
    return jnp.dot(a, b, preferred_element_type=F32)


def _dot_nt(a, b):
    return lax.dot_general(a, b, (((1,), (1,)), ((), ())), preferred_element_type=F32)


def _inproj_body(x_ref, g_ref, w_ref, qkg_ref, ones_ref, cos_ref, sin_ref,
                 q_ref, k_ref, v_ref, u_ref):
    n_qk = q_ref.shape[1] + k_ref.shape[1]
    h = _rms(x_ref[...]) * g_ref[...]
    proj = _dot(h.astype(BF16), w_ref[...])
    qk = proj[:, :n_qk]
    ssq = _dot((qk * qk).astype(BF16), ones_ref[...])
    qk = qk * lax.rsqrt(ssq * (1.0 / HEAD_DIM) + EPS) * qkg_ref[...]
    cos = cos_ref[...]
    sin = sin_ref[...]
    lane = lax.broadcasted_iota(jnp.int32, cos.shape, 1)
    first = (lane % 32) < 16
    nq = q_ref.shape[1] // LANES
    for c in range(n_qk // LANES):
        xc = qk[:, c * LANES:(c + 1) * LANES]
        partner = jnp.where(first, pltpu.roll(xc, LANES - 16, 1), pltpu.roll(xc, 16, 1))
        rot = xc * cos + partner * sin
        if c < nq:
            q_ref[:, c * LANES:(c + 1) * LANES] = (rot * (HEAD_DIM ** -0.5)).astype(BF16)
        else:
            k_ref[:, (c - nq) * LANES:(c - nq + 1) * LANES] = rot.astype(BF16)
    nv = n_qk + v_ref.shape[1]
    v_ref[...] = proj[:, n_qk:nv].astype(BF16)
    u_ref[...] = proj[:, nv:].astype(BF16)


def _rope_tables(S):
    t = jnp.arange(S, dtype=jnp.int32)
    freqs = ROPE_THETA ** (-jnp.arange(16, dtype=F32) / 16)
    halves = []
    for pos in (t // GRID_W, t % GRID_W):
        ang = pos.astype(F32)[:, None] * freqs[None, :]
        halves.append((jnp.cos(ang), jnp.sin(ang)))
    cos_h = jnp.concatenate([c for c, _ in halves for _ in range(2)], axis=1)
    sin_h = jnp.concatenate([sg * s for _, s in halves for sg in (-1.0, 1.0)], axis=1)
    return jnp.concatenate([cos_h, cos_h], axis=1), jnp.concatenate([sin_h, sin_h], axis=1)


def _inproj(x2d, S, mix_g, w_in, q_g, k_g):
    T, D = x2d.shape
    tm = 512
    d_ssm = w_in.shape[1] - D_ATT - 2 * D_KV
    wq = w_in[:, :D_ATT]
    wk = w_in[:, D_ATT:D_ATT + D_KV]
    wv = w_in[:, D_ATT + D_KV:D_ATT + 2 * D_KV]
    wu = w_in[:, D_ATT + 2 * D_KV:]
    dup = lambda w: jnp.concatenate([w[:, :64], w[:, :64], w[:, 64:], w[:, 64:]], axis=1)
    w_ext = jnp.concatenate([wq, dup(wk), dup(wv), wu], axis=1).astype(BF16)
    n_qk = D_ATT + 2 * D_KV
    qkg = jnp.concatenate([jnp.tile(q_g, ATT_HEADS), jnp.tile(k_g, 2 * ATT_KV_HEADS)])[None, :]
    hid = np.arange(n_qk) // HEAD_DIM
    ones = jnp.asarray((hid[:, None] == hid[None, :]).astype(np.float32), dtype=BF16)
    cos_t, sin_t = _rope_tables(S)
    sb = S // tm
    const = lambda i: (0, 0)
    return pl.pallas_call(
        _inproj_body,
        grid=(T // tm,),
        in_specs=[
            pl.BlockSpec((tm, D), lambda i: (i, 0)),
            pl.BlockSpec((1, D), const),
            pl.BlockSpec(w_ext.shape, const),
            pl.BlockSpec((1, n_qk), const),
            pl.BlockSpec((n_qk, n_qk), const),
            pl.BlockSpec((tm, LANES), lambda i: (i % sb, 0)),
            pl.BlockSpec((tm, LANES), lambda i: (i % sb, 0)),
        ],
        out_specs=[
            pl.BlockSpec((tm, D_ATT), lambda i: (i, 0)),
            pl.BlockSpec((tm, 2 * D_KV), lambda i: (i, 0)),
            pl.BlockSpec((tm, 2 * D_KV), lambda i: (i, 0)),
            pl.BlockSpec((tm, d_ssm), lambda i: (i, 0)),
        ],
        out_shape=[
            jax.ShapeDtypeStruct((T, D_ATT), BF16),
            jax.ShapeDtypeStruct((T, 2 * D_KV), BF16),
            jax.ShapeDtypeStruct((T, 2 * D_KV), BF16),
            jax.ShapeDtypeStruct((T, d_ssm), BF16),
        ],
        compiler_params=_cparams("arbitrary"),
        name="inproj",
    )(x2d, mix_g[None, :], w_ext, qkg, ones, cos_t, sin_t)


def _attn_body(q_ref, k_ref, v_ref, o_ref, klo, khi, vlo, vhi):
    @pl.when(pl.program_id(2) == 0)
    def _():
        lo = lax.broadcasted_iota(jnp.int32, k_ref.shape, 1) < HEAD_DIM
        kd = k_ref[...]
        vd = v_ref[...]
        zero = jnp.zeros_like(kd)
        klo[...] = jnp.where(lo, kd, zero)
        khi[...] = jnp.where(lo, zero, kd)
        vlo[...] = jnp.where(lo, vd, zero)
        vhi[...] = jnp.where(lo, zero, vd)

    for p in range(2):
        qp = q_ref[:, p * LANES:(p + 1) * LANES]
        acc = None
        for km, vm in ((klo, vlo), (khi, vhi)):
            s = _dot_nt(qp, km[...])
            e = jnp.exp(s - jnp.max(s, axis=-1, keepdims=True))
            inv = 1.0 / jnp.sum(e, axis=-1, keepdims=True)
            o = _dot(e.astype(BF16), vm[...]) * inv
            acc = o if acc is None else acc + o
        o_ref[:, p * LANES:(p + 1) * LANES] = acc.astype(BF16)


def _attention(q, kd, vd, B, S):
    tq = 256
    q3 = q.reshape(B, S, D_ATT)
    k3 = kd.reshape(B, S, 2 * D_KV)
    v3 = vd.reshape(B, S, 2 * D_KV)
    gw = D_ATT // ATT_KV_HEADS
    out = pl.pallas_call(
        _attn_body,
        grid=(B, ATT_KV_HEADS, S // tq),
        in_specs=[
            pl.BlockSpec((None, tq, gw), lambda b, j, i: (b, i, j)),
            pl.BlockSpec((None, S, LANES), lambda b, j, i: (b, 0, j)),
            pl.BlockSpec((None, S, LANES), lambda b, j, i: (b, 0, j)),
        ],
        out_specs=pl.BlockSpec((None, tq, gw), lambda b, j, i: (b, i, j)),
        out_shape=jax.ShapeDtypeStruct((B, S, D_ATT), BF16),
        scratch_shapes=[pltpu.VMEM((S, LANES), BF16)] * 4,
        compiler_params=_cparams("arbitrary", "arbitrary", "arbitrary"),
        name="attn",
    )(q3, k3, v3)
    return out.reshape(B * S, D_ATT)


def _s5_operators(lam_re, lam_im, log_step, b_re, b_im, c_re, c_im):
    hp = lax.Precision.HIGHEST
    lr = jnp.minimum(lam_re.astype(F32), -1e-4)
    li = lam_im.astype(F32)
    dt = jnp.exp(log_step.astype(F32))[..., None]
    mag = jnp.exp(lr * dt)
    ab_r = mag * jnp.cos(li * dt)
    ab_i = mag * jnp.sin(li * dt)
    den = lr * lr + li * li
    nr = ab_r - 1.0
    f_r = (nr * lr + ab_i * li) / den
    f_i = (ab_i * lr - nr * li) / den
    br = b_re.astype(F32)
    bi = b_im.astype(F32)
    bb_r = f_r[..., None] * br - f_i[..., None] * bi
    bb_i = f_r[..., None] * bi + f_i[..., None] * br
    pr = [jnp.ones_like(ab_r)]
    pi = [jnp.zeros_like(ab_i)]
    for _ in range(CHUNK):
        pr.append(pr[-1] * ab_r - pi[-1] * ab_i)
        pi.append(pr[-2] * ab_i + pi[-1] * ab_r)
    pw_r = jnp.stack(pr, axis=2)
    pw_i = jnp.stack(pi, axis=2)
    AB_r = pw_r[..., None] * bb_r[:, :, None] - pw_i[..., None] * bb_i[:, :, None]
    AB_i = pw_r[..., None] * bb_i[:, :, None] + pw_i[..., None] * bb_r[:, :, None]
    cr = c_re.astype(F32)
    ci = c_im.astype(F32)
    K = (jnp.einsum('dgcp,dgnpi->dgnci', cr, AB_r, precision=hp)
         - jnp.einsum('dgcp,dgnpi->dgnci', ci, AB_i, precision=hp))
    CA_r = cr[:, :, None] * pw_r[:, :, :, None] - ci[:, :, None] * pw_i[:, :, :, None]
    CA_i = cr[:, :, None] * pw_i[:, :, :, None] + ci[:, :, None] * pw_r[:, :, :, None]
    G = lr.shape[1]
    C = SSM_GROUP
    L = CHUNK
    Kcat = jnp.concatenate([K[1][:, 1:L][:, ::-1], (K[0][:, 0] + K[1][:, 0])[:, None], K[0][:, 1:L]],
                           axis=1)
    jj, tt = np.meshgrid(np.arange(L), np.arange(L), indexing='ij')
    M = Kcat[:, tt - jj + (L - 1)]
    M = M.transpose(0, 1, 4, 2, 3).reshape(G, L * C, L * C)
    rev = np.arange(L - 1, -1, -1)
    ef_r = AB_r[0][:, rev].transpose(0, 1, 3, 2).reshape(G, L * C, SSM_STATE)
    ef_i = AB_i[0][:, rev].transpose(0, 1, 3, 2).reshape(G, L * C, SSM_STATE)
    eb_r = AB_r[1][:, :L].transpose(0, 1, 3, 2).reshape(G, L * C, SSM_STATE)
    eb_i = AB_i[1][:, :L].transpose(0, 1, 3, 2).reshape(G, L * C, SSM_STATE)
    E = jnp.concatenate([ef_r, eb_r, ef_i, eb_i], axis=2)
    W1 = jnp.concatenate([M, E], axis=2).astype(BF16)
    df_r = CA_r[0][:, 1:L + 1].transpose(0, 3, 1, 2).reshape(G, SSM_STATE, L * C)
    df_i = CA_i[0][:, 1:L + 1].transpose(0, 3, 1, 2).reshape(G, SSM_STATE, L * C)
    db_r = CA_r[1][:, 1:L + 1][:, ::-1].transpose(0, 3, 1, 2).reshape(G, SSM_STATE, L * C)
    db_i = CA_i[1][:, 1:L + 1][:, ::-1].transpose(0, 3, 1, 2).reshape(G, SSM_STATE, L * C)
    zero = jnp.zeros_like(df_r)
    DA = jnp.concatenate([df_r, zero, -df_i, zero], axis=1).astype(BF16)
    DB = jnp.concatenate([zero, db_r, zero, -db_i], axis=1).astype(BF16)
    a16 = jnp.stack([jnp.concatenate([pw_r[0][:, L], pw_r[1][:, L]], axis=1),
                     jnp.concatenate([pw_i[0][:, L], pw_i[1][:, L]], axis=1)], axis=1)
    return W1, DA, DB, a16


def _s5_body(u_ref, w1_ref, da_ref, db_ref, a_ref, d_ref, z_ref, y_s, x_s, ha_s, hb_s, *, nb):
    R = u_ref.shape[0]
    nchunk = R // nb
    half = x_s.shape[1] // 2
    u = u_ref[...]
    xy = _dot(u, w1_ref[...])
    y_s[...] = xy[:, :2 * half]
    x_s[...] = xy[:, 2 * half:]
    ar = jnp.broadcast_to(a_ref[0:1, :], (nb, half))
    ai = jnp.broadcast_to(a_ref[1:2, :], (nb, half))
    fwd = lax.broadcasted_iota(jnp.int32, (nb, half), 1) < SSM_STATE

    def step(c, carry):
        sr, si = carry
        rf = pl.multiple_of(c * nb, nb)
        rb = pl.multiple_of((nchunk - 1 - c) * nb, nb)
        ha_s[pl.ds(rf, nb), :] = jnp.concatenate([sr, si], axis=1)
        hb_s[pl.ds(rb, nb), :] = jnp.concatenate([sr, si], axis=1)
        xf = x_s[pl.ds(rf, nb), :]
        xb = x_s[pl.ds(rb, nb), :]
        xr = jnp.where(fwd, xf[:, :half], xb[:, :half])
        xi = jnp.where(fwd, xf[:, half:], xb[:, half:])
        return sr * ar - si * ai + xr, sr * ai + si * ar + xi

    zero = jnp.zeros((nb, half), F32)
    lax.fori_loop(0, nchunk, step, (zero, zero))
    y = (y_s[...] + _dot(ha_s[...].astype(BF16), da_ref[...])
         + _dot(hb_s[...].astype(BF16), db_ref[...]) + d_ref[...] * u.astype(F32))
    z_ref[...] = jax.nn.gelu(y).astype(BF16)


def _s5(u, B, S, lam_re, lam_im, log_step, b_re, b_im, c_re, c_im, ssm_d):
    T, d_ssm = u.shape
    G = d_ssm // SSM_GROUP
    nchunk = S // CHUNK
    R = nchunk * B
    W = CHUNK * SSM_GROUP
    W1, DA, DB, a16 = _s5_operators(lam_re, lam_im, log_step, b_re, b_im, c_re, c_im)
    dvec = jnp.tile(ssm_d.astype(F32).reshape(G, 1, SSM_GROUP), (1, CHUNK, 1)).reshape(G, 1, W)
    ug = u.reshape(B, nchunk, CHUNK, G, SSM_GROUP).transpose(3, 1, 0, 2, 4).reshape(G, R, W)
    blk = lambda *s: pl.BlockSpec((None,) + s, lambda g: (g,) + (0,) * len(s))
    z = pl.pallas_call(
        functools.partial(_s5_body, nb=B),
        grid=(G,),
        in_specs=[blk(R, W), blk(W, 2 * W), blk(W, W), blk(W, W), blk(2, W // 2), blk(1, W)],
        out_specs=blk(R, W),
        out_shape=jax.ShapeDtypeStruct((G, R, W), BF16),
        scratch_shapes=[pltpu.VMEM((R, W), F32)] * 4,
        compiler_params=_cparams("arbitrary"),
        name="s5",
    )(ug, W1, DA, DB, a16, dvec)
    return z.reshape(G, nchunk, B, CHUNK, SSM_GROUP).transpose(2, 1, 3, 0, 4).reshape(T, d_ssm)


def _memkv_body(m_ref, g_ref, wk_ref, wv_ref, k_ref, v_ref):
    h = (_rms(m_ref[...]) * g_ref[...]).astype(BF16)
    k_ref[...] = _dot(h, wk_ref[...]).astype(BF16)
    v_ref[...] = _dot(h, wv_ref[...]).astype(BF16)


def _memkv(mem, g, wk, wv):
    B, M, D = mem.shape
    const = lambda b: (0, 0)
    row = pl.BlockSpec((None, M, D), lambda b: (b, 0, 0))
    return pl.pallas_call(
        _memkv_body,
        grid=(B,),
        in_specs=[row, pl.BlockSpec((1, D), const), pl.BlockSpec((D, D), const),
                  pl.BlockSpec((D, D), const)],
        out_specs=[row, row],
        out_shape=[jax.ShapeDtypeStruct((B, M, D), BF16)] * 2,
        compiler_params=_cparams("arbitrary"),
        name="memkv",
    )(mem, g[None, :], wk.astype(BF16), wv.astype(BF16))


def _mid_body(x_ref, ya_ref, z_ref, gw_ref, gb_ref, ag_ref, sg_ref, wo_ref, xg_ref,
              wq_ref, xwo_ref, k_ref, v_ref, o_ref):
    D = x_ref.shape[1]
    z = z_ref[...]
    s = z.astype(F32) * jax.nn.sigmoid(_dot(z, gw_ref[...]) + gb_ref[...])
    ya = _rms(ya_ref[...].astype(F32)) * ag_ref[...]
    ys = _rms(s) * sg_ref[...]
    y = jnp.concatenate([ya, ys], axis=1).astype(BF16)
    x1 = x_ref[...] + _dot(y, wo_ref[...])
    hq = (_rms(x1) * xg_ref[...]).astype(BF16)
    dh = D // XA_HEADS
    q = (_dot(hq, wq_ref[...]) * (dh ** -0.5)).astype(BF16)
    outs = []
    for h in range(XA_HEADS):
        sl = slice(h * dh, (h + 1) * dh)
        sc = _dot_nt(q[:, sl], k_ref[:, sl])
        e = jnp.exp(sc - jnp.max(sc, axis=-1, keepdims=True))
        inv = 1.0 / jnp.sum(e, axis=-1, keepdims=True)
        outs.append((_dot(e.astype(BF16), v_ref[:, sl]) * inv).astype(BF16))
    o = jnp.concatenate(outs, axis=1)
    o_ref[:, :D] = x1 + _dot(o, xwo_ref[...])
    o_ref[:, D:] = jnp.zeros((o_ref.shape[0], o_ref.shape[1] - D), F32)


def _mid(x2d, S, y_att, z, glu_w, glu_b, att_g, ssm_g, w_out, xa_g, wq, wo, kmem, vmem):
    T, D = x2d.shape
    M = kmem.shape[1]
    tm = 256
    sb = S // tm
    const = lambda i: (0, 0)
    tok = lambda w: pl.BlockSpec((tm, w), lambda i: (i, 0))
    full = lambda a: pl.BlockSpec(a.shape, const)
    vec = lambda a: a.astype(F32)[None, :]
    mem = pl.BlockSpec((None, M, D), lambda i: (i // sb, 0, 0))
    args = [x2d, y_att, z, glu_w.astype(BF16), vec(glu_b), vec(att_g), vec(ssm_g), w_out.astype(BF16),
            vec(xa_g), wq.astype(BF16), wo.astype(BF16), kmem, vmem]
    specs = [tok(D), tok(y_att.shape[1]), tok(z.shape[1])] + [full(a) for a in args[3:11]] + [mem, mem]
    return pl.pallas_call(
        _mid_body,
        grid=(T // tm,),
        in_specs=specs,
        out_specs=pl.BlockSpec((tm, D + LANES), lambda i: (i, 0)),
        out_shape=jax.ShapeDtypeStruct((T, D + LANES), F32),
        compiler_params=_cparams("arbitrary"),
        name="mid",
    )(*args)


def _split_bf16(a):
    hi = a.astype(BF16)
    return hi, (a - hi.astype(F32)).astype(BF16)


def _route_body(x_ref, g_ref, whi_ref, wlo_ref, b_ref, meta_ref, cnt_ref, carry):
    i = pl.program_id(0)
    tm, D = x_ref.shape[0], g_ref.shape[1]

    @pl.when(i == 0)
    def _():
        carry[...] = jnp.zeros_like(carry)

    h = _rms(x_ref[:, :D]) * g_ref[...]
    hhi, hlo = _split_bf16(h)
    logits = (_dot(hhi, whi_ref[...]) + _dot(hlo, whi_ref[...]) + _dot(hhi, wlo_ref[...])
              + b_ref[...])
    lane = lax.broadcasted_iota(jnp.int32, logits.shape, 1)
    ninf = jnp.float32(-jnp.inf)
    big = jnp.int32(LANES)
    rmax = lambda v: jnp.max(v, axis=-1, keepdims=True)
    rmin = lambda v: jnp.min(v, axis=-1, keepdims=True)
    is_grp = lane < N_EXPERT_GROUPS
    gl = jnp.where(is_grp, logits, ninf)
    gmax = rmax(gl)
    g_sel = rmin(jnp.where(gl == gmax, lane, big))
    g_w = 1.0 / jnp.sum(jnp.where(is_grp, jnp.exp(gl - gmax), 0.0), axis=-1, keepdims=True)
    e_id = lane - N_EXPERT_GROUPS
    in_grp = (e_id >= 0) & (e_id < N_EXPERTS) & ((e_id >> 2) == g_sel)
    el = jnp.where(in_grp, logits, ninf)
    v1 = rmax(el)
    i1 = rmin(jnp.where(el == v1, lane, big))
    el2 = jnp.where(lane == i1, ninf, el)
    v2 = rmax(el2)
    i2 = rmin(jnp.where(el2 == v2, lane, big))
    t = jnp.exp(v2 - v1)
    w1 = g_w / (1.0 + t)
    w2 = g_w * t / (1.0 + t)
    first_low = i1 < i2
    ia = (jnp.minimum(i1, i2) - N_EXPERT_GROUPS) & 3
    ib = (jnp.maximum(i1, i2) - N_EXPERT_GROUPS) & 3
    wa = jnp.where(first_low, w1, w2)
    wb = jnp.where(first_low, w2, w1)
    cls = g_sel * PAIRS_PER_GROUP + ((ia * (7 - ia)) >> 1) + ib - ia - 1
    onehot = (lane == cls).astype(F32)
    r = lax.broadcasted_iota(jnp.int32, (tm, tm), 0)
    c = lax.broadcasted_iota(jnp.int32, (tm, tm), 1)
    before = _dot((c < r).astype(BF16), onehot.astype(BF16))
    rank = jnp.sum(onehot * (before + carry[...]), axis=-1, keepdims=True)
    carry[...] = carry[...] + jnp.sum(onehot, axis=0, keepdims=True)
    cnt_ref[...] = carry[...]
    meta = jnp.where(lane == META_CLS, cls.astype(F32),
                     jnp.where(lane == META_RANK, rank,
                               jnp.where(lane == META_WA, wa,
                                         jnp.where(lane == META_WB, wb, 0.0))))
    meta_ref[...] = meta


def _route(x2a, D, moe_g, grp_w, grp_b, exp_w, exp_b):
    T = x2a.shape[0]
    tm = 256
    pad = LANES - N_EXPERT_GROUPS - N_EXPERTS
    w = jnp.concatenate([grp_w, exp_w, jnp.zeros((D, pad), F32)], axis=1).astype(F32)
    b = jnp.concatenate([grp_b, exp_b, jnp.zeros((pad,), F32)]).astype(F32)[None, :]
    whi, wlo = _split_bf16(w)
    const = lambda i: (0, 0)
    ncol = D // LANES
    return pl.pallas_call(
        _route_body,
        grid=(T // tm,),
        in_specs=[pl.BlockSpec((tm, D + LANES), lambda i: (i, 0)),
                  pl.BlockSpec((1, D), const), pl.BlockSpec((D, LANES), const),
                  pl.BlockSpec((D, LANES), const), pl.BlockSpec((1, LANES), const)],
        out_specs=[pl.BlockSpec((tm, LANES), lambda i: (i, ncol)),
                   pl.BlockSpec((1, LANES), const)],
        out_shape=[jax.ShapeDtypeStruct(x2a.shape, F32), jax.ShapeDtypeStruct((1, LANES), F32)],
        scratch_shapes=[pltpu.VMEM((1, LANES), F32)],
        input_output_aliases={0: 0},
        compiler_params=_cparams("arbitrary"),
        name="route",
    )(x2a, moe_g.astype(F32)[None, :], whi, wlo, b)


ROW_COPY_UNROLL = 8


def _row_copy_body(pos_ref, src_ref, *rest, rows, scatter):
    dst_ref, sem = rest[-2:]

    def copy(r):
        p = pos_ref[r]
        if scatter:
            return pltpu.make_async_copy(src_ref.at[pl.ds(r, 1)], dst_ref.at[pl.ds(p, 1)], sem)
        return pltpu.make_async_copy(src_ref.at[pl.ds(p, 1)], dst_ref.at[pl.ds(r, 1)], sem)

    def start(j, _):
        for k in range(ROW_COPY_UNROLL):
            copy(j * ROW_COPY_UNROLL + k).start(priority=k % 2)
        return 0

    def wait(j, _):
        for k in range(ROW_COPY_UNROLL):
            copy(j * ROW_COPY_UNROLL + k).wait()
        return 0

    lax.fori_loop(0, rows // ROW_COPY_UNROLL, start, 0)
    lax.fori_loop(0, rows // ROW_COPY_UNROLL, wait, 0)


def _row_copy(pos, src, *, dst_init=None, dst_rows=None, name):
    T = pos.shape[0]
    rows = 512
    scatter = dst_init is not None
    hbm = pl.BlockSpec(memory_space=pl.ANY)
    tok = pl.BlockSpec((rows, src.shape[1]), lambda i: (i, 0))
    out_rows = dst_init.shape[0] if scatter else dst_rows
    return pl.pallas_call(
        functools.partial(_row_copy_body, rows=rows, scatter=scatter),
        grid=(T // rows,),
        in_specs=[pl.BlockSpec((rows,), lambda i: (i,), memory_space=pltpu.SMEM)]
        + ([tok, hbm] if scatter else [hbm]),
        out_specs=hbm if scatter else tok,
        out_shape=jax.ShapeDtypeStruct((out_rows, src.shape[1]), src.dtype),
        scratch_shapes=[pltpu.SemaphoreType.DMA(())],
        input_output_aliases={2: 0} if scatter else {},
        compiler_params=pltpu.CompilerParams(dimension_semantics=("arbitrary",)),
        name=name,
    )(*((pos, src, dst_init) if scatter else (pos, src)))


def _ffn_body(ea_ref, eb_ref, nt_ref, xs_ref, g_ref, fg_ref, wga, wua, wda, wgb, wub, wdb, o_ref):
    del ea_ref, eb_ref
    D = g_ref.shape[1]
    valid = pl.program_id(0) < nt_ref[0]

    @pl.when(valid)
    def _():
        x = xs_ref[:, :D]
        meta = xs_ref[:, D:]
        h = (_rms(x) * g_ref[...]).astype(BF16)
        y = x
        for wg, wu, wd, lane in ((wga, wua, wda, META_WA), (wgb, wub, wdb, META_WB)):
            a = jax.nn.silu(_dot(h, wg[...])) * _dot(h, wu[...]) * meta[:, lane:lane + 1]
            y = y + _dot(a.astype(BF16), wd[...])
        o_ref[...] = _rms(y) * fg_ref[...]

    @pl.when(jnp.logical_not(valid))
    def _():
        o_ref[...] = jnp.zeros_like(o_ref)


def _ffn(xs, D, tile_ea, tile_eb, n_tiles, moe_g, final_g, w_gate, w_up, w_down, tm):
    P = xs.shape[0]
    nt_max = P // tm
    De = w_gate.shape[2]
    wg = w_gate.astype(BF16)
    wu = w_up.astype(BF16)
    wd = w_down.astype(BF16)
    const = lambda i, ea, eb, nt: (0, 0)
    sel_a = lambda i, ea, eb, nt: (ea[i], 0, 0)
    sel_b = lambda i, ea, eb, nt: (eb[i], 0, 0)
    up = lambda sel: pl.BlockSpec((None, D, De), sel)
    down = lambda sel: pl.BlockSpec((None, De, D), sel)
    grid_spec = pltpu.PrefetchScalarGridSpec(
        num_scalar_prefetch=3,
        grid=(nt_max,),
        in_specs=[pl.BlockSpec((tm, D + LANES), lambda i, ea, eb, nt: (jnp.minimum(i, nt[0] - 1), 0)),
                  pl.BlockSpec((1, D), const), pl.BlockSpec((1, D), const),
                  up(sel_a), up(sel_a), down(sel_a), up(sel_b), up(sel_b), down(sel_b)],
        out_specs=pl.BlockSpec((tm, D), lambda i, ea, eb, nt: (i, 0)),
    )
    return pl.pallas_call(
        _ffn_body,
        grid_spec=grid_spec,
        out_shape=jax.ShapeDtypeStruct((P, D), F32),
        compiler_params=_cparams("arbitrary"),
        name="ffn",
    )(tile_ea, tile_eb, n_tiles, xs, moe_g.astype(F32)[None, :], final_g.astype(F32)[None, :],
      wg, wu, wd, wg, wu, wd)


def _moe(x2a, D, moe_g, grp_w, grp_b, exp_w, exp_b, w_gate, w_up, w_down, final_g):
    T = x2a.shape[0]
    tm = 256
    x2a, counts = _route(x2a, D, moe_g, grp_w, grp_b, exp_w, exp_b)
    cnt = counts[0, :N_CLASSES].astype(jnp.int32)
    ntile_c = (cnt + tm - 1) // tm
    tile_end = jnp.cumsum(ntile_c)
    off = (tile_end - ntile_c) * tm
    cls = x2a[:, D + META_CLS].astype(jnp.int32)
    rank = x2a[:, D + META_RANK].astype(jnp.int32)
    pos = off[cls] + rank
    nt_max = T // tm + N_CLASSES
    n_tiles = tile_end[-1:]
    tile_id = jnp.arange(nt_max, dtype=jnp.int32)
    tile_cls = jnp.minimum(jnp.sum((tile_end[None, :] <= tile_id[:, None]).astype(jnp.int32), axis=1),
                           N_CLASSES - 1)
    tile_cls = jnp.where(jnp.arange(nt_max) < n_tiles[0], tile_cls, tile_cls[jnp.maximum(n_tiles[0] - 1, 0)])
    pa = jnp.asarray([0, 0, 0, 1, 1, 2], jnp.int32)
    pb = jnp.asarray([1, 2, 3, 2, 3, 3], jnp.int32)
    grp = tile_cls // PAIRS_PER_GROUP
    tile_ea = grp * EXPERTS_PER_GROUP + pa[tile_cls % PAIRS_PER_GROUP]
    tile_eb = grp * EXPERTS_PER_GROUP + pb[tile_cls % PAIRS_PER_GROUP]
    xs = _row_copy(pos, x2a, dst_init=jnp.zeros((nt_max * tm, D + LANES), F32), name="dispatch")
    ys = _ffn(xs, D, tile_ea, tile_eb, n_tiles.astype(jnp.int32), moe_g, final_g, w_gate, w_up, w_down, tm)
    return _row_copy(pos, ys, dst_rows=T, name="unsort")


def kernel(x, mem, mix_norm_g, w_in, q_norm_g, k_norm_g, ssm_lambda_re, ssm_lambda_im, ssm_log_step, ssm_b_re, ssm_b_im, ssm_c_re, ssm_c_im, ssm_d, ssm_glu_w, ssm_glu_b, att_out_g, ssm_out_g, w_out, xa_norm_g, mem_norm_g, xa_wq, xa_wk, xa_wv, xa_wo, moe_norm_g, router_grp_w, router_grp_b, router_exp_w, router_exp_b, exp_w_gate, exp_w_up, exp_w_down, final_norm_g):
    B, S, D = x.shape
    assert mix_norm_g.shape[0] == 1, "single layer"
    x2d = x.reshape(B * S, D)
    q, kd, vd, u = _inproj(x2d, S, mix_norm_g[0], w_in[0], q_norm_g[0], k_norm_g[0])
    y_att = _attention(q, kd, vd, B, S)
    z = _s5(u, B, S, ssm_lambda_re[0], ssm_lambda_im[0], ssm_log_step[0], ssm_b_re[0], ssm_b_im[0],
            ssm_c_re[0], ssm_c_im[0], ssm_d[0])
    kmem, vmem = _memkv(mem, mem_norm_g[0], xa_wk[0], xa_wv[0])
    x2a = _mid(x2d, S, y_att, z, ssm_glu_w[0], ssm_glu_b[0], att_out_g[0], ssm_out_g[0], w_out[0],
               xa_norm_g[0], xa_wq[0], xa_wo[0], kmem, vmem)
    out = _moe(x2a, D, moe_norm_g[0], router_grp_w[0], router_grp_b[0], router_exp_w[0],
               router_exp_b[0], exp_w_gate[0], exp_w_up[0], exp_w_down[0], final_norm_g)
    return out.reshape(B, S, D)
```

```python
import functools

import numpy as np
import jax
import jax.numpy as jnp
from jax import lax
from jax.experimental import pallas as pl
from jax.experimental.pallas import tpu as pltpu

F32 = jnp.float32
BF16 = jnp.bfloat16

EPS = 1e-6
GRID_W = 64
ROPE_THETA = 10000.0
ATT_HEADS = 8
ATT_KV_HEADS = 2
HEAD_DIM = 64
D_ATT = ATT_HEADS * HEAD_DIM
D_KV = ATT_KV_HEADS * HEAD_DIM
SSM_GROUP = 16
SSM_STATE = 64
XA_HEADS = 4
N_EXPERT_GROUPS = 4
EXPERTS_PER_GROUP = 4
N_EXPERTS = N_EXPERT_GROUPS * EXPERTS_PER_GROUP
PAIRS_PER_GROUP = 6
N_CLASSES = N_EXPERT_GROUPS * PAIRS_PER_GROUP

LANES = 128
SUBLANES = 8
CHUNK = 16
VMEM_LIMIT = 56 * 1024 * 1024

META_CLS, META_RANK, META_WA, META_WB = 0, 1, 2, 3


def _cparams(*sem):
    return pltpu.CompilerParams(dimension_semantics=sem, vmem_limit_bytes=VMEM_LIMIT)


def _rms(x):
    return x * lax.rsqrt(jnp.mean(x * x, axis=-1, keepdims=True) + EPS)


def _dot(a, b):
    return jnp.dot(a, b, preferred_element_type=F32)


def _dot_nt(a, b):
    return lax.dot_general(a, b, (((1,), (1,)), ((), ())), preferred_element_type=F32)


def _inproj_body(x_ref, g_ref, w_ref, qkg_ref, ones_ref, cos_ref, sin_ref,
                 q_ref, k_ref, v_ref, u_ref):
    n_qk = q_ref.shape[1] + k_ref.shape[1]
    h = _rms(x_ref[...]) * g_ref[...]
    proj = _dot(h.astype(BF16), w_ref[...])
    qk = proj[:, :n_qk]
    ssq = _dot((qk * qk).astype(BF16), ones_ref[...])
    qk = qk * lax.rsqrt(ssq * (1.0 / HEAD_DIM) + EPS) * qkg_ref[...]
    cos = cos_ref[...]
    sin = sin_ref[...]
    lane = lax.broadcasted_iota(jnp.int32, cos.shape, 1)
    first = (lane % 32) < 16
    nq = q_ref.shape[1] // LANES
    for c in range(n_qk // LANES):
        xc = qk[:, c * LANES:(c + 1) * LANES]
        partner = jnp.where(first, pltpu.roll(xc, LANES - 16, 1), pltpu.roll(xc, 16, 1))
        rot = xc * cos + partner * sin
        if c < nq:
            q_ref[:, c * LANES:(c + 1) * LANES] = (rot * (HEAD_DIM ** -0.5)).astype(BF16)
        else:
            k_ref[:, (c - nq) * LANES:(c - nq + 1) * LANES] = rot.astype(BF16)
    nv = n_qk + v_ref.shape[1]
    v_ref[...] = proj[:, n_qk:nv].astype(BF16)
    for j in range(u_ref.shape[0]):
        u_ref[j] = proj[:, nv + j * LANES:nv + (j + 1) * LANES].astype(BF16)


def _rope_tables(S):
    t = jnp.arange(S, dtype=jnp.int32)
    freqs = ROPE_THETA ** (-jnp.arange(16, dtype=F32) / 16)
    halves = []
    for pos in (t // GRID_W, t % GRID_W):
        ang = pos.astype(F32)[:, None] * freqs[None, :]
        halves.append((jnp.cos(ang), jnp.sin(ang)))
    cos_h = jnp.concatenate([c for c, _ in halves for _ in range(2)], axis=1)
    sin_h = jnp.concatenate([sg * s for _, s in halves for sg in (-1.0, 1.0)], axis=1)
    return jnp.concatenate([cos_h, cos_h], axis=1), jnp.concatenate([sin_h, sin_h], axis=1)


def _inproj(x2d, S, mix_g, w_in, q_g, k_g):
    T, D = x2d.shape
    tm = 512
    d_ssm = w_in.shape[1] - D_ATT - 2 * D_KV
    wq = w_in[:, :D_ATT]
    wk = w_in[:, D_ATT:D_ATT + D_KV]
    wv = w_in[:, D_ATT + D_KV:D_ATT + 2 * D_KV]
    wu = w_in[:, D_ATT + 2 * D_KV:]
    dup = lambda w: jnp.concatenate([w[:, :64], w[:, :64], w[:, 64:], w[:, 64:]], axis=1)
    w_ext = jnp.concatenate([wq, dup(wk), dup(wv), wu], axis=1).astype(BF16)
    n_qk = D_ATT + 2 * D_KV
    qkg = jnp.concatenate([jnp.tile(q_g, ATT_HEADS), jnp.tile(k_g, 2 * ATT_KV_HEADS)])[None, :]
    hid = np.arange(n_qk) // HEAD_DIM
    ones = jnp.asarray((hid[:, None] == hid[None, :]).astype(np.float32), dtype=BF16)
    cos_t, sin_t = _rope_tables(S)
    sb = S // tm
    const = lambda i: (0, 0)
    return pl.pallas_call(
        _inproj_body,
        grid=(T // tm,),
        in_specs=[
            pl.BlockSpec((tm, D), lambda i: (i, 0)),
            pl.BlockSpec((1, D), const),
            pl.BlockSpec(w_ext.shape, const),
            pl.BlockSpec((1, n_qk), const),
            pl.BlockSpec((n_qk, n_qk), const),
            pl.BlockSpec((tm, LANES), lambda i: (i % sb, 0)),
            pl.BlockSpec((tm, LANES), lambda i: (i % sb, 0)),
        ],
        out_specs=[
            pl.BlockSpec((tm, D_ATT), lambda i: (i, 0)),
            pl.BlockSpec((tm, 2 * D_KV), lambda i: (i, 0)),
            pl.BlockSpec((tm, 2 * D_KV), lambda i: (i, 0)),
            pl.BlockSpec((d_ssm // LANES, tm, LANES), lambda i: (0, i, 0)),
        ],
        out_shape=[
            jax.ShapeDtypeStruct((T, D_ATT), BF16),
            jax.ShapeDtypeStruct((T, 2 * D_KV), BF16),
            jax.ShapeDtypeStruct((T, 2 * D_KV), BF16),
            jax.ShapeDtypeStruct((d_ssm // LANES, T, LANES), BF16),
        ],
        compiler_params=_cparams("arbitrary"),
        name="inproj",
    )(x2d, mix_g[None, :], w_ext, qkg, ones, cos_t, sin_t)


def _attn_body(q_ref, k_ref, v_ref, o_ref, klo, khi, vlo, vhi):
    @pl.when(pl.program_id(2) == 0)
    def _():
        lo = lax.broadcasted_iota(jnp.int32, k_ref.shape, 1) < HEAD_DIM
        kd = k_ref[...]
        vd = v_ref[...]
        zero = jnp.zeros_like(kd)
        klo[...] = jnp.where(lo, kd, zero)
        khi[...] = jnp.where(lo, zero, kd)
        vlo[...] = jnp.where(lo, vd, zero)
        vhi[...] = jnp.where(lo, zero, vd)

    for p in range(2):
        qp = q_ref[:, p * LANES:(p + 1) * LANES]
        acc = None
        for km, vm in ((klo, vlo), (khi, vhi)):
            s = _dot_nt(qp, km[...])
            e = jnp.exp(s - jnp.max(s, axis=-1, keepdims=True))
            inv = 1.0 / jnp.sum(e, axis=-1, keepdims=True)
            o = _dot(e.astype(BF16), vm[...]) * inv
            acc = o if acc is None else acc + o
        o_ref[:, p * LANES:(p + 1) * LANES] = acc.astype(BF16)


def _attention(q, kd, vd, B, S):
    tq = 256
    q3 = q.reshape(B, S, D_ATT)
    k3 = kd.reshape(B, S, 2 * D_KV)
    v3 = vd.reshape(B, S, 2 * D_KV)
    gw = D_ATT // ATT_KV_HEADS
    out = pl.pallas_call(
        _attn_body,
        grid=(B, ATT_KV_HEADS, S // tq),
        in_specs=[
            pl.BlockSpec((None, tq, gw), lambda b, j, i: (b, i, j)),
            pl.BlockSpec((None, S, LANES), lambda b, j, i: (b, 0, j)),
            pl.BlockSpec((None, S, LANES), lambda b, j, i: (b, 0, j)),
        ],
        out_specs=pl.BlockSpec((None, tq, gw), lambda b, j, i: (b, i, j)),
        out_shape=jax.ShapeDtypeStruct((B, S, D_ATT), BF16),
        scratch_shapes=[pltpu.VMEM((S, LANES), BF16)] * 4,
        compiler_params=_cparams("arbitrary", "arbitrary", "arbitrary"),
        name="attn",
    )(q3, k3, v3)
    return out.reshape(B * S, D_ATT)


def _s5_operators(lam_re, lam_im, log_step, b_re, b_im, c_re, c_im):
    hp = lax.Precision.HIGHEST
    lr = jnp.minimum(lam_re.astype(F32), -1e-4)
    li = lam_im.astype(F32)
    dt = jnp.exp(log_step.astype(F32))[..., None]
    mag = jnp.exp(lr * dt)
    ab_r = mag * jnp.cos(li * dt)
    ab_i = mag * jnp.sin(li * dt)
    den = lr * lr + li * li
    nr = ab_r - 1.0
    f_r = (nr * lr + ab_i * li) / den
    f_i = (ab_i * lr - nr * li) / den
    br = b_re.astype(F32)
    bi = b_im.astype(F32)
    bb_r = f_r[..., None] * br - f_i[..., None] * bi
    bb_i = f_r[..., None] * bi + f_i[..., None] * br
    pr = [jnp.ones_like(ab_r)]
    pi = [jnp.zeros_like(ab_i)]
    for _ in range(CHUNK):
        pr.append(pr[-1] * ab_r - pi[-1] * ab_i)
        pi.append(pr[-2] * ab_i + pi[-1] * ab_r)
    pw_r = jnp.stack(pr, axis=2)
    pw_i = jnp.stack(pi, axis=2)
    AB_r = pw_r[..., None] * bb_r[:, :, None] - pw_i[..., None] * bb_i[:, :, None]
    AB_i = pw_r[..., None] * bb_i[:, :, None] + pw_i[..., None] * bb_r[:, :, None]
    cr = c_re.astype(F32)
    ci = c_im.astype(F32)
    K = (jnp.einsum('dgcp,dgnpi->dgnci', cr, AB_r, precision=hp)
         - jnp.einsum('dgcp,dgnpi->dgnci', ci, AB_i, precision=hp))
    CA_r = cr[:, :, None] * pw_r[:, :, :, None] - ci[:, :, None] * pw_i[:, :, :, None]
    CA_i = cr[:, :, None] * pw_i[:, :, :, None] + ci[:, :, None] * pw_r[:, :, :, None]
    G = lr.shape[1]
    C = SSM_GROUP
    L = CHUNK
    Kcat = jnp.concatenate([K[1][:, 1:L][:, ::-1], (K[0][:, 0] + K[1][:, 0])[:, None], K[0][:, 1:L]],
                           axis=1)
    jj, tt = np.meshgrid(np.arange(L), np.arange(L), indexing='ij')
    M = Kcat[:, tt - jj + (L - 1)]
    M = M.transpose(0, 1, 4, 2, 3).reshape(G, L * C, L * C)
    rev = np.arange(L - 1, -1, -1)
    ef_r = AB_r[0][:, rev].transpose(0, 1, 3, 2).reshape(G, L * C, SSM_STATE)
    ef_i = AB_i[0][:, rev].transpose(0, 1, 3, 2).reshape(G, L * C, SSM_STATE)
    eb_r = AB_r[1][:, :L].transpose(0, 1, 3, 2).reshape(G, L * C, SSM_STATE)
    eb_i = AB_i[1][:, :L].transpose(0, 1, 3, 2).reshape(G, L * C, SSM_STATE)
    E = jnp.concatenate([ef_r, eb_r, ef_i, eb_i], axis=2)
    W1 = jnp.concatenate([M, E], axis=2).astype(BF16)
    df_r = CA_r[0][:, 1:L + 1].transpose(0, 3, 1, 2).reshape(G, SSM_STATE, L * C)
    df_i = CA_i[0][:, 1:L + 1].transpose(0, 3, 1, 2).reshape(G, SSM_STATE, L * C)
    db_r = CA_r[1][:, 1:L + 1][:, ::-1].transpose(0, 3, 1, 2).reshape(G, SSM_STATE, L * C)
    db_i = CA_i[1][:, 1:L + 1][:, ::-1].transpose(0, 3, 1, 2).reshape(G, SSM_STATE, L * C)
    zero = jnp.zeros_like(df_r)
    DA = jnp.concatenate([df_r, zero, -df_i, zero], axis=1).astype(BF16)
    DB = jnp.concatenate([zero, db_r, zero, -db_i], axis=1).astype(BF16)
    a16 = jnp.stack([jnp.concatenate([pw_r[0][:, L], pw_r[1][:, L]], axis=1),
                     jnp.concatenate([pw_i[0][:, L], pw_i[1][:, L]], axis=1)], axis=1)
    return W1, DA, DB, a16


GROUPS_PER_SET = LANES // SSM_GROUP
S5_ROW_CHUNK = 256


def _s5_body(u_ref, pin_ref, w1_ref, da_ref, db_ref, a_ref, d_ref, z_ref,
             uperm, zperm, y_s, xr_s, xi_s, har_s, hai_s, hbr_s, hbi_s, *, nb):
    R = u_ref.shape[0]
    nchunk = R // nb
    W = uperm.shape[2]
    half = W // 2
    g = pl.program_id(1)

    @pl.when(g == 0)
    def _():
        for r0 in range(0, R, S5_ROW_CHUNK):
            up = _dot(u_ref[r0:r0 + S5_ROW_CHUNK, :], pin_ref[...]).astype(BF16)
            for k in range(GROUPS_PER_SET):
                uperm[k, r0:r0 + S5_ROW_CHUNK, :] = up[:, k * W:(k + 1) * W]

    u = uperm[g]
    xy = _dot(u, w1_ref[...])
    y_s[...] = xy[:, :W]
    xr_s[...] = xy[:, W:W + half]
    xi_s[...] = xy[:, W + half:]
    ar = jnp.broadcast_to(a_ref[0:1, :], (nb, half))
    ai = jnp.broadcast_to(a_ref[1:2, :], (nb, half))
    fwd = lax.broadcasted_iota(jnp.int32, (nb, half), 1) < SSM_STATE

    def step(c, carry):
        sr, si = carry
        rf = pl.ds(c, nb, stride=nchunk)
        rb = pl.ds(nchunk - 1 - c, nb, stride=nchunk)
        har_s[rf, :] = sr
        hai_s[rf, :] = si
        hbr_s[rb, :] = sr
        hbi_s[rb, :] = si
        xr = jnp.where(fwd, xr_s[rf, :], xr_s[rb, :])
        xi = jnp.where(fwd, xi_s[rf, :], xi_s[rb, :])
        return sr * ar - si * ai + xr, sr * ai + si * ar + xi

    zero = jnp.zeros((nb, half), F32)
    lax.fori_loop(0, nchunk, step, (zero, zero))
    ha = jnp.concatenate([har_s[...], hai_s[...]], axis=1).astype(BF16)
    hb = jnp.concatenate([hbr_s[...], hbi_s[...]], axis=1).astype(BF16)
    y = y_s[...] + _dot(ha, da_ref[...]) + _dot(hb, db_ref[...]) + d_ref[...] * u.astype(F32)
    zperm[g] = jax.nn.gelu(y).astype(BF16)

    @pl.when(g == GROUPS_PER_SET - 1)
    def _():
        for r0 in range(0, R, S5_ROW_CHUNK):
            zc = jnp.concatenate([zperm[k, r0:r0 + S5_ROW_CHUNK, :] for k in range(GROUPS_PER_SET)], axis=1)
            z_ref[r0:r0 + S5_ROW_CHUNK, :] = _dot_nt(zc, pin_ref[...]).astype(BF16)


def _s5(u_set, B, S, lam_re, lam_im, log_step, b_re, b_im, c_re, c_im, ssm_d):
    n_set, T, _ = u_set.shape
    G = n_set * GROUPS_PER_SET
    nchunk = S // CHUNK
    R = nchunk * B
    W = CHUNK * SSM_GROUP
    WS = CHUNK * LANES
    W1, DA, DB, a16 = _s5_operators(lam_re, lam_im, log_step, b_re, b_im, c_re, c_im)
    dvec = jnp.tile(ssm_d.astype(F32).reshape(G, 1, SSM_GROUP), (1, CHUNK, 1)).reshape(G, 1, W)
    lane = np.arange(WS)
    step, grp, ch = lane // LANES, (lane % LANES) // SSM_GROUP, lane % SSM_GROUP
    pin = jnp.asarray(((grp * CHUNK + step) * SSM_GROUP + ch)[:, None] == lane[None, :], dtype=BF16)
    per_set = pl.BlockSpec((None, R, WS), lambda s, g: (s, 0, 0))
    blk = lambda *sh: pl.BlockSpec((None,) + sh, lambda s, g: (s * GROUPS_PER_SET + g,) + (0,) * len(sh))
    z = pl.pallas_call(
        functools.partial(_s5_body, nb=B),
        grid=(n_set, GROUPS_PER_SET),
        in_specs=[per_set, pl.BlockSpec((WS, WS), lambda s, g: (0, 0)),
                  blk(W, 2 * W), blk(W, W), blk(W, W), blk(2, W // 2), blk(1, W)],
        out_specs=per_set,
        out_shape=jax.ShapeDtypeStruct((n_set, R, WS), BF16),
        scratch_shapes=[pltpu.VMEM((GROUPS_PER_SET, R, W), BF16)] * 2 + [pltpu.VMEM((R, W), F32)]
        + [pltpu.VMEM((R, W // 2), F32)] * 6,
        compiler_params=_cparams("arbitrary", "arbitrary"),
        name="s5",
    )(u_set.reshape(n_set, R, WS), pin, W1, DA, DB, a16, dvec)
    return z.reshape(n_set, T, LANES)


def _memkv_body(m_ref, g_ref, wk_ref, wv_ref, k_ref, v_ref):
    h = (_rms(m_ref[...]) * g_ref[...]).astype(BF16)
    k_ref[...] = _dot(h, wk_ref[...]).astype(BF16)
    v_ref[...] = _dot(h, wv_ref[...]).astype(BF16)


def _memkv(mem, g, wk, wv):
    B, M, D = mem.shape
    const = lambda b: (0, 0)
    row = pl.BlockSpec((None, M, D), lambda b: (b, 0, 0))
    return pl.pallas_call(
        _memkv_body,
        grid=(B,),
        in_specs=[row, pl.BlockSpec((1, D), const), pl.BlockSpec((D, D), const),
                  pl.BlockSpec((D, D), const)],
        out_specs=[row, row],
        out_shape=[jax.ShapeDtypeStruct((B, M, D), BF16)] * 2,
        compiler_params=_cparams("arbitrary"),
        name="memkv",
    )(mem, g[None, :], wk.astype(BF16), wv.astype(BF16))


def _mid_body(x_ref, ya_ref, z_ref, gw_ref, gb_ref, ag_ref, sg_ref, wo_ref, xg_ref,
              wq_ref, xwo_ref, k_ref, v_ref, o_ref):
    D = x_ref.shape[1]
    z = jnp.concatenate([z_ref[j] for j in range(z_ref.shape[0])], axis=1)
    s = z.astype(F32) * jax.nn.sigmoid(_dot(z, gw_ref[...]) + gb_ref[...])
    ya = _rms(ya_ref[...].astype(F32)) * ag_ref[...]
    ys = _rms(s) * sg_ref[...]
    y = jnp.concatenate([ya, ys], axis=1).astype(BF16)
    x1 = x_ref[...] + _dot(y, wo_ref[...])
    hq = (_rms(x1) * xg_ref[...]).astype(BF16)
    dh = D // XA_HEADS
    q = (_dot(hq, wq_ref[...]) * (dh ** -0.5)).astype(BF16)
    outs = []
    for h in range(XA_HEADS):
        sl = slice(h * dh, (h + 1) * dh)
        sc = _dot_nt(q[:, sl], k_ref[:, sl])
        e = jnp.exp(sc - jnp.max(sc, axis=-1, keepdims=True))
        inv = 1.0 / jnp.sum(e, axis=-1, keepdims=True)
        outs.append((_dot(e.astype(BF16), v_ref[:, sl]) * inv).astype(BF16))
    o = jnp.concatenate(outs, axis=1)
    o_ref[:, :D] = x1 + _dot(o, xwo_ref[...])
    o_ref[:, D:] = jnp.zeros((o_ref.shape[0], o_ref.shape[1] - D), F32)


def _mid(x2d, S, y_att, z, glu_w, glu_b, att_g, ssm_g, w_out, xa_g, wq, wo, kmem, vmem):
    T, D = x2d.shape
    M = kmem.shape[1]
    tm = 256
    sb = S // tm
    const = lambda i: (0, 0)
    tok = lambda w: pl.BlockSpec((tm, w), lambda i: (i, 0))
    full = lambda a: pl.BlockSpec(a.shape, const)
    vec = lambda a: a.astype(F32)[None, :]
    mem = pl.BlockSpec((None, M, D), lambda i: (i // sb, 0, 0))
    args = [x2d, y_att, z, glu_w.astype(BF16), vec(glu_b), vec(att_g), vec(ssm_g), w_out.astype(BF16),
            vec(xa_g), wq.astype(BF16), wo.astype(BF16), kmem, vmem]
    zspec = pl.BlockSpec((z.shape[0], tm, LANES), lambda i: (0, i, 0))
    specs = [tok(D), tok(y_att.shape[1]), zspec] + [full(a) for a in args[3:11]] + [mem, mem]
    return pl.pallas_call(
        _mid_body,
        grid=(T // tm,),
        in_specs=specs,
        out_specs=pl.BlockSpec((tm, D + LANES), lambda i: (i, 0)),
        out_shape=jax.ShapeDtypeStruct((T, D + LANES), F32),
        compiler_params=_cparams("arbitrary"),
        name="mid",
    )(*args)


def _split_bf16(a):
    hi = a.astype(BF16)
    return hi, (a - hi.astype(F32)).astype(BF16)


def _route_body(x_ref, g_ref, whi_ref, wlo_ref, b_ref, meta_ref, cnt_ref, carry):
    i = pl.program_id(0)
    tm, D = x_ref.shape[0], g_ref.shape[1]

    @pl.when(i == 0)
    def _():
        carry[...] = jnp.zeros_like(carry)

    h = _rms(x_ref[:, :D]) * g_ref[...]
    hhi, hlo = _split_bf16(h)
    logits = (_dot(hhi, whi_ref[...]) + _dot(hlo, whi_ref[...]) + _dot(hhi, wlo_ref[...])
              + b_ref[...])
    lane = lax.broadcasted_iota(jnp.int32, logits.shape, 1)
    ninf = jnp.float32(-jnp.inf)
    big = jnp.int32(LANES)
    rmax = lambda v: jnp.max(v, axis=-1, keepdims=True)
    rmin = lambda v: jnp.min(v, axis=-1, keepdims=True)
    is_grp = lane < N_EXPERT_GROUPS
    gl = jnp.where(is_grp, logits, ninf)
    gmax = rmax(gl)
    g_sel = rmin(jnp.where(gl == gmax, lane, big))
    g_w = 1.0 / jnp.sum(jnp.where(is_grp, jnp.exp(gl - gmax), 0.0), axis=-1, keepdims=True)
    e_id = lane - N_EXPERT_GROUPS
    in_grp = (e_id >= 0) & (e_id < N_EXPERTS) & ((e_id >> 2) == g_sel)
    el = jnp.where(in_grp, logits, ninf)
    v1 = rmax(el)
    i1 = rmin(jnp.where(el == v1, lane, big))
    el2 = jnp.where(lane == i1, ninf, el)
    v2 = rmax(el2)
    i2 = rmin(jnp.where(el2 == v2, lane, big))
    t = jnp.exp(v2 - v1)
    w1 = g_w / (1.0 + t)
    w2 = g_w * t / (1.0 + t)
    first_low = i1 < i2
    ia = (jnp.minimum(i1, i2) - N_EXPERT_GROUPS) & 3
    ib = (jnp.maximum(i1, i2) - N_EXPERT_GROUPS) & 3
    wa = jnp.where(first_low, w1, w2)
    wb = jnp.where(first_low, w2, w1)
    cls = g_sel * PAIRS_PER_GROUP + ((ia * (7 - ia)) >> 1) + ib - ia - 1
    onehot = (lane == cls).astype(F32)
    r = lax.broadcasted_iota(jnp.int32, (tm, tm), 0)
    c = lax.broadcasted_iota(jnp.int32, (tm, tm), 1)
    before = _dot((c < r).astype(BF16), onehot.astype(BF16))
    rank = jnp.sum(onehot * (before + carry[...]), axis=-1, keepdims=True)
    carry[...] = carry[...] + jnp.sum(onehot, axis=0, keepdims=True)
    cnt_ref[...] = carry[...]
    meta = jnp.where(lane == META_CLS, cls.astype(F32),
                     jnp.where(lane == META_RANK, rank,
                               jnp.where(lane == META_WA, wa,
                                         jnp.where(lane == META_WB, wb, 0.0))))
    meta_ref[...] = meta


def _route(x2a, D, moe_g, grp_w, grp_b, exp_w, exp_b):
    T = x2a.shape[0]
    tm = 256
    pad = LANES - N_EXPERT_GROUPS - N_EXPERTS
    w = jnp.concatenate([grp_w, exp_w, jnp.zeros((D, pad), F32)], axis=1).astype(F32)
    b = jnp.concatenate([grp_b, exp_b, jnp.zeros((pad,), F32)]).astype(F32)[None, :]
    whi, wlo = _split_bf16(w)
    const = lambda i: (0, 0)
    ncol = D // LANES
    return pl.pallas_call(
        _route_body,
        grid=(T // tm,),
        in_specs=[pl.BlockSpec((tm, D + LANES), lambda i: (i, 0)),
                  pl.BlockSpec((1, D), const), pl.BlockSpec((D, LANES), const),
                  pl.BlockSpec((D, LANES), const), pl.BlockSpec((1, LANES), const)],
        out_specs=[pl.BlockSpec((tm, LANES), lambda i: (i, ncol)),
                   pl.BlockSpec((1, LANES), const)],
        out_shape=[jax.ShapeDtypeStruct(x2a.shape, F32), jax.ShapeDtypeStruct((1, LANES), F32)],
        scratch_shapes=[pltpu.VMEM((1, LANES), F32)],
        input_output_aliases={0: 0},
        compiler_params=_cparams("arbitrary"),
        name="route",
    )(x2a, moe_g.astype(F32)[None, :], whi, wlo, b)


ROW_COPY_UNROLL = 8


def _row_copy_body(pos_ref, src_ref, *rest, rows, scatter):
    dst_ref, sem = rest[-2:]

    def copy(r):
        p = pos_ref[r]
        if scatter:
            return pltpu.make_async_copy(src_ref.at[pl.ds(r, 1)], dst_ref.at[pl.ds(p, 1)], sem)
        return pltpu.make_async_copy(src_ref.at[pl.ds(p, 1)], dst_ref.at[pl.ds(r, 1)], sem)

    def start(j, _):
        for k in range(ROW_COPY_UNROLL):
            copy(j * ROW_COPY_UNROLL + k).start(priority=k % 2)
        return 0

    def wait(j, _):
        for k in range(ROW_COPY_UNROLL):
            copy(j * ROW_COPY_UNROLL + k).wait()
        return 0

    lax.fori_loop(0, rows // ROW_COPY_UNROLL, start, 0)
    lax.fori_loop(0, rows // ROW_COPY_UNROLL, wait, 0)


def _row_copy(pos, src, *, dst_init=None, dst_rows=None, name):
    T = pos.shape[0]
    rows = 512
    scatter = dst_init is not None
    hbm = pl.BlockSpec(memory_space=pl.ANY)
    tok = pl.BlockSpec((rows, src.shape[1]), lambda i: (i, 0))
    out_rows = dst_init.shape[0] if scatter else dst_rows
    return pl.pallas_call(
        functools.partial(_row_copy_body, rows=rows, scatter=scatter),
        grid=(T // rows,),
        in_specs=[pl.BlockSpec((rows,), lambda i: (i,), memory_space=pltpu.SMEM)]
        + ([tok, hbm] if scatter else [hbm]),
        out_specs=hbm if scatter else tok,
        out_shape=jax.ShapeDtypeStruct((out_rows, src.shape[1]), src.dtype),
        scratch_shapes=[pltpu.SemaphoreType.DMA(())],
        input_output_aliases={2: 0} if scatter else {},
        compiler_params=pltpu.CompilerParams(dimension_semantics=("arbitrary",)),
        name=name,
    )(*((pos, src, dst_init) if scatter else (pos, src)))


def _ffn_body(ea_ref, eb_ref, nt_ref, xs_ref, g_ref, fg_ref, wga, wua, wda, wgb, wub, wdb, o_ref):
    del ea_ref, eb_ref
    D = g_ref.shape[1]
    valid = pl.program_id(0) < nt_ref[0]

    @pl.when(valid)
    def _():
        x = xs_ref[:, :D]
        meta = xs_ref[:, D:]
        h = (_rms(x) * g_ref[...]).astype(BF16)
        y = x
        for wg, wu, wd, lane in ((wga, wua, wda, META_WA), (wgb, wub, wdb, META_WB)):
            a = jax.nn.silu(_dot(h, wg[...])) * _dot(h, wu[...]) * meta[:, lane:lane + 1]
            y = y + _dot(a.astype(BF16), wd[...])
        o_ref[...] = _rms(y) * fg_ref[...]

    @pl.when(jnp.logical_not(valid))
    def _():
        o_ref[...] = jnp.zeros_like(o_ref)


def _ffn(xs, D, tile_ea, tile_eb, n_tiles, moe_g, final_g, w_gate, w_up, w_down, tm):
    P = xs.shape[0]
    nt_max = P // tm
    De = w_gate.shape[2]
    wg = w_gate.astype(BF16)
    wu = w_up.astype(BF16)
    wd = w_down.astype(BF16)
    const = lambda i, ea, eb, nt: (0, 0)
    sel_a = lambda i, ea, eb, nt: (ea[i], 0, 0)
    sel_b = lambda i, ea, eb, nt: (eb[i], 0, 0)
    up = lambda sel: pl.BlockSpec((None, D, De), sel)
    down = lambda sel: pl.BlockSpec((None, De, D), sel)
    grid_spec = pltpu.PrefetchScalarGridSpec(
        num_scalar_prefetch=3,
        grid=(nt_max,),
        in_specs=[pl.BlockSpec((tm, D + LANES), lambda i, ea, eb, nt: (jnp.maximum(jnp.minimum(i, nt[0] - 1), 0), 0)),
                  pl.BlockSpec((1, D), const), pl.BlockSpec((1, D), const),
                  up(sel_a), up(sel_a), down(sel_a), up(sel_b), up(sel_b), down(sel_b)],
        out_specs=pl.BlockSpec((tm, D), lambda i, ea, eb, nt: (i, 0)),
    )
    return pl.pallas_call(
        _ffn_body,
        grid_spec=grid_spec,
        out_shape=jax.ShapeDtypeStruct((P, D), F32),
        compiler_params=_cparams("arbitrary"),
        name="ffn",
    )(tile_ea, tile_eb, n_tiles, xs, moe_g.astype(F32)[None, :], final_g.astype(F32)[None, :],
      wg, wu, wd, wg, wu, wd)


def _moe(x2a, D, moe_g, grp_w, grp_b, exp_w, exp_b, w_gate, w_up, w_down, final_g):
    T = x2a.shape[0]
    tm = 256
    x2a, counts = _route(x2a, D, moe_g, grp_w, grp_b, exp_w, exp_b)
    cnt = counts[0, :N_CLASSES].astype(jnp.int32)
    ntile_c = (cnt + tm - 1) // tm
    tile_end = jnp.cumsum(ntile_c)
    off = (tile_end - ntile_c) * tm
    cls = x2a[:, D + META_CLS].astype(jnp.int32)
    rank = x2a[:, D + META_RANK].astype(jnp.int32)
    pos = off[cls] + rank
    nt_max = T // tm + N_CLASSES
    n_tiles = tile_end[-1:]
    tile_id = jnp.arange(nt_max, dtype=jnp.int32)
    tile_cls = jnp.minimum(jnp.sum((tile_end[None, :] <= tile_id[:, None]).astype(jnp.int32), axis=1),
                           N_CLASSES - 1)
    tile_cls = jnp.where(jnp.arange(nt_max) < n_tiles[0], tile_cls, tile_cls[jnp.maximum(n_tiles[0] - 1, 0)])
    pa = jnp.asarray([0, 0, 0, 1, 1, 2], jnp.int32)
    pb = jnp.asarray([1, 2, 3, 2, 3, 3], jnp.int32)
    grp = tile_cls // PAIRS_PER_GROUP
    tile_ea = grp * EXPERTS_PER_GROUP + pa[tile_cls % PAIRS_PER_GROUP]
    tile_eb = grp * EXPERTS_PER_GROUP + pb[tile_cls % PAIRS_PER_GROUP]
    xs = _row_copy(pos, x2a, dst_init=jnp.zeros((nt_max * tm, D + LANES), F32), name="dispatch")
    ys = _ffn(xs, D, tile_ea, tile_eb, n_tiles.astype(jnp.int32), moe_g, final_g, w_gate, w_up, w_down, tm)
    return _row_copy(pos, ys, dst_rows=T, name="unsort")


def kernel(x, mem, mix_norm_g, w_in, q_norm_g, k_norm_g, ssm_lambda_re, ssm_lambda_im, ssm_log_step, ssm_b_re, ssm_b_im, ssm_c_re, ssm_c_im, ssm_d, ssm_glu_w, ssm_glu_b, att_out_g, ssm_out_g, w_out, xa_norm_g, mem_norm_g, xa_wq, xa_wk, xa_wv, xa_wo, moe_norm_g, router_grp_w, router_grp_b, router_exp_w, router_exp_b, exp_w_gate, exp_w_up, exp_w_down, final_norm_g):
    B, S, D = x.shape
    assert mix_norm_g.shape[0] == 1, "single layer"
    x2d = x.reshape(B * S, D)
    q, kd, vd, u = _inproj(x2d, S, mix_norm_g[0], w_in[0], q_norm_g[0], k_norm_g[0])
    y_att = _attention(q, kd, vd, B, S)
    z = _s5(u, B, S, ssm_lambda_re[0], ssm_lambda_im[0], ssm_log_step[0], ssm_b_re[0], ssm_b_im[0],
            ssm_c_re[0], ssm_c_im[0], ssm_d[0])
    kmem, vmem = _memkv(mem, mem_norm_g[0], xa_wk[0], xa_wv[0])
    x2a = _mid(x2d, S, y_att, z, ssm_glu_w[0], ssm_glu_b[0], att_out_g[0], ssm_out_g[0], w_out[0],
               xa_norm_g[0], xa_wq[0], xa_wo[0], kmem, vmem)
    out = _moe(x2a, D, moe_norm_g[0], router_grp_w[0], router_grp_b[0], router_exp_w[0],
               router_exp_b[0], exp_w_gate[0], exp_w_up[0], exp_w_down[0], final_norm_g)
    return out.reshape(B, S, D)
```

```python
import functools

import numpy as np
import jax
import jax.numpy as jnp
from jax import lax
from jax.experimental import pallas as pl
from jax.experimental.pallas import tpu as pltpu

F32 = jnp.float32
BF16 = jnp.bfloat16

EPS = 1e-6
GRID_W = 64
ROPE_THETA = 10000.0
ATT_HEADS = 8
ATT_KV_HEADS = 2
HEAD_DIM = 64
D_ATT = ATT_HEADS * HEAD_DIM
D_KV = ATT_KV_HEADS * HEAD_DIM
SSM_GROUP = 16
SSM_STATE = 64
XA_HEADS = 4
N_EXPERT_GROUPS = 4
EXPERTS_PER_GROUP = 4
N_EXPERTS = N_EXPERT_GROUPS * EXPERTS_PER_GROUP
PAIRS_PER_GROUP = 6
N_CLASSES = N_EXPERT_GROUPS * PAIRS_PER_GROUP

LANES = 128
SUBLANES = 8
CHUNK = 16
VMEM_LIMIT = 56 * 1024 * 1024

META_CLS, META_RANK, META_WA, META_WB = 0, 1, 2, 3


def _cparams(*sem):
    return pltpu.CompilerParams(dimension_semantics=sem, vmem_limit_bytes=VMEM_LIMIT)


def _rms(x):
    return x * lax.rsqrt(jnp.mean(x * x, axis=-1, keepdims=True) + EPS)


def _dot(a, b):
    return jnp.dot(a, b, preferred_element_type=F32)


def _dot_nt(a, b):
    return lax.dot_general(a, b, (((1,), (1,)), ((), ())), preferred_element_type=F32)


def _inproj_body(x_ref, g_ref, w_ref, qkg_ref, ones_ref, cos_ref, sin_ref,
                 q_ref, k_ref, v_ref, u_ref, us):
    n_qk = q_ref.shape[1] + k_ref.shape[1]
    h = _rms(x_ref[...]) * g_ref[...]
    proj = _dot(h.astype(BF16), w_ref[...])
    qk = proj[:, :n_qk]
    ssq = _dot((qk * qk).astype(BF16), ones_ref[...])
    qk = qk * lax.rsqrt(ssq * (1.0 / HEAD_DIM) + EPS) * qkg_ref[...]
    cos = cos_ref[...]
    sin = sin_ref[...]
    lane = lax.broadcasted_iota(jnp.int32, cos.shape, 1)
    first = (lane % 32) < 16
    nq = q_ref.shape[1] // LANES
    for c in range(n_qk // LANES):
        xc = qk[:, c * LANES:(c + 1) * LANES]
        partner = jnp.where(first, pltpu.roll(xc, LANES - 16, 1), pltpu.roll(xc, 16, 1))
        rot = xc * cos + partner * sin
        if c < nq:
            q_ref[:, c * LANES:(c + 1) * LANES] = (rot * (HEAD_DIM ** -0.5)).astype(BF16)
        else:
            k_ref[:, (c - nq) * LANES:(c - nq + 1) * LANES] = rot.astype(BF16)
    nv = n_qk + v_ref.shape[1]
    v_ref[...] = proj[:, n_qk:nv].astype(BF16)
    rows = u_ref.shape[1]
    for j in range(u_ref.shape[0]):
        us[j] = proj[:, nv + j * LANES:nv + (j + 1) * LANES]
        for t in range(CHUNK):
            u_ref[j, :, t * LANES:(t + 1) * LANES] = us[j, pl.ds(t, rows, stride=CHUNK), :].astype(BF16)


def _rope_tables(S):
    t = jnp.arange(S, dtype=jnp.int32)
    freqs = ROPE_THETA ** (-jnp.arange(16, dtype=F32) / 16)
    halves = []
    for pos in (t // GRID_W, t % GRID_W):
        ang = pos.astype(F32)[:, None] * freqs[None, :]
        halves.append((jnp.cos(ang), jnp.sin(ang)))
    cos_h = jnp.concatenate([c for c, _ in halves for _ in range(2)], axis=1)
    sin_h = jnp.concatenate([sg * s for _, s in halves for sg in (-1.0, 1.0)], axis=1)
    return jnp.concatenate([cos_h, cos_h], axis=1), jnp.concatenate([sin_h, sin_h], axis=1)


def _inproj(x2d, S, mix_g, w_in, q_g, k_g):
    T, D = x2d.shape
    tm = 512
    d_ssm = w_in.shape[1] - D_ATT - 2 * D_KV
    wq = w_in[:, :D_ATT]
    wk = w_in[:, D_ATT:D_ATT + D_KV]
    wv = w_in[:, D_ATT + D_KV:D_ATT + 2 * D_KV]
    wu = w_in[:, D_ATT + 2 * D_KV:]
    dup = lambda w: jnp.concatenate([w[:, :64], w[:, :64], w[:, 64:], w[:, 64:]], axis=1)
    w_ext = jnp.concatenate([wq, dup(wk), dup(wv), wu], axis=1).astype(BF16)
    n_qk = D_ATT + 2 * D_KV
    qkg = jnp.concatenate([jnp.tile(q_g, ATT_HEADS), jnp.tile(k_g, 2 * ATT_KV_HEADS)])[None, :]
    hid = np.arange(n_qk) // HEAD_DIM
    ones = jnp.asarray((hid[:, None] == hid[None, :]).astype(np.float32), dtype=BF16)
    cos_t, sin_t = _rope_tables(S)
    sb = S // tm
    const = lambda i: (0, 0)
    return pl.pallas_call(
        _inproj_body,
        grid=(T // tm,),
        in_specs=[
            pl.BlockSpec((tm, D), lambda i: (i, 0)),
            pl.BlockSpec((1, D), const),
            pl.BlockSpec(w_ext.shape, const),
            pl.BlockSpec((1, n_qk), const),
            pl.BlockSpec((n_qk, n_qk), const),
            pl.BlockSpec((tm, LANES), lambda i: (i % sb, 0)),
            pl.BlockSpec((tm, LANES), lambda i: (i % sb, 0)),
        ],
        out_specs=[
            pl.BlockSpec((tm, D_ATT), lambda i: (i, 0)),
            pl.BlockSpec((tm, 2 * D_KV), lambda i: (i, 0)),
            pl.BlockSpec((tm, 2 * D_KV), lambda i: (i, 0)),
            pl.BlockSpec((d_ssm // LANES, tm // CHUNK, CHUNK * LANES), lambda i: (0, i, 0)),
        ],
        out_shape=[
            jax.ShapeDtypeStruct((T, D_ATT), BF16),
            jax.ShapeDtypeStruct((T, 2 * D_KV), BF16),
            jax.ShapeDtypeStruct((T, 2 * D_KV), BF16),
            jax.ShapeDtypeStruct((d_ssm // LANES, T // CHUNK, CHUNK * LANES), BF16),
        ],
        scratch_shapes=[pltpu.VMEM((d_ssm // LANES, tm, LANES), F32)],
        compiler_params=_cparams("arbitrary"),
        name="inproj",
    )(x2d, mix_g[None, :], w_ext, qkg, ones, cos_t, sin_t)


def _attn_body(q_ref, k_ref, v_ref, o_ref, klo, khi, vlo, vhi):
    @pl.when(pl.program_id(2) == 0)
    def _():
        lo = lax.broadcasted_iota(jnp.int32, k_ref.shape, 1) < HEAD_DIM
        kd = k_ref[...]
        vd = v_ref[...]
        zero = jnp.zeros_like(kd)
        klo[...] = jnp.where(lo, kd, zero)
        khi[...] = jnp.where(lo, zero, kd)
        vlo[...] = jnp.where(lo, vd, zero)
        vhi[...] = jnp.where(lo, zero, vd)

    for p in range(2):
        qp = q_ref[:, p * LANES:(p + 1) * LANES]
        acc = None
        for km, vm in ((klo, vlo), (khi, vhi)):
            s = _dot_nt(qp, km[...])
            e = jnp.exp(s - jnp.max(s, axis=-1, keepdims=True))
            inv = 1.0 / jnp.sum(e, axis=-1, keepdims=True)
            o = _dot(e.astype(BF16), vm[...]) * inv
            acc = o if acc is None else acc + o
        o_ref[:, p * LANES:(p + 1) * LANES] = acc.astype(BF16)


def _attention(q, kd, vd, B, S):
    tq = 256
    q3 = q.reshape(B, S, D_ATT)
    k3 = kd.reshape(B, S, 2 * D_KV)
    v3 = vd.reshape(B, S, 2 * D_KV)
    gw = D_ATT // ATT_KV_HEADS
    out = pl.pallas_call(
        _attn_body,
        grid=(B, ATT_KV_HEADS, S // tq),
        in_specs=[
            pl.BlockSpec((None, tq, gw), lambda b, j, i: (b, i, j)),
            pl.BlockSpec((None, S, LANES), lambda b, j, i: (b, 0, j)),
            pl.BlockSpec((None, S, LANES), lambda b, j, i: (b, 0, j)),
        ],
        out_specs=pl.BlockSpec((None, tq, gw), lambda b, j, i: (b, i, j)),
        out_shape=jax.ShapeDtypeStruct((B, S, D_ATT), BF16),
        scratch_shapes=[pltpu.VMEM((S, LANES), BF16)] * 4,
        compiler_params=_cparams("arbitrary", "arbitrary", "arbitrary"),
        name="attn",
    )(q3, k3, v3)
    return out.reshape(B * S, D_ATT)


def _dot_tn3(a, b):
    tn = lambda x, y: lax.dot_general(x, y, (((0,), (0,)), ((), ())), preferred_element_type=F32)
    ah, al = _split_bf16(a)
    bh, bl = _split_bf16(b)
    return tn(ah, bh) + tn(al, bh) + tn(ah, bl)


def _s5op_body(lam_ref, bt_ref, ct_ref, m_ref, et_ref, da_ref, db_ref):
    W = m_ref.shape[1]
    L = CHUNK
    step = (lax.broadcasted_iota(jnp.int32, (1, W), 1) // SSM_GROUP).astype(F32)

    def direction(d):
        lam = lam_ref[d]
        lr = jnp.minimum(lam[:, 0:1], -1e-4)
        li = lam[:, 1:2]
        dt = jnp.exp(lam[:, 2:3])
        mag = jnp.exp(lr * dt)
        ab_r = mag * jnp.cos(li * dt)
        ab_i = mag * jnp.sin(li * dt)
        den = lr * lr + li * li
        nr = ab_r - 1.0
        f_r = (nr * lr + ab_i * li) / den
        f_i = (ab_i * lr - nr * li) / den
        bb_r = f_r * bt_ref[d, 0] - f_i * bt_ref[d, 1]
        bb_i = f_r * bt_ref[d, 1] + f_i * bt_ref[d, 0]
        cr, ci = ct_ref[d, 0], ct_ref[d, 1]

        def power(n):
            m = jnp.exp((lr * dt) * n)
            return m * jnp.cos((li * dt) * n), m * jnp.sin((li * dt) * n)

        def a_b(n):
            pr, pi = power(n)
            return pr * bb_r - pi * bb_i, pr * bb_i + pi * bb_r

        def c_a(n):
            pr, pi = power(n)
            return cr * pr - ci * pi, cr * pi + ci * pr

        return bb_r, bb_i, cr, ci, a_b, c_a

    fbr, fbi, _, _, f_ab, f_ca = direction(0)
    bbr, bbi, bcr, bci, b_ab, b_ca = direction(1)
    etf_r, etf_i = f_ab(L - 1.0 - step)
    etb_r, etb_i = b_ab(step)
    et_ref[...] = jnp.concatenate([etf_r, etb_r, etf_i, etb_i], axis=0).astype(BF16)
    df_r, df_i = f_ca(step + 1.0)
    db_r, db_i = b_ca(L - step)
    zero = jnp.zeros_like(df_r)
    da_ref[...] = jnp.concatenate([df_r, zero, -df_i, zero], axis=0).astype(BF16)
    db_ref[...] = jnp.concatenate([zero, db_r, zero, -db_i], axis=0).astype(BF16)
    kf_r, kf_i = f_ca(step)
    lag0 = step == 0.0
    lhs_f = jnp.concatenate([fbr, -fbi], axis=0)
    lhs_b = jnp.concatenate([bbr, -bbi], axis=0)
    q_b = _dot_tn3(lhs_b, jnp.concatenate([db_r, db_i], axis=0))
    q_f = _dot_tn3(jnp.concatenate([lhs_f, lhs_b], axis=0),
                   jnp.concatenate([kf_r, kf_i, jnp.where(lag0, bcr, 0.0), jnp.where(lag0, bci, 0.0)],
                                   axis=0))
    z = jnp.concatenate([q_b, q_f], axis=1)
    for j in range(L):
        rows = z[j * SSM_GROUP:(j + 1) * SSM_GROUP, :]
        m_ref[j * SSM_GROUP:(j + 1) * SSM_GROUP, :] = pltpu.roll(
            rows, (2 * W - (L - j) * SSM_GROUP) % (2 * W), 1)[:, :W].astype(BF16)


def _s5_operators(lam_re, lam_im, log_step, b_re, b_im, c_re, c_im):
    _, G, P = lam_re.shape
    W = CHUNK * SSM_GROUP
    lam = jnp.stack([lam_re, lam_im, jnp.broadcast_to(log_step[..., None], lam_re.shape)], axis=-1).astype(F32)
    tile = lambda a: jnp.tile(a.astype(F32), (1, 1, 1, CHUNK))
    bt = jnp.stack([tile(b_re), tile(b_im)], axis=1)
    ct = jnp.stack([tile(c_re.swapaxes(2, 3)), tile(c_im.swapaxes(2, 3))], axis=1)
    per_g = pl.BlockSpec((2, 2, None, P, W), lambda g: (0, 0, g, 0, 0))
    out = pl.BlockSpec((None, W, W), lambda g: (g, 0, 0))
    return pl.pallas_call(
        _s5op_body,
        grid=(G,),
        in_specs=[pl.BlockSpec((2, None, P, 3), lambda g: (0, g, 0, 0)), per_g, per_g],
        out_specs=[out] * 4,
        out_shape=[jax.ShapeDtypeStruct((G, W, W), BF16)] * 4,
        compiler_params=_cparams("arbitrary"),
        name="s5op",
    )(lam, bt, ct)


GROUPS_PER_SET = LANES // SSM_GROUP
S5_ROW_CHUNK = 256


def _s5_body(u_ref, pin_ref, m_ref, et_ref, da_ref, db_ref, a_ref, d_ref, z_ref,
             uperm, zperm, y_s, xr_s, xi_s, har_s, hai_s, hbr_s, hbi_s, *, nb):
    R = u_ref.shape[0]
    nchunk = R // nb
    W = uperm.shape[2]
    half = W // 2
    g = pl.program_id(1)

    @pl.when(g == 0)
    def _():
        for r0 in range(0, R, S5_ROW_CHUNK):
            up = _dot(u_ref[r0:r0 + S5_ROW_CHUNK, :], pin_ref[...]).astype(BF16)
            for k in range(GROUPS_PER_SET):
                uperm[k, r0:r0 + S5_ROW_CHUNK, :] = up[:, k * W:(k + 1) * W]

    u = uperm[g]
    y_s[...] = _dot(u, m_ref[...])
    x = _dot_nt(u, et_ref[...])
    xr_s[...] = x[:, :half]
    xi_s[...] = x[:, half:]
    lr = jnp.minimum(a_ref[0:1, :], -1e-4)
    dt = jnp.exp(a_ref[2:3, :])
    mag = jnp.exp(lr * dt * CHUNK)
    ang = a_ref[1:2, :] * dt * CHUNK
    ar = jnp.broadcast_to(mag * jnp.cos(ang), (nb, half))
    ai = jnp.broadcast_to(mag * jnp.sin(ang), (nb, half))
    fwd = lax.broadcasted_iota(jnp.int32, (nb, half), 1) < SSM_STATE

    def step(c, carry):
        sr, si = carry
        rf = pl.ds(c, nb, stride=nchunk)
        rb = pl.ds(nchunk - 1 - c, nb, stride=nchunk)
        har_s[rf, :] = sr
        hai_s[rf, :] = si
        hbr_s[rb, :] = sr
        hbi_s[rb, :] = si
        xr = jnp.where(fwd, xr_s[rf, :], xr_s[rb, :])
        xi = jnp.where(fwd, xi_s[rf, :], xi_s[rb, :])
        return sr * ar - si * ai + xr, sr * ai + si * ar + xi

    zero = jnp.zeros((nb, half), F32)
    lax.fori_loop(0, nchunk, step, (zero, zero))
    ha = jnp.concatenate([har_s[...], hai_s[...]], axis=1).astype(BF16)
    hb = jnp.concatenate([hbr_s[...], hbi_s[...]], axis=1).astype(BF16)
    y = y_s[...] + _dot(ha, da_ref[...]) + _dot(hb, db_ref[...]) + d_ref[...] * u.astype(F32)
    zperm[g] = jax.nn.gelu(y).astype(BF16)

    @pl.when(g == GROUPS_PER_SET - 1)
    def _():
        for r0 in range(0, R, S5_ROW_CHUNK):
            zc = jnp.concatenate([zperm[k, r0:r0 + S5_ROW_CHUNK, :] for k in range(GROUPS_PER_SET)], axis=1)
            z_ref[r0:r0 + S5_ROW_CHUNK, :] = _dot_nt(zc, pin_ref[...]).astype(BF16)


def _s5(u_set, B, S, lam_re, lam_im, log_step, b_re, b_im, c_re, c_im, ssm_d):
    n_set, R, WS = u_set.shape
    G = n_set * GROUPS_PER_SET
    W = CHUNK * SSM_GROUP
    assert R == (S // CHUNK) * B and WS == CHUNK * LANES
    M, ET, DA, DB = _s5_operators(lam_re, lam_im, log_step, b_re, b_im, c_re, c_im)
    both = lambda a: jnp.concatenate([a[0], a[1]], axis=-1)
    lam = jnp.stack([both(lam_re), both(lam_im),
                     both(jnp.broadcast_to(log_step[..., None], lam_re.shape))], axis=1).astype(F32)
    dvec = jnp.tile(ssm_d.astype(F32).reshape(G, 1, SSM_GROUP), (1, CHUNK, 1)).reshape(G, 1, W)
    lane = np.arange(WS)
    step, grp, ch = lane // LANES, (lane % LANES) // SSM_GROUP, lane % SSM_GROUP
    pin = jnp.asarray(((grp * CHUNK + step) * SSM_GROUP + ch)[:, None] == lane[None, :], dtype=BF16)
    per_set = pl.BlockSpec((None, R, WS), lambda s, g: (s, 0, 0))
    blk = lambda *sh: pl.BlockSpec((None,) + sh, lambda s, g: (s * GROUPS_PER_SET + g,) + (0,) * len(sh))
    z = pl.pallas_call(
        functools.partial(_s5_body, nb=B),
        grid=(n_set, GROUPS_PER_SET),
        in_specs=[per_set, pl.BlockSpec((WS, WS), lambda s, g: (0, 0)),
                  blk(W, W), blk(W, W), blk(W, W), blk(W, W), blk(3, W // 2), blk(1, W)],
        out_specs=per_set,
        out_shape=jax.ShapeDtypeStruct((n_set, R, WS), BF16),
        scratch_shapes=[pltpu.VMEM((GROUPS_PER_SET, R, W), BF16)] * 2 + [pltpu.VMEM((R, W), F32)]
        + [pltpu.VMEM((R, W // 2), F32)] * 6,
        compiler_params=_cparams("arbitrary", "arbitrary"),
        name="s5",
    )(u_set, pin, M, ET, DA, DB, lam, dvec)
    return z


def _memkv_body(m_ref, g_ref, wk_ref, wv_ref, k_ref, v_ref):
    h = (_rms(m_ref[...]) * g_ref[...]).astype(BF16)
    k_ref[...] = _dot(h, wk_ref[...]).astype(BF16)
    v_ref[...] = _dot(h, wv_ref[...]).astype(BF16)


def _memkv(mem, g, wk, wv):
    B, M, D = mem.shape
    const = lambda b: (0, 0)
    row = pl.BlockSpec((None, M, D), lambda b: (b, 0, 0))
    return pl.pallas_call(
        _memkv_body,
        grid=(B,),
        in_specs=[row, pl.BlockSpec((1, D), const), pl.BlockSpec((D, D), const),
                  pl.BlockSpec((D, D), const)],
        out_specs=[row, row],
        out_shape=[jax.ShapeDtypeStruct((B, M, D), BF16)] * 2,
        compiler_params=_cparams("arbitrary"),
        name="memkv",
    )(mem, g[None, :], wk.astype(BF16), wv.astype(BF16))


def _mid_body(x_ref, ya_ref, z_ref, gw_ref, gb_ref, ag_ref, sg_ref, wo_ref, xg_ref,
              wq_ref, xwo_ref, k_ref, v_ref, o_ref, zs):
    D = x_ref.shape[1]
    rows = z_ref.shape[1]
    for j in range(z_ref.shape[0]):
        for t in range(CHUNK):
            zs[j, pl.ds(t, rows, stride=CHUNK), :] = z_ref[j, :, t * LANES:(t + 1) * LANES].astype(F32)
    z = jnp.concatenate([zs[j] for j in range(z_ref.shape[0])], axis=1).astype(BF16)
    s = z.astype(F32) * jax.nn.sigmoid(_dot(z, gw_ref[...]) + gb_ref[...])
    ya = _rms(ya_ref[...].astype(F32)) * ag_ref[...]
    ys = _rms(s) * sg_ref[...]
    y = jnp.concatenate([ya, ys], axis=1).astype(BF16)
    x1 = x_ref[...] + _dot(y, wo_ref[...])
    hq = (_rms(x1) * xg_ref[...]).astype(BF16)
    dh = D // XA_HEADS
    q = (_dot(hq, wq_ref[...]) * (dh ** -0.5)).astype(BF16)
    outs = []
    for h in range(XA_HEADS):
        sl = slice(h * dh, (h + 1) * dh)
        sc = _dot_nt(q[:, sl], k_ref[:, sl])
        e = jnp.exp(sc - jnp.max(sc, axis=-1, keepdims=True))
        inv = 1.0 / jnp.sum(e, axis=-1, keepdims=True)
        outs.append((_dot(e.astype(BF16), v_ref[:, sl]) * inv).astype(BF16))
    o = jnp.concatenate(outs, axis=1)
    o_ref[:, :D] = x1 + _dot(o, xwo_ref[...])
    o_ref[:, D:] = jnp.zeros((o_ref.shape[0], o_ref.shape[1] - D), F32)


def _mid(x2d, S, y_att, z, glu_w, glu_b, att_g, ssm_g, w_out, xa_g, wq, wo, kmem, vmem):
    T, D = x2d.shape
    M = kmem.shape[1]
    tm = 256
    sb = S // tm
    const = lambda i: (0, 0)
    tok = lambda w: pl.BlockSpec((tm, w), lambda i: (i, 0))
    full = lambda a: pl.BlockSpec(a.shape, const)
    vec = lambda a: a.astype(F32)[None, :]
    mem = pl.BlockSpec((None, M, D), lambda i: (i // sb, 0, 0))
    args = [x2d, y_att, z, glu_w.astype(BF16), vec(glu_b), vec(att_g), vec(ssm_g), w_out.astype(BF16),
            vec(xa_g), wq.astype(BF16), wo.astype(BF16), kmem, vmem]
    zspec = pl.BlockSpec((z.shape[0], tm // CHUNK, CHUNK * LANES), lambda i: (0, i, 0))
    specs = [tok(D), tok(y_att.shape[1]), zspec] + [full(a) for a in args[3:11]] + [mem, mem]
    return pl.pallas_call(
        _mid_body,
        grid=(T // tm,),
        in_specs=specs,
        out_specs=pl.BlockSpec((tm, D + LANES), lambda i: (i, 0)),
        out_shape=jax.ShapeDtypeStruct((T, D + LANES), F32),
        scratch_shapes=[pltpu.VMEM((z.shape[0], tm, LANES), F32)],
        compiler_params=_cparams("arbitrary"),
        name="mid",
    )(*args)


def _split_bf16(a):
    hi = a.astype(BF16)
    return hi, (a - hi.astype(F32)).astype(BF16)


def _route_body(x_ref, g_ref, whi_ref, wlo_ref, b_ref, meta_ref, cnt_ref, carry):
    i = pl.program_id(0)
    tm, D = x_ref.shape[0], g_ref.shape[1]

    @pl.when(i == 0)
    def _():
        carry[...] = jnp.zeros_like(carry)

    h = _rms(x_ref[:, :D]) * g_ref[...]
    hhi, hlo = _split_bf16(h)
    logits = (_dot(hhi, whi_ref[...]) + _dot(hlo, whi_ref[...]) + _dot(hhi, wlo_ref[...])
              + b_ref[...])
    lane = lax.broadcasted_iota(jnp.int32, logits.shape, 1)
    ninf = jnp.float32(-jnp.inf)
    big = jnp.int32(LANES)
    rmax = lambda v: jnp.max(v, axis=-1, keepdims=True)
    rmin = lambda v: jnp.min(v, axis=-1, keepdims=True)
    is_grp = lane < N_EXPERT_GROUPS
    gl = jnp.where(is_grp, logits, ninf)
    gmax = rmax(gl)
    g_sel = rmin(jnp.where(gl == gmax, lane, big))
    g_w = 1.0 / jnp.sum(jnp.where(is_grp, jnp.exp(gl - gmax), 0.0), axis=-1, keepdims=True)
    e_id = lane - N_EXPERT_GROUPS
    in_grp = (e_id >= 0) & (e_id < N_EXPERTS) & ((e_id >> 2) == g_sel)
    el = jnp.where(in_grp, logits, ninf)
    v1 = rmax(el)
    i1 = rmin(jnp.where(el == v1, lane, big))
    el2 = jnp.where(lane == i1, ninf, el)
    v2 = rmax(el2)
    i2 = rmin(jnp.where(el2 == v2, lane, big))
    t = jnp.exp(v2 - v1)
    w1 = g_w / (1.0 + t)
    w2 = g_w * t / (1.0 + t)
    first_low = i1 < i2
    ia = (jnp.minimum(i1, i2) - N_EXPERT_GROUPS) & 3
    ib = (jnp.maximum(i1, i2) - N_EXPERT_GROUPS) & 3
    wa = jnp.where(first_low, w1, w2)
    wb = jnp.where(first_low, w2, w1)
    cls = g_sel * PAIRS_PER_GROUP + ((ia * (7 - ia)) >> 1) + ib - ia - 1
    onehot = (lane == cls).astype(F32)
    r = lax.broadcasted_iota(jnp.int32, (tm, tm), 0)
    c = lax.broadcasted_iota(jnp.int32, (tm, tm), 1)
    before = _dot((c < r).astype(BF16), onehot.astype(BF16))
    rank = jnp.sum(onehot * (before + carry[...]), axis=-1, keepdims=True)
    carry[...] = carry[...] + jnp.sum(onehot, axis=0, keepdims=True)
    cnt_ref[...] = carry[...]
    meta = jnp.where(lane == META_CLS, cls.astype(F32),
                     jnp.where(lane == META_RANK, rank,
                               jnp.where(lane == META_WA, wa,
                                         jnp.where(lane == META_WB, wb, 0.0))))
    meta_ref[...] = meta


def _route(x2a, D, moe_g, grp_w, grp_b, exp_w, exp_b):
    T = x2a.shape[0]
    tm = 256
    pad = LANES - N_EXPERT_GROUPS - N_EXPERTS
    w = jnp.concatenate([grp_w, exp_w, jnp.zeros((D, pad), F32)], axis=1).astype(F32)
    b = jnp.concatenate([grp_b, exp_b, jnp.zeros((pad,), F32)]).astype(F32)[None, :]
    whi, wlo = _split_bf16(w)
    const = lambda i: (0, 0)
    ncol = D // LANES
    return pl.pallas_call(
        _route_body,
        grid=(T // tm,),
        in_specs=[pl.BlockSpec((tm, D + LANES), lambda i: (i, 0)),
                  pl.BlockSpec((1, D), const), pl.BlockSpec((D, LANES), const),
                  pl.BlockSpec((D, LANES), const), pl.BlockSpec((1, LANES), const)],
        out_specs=[pl.BlockSpec((tm, LANES), lambda i: (i, ncol)),
                   pl.BlockSpec((1, LANES), const)],
        out_shape=[jax.ShapeDtypeStruct(x2a.shape, F32), jax.ShapeDtypeStruct((1, LANES), F32)],
        scratch_shapes=[pltpu.VMEM((1, LANES), F32)],
        input_output_aliases={0: 0},
        compiler_params=_cparams("arbitrary"),
        name="route",
    )(x2a, moe_g.astype(F32)[None, :], whi, wlo, b)


ROW_COPY_UNROLL = 8


def _row_copy_body(pos_ref, src_ref, *rest, rows, scatter):
    dst_ref, sem = rest[-2:]

    def copy(r):
        p = pos_ref[r]
        if scatter:
            return pltpu.make_async_copy(src_ref.at[pl.ds(r, 1)], dst_ref.at[pl.ds(p, 1)], sem)
        return pltpu.make_async_copy(src_ref.at[pl.ds(p, 1)], dst_ref.at[pl.ds(r, 1)], sem)

    def start(j, _):
        for k in range(ROW_COPY_UNROLL):
            copy(j * ROW_COPY_UNROLL + k).start(priority=k % 2)
        return 0

    def wait(j, _):
        for k in range(ROW_COPY_UNROLL):
            copy(j * ROW_COPY_UNROLL + k).wait()
        return 0

    lax.fori_loop(0, rows // ROW_COPY_UNROLL, start, 0)
    lax.fori_loop(0, rows // ROW_COPY_UNROLL, wait, 0)


def _row_copy(pos, src, *, dst_init=None, dst_rows=None, name):
    T = pos.shape[0]
    rows = 512
    scatter = dst_init is not None
    hbm = pl.BlockSpec(memory_space=pl.ANY)
    tok = pl.BlockSpec((rows, src.shape[1]), lambda i: (i, 0))
    out_rows = dst_init.shape[0] if scatter else dst_rows
    return pl.pallas_call(
        functools.partial(_row_copy_body, rows=rows, scatter=scatter),
        grid=(T // rows,),
        in_specs=[pl.BlockSpec((rows,), lambda i: (i,), memory_space=pltpu.SMEM)]
        + ([tok, hbm] if scatter else [hbm]),
        out_specs=hbm if scatter else tok,
        out_shape=jax.ShapeDtypeStruct((out_rows, src.shape[1]), src.dtype),
        scratch_shapes=[pltpu.SemaphoreType.DMA(())],
        input_output_aliases={2: 0} if scatter else {},
        compiler_params=pltpu.CompilerParams(dimension_semantics=("arbitrary",)),
        name=name,
    )(*((pos, src, dst_init) if scatter else (pos, src)))


def _ffn_body(ea_ref, eb_ref, nt_ref, xs_ref, g_ref, fg_ref, wga, wua, wda, wgb, wub, wdb, o_ref):
    del ea_ref, eb_ref
    D = g_ref.shape[1]
    valid = pl.program_id(0) < nt_ref[0]

    @pl.when(valid)
    def _():
        x = xs_ref[:, :D]
        meta = xs_ref[:, D:]
        h = (_rms(x) * g_ref[...]).astype(BF16)
        y = x
        for wg, wu, wd, lane in ((wga, wua, wda, META_WA), (wgb, wub, wdb, META_WB)):
            a = jax.nn.silu(_dot(h, wg[...])) * _dot(h, wu[...]) * meta[:, lane:lane + 1]
            y = y + _dot(a.astype(BF16), wd[...])
        o_ref[...] = _rms(y) * fg_ref[...]

    @pl.when(jnp.logical_not(valid))
    def _():
        o_ref[...] = jnp.zeros_like(o_ref)


def _ffn(xs, D, tile_ea, tile_eb, n_tiles, moe_g, final_g, w_gate, w_up, w_down, tm):
    P = xs.shape[0]
    nt_max = P // tm
    De = w_gate.shape[2]
    wg = w_gate.astype(BF16)
    wu = w_up.astype(BF16)
    wd = w_down.astype(BF16)
    const = lambda i, ea, eb, nt: (0, 0)
    sel_a = lambda i, ea, eb, nt: (ea[i], 0, 0)
    sel_b = lambda i, ea, eb, nt: (eb[i], 0, 0)
    up = lambda sel: pl.BlockSpec((None, D, De), sel)
    down = lambda sel: pl.BlockSpec((None, De, D), sel)
    grid_spec = pltpu.PrefetchScalarGridSpec(
        num_scalar_prefetch=3,
        grid=(nt_max,),
        in_specs=[pl.BlockSpec((tm, D + LANES), lambda i, ea, eb, nt: (jnp.maximum(jnp.minimum(i, nt[0] - 1), 0), 0)),
                  pl.BlockSpec((1, D), const), pl.BlockSpec((1, D), const),
                  up(sel_a), up(sel_a), down(sel_a), up(sel_b), up(sel_b), down(sel_b)],
        out_specs=pl.BlockSpec((tm, D), lambda i, ea, eb, nt: (i, 0)),
    )
    return pl.pallas_call(
        _ffn_body,
        grid_spec=grid_spec,
        out_shape=jax.ShapeDtypeStruct((P, D), F32),
        compiler_params=_cparams("arbitrary"),
        name="ffn",
    )(tile_ea, tile_eb, n_tiles, xs, moe_g.astype(F32)[None, :], final_g.astype(F32)[None, :],
      wg, wu, wd, wg, wu, wd)


def _moe(x2a, D, moe_g, grp_w, grp_b, exp_w, exp_b, w_gate, w_up, w_down, final_g):
    T = x2a.shape[0]
    tm = 256
    x2a, counts = _route(x2a, D, moe_g, grp_w, grp_b, exp_w, exp_b)
    cnt = counts[0, :N_CLASSES].astype(jnp.int32)
    ntile_c = (cnt + tm - 1) // tm
    tile_end = jnp.cumsum(ntile_c)
    off = (tile_end - ntile_c) * tm
    cls = x2a[:, D + META_CLS].astype(jnp.int32)
    rank = x2a[:, D + META_RANK].astype(jnp.int32)
    pos = off[cls] + rank
    nt_max = T // tm + N_CLASSES
    n_tiles = tile_end[-1:]
    tile_id = jnp.arange(nt_max, dtype=jnp.int32)
    tile_cls = jnp.minimum(jnp.sum((tile_end[None, :] <= tile_id[:, None]).astype(jnp.int32), axis=1),
                           N_CLASSES - 1)
    tile_cls = jnp.where(jnp.arange(nt_max) < n_tiles[0], tile_cls, tile_cls[jnp.maximum(n_tiles[0] - 1, 0)])
    pa = jnp.asarray([0, 0, 0, 1, 1, 2], jnp.int32)
    pb = jnp.asarray([1, 2, 3, 2, 3, 3], jnp.int32)
    grp = tile_cls // PAIRS_PER_GROUP
    tile_ea = grp * EXPERTS_PER_GROUP + pa[tile_cls % PAIRS_PER_GROUP]
    tile_eb = grp * EXPERTS_PER_GROUP + pb[tile_cls % PAIRS_PER_GROUP]
    xs = _row_copy(pos, x2a, dst_init=jnp.zeros((nt_max * tm, D + LANES), F32), name="dispatch")
    ys = _ffn(xs, D, tile_ea, tile_eb, n_tiles.astype(jnp.int32), moe_g, final_g, w_gate, w_up, w_down, tm)
    return _row_copy(pos, ys, dst_rows=T, name="unsort")


def kernel(x, mem, mix_norm_g, w_in, q_norm_g, k_norm_g, ssm_lambda_re, ssm_lambda_im, ssm_log_step, ssm_b_re, ssm_b_im, ssm_c_re, ssm_c_im, ssm_d, ssm_glu_w, ssm_glu_b, att_out_g, ssm_out_g, w_out, xa_norm_g, mem_norm_g, xa_wq, xa_wk, xa_wv, xa_wo, moe_norm_g, router_grp_w, router_grp_b, router_exp_w, router_exp_b, exp_w_gate, exp_w_up, exp_w_down, final_norm_g):
    B, S, D = x.shape
    assert mix_norm_g.shape[0] == 1, "single layer"
    x2d = x.reshape(B * S, D)
    q, kd, vd, u = _inproj(x2d, S, mix_norm_g[0], w_in[0], q_norm_g[0], k_norm_g[0])
    y_att = _attention(q, kd, vd, B, S)
    z = _s5(u, B, S, ssm_lambda_re[0], ssm_lambda_im[0], ssm_log_step[0], ssm_b_re[0], ssm_b_im[0],
            ssm_c_re[0], ssm_c_im[0], ssm_d[0])
    kmem, vmem = _memkv(mem, mem_norm_g[0], xa_wk[0], xa_wv[0])
    x2a = _mid(x2d, S, y_att, z, ssm_glu_w[0], ssm_glu_b[0], att_out_g[0], ssm_out_g[0], w_out[0],
               xa_norm_g[0], xa_wq[0], xa_wo[0], kmem, vmem)
    out = _moe(x2a, D, moe_norm_g[0], router_grp_w[0], router_grp_b[0], router_exp_w[0],
               router_exp_b[0], exp_w_gate[0], exp_w_up[0], exp_w_down[0], final_norm_g)
    return out.reshape(B, S, D)
```

```python
import functools

import numpy as np
import jax
import jax.numpy as jnp
from jax import lax
from jax.experimental import pallas as pl
from jax.experimental.pallas import tpu as pltpu

F32 = jnp.float32
BF16 = jnp.bfloat16

EPS = 1e-6
GRID_W = 64
ROPE_THETA = 10000.0
ATT_HEADS = 8
ATT_KV_HEADS = 2
HEAD_DIM = 64
D_ATT = ATT_HEADS * HEAD_DIM
D_KV = ATT_KV_HEADS * HEAD_DIM
SSM_GROUP = 16
SSM_STATE = 64
XA_HEADS = 4
N_EXPERT_GROUPS = 4
EXPERTS_PER_GROUP = 4
N_EXPERTS = N_EXPERT_GROUPS * EXPERTS_PER_GROUP
PAIRS_PER_GROUP = 6
N_CLASSES = N_EXPERT_GROUPS * PAIRS_PER_GROUP

LANES = 128
SUBLANES = 8
CHUNK = 16
VMEM_LIMIT = 56 * 1024 * 1024

META_CLS, META_RANK, META_WA, META_WB, META_POS = 0, 1, 2, 3, 4


def _cparams(*sem):
    return pltpu.CompilerParams(dimension_semantics=sem, vmem_limit_bytes=VMEM_LIMIT)


def _rms(x):
    return x * lax.rsqrt(jnp.mean(x * x, axis=-1, keepdims=True) + EPS)


def _dot(a, b):
    return jnp.dot(a, b, preferred_element_type=F32)


def _dot_nt(a, b):
    return lax.dot_general(a, b, (((1,), (1,)), ((), ())), preferred_element_type=F32)


def _inproj_body(x_ref, g_ref, w_ref, qkg_ref, ones_ref, cos_ref, sin_ref,
                 q_ref, k_ref, v_ref, u_ref, us):
    n_qk = q_ref.shape[1] + k_ref.shape[1]
    h = _rms(x_ref[...]) * g_ref[...]
    proj = _dot(h.astype(BF16), w_ref[...])
    qk = proj[:, :n_qk]
    ssq = _dot((qk * qk).astype(BF16), ones_ref[...])
    qk = qk * lax.rsqrt(ssq * (1.0 / HEAD_DIM) + EPS) * qkg_ref[...]
    cos = cos_ref[...]
    sin = sin_ref[...]
    lane = lax.broadcasted_iota(jnp.int32, cos.shape, 1)
    first = (lane % 32) < 16
    nq = q_ref.shape[1] // LANES
    for c in range(n_qk // LANES):
        xc = qk[:, c * LANES:(c + 1) * LANES]
        partner = jnp.where(first, pltpu.roll(xc, LANES - 16, 1), pltpu.roll(xc, 16, 1))
        rot = xc * cos + partner * sin
        if c < nq:
            q_ref[:, c * LANES:(c + 1) * LANES] = (rot * (HEAD_DIM ** -0.5)).astype(BF16)
        else:
            k_ref[:, (c - nq) * LANES:(c - nq + 1) * LANES] = rot.astype(BF16)
    nv = n_qk + v_ref.shape[1]
    v_ref[...] = proj[:, n_qk:nv].astype(BF16)
    rows = u_ref.shape[1]
    for j in range(u_ref.shape[0]):
        us[j] = proj[:, nv + j * LANES:nv + (j + 1) * LANES]
        for t in range(CHUNK):
            u_ref[j, :, t * LANES:(t + 1) * LANES] = us[j, pl.ds(t, rows, stride=CHUNK), :].astype(BF16)


def _rope_tables(S):
    t = jnp.arange(S, dtype=jnp.int32)
    freqs = ROPE_THETA ** (-jnp.arange(16, dtype=F32) / 16)
    halves = []
    for pos in (t // GRID_W, t % GRID_W):
        ang = pos.astype(F32)[:, None] * freqs[None, :]
        halves.append((jnp.cos(ang), jnp.sin(ang)))
    cos_h = jnp.concatenate([c for c, _ in halves for _ in range(2)], axis=1)
    sin_h = jnp.concatenate([sg * s for _, s in halves for sg in (-1.0, 1.0)], axis=1)
    return jnp.concatenate([cos_h, cos_h], axis=1), jnp.concatenate([sin_h, sin_h], axis=1)


def _inproj(x2d, S, mix_g, w_in, q_g, k_g):
    T, D = x2d.shape
    tm = 512
    d_ssm = w_in.shape[1] - D_ATT - 2 * D_KV
    wq = w_in[:, :D_ATT]
    wk = w_in[:, D_ATT:D_ATT + D_KV]
    wv = w_in[:, D_ATT + D_KV:D_ATT + 2 * D_KV]
    wu = w_in[:, D_ATT + 2 * D_KV:]
    dup = lambda w: jnp.concatenate([w[:, :64], w[:, :64], w[:, 64:], w[:, 64:]], axis=1)
    w_ext = jnp.concatenate([wq, dup(wk), dup(wv), wu], axis=1).astype(BF16)
    n_qk = D_ATT + 2 * D_KV
    qkg = jnp.concatenate([jnp.tile(q_g, ATT_HEADS), jnp.tile(k_g, 2 * ATT_KV_HEADS)])[None, :]
    hid = np.arange(n_qk) // HEAD_DIM
    ones = jnp.asarray((hid[:, None] == hid[None, :]).astype(np.float32), dtype=BF16)
    cos_t, sin_t = _rope_tables(S)
    sb = S // tm
    const = lambda i: (0, 0)
    return pl.pallas_call(
        _inproj_body,
        grid=(T // tm,),
        in_specs=[
            pl.BlockSpec((tm, D), lambda i: (i, 0)),
            pl.BlockSpec((1, D), const),
            pl.BlockSpec(w_ext.shape, const),
            pl.BlockSpec((1, n_qk), const),
            pl.BlockSpec((n_qk, n_qk), const),
            pl.BlockSpec((tm, LANES), lambda i: (i % sb, 0)),
            pl.BlockSpec((tm, LANES), lambda i: (i % sb, 0)),
        ],
        out_specs=[
            pl.BlockSpec((tm, D_ATT), lambda i: (i, 0)),
            pl.BlockSpec((tm, 2 * D_KV), lambda i: (i, 0)),
            pl.BlockSpec((tm, 2 * D_KV), lambda i: (i, 0)),
            pl.BlockSpec((d_ssm // LANES, tm // CHUNK, CHUNK * LANES), lambda i: (0, i, 0)),
        ],
        out_shape=[
            jax.ShapeDtypeStruct((T, D_ATT), BF16),
            jax.ShapeDtypeStruct((T, 2 * D_KV), BF16),
            jax.ShapeDtypeStruct((T, 2 * D_KV), BF16),
            jax.ShapeDtypeStruct((d_ssm // LANES, T // CHUNK, CHUNK * LANES), BF16),
        ],
        scratch_shapes=[pltpu.VMEM((d_ssm // LANES, tm, LANES), F32)],
        compiler_params=_cparams("arbitrary"),
        name="inproj",
    )(x2d, mix_g[None, :], w_ext, qkg, ones, cos_t, sin_t)


ATT_KEY_CHUNK = 256
ATT_ROW_BLOCK = 256


def _attn_body(q_ref, k_ref, v_ref, o_ref, klo, khi, vlo, vhi, s_scr, e_scr):
    @pl.when(pl.program_id(2) == 0)
    def _():
        lane = lax.broadcasted_iota(jnp.int32, k_ref.shape, 1)
        lo = lane < HEAD_DIM
        kd = k_ref[...]
        vd = v_ref[...]
        zero = jnp.zeros_like(kd)
        klo[...] = jnp.where(lo, kd, zero)
        khi[...] = jnp.where(lo, zero, kd)
        vlo[...] = jnp.where(lo, vd, (lane == HEAD_DIM).astype(BF16))
        vhi[...] = jnp.where(lo, (lane == 0).astype(BF16), vd)

    S = klo.shape[0]
    rb = s_scr.shape[1]
    units = [(r0, p, km, vm) for r0 in range(0, q_ref.shape[0], rb)
             for p in range(2) for km, vm in ((klo, vlo), (khi, vhi))]

    def scores(i):
        r0, p, km, _ = units[i]
        qp = q_ref[r0:r0 + rb, p * LANES:(p + 1) * LANES]
        m = None
        for c in range(0, S, ATT_KEY_CHUNK):
            s = _dot_nt(qp, km[c:c + ATT_KEY_CHUNK, :])
            s_scr[i % 2, :, c:c + ATT_KEY_CHUNK] = s
            for l0 in range(0, ATT_KEY_CHUNK, LANES):
                m = s[:, l0:l0 + LANES] if m is None else jnp.maximum(m, s[:, l0:l0 + LANES])
        return jnp.max(m, axis=-1, keepdims=True)

    def values(i, m):
        vm = units[i][3]
        for c in range(0, S, ATT_KEY_CHUNK):
            e_scr[i % 2, :, c:c + ATT_KEY_CHUNK] = jnp.exp(s_scr[i % 2, :, c:c + ATT_KEY_CHUNK] - m).astype(BF16)
        acc = _dot(e_scr[i % 2], vm[...])
        ones_lane = HEAD_DIM if vm is vlo else 0
        return acc * (1.0 / acc[:, ones_lane:ones_lane + 1])

    lo = lax.broadcasted_iota(jnp.int32, (rb, LANES), 1) < HEAD_DIM
    m_prev = scores(0)
    first = None
    for i, (r0, p, _, vm) in enumerate(units):
        m_next = scores(i + 1) if i + 1 < len(units) else None
        o = values(i, m_prev)
        m_prev = m_next
        if vm is vlo:
            first = o
        else:
            o_ref[r0:r0 + rb, p * LANES:(p + 1) * LANES] = jnp.where(lo, first, o).astype(BF16)


def _attention(q, kd, vd, B, S):
    tq = 2 * ATT_ROW_BLOCK
    q3 = q.reshape(B, S, D_ATT)
    k3 = kd.reshape(B, S, 2 * D_KV)
    v3 = vd.reshape(B, S, 2 * D_KV)
    gw = D_ATT // ATT_KV_HEADS
    out = pl.pallas_call(
        _attn_body,
        grid=(B, ATT_KV_HEADS, S // tq),
        in_specs=[
            pl.BlockSpec((None, tq, gw), lambda b, j, i: (b, i, j)),
            pl.BlockSpec((None, S, LANES), lambda b, j, i: (b, 0, j)),
            pl.BlockSpec((None, S, LANES), lambda b, j, i: (b, 0, j)),
        ],
        out_specs=pl.BlockSpec((None, tq, gw), lambda b, j, i: (b, i, j)),
        out_shape=jax.ShapeDtypeStruct((B, S, D_ATT), BF16),
        scratch_shapes=[pltpu.VMEM((S, LANES), BF16)] * 4
        + [pltpu.VMEM((2, ATT_ROW_BLOCK, S), F32), pltpu.VMEM((2, ATT_ROW_BLOCK, S), BF16)],
        compiler_params=_cparams("arbitrary", "arbitrary", "arbitrary"),
        name="attn",
    )(q3, k3, v3)
    return out.reshape(B * S, D_ATT)


def _dot_tn3(a, b):
    tn = lambda x, y: lax.dot_general(x, y, (((0,), (0,)), ((), ())), preferred_element_type=F32)
    ah, al = _split_bf16(a)
    bh, bl = _split_bf16(b)
    return tn(ah, bh) + tn(al, bh) + tn(ah, bl)


def _s5op_body(lam_ref, bt_ref, ct_ref, m_ref, et_ref, da_ref, db_ref):
    W = m_ref.shape[1]
    L = CHUNK
    step = (lax.broadcasted_iota(jnp.int32, (1, W), 1) // SSM_GROUP).astype(F32)

    def direction(d):
        lam = lam_ref[d]
        lr = jnp.minimum(lam[:, 0:1], -1e-4)
        li = lam[:, 1:2]
        dt = jnp.exp(lam[:, 2:3])
        mag = jnp.exp(lr * dt)
        ab_r = mag * jnp.cos(li * dt)
        ab_i = mag * jnp.sin(li * dt)
        den = lr * lr + li * li
        nr = ab_r - 1.0
        f_r = (nr * lr + ab_i * li) / den
        f_i = (ab_i * lr - nr * li) / den
        bb_r = f_r * bt_ref[d, 0] - f_i * bt_ref[d, 1]
        bb_i = f_r * bt_ref[d, 1] + f_i * bt_ref[d, 0]
        cr, ci = ct_ref[d, 0], ct_ref[d, 1]

        def power(n):
            m = jnp.exp((lr * dt) * n)
            return m * jnp.cos((li * dt) * n), m * jnp.sin((li * dt) * n)

        def a_b(n):
            pr, pi = power(n)
            return pr * bb_r - pi * bb_i, pr * bb_i + pi * bb_r

        def c_a(n):
            pr, pi = power(n)
            return cr * pr - ci * pi, cr * pi + ci * pr

        return bb_r, bb_i, cr, ci, a_b, c_a

    fbr, fbi, _, _, f_ab, f_ca = direction(0)
    bbr, bbi, bcr, bci, b_ab, b_ca = direction(1)
    etf_r, etf_i = f_ab(L - 1.0 - step)
    etb_r, etb_i = b_ab(step)
    et_ref[...] = jnp.concatenate([etf_r, etb_r, etf_i, etb_i], axis=0).astype(BF16)
    df_r, df_i = f_ca(step + 1.0)
    db_r, db_i = b_ca(L - step)
    zero = jnp.zeros_like(df_r)
    da_ref[...] = jnp.concatenate([df_r, zero, -df_i, zero], axis=0).astype(BF16)
    db_ref[...] = jnp.concatenate([zero, db_r, zero, -db_i], axis=0).astype(BF16)
    kf_r, kf_i = f_ca(step)
    lag0 = step == 0.0
    lhs_f = jnp.concatenate([fbr, -fbi], axis=0)
    lhs_b = jnp.concatenate([bbr, -bbi], axis=0)
    q_b = _dot_tn3(lhs_b, jnp.concatenate([db_r, db_i], axis=0))
    q_f = _dot_tn3(jnp.concatenate([lhs_f, lhs_b], axis=0),
                   jnp.concatenate([kf_r, kf_i, jnp.where(lag0, bcr, 0.0), jnp.where(lag0, bci, 0.0)],
                                   axis=0))
    z = jnp.concatenate([q_b, q_f], axis=1)
    for j in range(L):
        rows = z[j * SSM_GROUP:(j + 1) * SSM_GROUP, :]
        m_ref[j * SSM_GROUP:(j + 1) * SSM_GROUP, :] = pltpu.roll(
            rows, (2 * W - (L - j) * SSM_GROUP) % (2 * W), 1)[:, :W].astype(BF16)


def _s5_operators(lam_re, lam_im, log_step, b_re, b_im, c_re, c_im):
    _, G, P = lam_re.shape
    W = CHUNK * SSM_GROUP
    lam = jnp.stack([lam_re, lam_im, jnp.broadcast_to(log_step[..., None], lam_re.shape)], axis=-1).astype(F32)
    tile = lambda a: jnp.tile(a.astype(F32), (1, 1, 1, CHUNK))
    bt = jnp.stack([tile(b_re), tile(b_im)], axis=1)
    ct = jnp.stack([tile(c_re.swapaxes(2, 3)), tile(c_im.swapaxes(2, 3))], axis=1)
    per_g = pl.BlockSpec((2, 2, None, P, W), lambda g: (0, 0, g, 0, 0))
    out = pl.BlockSpec((None, W, W), lambda g: (g, 0, 0))
    return pl.pallas_call(
        _s5op_body,
        grid=(G,),
        in_specs=[pl.BlockSpec((2, None, P, 3), lambda g: (0, g, 0, 0)), per_g, per_g],
        out_specs=[out] * 4,
        out_shape=[jax.ShapeDtypeStruct((G, W, W), BF16)] * 4,
        compiler_params=_cparams("arbitrary"),
        name="s5op",
    )(lam, bt, ct)


GROUPS_PER_SET = LANES // SSM_GROUP
S5_ROW_CHUNK = 256


def _s5_body(u_ref, pin_ref, m_ref, et_ref, da_ref, db_ref, a_ref, d_ref, z_ref,
             uperm, zperm, y_s, xr_s, xi_s, har_s, hai_s, hbr_s, hbi_s, *, nb):
    R = u_ref.shape[0]
    nchunk = R // nb
    W = uperm.shape[2]
    half = W // 2
    g = pl.program_id(1)

    @pl.when(g == 0)
    def _():
        for r0 in range(0, R, S5_ROW_CHUNK):
            up = _dot(u_ref[r0:r0 + S5_ROW_CHUNK, :], pin_ref[...]).astype(BF16)
            for k in range(GROUPS_PER_SET):
                uperm[k, r0:r0 + S5_ROW_CHUNK, :] = up[:, k * W:(k + 1) * W]

    u = uperm[g]
    y_s[...] = _dot(u, m_ref[...])
    x = _dot_nt(u, et_ref[...])
    xr_s[...] = x[:, :half]
    xi_s[...] = x[:, half:]
    lr = jnp.minimum(a_ref[0:1, :], -1e-4)
    dt = jnp.exp(a_ref[2:3, :])
    mag = jnp.exp(lr * dt * CHUNK)
    ang = a_ref[1:2, :] * dt * CHUNK
    ar = jnp.broadcast_to(mag * jnp.cos(ang), (nb, half))
    ai = jnp.broadcast_to(mag * jnp.sin(ang), (nb, half))
    fwd = lax.broadcasted_iota(jnp.int32, (nb, half), 1) < SSM_STATE

    def step(c, carry):
        sr, si = carry
        rf = pl.ds(c, nb, stride=nchunk)
        rb = pl.ds(nchunk - 1 - c, nb, stride=nchunk)
        har_s[rf, :] = sr
        hai_s[rf, :] = si
        hbr_s[rb, :] = sr
        hbi_s[rb, :] = si
        xr = jnp.where(fwd, xr_s[rf, :], xr_s[rb, :])
        xi = jnp.where(fwd, xi_s[rf, :], xi_s[rb, :])
        return sr * ar - si * ai + xr, sr * ai + si * ar + xi

    zero = jnp.zeros((nb, half), F32)
    lax.fori_loop(0, nchunk, step, (zero, zero))
    ha = jnp.concatenate([har_s[...], hai_s[...]], axis=1).astype(BF16)
    hb = jnp.concatenate([hbr_s[...], hbi_s[...]], axis=1).astype(BF16)
    y = y_s[...] + _dot(ha, da_ref[...]) + _dot(hb, db_ref[...]) + d_ref[...] * u.astype(F32)
    zperm[g] = jax.nn.gelu(y).astype(BF16)

    @pl.when(g == GROUPS_PER_SET - 1)
    def _():
        for r0 in range(0, R, S5_ROW_CHUNK):
            zc = jnp.concatenate([zperm[k, r0:r0 + S5_ROW_CHUNK, :] for k in range(GROUPS_PER_SET)], axis=1)
            z_ref[r0:r0 + S5_ROW_CHUNK, :] = _dot_nt(zc, pin_ref[...]).astype(BF16)


def _s5(u_set, B, S, lam_re, lam_im, log_step, b_re, b_im, c_re, c_im, ssm_d):
    n_set, R, WS = u_set.shape
    G = n_set * GROUPS_PER_SET
    W = CHUNK * SSM_GROUP
    assert R == (S // CHUNK) * B and WS == CHUNK * LANES
    M, ET, DA, DB = _s5_operators(lam_re, lam_im, log_step, b_re, b_im, c_re, c_im)
    both = lambda a: jnp.concatenate([a[0], a[1]], axis=-1)
    lam = jnp.stack([both(lam_re), both(lam_im),
                     both(jnp.broadcast_to(log_step[..., None], lam_re.shape))], axis=1).astype(F32)
    dvec = jnp.tile(ssm_d.astype(F32).reshape(G, 1, SSM_GROUP), (1, CHUNK, 1)).reshape(G, 1, W)
    lane = np.arange(WS)
    step, grp, ch = lane // LANES, (lane % LANES) // SSM_GROUP, lane % SSM_GROUP
    pin = jnp.asarray(((grp * CHUNK + step) * SSM_GROUP + ch)[:, None] == lane[None, :], dtype=BF16)
    per_set = pl.BlockSpec((None, R, WS), lambda s, g: (s, 0, 0))
    blk = lambda *sh: pl.BlockSpec((None,) + sh, lambda s, g: (s * GROUPS_PER_SET + g,) + (0,) * len(sh))
    z = pl.pallas_call(
        functools.partial(_s5_body, nb=B),
        grid=(n_set, GROUPS_PER_SET),
        in_specs=[per_set, pl.BlockSpec((WS, WS), lambda s, g: (0, 0)),
                  blk(W, W), blk(W, W), blk(W, W), blk(W, W), blk(3, W // 2), blk(1, W)],
        out_specs=per_set,
        out_shape=jax.ShapeDtypeStruct((n_set, R, WS), BF16),
        scratch_shapes=[pltpu.VMEM((GROUPS_PER_SET, R, W), BF16)] * 2 + [pltpu.VMEM((R, W), F32)]
        + [pltpu.VMEM((R, W // 2), F32)] * 6,
        compiler_params=_cparams("arbitrary", "arbitrary"),
        name="s5",
    )(u_set, pin, M, ET, DA, DB, lam, dvec)
    return z


def _memkv_body(m_ref, g_ref, wk_ref, wv_ref, k_ref, v_ref):
    h = (_rms(m_ref[...]) * g_ref[...]).astype(BF16)
    k_ref[...] = _dot(h, wk_ref[...]).astype(BF16)
    v_ref[...] = _dot(h, wv_ref[...]).astype(BF16)


def _memkv(mem, g, wk, wv):
    B, M, D = mem.shape
    const = lambda b: (0, 0)
    row = pl.BlockSpec((None, M, D), lambda b: (b, 0, 0))
    return pl.pallas_call(
        _memkv_body,
        grid=(B,),
        in_specs=[row, pl.BlockSpec((1, D), const), pl.BlockSpec((D, D), const),
                  pl.BlockSpec((D, D), const)],
        out_specs=[row, row],
        out_shape=[jax.ShapeDtypeStruct((B, M, D), BF16)] * 2,
        compiler_params=_cparams("arbitrary"),
        name="memkv",
    )(mem, g[None, :], wk.astype(BF16), wv.astype(BF16))


def _mid_body(x_ref, ya_ref, z_ref, gw_ref, gb_ref, ag_ref, sg_ref, wo_ref, xg_ref,
              wq_ref, xwo_ref, k_ref, v_ref, o_ref, zs):
    D = x_ref.shape[1]
    rows = z_ref.shape[1]
    for j in range(z_ref.shape[0]):
        for t in range(CHUNK):
            zs[j, pl.ds(t, rows, stride=CHUNK), :] = z_ref[j, :, t * LANES:(t + 1) * LANES].astype(F32)
    z = jnp.concatenate([zs[j] for j in range(z_ref.shape[0])], axis=1).astype(BF16)
    s = z.astype(F32) * jax.nn.sigmoid(_dot(z, gw_ref[...]) + gb_ref[...])
    ya = _rms(ya_ref[...].astype(F32)) * ag_ref[...]
    ys = _rms(s) * sg_ref[...]
    y = jnp.concatenate([ya, ys], axis=1).astype(BF16)
    x1 = x_ref[...] + _dot(y, wo_ref[...])
    hq = (_rms(x1) * xg_ref[...]).astype(BF16)
    dh = D // XA_HEADS
    q = (_dot(hq, wq_ref[...]) * (dh ** -0.5)).astype(BF16)
    outs = []
    for h in range(XA_HEADS):
        sl = slice(h * dh, (h + 1) * dh)
        sc = _dot_nt(q[:, sl], k_ref[:, sl])
        e = jnp.exp(sc - jnp.max(sc, axis=-1, keepdims=True))
        inv = 1.0 / jnp.sum(e, axis=-1, keepdims=True)
        outs.append((_dot(e.astype(BF16), v_ref[:, sl]) * inv).astype(BF16))
    o = jnp.concatenate(outs, axis=1)
    o_ref[:, :D] = x1 + _dot(o, xwo_ref[...])
    o_ref[:, D:] = jnp.zeros((o_ref.shape[0], o_ref.shape[1] - D), F32)


def _mid(x2d, S, y_att, z, glu_w, glu_b, att_g, ssm_g, w_out, xa_g, wq, wo, kmem, vmem):
    T, D = x2d.shape
    M = kmem.shape[1]
    tm = 256
    sb = S // tm
    const = lambda i: (0, 0)
    tok = lambda w: pl.BlockSpec((tm, w), lambda i: (i, 0))
    full = lambda a: pl.BlockSpec(a.shape, const)
    vec = lambda a: a.astype(F32)[None, :]
    mem = pl.BlockSpec((None, M, D), lambda i: (i // sb, 0, 0))
    args = [x2d, y_att, z, glu_w.astype(BF16), vec(glu_b), vec(att_g), vec(ssm_g), w_out.astype(BF16),
            vec(xa_g), wq.astype(BF16), wo.astype(BF16), kmem, vmem]
    zspec = pl.BlockSpec((z.shape[0], tm // CHUNK, CHUNK * LANES), lambda i: (0, i, 0))
    specs = [tok(D), tok(y_att.shape[1]), zspec] + [full(a) for a in args[3:11]] + [mem, mem]
    return pl.pallas_call(
        _mid_body,
        grid=(T // tm,),
        in_specs=specs,
        out_specs=pl.BlockSpec((tm, D + LANES), lambda i: (i, 0)),
        out_shape=jax.ShapeDtypeStruct((T, D + LANES), F32),
        scratch_shapes=[pltpu.VMEM((z.shape[0], tm, LANES), F32)],
        compiler_params=_cparams("arbitrary"),
        name="mid",
    )(*args)


def _split_bf16(a):
    hi = a.astype(BF16)
    return hi, (a - hi.astype(F32)).astype(BF16)


ROUTE_ROWS = 32
MOE_TILE = 256


def _route_body(x_ref, g_ref, whi_ref, wlo_ref, b_ref, tri_ref, meta_ref, mt_ref, cnt_ref, carry):
    i = pl.program_id(0)
    tm, D = x_ref.shape[0], g_ref.shape[1]

    @pl.when(i == 0)
    def _():
        carry[...] = jnp.zeros_like(carry)

    h = _rms(x_ref[:, :D]) * g_ref[...]
    hhi, hlo = _split_bf16(h)
    logits = (_dot_nt(whi_ref[...], hhi) + _dot_nt(whi_ref[...], hlo) + _dot_nt(wlo_ref[...], hhi)
              + b_ref[:, 0:1])
    ninf = jnp.float32(-jnp.inf)
    cmax = lambda v: jnp.max(v, axis=0, keepdims=True)
    cmin = lambda v: jnp.min(v, axis=0, keepdims=True)
    gl = logits[N_EXPERTS:N_EXPERTS + N_EXPERT_GROUPS]
    grow = lax.broadcasted_iota(jnp.int32, gl.shape, 0)
    gmax = cmax(gl)
    g_sel = cmin(jnp.where(gl == gmax, grow, N_EXPERT_GROUPS))
    g_w = 1.0 / jnp.sum(jnp.exp(gl - gmax), axis=0, keepdims=True)
    el = logits[:N_EXPERTS]
    erow = lax.broadcasted_iota(jnp.int32, el.shape, 0)
    el = jnp.where((erow >> 2) == g_sel, el, ninf)
    v1 = cmax(el)
    i1 = cmin(jnp.where(el == v1, erow, N_EXPERTS))
    el2 = jnp.where(erow == i1, ninf, el)
    v2 = cmax(el2)
    i2 = cmin(jnp.where(el2 == v2, erow, N_EXPERTS))
    t = jnp.exp(v2 - v1)
    w1 = g_w / (1.0 + t)
    w2 = g_w * t / (1.0 + t)
    first_low = i1 < i2
    ia = jnp.minimum(i1, i2) & 3
    ib = jnp.maximum(i1, i2) & 3
    wa = jnp.where(first_low, w1, w2)
    wb = jnp.where(first_low, w2, w1)
    cls = g_sel * PAIRS_PER_GROUP + ((ia * (7 - ia)) >> 1) + ib - ia - 1
    crow = lax.broadcasted_iota(jnp.int32, (ROUTE_ROWS, tm), 0)
    onehot = (crow == cls).astype(F32)
    before = _dot(onehot.astype(BF16), tri_ref[...])
    rank = jnp.sum(onehot * (before + carry[:, 0:1]), axis=0, keepdims=True)
    carry[...] = carry[...] + jnp.sum(onehot, axis=1, keepdims=True)
    cnt_ref[...] = carry[...]

    def record(rows):
        r = lax.broadcasted_iota(jnp.int32, (rows, tm), 0)
        return jnp.where(r == META_CLS, cls.astype(F32),
                         jnp.where(r == META_RANK, rank,
                                   jnp.where(r == META_WA, wa, jnp.where(r == META_WB, wb, 0.0))))

    meta_ref[...] = record(LANES).T
    mt_ref[:, pl.ds(pl.multiple_of(i * tm, tm), tm)] = record(mt_ref.shape[0])

    @pl.when(i == pl.num_programs(0) - 1)
    def _():
        ntile = jnp.floor((carry[...] + (MOE_TILE - 1.0)) * (1.0 / MOE_TILE))
        r = lax.broadcasted_iota(jnp.int32, (ROUTE_ROWS, ROUTE_ROWS), 0)
        c = lax.broadcasted_iota(jnp.int32, (ROUTE_ROWS, ROUTE_ROWS), 1)
        off = _dot((c < r).astype(BF16), ntile.astype(BF16)) * MOE_TILE
        cls_all = mt_ref[META_CLS:META_CLS + 1, :]
        pos = mt_ref[META_RANK:META_RANK + 1, :]
        for k in range(N_CLASSES):
            pos = pos + jnp.where(cls_all == k, off[k:k + 1, 0:1], 0.0)
        mt_ref[META_POS:META_POS + 1, :] = pos


def _route(x2a, D, moe_g, grp_w, grp_b, exp_w, exp_b):
    T = x2a.shape[0]
    tm = 256
    pad = ROUTE_ROWS - N_EXPERT_GROUPS - N_EXPERTS
    w = jnp.concatenate([exp_w, grp_w, jnp.zeros((D, pad), F32)], axis=1).astype(F32).T
    b = jnp.concatenate([exp_b, grp_b, jnp.zeros((pad,), F32)]).astype(F32)
    whi, wlo = _split_bf16(w)
    tri = jnp.asarray(np.arange(tm)[:, None] < np.arange(tm)[None, :], dtype=BF16)
    const = lambda i: (0, 0)
    ncol = D // LANES
    return pl.pallas_call(
        _route_body,
        grid=(T // tm,),
        in_specs=[pl.BlockSpec((tm, D + LANES), lambda i: (i, 0)),
                  pl.BlockSpec((1, D), const), pl.BlockSpec((ROUTE_ROWS, D), const),
                  pl.BlockSpec((ROUTE_ROWS, D), const), pl.BlockSpec((ROUTE_ROWS, LANES), const),
                  pl.BlockSpec((tm, tm), const)],
        out_specs=[pl.BlockSpec((tm, LANES), lambda i: (i, ncol)),
                   pl.BlockSpec((SUBLANES, T), const),
                   pl.BlockSpec((ROUTE_ROWS, LANES), const)],
        out_shape=[jax.ShapeDtypeStruct(x2a.shape, F32), jax.ShapeDtypeStruct((SUBLANES, T), F32),
                   jax.ShapeDtypeStruct((ROUTE_ROWS, LANES), F32)],
        scratch_shapes=[pltpu.VMEM((ROUTE_ROWS, LANES), F32)],
        input_output_aliases={0: 0},
        compiler_params=_cparams("arbitrary"),
        name="route",
    )(x2a, moe_g.astype(F32)[None, :], whi, wlo, jnp.broadcast_to(b[:, None], (ROUTE_ROWS, LANES)), tri)


ROW_COPY_UNROLL = 8


def _row_copy_body(pos_ref, src_ref, *rest, rows, scatter):
    dst_ref, sem = rest[-2:]

    def copy(r):
        p = pos_ref[r]
        if scatter:
            return pltpu.make_async_copy(src_ref.at[pl.ds(r, 1)], dst_ref.at[pl.ds(p, 1)], sem)
        return pltpu.make_async_copy(src_ref.at[pl.ds(p, 1)], dst_ref.at[pl.ds(r, 1)], sem)

    def start(j, _):
        for k in range(ROW_COPY_UNROLL):
            copy(j * ROW_COPY_UNROLL + k).start(priority=k % 2)
        return 0

    def wait(j, _):
        for k in range(ROW_COPY_UNROLL):
            copy(j * ROW_COPY_UNROLL + k).wait()
        return 0

    lax.fori_loop(0, rows // ROW_COPY_UNROLL, start, 0)
    lax.fori_loop(0, rows // ROW_COPY_UNROLL, wait, 0)


def _row_copy(pos, src, *, dst_init=None, dst_rows=None, name):
    T = pos.shape[0]
    rows = 512
    scatter = dst_init is not None
    hbm = pl.BlockSpec(memory_space=pl.ANY)
    tok = pl.BlockSpec((rows, src.shape[1]), lambda i: (i, 0))
    out_rows = dst_init.shape[0] if scatter else dst_rows
    return pl.pallas_call(
        functools.partial(_row_copy_body, rows=rows, scatter=scatter),
        grid=(T // rows,),
        in_specs=[pl.BlockSpec((rows,), lambda i: (i,), memory_space=pltpu.SMEM)]
        + ([tok, hbm] if scatter else [hbm]),
        out_specs=hbm if scatter else tok,
        out_shape=jax.ShapeDtypeStruct((out_rows, src.shape[1]), src.dtype),
        scratch_shapes=[pltpu.SemaphoreType.DMA(())],
        input_output_aliases={2: 0} if scatter else {},
        compiler_params=pltpu.CompilerParams(dimension_semantics=("arbitrary",)),
        name=name,
    )(*((pos, src, dst_init) if scatter else (pos, src)))


def _ffn_body(ea_ref, eb_ref, nt_ref, xs_ref, g_ref, fg_ref, wga, wua, wda, wgb, wub, wdb, o_ref):
    del ea_ref, eb_ref
    D = g_ref.shape[1]
    valid = pl.program_id(0) < nt_ref[0]

    @pl.when(valid)
    def _():
        x = xs_ref[:, :D]
        meta = xs_ref[:, D:]
        h = (_rms(x) * g_ref[...]).astype(BF16)
        y = x
        for wg, wu, wd, lane in ((wga, wua, wda, META_WA), (wgb, wub, wdb, META_WB)):
            a = jax.nn.silu(_dot(h, wg[...])) * _dot(h, wu[...]) * meta[:, lane:lane + 1]
            y = y + _dot(a.astype(BF16), wd[...])
        o_ref[...] = _rms(y) * fg_ref[...]

    @pl.when(jnp.logical_not(valid))
    def _():
        o_ref[...] = jnp.zeros_like(o_ref)


def _ffn(xs, D, tile_ea, tile_eb, n_tiles, moe_g, final_g, w_gate, w_up, w_down):
    tm = MOE_TILE
    P = xs.shape[0]
    nt_max = P // tm
    De = w_gate.shape[2]
    wg = w_gate.astype(BF16)
    wu = w_up.astype(BF16)
    wd = w_down.astype(BF16)
    const = lambda i, ea, eb, nt: (0, 0)
    sel_a = lambda i, ea, eb, nt: (ea[i], 0, 0)
    sel_b = lambda i, ea, eb, nt: (eb[i], 0, 0)
    up = lambda sel: pl.BlockSpec((None, D, De), sel)
    down = lambda sel: pl.BlockSpec((None, De, D), sel)
    grid_spec = pltpu.PrefetchScalarGridSpec(
        num_scalar_prefetch=3,
        grid=(nt_max,),
        in_specs=[pl.BlockSpec((tm, D + LANES),
                               lambda i, ea, eb, nt: (jnp.maximum(jnp.minimum(i, nt[0] - 1), 0), 0)),
                  pl.BlockSpec((1, D), const), pl.BlockSpec((1, D), const),
                  up(sel_a), up(sel_a), down(sel_a), up(sel_b), up(sel_b), down(sel_b)],
        out_specs=pl.BlockSpec((tm, D), lambda i, ea, eb, nt: (i, 0)),
    )
    return pl.pallas_call(
        _ffn_body,
        grid_spec=grid_spec,
        out_shape=jax.ShapeDtypeStruct((P, D), F32),
        compiler_params=_cparams("arbitrary"),
        name="ffn",
    )(tile_ea, tile_eb, n_tiles, xs, moe_g.astype(F32)[None, :], final_g.astype(F32)[None, :],
      wg, wu, wd, wg, wu, wd)


def _moe(x2a, D, moe_g, grp_w, grp_b, exp_w, exp_b, w_gate, w_up, w_down, final_g):
    T = x2a.shape[0]
    tm = MOE_TILE
    x2a, record, counts = _route(x2a, D, moe_g, grp_w, grp_b, exp_w, exp_b)
    pos = record[META_POS].astype(jnp.int32)
    cnt = counts[:N_CLASSES, 0].astype(jnp.int32)
    ntile_c = (cnt + tm - 1) // tm
    tile_end = jnp.cumsum(ntile_c)
    nt_max = T // tm + N_CLASSES
    n_tiles = tile_end[-1:]
    tile_id = jnp.arange(nt_max, dtype=jnp.int32)
    tile_cls = jnp.minimum(jnp.sum((tile_end[None, :] <= tile_id[:, None]).astype(jnp.int32), axis=1),
                           N_CLASSES - 1)
    tile_cls = jnp.where(tile_id < n_tiles[0], tile_cls, tile_cls[jnp.maximum(n_tiles[0] - 1, 0)])
    pa =jnp.asarray([0, 0, 0, 1, 1, 2], jnp.int32)
    pb = jnp.asarray([1, 2, 3, 2, 3, 3], jnp.int32)
    grp = tile_cls // PAIRS_PER_GROUP
    tile_ea = grp * EXPERTS_PER_GROUP + pa[tile_cls % PAIRS_PER_GROUP]
    tile_eb = grp * EXPERTS_PER_GROUP + pb[tile_cls % PAIRS_PER_GROUP]
    xs = _row_copy(pos, x2a, dst_init=jnp.zeros((nt_max * tm, D + LANES), F32), name="dispatch")
    ys = _ffn(xs, D, tile_ea, tile_eb, n_tiles.astype(jnp.int32), moe_g, final_g, w_gate, w_up, w_down)
    return _row_copy(pos, ys, dst_rows=T, name="unsort")


def kernel(x, mem, mix_norm_g, w_in, q_norm_g, k_norm_g, ssm_lambda_re, ssm_lambda_im, ssm_log_step, ssm_b_re, ssm_b_im, ssm_c_re, ssm_c_im, ssm_d, ssm_glu_w, ssm_glu_b, att_out_g, ssm_out_g, w_out, xa_norm_g, mem_norm_g, xa_wq, xa_wk, xa_wv, xa_wo, moe_norm_g, router_grp_w, router_grp_b, router_exp_w, router_exp_b, exp_w_gate, exp_w_up, exp_w_down, final_norm_g):
    B, S, D = x.shape
    assert mix_norm_g.shape[0] == 1, "single layer"
    x2d = x.reshape(B * S, D)
    q, kd, vd, u = _inproj(x2d, S, mix_norm_g[0], w_in[0], q_norm_g[0], k_norm_g[0])
    y_att = _attention(q, kd, vd, B, S)
    z = _s5(u, B, S, ssm_lambda_re[0], ssm_lambda_im[0], ssm_log_step[0], ssm_b_re[0], ssm_b_im[0],
            ssm_c_re[0], ssm_c_im[0], ssm_d[0])
    kmem, vmem = _memkv(mem, mem_norm_g[0], xa_wk[0], xa_wv[0])
    x2a = _mid(x2d, S, y_att, z, ssm_glu_w[0], ssm_glu_b[0], att_out_g[0], ssm_out_g[0], w_out[0],
               xa_norm_g[0], xa_wq[0], xa_wo[0], kmem, vmem)
    out = _moe(x2a, D, moe_norm_g[0], router_grp_w[0], router_grp_b[0], router_exp_w[0],
               router_exp_b[0], exp_w_gate[0], exp_w_up[0], exp_w_down[0], final_norm_g)
    return out.reshape(B, S, D)
```

```python
import functools

import numpy as np
import jax
import jax.numpy as jnp
from jax import lax
from jax.experimental import pallas as pl
from jax.experimental.pallas import tpu as pltpu

F32 = jnp.float32
BF16 = jnp.bfloat16

EPS = 1e-6
GRID_W = 64
ROPE_THETA = 10000.0
ATT_HEADS = 8
ATT_KV_HEADS = 2
HEAD_DIM = 64
D_ATT = ATT_HEADS * HEAD_DIM
D_KV = ATT_KV_HEADS * HEAD_DIM
SSM_GROUP = 16
SSM_STATE = 64
XA_HEADS = 4
N_EXPERT_GROUPS = 4
EXPERTS_PER_GROUP = 4
N_EXPERTS = N_EXPERT_GROUPS * EXPERTS_PER_GROUP
PAIRS_PER_GROUP = 6
N_CLASSES = N_EXPERT_GROUPS * PAIRS_PER_GROUP

LANES = 128
SUBLANES = 8
CHUNK = 16
VMEM_LIMIT = 56 * 1024 * 1024

META_CLS, META_RANK, META_WA, META_WB, META_POS = 0, 1, 2, 3, 4


def _cparams(*sem):
    return pltpu.CompilerParams(dimension_semantics=sem, vmem_limit_bytes=VMEM_LIMIT)


def _rms(x):
    return x * lax.rsqrt(jnp.mean(x * x, axis=-1, keepdims=True) + EPS)


def _dot(a, b):
    return jnp.dot(a, b, preferred_element_type=F32)


def _dot_nt(a, b):
    return lax.dot_general(a, b, (((1,), (1,)), ((), ())), preferred_element_type=F32)


def _inproj_body(x_ref, g_ref, w_ref, qkg_ref, ones_ref, cos_ref, sin_ref,
                 q_ref, k_ref, v_ref, u_ref, us):
    n_qk = q_ref.shape[1] + k_ref.shape[1]
    h = _rms(x_ref[...]) * g_ref[...]
    proj = _dot(h.astype(BF16), w_ref[...])
    qk = proj[:, :n_qk]
    ssq = _dot((qk * qk).astype(BF16), ones_ref[...])
    qk = qk * lax.rsqrt(ssq * (1.0 / HEAD_DIM) + EPS) * qkg_ref[...]
    cos = cos_ref[...]
    sin = sin_ref[...]
    lane = lax.broadcasted_iota(jnp.int32, cos.shape, 1)
    first = (lane % 32) < 16
    nq = q_ref.shape[1] // LANES
    for c in range(n_qk // LANES):
        xc = qk[:, c * LANES:(c + 1) * LANES]
        partner = jnp.where(first, pltpu.roll(xc, LANES - 16, 1), pltpu.roll(xc, 16, 1))
        rot = xc * cos + partner * sin
        if c < nq:
            q_ref[:, c * LANES:(c + 1) * LANES] = (rot * (HEAD_DIM ** -0.5)).astype(BF16)
        else:
            k_ref[:, (c - nq) * LANES:(c - nq + 1) * LANES] = rot.astype(BF16)
    nv = n_qk + v_ref.shape[1]
    v_ref[...] = proj[:, n_qk:nv].astype(BF16)
    rows = u_ref.shape[1]
    for j in range(u_ref.shape[0]):
        us[j] = proj[:, nv + j * LANES:nv + (j + 1) * LANES]
        for t in range(CHUNK):
            u_ref[j, :, t * LANES:(t + 1) * LANES] = us[j, pl.ds(t, rows, stride=CHUNK), :].astype(BF16)


def _rope_tables(S):
    t = jnp.arange(S, dtype=jnp.int32)
    freqs = ROPE_THETA ** (-jnp.arange(16, dtype=F32) / 16)
    halves = []
    for pos in (t // GRID_W, t % GRID_W):
        ang = pos.astype(F32)[:, None] * freqs[None, :]
        halves.append((jnp.cos(ang), jnp.sin(ang)))
    cos_h = jnp.concatenate([c for c, _ in halves for _ in range(2)], axis=1)
    sin_h = jnp.concatenate([sg * s for _, s in halves for sg in (-1.0, 1.0)], axis=1)
    return jnp.concatenate([cos_h, cos_h], axis=1), jnp.concatenate([sin_h, sin_h], axis=1)


def _inproj(x2d, S, mix_g, w_in, q_g, k_g):
    T, D = x2d.shape
    tm = 512
    d_ssm = w_in.shape[1] - D_ATT - 2 * D_KV
    wq = w_in[:, :D_ATT]
    wk = w_in[:, D_ATT:D_ATT + D_KV]
    wv = w_in[:, D_ATT + D_KV:D_ATT + 2 * D_KV]
    wu = w_in[:, D_ATT + 2 * D_KV:]
    dup = lambda w: jnp.concatenate([w[:, :64], w[:, :64], w[:, 64:], w[:, 64:]], axis=1)
    w_ext = jnp.concatenate([wq, dup(wk), dup(wv), wu], axis=1).astype(BF16)
    n_qk = D_ATT + 2 * D_KV
    qkg = jnp.concatenate([jnp.tile(q_g, ATT_HEADS), jnp.tile(k_g, 2 * ATT_KV_HEADS)])[None, :]
    hid = np.arange(n_qk) // HEAD_DIM
    ones = jnp.asarray((hid[:, None] == hid[None, :]).astype(np.float32), dtype=BF16)
    cos_t, sin_t = _rope_tables(S)
    sb = S // tm
    const = lambda i: (0, 0)
    return pl.pallas_call(
        _inproj_body,
        grid=(T // tm,),
        in_specs=[
            pl.BlockSpec((tm, D), lambda i: (i, 0)),
            pl.BlockSpec((1, D), const),
            pl.BlockSpec(w_ext.shape, const),
            pl.BlockSpec((1, n_qk), const),
            pl.BlockSpec((n_qk, n_qk), const),
            pl.BlockSpec((tm, LANES), lambda i: (i % sb, 0)),
            pl.BlockSpec((tm, LANES), lambda i: (i % sb, 0)),
        ],
        out_specs=[
            pl.BlockSpec((tm, D_ATT), lambda i: (i, 0)),
            pl.BlockSpec((tm, 2 * D_KV), lambda i: (i, 0)),
            pl.BlockSpec((tm, 2 * D_KV), lambda i: (i, 0)),
            pl.BlockSpec((d_ssm // LANES, tm // CHUNK, CHUNK * LANES), lambda i: (0, i, 0)),
        ],
        out_shape=[
            jax.ShapeDtypeStruct((T, D_ATT), BF16),
            jax.ShapeDtypeStruct((T, 2 * D_KV), BF16),
            jax.ShapeDtypeStruct((T, 2 * D_KV), BF16),
            jax.ShapeDtypeStruct((d_ssm // LANES, T // CHUNK, CHUNK * LANES), BF16),
        ],
        scratch_shapes=[pltpu.VMEM((d_ssm // LANES, tm, LANES), F32)],
        compiler_params=_cparams("arbitrary"),
        name="inproj",
    )(x2d, mix_g[None, :], w_ext, qkg, ones, cos_t, sin_t)


ATT_KEY_CHUNK = 256
ATT_ROW_BLOCK = 256


def _attn_body(q_ref, k_ref, v_ref, o_ref, klo, khi, vlo, vhi, s_scr, e_scr):
    @pl.when(pl.program_id(2) == 0)
    def _():
        lane = lax.broadcasted_iota(jnp.int32, k_ref.shape, 1)
        lo = lane < HEAD_DIM
        kd = k_ref[...]
        vd = v_ref[...]
        zero = jnp.zeros_like(kd)
        klo[...] = jnp.where(lo, kd, zero)
        khi[...] = jnp.where(lo, zero, kd)
        vlo[...] = jnp.where(lo, vd, (lane == HEAD_DIM).astype(BF16))
        vhi[...] = jnp.where(lo, (lane == 0).astype(BF16), vd)

    S = klo.shape[0]
    rb = s_scr.shape[1]
    units = [(r0, p, km, vm) for r0 in range(0, q_ref.shape[0], rb)
             for p in range(2) for km, vm in ((klo, vlo), (khi, vhi))]

    def scores(i):
        r0, p, km, _ = units[i]
        qp = q_ref[r0:r0 + rb, p * LANES:(p + 1) * LANES]
        m = None
        for c in range(0, S, ATT_KEY_CHUNK):
            s = _dot_nt(qp, km[c:c + ATT_KEY_CHUNK, :])
            s_scr[i % 2, :, c:c + ATT_KEY_CHUNK] = s
            for l0 in range(0, ATT_KEY_CHUNK, LANES):
                m = s[:, l0:l0 + LANES] if m is None else jnp.maximum(m, s[:, l0:l0 + LANES])
        return jnp.max(m, axis=-1, keepdims=True)

    def values(i, m):
        vm = units[i][3]
        for c in range(0, S, ATT_KEY_CHUNK):
            e_scr[i % 2, :, c:c + ATT_KEY_CHUNK] = jnp.exp(s_scr[i % 2, :, c:c + ATT_KEY_CHUNK] - m).astype(BF16)
        acc = _dot(e_scr[i % 2], vm[...])
        ones_lane = HEAD_DIM if vm is vlo else 0
        return acc * (1.0 / acc[:, ones_lane:ones_lane + 1])

    lo = lax.broadcasted_iota(jnp.int32, (rb, LANES), 1) < HEAD_DIM
    m_prev = scores(0)
    first = None
    for i, (r0, p, _, vm) in enumerate(units):
        m_next = scores(i + 1) if i + 1 < len(units) else None
        o = values(i, m_prev)
        m_prev = m_next
        if vm is vlo:
            first = o
        else:
            o_ref[r0:r0 + rb, p * LANES:(p + 1) * LANES] = jnp.where(lo, first, o).astype(BF16)


def _attention(q, kd, vd, B, S):
    tq = 2 * ATT_ROW_BLOCK
    q3 = q.reshape(B, S, D_ATT)
    k3 = kd.reshape(B, S, 2 * D_KV)
    v3 = vd.reshape(B, S, 2 * D_KV)
    gw = D_ATT // ATT_KV_HEADS
    out = pl.pallas_call(
        _attn_body,
        grid=(B, ATT_KV_HEADS, S // tq),
        in_specs=[
            pl.BlockSpec((None, tq, gw), lambda b, j, i: (b, i, j)),
            pl.BlockSpec((None, S, LANES), lambda b, j, i: (b, 0, j)),
            pl.BlockSpec((None, S, LANES), lambda b, j, i: (b, 0, j)),
        ],
        out_specs=pl.BlockSpec((None, tq, gw), lambda b, j, i: (b, i, j)),
        out_shape=jax.ShapeDtypeStruct((B, S, D_ATT), BF16),
        scratch_shapes=[pltpu.VMEM((S, LANES), BF16)] * 4
        + [pltpu.VMEM((2, ATT_ROW_BLOCK, S), F32), pltpu.VMEM((2, ATT_ROW_BLOCK, S), BF16)],
        compiler_params=_cparams("arbitrary", "arbitrary", "arbitrary"),
        name="attn",
    )(q3, k3, v3)
    return out.reshape(B * S, D_ATT)


def _dot_tn3(a, b):
    tn = lambda x, y: lax.dot_general(x, y, (((0,), (0,)), ((), ())), preferred_element_type=F32)
    ah, al = _split_bf16(a)
    bh, bl = _split_bf16(b)
    return tn(ah, bh) + tn(al, bh) + tn(ah, bl)


def _s5op_body(lam_ref, bt_ref, ct_ref, m_ref, et_ref, da_ref, db_ref):
    W = m_ref.shape[1]
    L = CHUNK
    step = (lax.broadcasted_iota(jnp.int32, (1, W), 1) // SSM_GROUP).astype(F32)

    def direction(d):
        lam = lam_ref[d]
        lr = jnp.minimum(lam[:, 0:1], -1e-4)
        li = lam[:, 1:2]
        dt = jnp.exp(lam[:, 2:3])
        mag = jnp.exp(lr * dt)
        ab_r = mag * jnp.cos(li * dt)
        ab_i = mag * jnp.sin(li * dt)
        den = lr * lr + li * li
        nr = ab_r - 1.0
        f_r = (nr * lr + ab_i * li) / den
        f_i = (ab_i * lr - nr * li) / den
        bb_r = f_r * bt_ref[d, 0] - f_i * bt_ref[d, 1]
        bb_i = f_r * bt_ref[d, 1] + f_i * bt_ref[d, 0]
        cr, ci = ct_ref[d, 0], ct_ref[d, 1]

        def power(n):
            m = jnp.exp((lr * dt) * n)
            return m * jnp.cos((li * dt) * n), m * jnp.sin((li * dt) * n)

        def a_b(n):
            pr, pi = power(n)
            return pr * bb_r - pi * bb_i, pr * bb_i + pi * bb_r

        def c_a(n):
            pr, pi = power(n)
            return cr * pr - ci * pi, cr * pi + ci * pr

        return bb_r, bb_i, cr, ci, a_b, c_a

    fbr, fbi, _, _, f_ab, f_ca = direction(0)
    bbr, bbi, bcr, bci, b_ab, b_ca = direction(1)
    etf_r, etf_i = f_ab(L - 1.0 - step)
    etb_r, etb_i = b_ab(step)
    et_ref[...] = jnp.concatenate([etf_r, etb_r, etf_i, etb_i], axis=0).astype(BF16)
    df_r, df_i = f_ca(step + 1.0)
    db_r, db_i = b_ca(L - step)
    zero = jnp.zeros_like(df_r)
    da_ref[...] = jnp.concatenate([df_r, zero, -df_i, zero], axis=0).astype(BF16)
    db_ref[...] = jnp.concatenate([zero, db_r, zero, -db_i], axis=0).astype(BF16)
    kf_r, kf_i = f_ca(step)
    lag0 = step == 0.0
    lhs_f = jnp.concatenate([fbr, -fbi], axis=0)
    lhs_b = jnp.concatenate([bbr, -bbi], axis=0)
    q_b = _dot_tn3(lhs_b, jnp.concatenate([db_r, db_i], axis=0))
    q_f = _dot_tn3(jnp.concatenate([lhs_f, lhs_b], axis=0),
                   jnp.concatenate([kf_r, kf_i, jnp.where(lag0, bcr, 0.0), jnp.where(lag0, bci, 0.0)],
                                   axis=0))
    z = jnp.concatenate([q_b, q_f], axis=1)
    for j in range(L):
        rows = z[j * SSM_GROUP:(j + 1) * SSM_GROUP, :]
        m_ref[j * SSM_GROUP:(j + 1) * SSM_GROUP, :] = pltpu.roll(
            rows, (2 * W - (L - j) * SSM_GROUP) % (2 * W), 1)[:, :W].astype(BF16)


def _s5_operators(lam_re, lam_im, log_step, b_re, b_im, c_re, c_im):
    _, G, P = lam_re.shape
    W = CHUNK * SSM_GROUP
    lam = jnp.stack([lam_re, lam_im, jnp.broadcast_to(log_step[..., None], lam_re.shape)], axis=-1).astype(F32)
    tile = lambda a: jnp.tile(a.astype(F32), (1, 1, 1, CHUNK))
    bt = jnp.stack([tile(b_re), tile(b_im)], axis=1)
    ct = jnp.stack([tile(c_re.swapaxes(2, 3)), tile(c_im.swapaxes(2, 3))], axis=1)
    per_g = pl.BlockSpec((2, 2, None, P, W), lambda g: (0, 0, g, 0, 0))
    out = pl.BlockSpec((None, W, W), lambda g: (g, 0, 0))
    return pl.pallas_call(
        _s5op_body,
        grid=(G,),
        in_specs=[pl.BlockSpec((2, None, P, 3), lambda g: (0, g, 0, 0)), per_g, per_g],
        out_specs=[out] * 4,
        out_shape=[jax.ShapeDtypeStruct((G, W, W), BF16)] * 4,
        compiler_params=_cparams("arbitrary"),
        name="s5op",
    )(lam, bt, ct)


GROUPS_PER_SET = LANES // SSM_GROUP
S5_SCAN_GROUPS = 4
S5_PERM_ROWS = 64


def _transpose8(blocks, axis, piece):
    n = blocks[0].shape[axis]
    q = lax.broadcasted_iota(jnp.int32, blocks[0].shape, axis) // piece
    v = list(blocks)
    for d in (4, 2, 1):
        keep = (q & d) == 0
        nxt = list(v)
        for a in range(8):
            if a & d == 0:
                b = a + d
                nxt[a] = jnp.where(keep, v[a], pltpu.roll(v[b], d * piece, axis))
                nxt[b] = jnp.where(keep, pltpu.roll(v[a], n - d * piece, axis), v[b])
        v = nxt
    return v


def _piece_transpose(blocks):
    return _transpose8(blocks, 1, SSM_GROUP)


def _s5_body(u_ref, m_ref, et_ref, da_ref, db_ref, a_ref, d_ref, z_ref,
             uperm, zperm, xr_n, xi_n, xr_t, xi_t, har_s, hai_s, hbr_s, hbi_s, *, nb):
    R = u_ref.shape[0]
    nchunk = R // nb
    W = uperm.shape[2]
    half = W // 2
    halves = W // LANES

    def permute_in(j, _):
        r = pl.ds(pl.multiple_of(j * S5_PERM_ROWS, S5_PERM_ROWS), S5_PERM_ROWS)
        for h in range(halves):
            out = _piece_transpose([u_ref[r, (h * GROUPS_PER_SET + k) * LANES:(h * GROUPS_PER_SET + k + 1) * LANES]
                                    for k in range(GROUPS_PER_SET)])
            for g in range(GROUPS_PER_SET):
                uperm[g, r, h * LANES:(h + 1) * LANES] = out[g]
        return 0

    lax.fori_loop(0, R // S5_PERM_ROWS, permute_in, 0)
    fwd = lax.broadcasted_iota(jnp.int32, (nb, half), 1) < SSM_STATE

    def swap_row_order(src_a, src_b, dst_a, dst_b, to_chunk_major):
        def blocks(cb, _):
            batch_major = [pl.ds(pl.multiple_of(p * nchunk + cb * nb, nb), nb) for p in range(nb)]
            chunk_major = [pl.ds(pl.multiple_of((cb * nb + j) * nb, nb), nb) for j in range(nb)]
            rows_in, rows_out = (batch_major, chunk_major) if to_chunk_major else (chunk_major, batch_major)
            for k in range(S5_SCAN_GROUPS):
                for src, dst in ((src_a, dst_a), (src_b, dst_b)):
                    out = _transpose8([src[k, r, :] for r in rows_in], 0, 1)
                    for j in range(nb):
                        dst[k, rows_out[j], :] = out[j]
            return 0
        lax.fori_loop(0, nchunk // nb, blocks, 0)

    for g0 in range(0, GROUPS_PER_SET, S5_SCAN_GROUPS):
        coef = []
        for k in range(S5_SCAN_GROUPS):
            g = g0 + k
            x = _dot_nt(uperm[g], et_ref[g])
            xr_n[k] = x[:, :half]
            xi_n[k] = x[:, half:]
            lr = jnp.minimum(a_ref[g, 0:1, :], -1e-4)
            dt = jnp.exp(a_ref[g, 2:3, :])
            mag = jnp.exp(lr * dt * CHUNK)
            ang = a_ref[g, 1:2, :] * dt * CHUNK
            coef.append((jnp.broadcast_to(mag * jnp.cos(ang), (nb, half)),
                         jnp.broadcast_to(mag * jnp.sin(ang), (nb, half))))
        swap_row_order(xr_n, xi_n, xr_t, xi_t, True)

        def step(c, carry):
            rf = pl.ds(pl.multiple_of(c * nb, nb), nb)
            rb = pl.ds(pl.multiple_of((nchunk - 1 - c) * nb, nb), nb)
            new = []
            for k in range(S5_SCAN_GROUPS):
                sr, si = carry[2 * k], carry[2 * k + 1]
                ar, ai = coef[k]
                har_s[k, rf, :] = sr
                hai_s[k, rf, :] = si
                hbr_s[k, rb, :] = sr
                hbi_s[k, rb, :] = si
                xr = jnp.where(fwd, xr_t[k, rf, :], xr_t[k, rb, :])
                xi = jnp.where(fwd, xi_t[k, rf, :], xi_t[k, rb, :])
                new += [sr * ar - si * ai + xr, sr * ai + si * ar + xi]
            return tuple(new)

        zero = jnp.zeros((nb, half), F32)
        lax.fori_loop(0, nchunk, step, (zero,) * (2 * S5_SCAN_GROUPS))
        for k in range(S5_SCAN_GROUPS):
            g = g0 + k
            ha = jnp.concatenate([har_s[k], hai_s[k]], axis=1).astype(BF16)
            hb = jnp.concatenate([hbr_s[k], hbi_s[k]], axis=1).astype(BF16)
            y_t = _dot(ha, da_ref[g]) + _dot(hb, db_ref[g])
            xr_n[k] = y_t[:, :half]
            xi_n[k] = y_t[:, half:]
        swap_row_order(xr_n, xi_n, xr_t, xi_t, False)
        for k in range(S5_SCAN_GROUPS):
            g = g0 + k
            u = uperm[g]
            y = _dot(u, m_ref[g]) + jnp.concatenate([xr_t[k], xi_t[k]], axis=1) + d_ref[g] * u.astype(F32)
            zperm[g] = jax.nn.gelu(y).astype(BF16)

    def permute_out(j, _):
        r = pl.ds(pl.multiple_of(j * S5_PERM_ROWS, S5_PERM_ROWS), S5_PERM_ROWS)
        for h in range(halves):
            out = _piece_transpose([zperm[g, r, h * LANES:(h + 1) * LANES] for g in range(GROUPS_PER_SET)])
            for k in range(GROUPS_PER_SET):
                z_ref[r, (h * GROUPS_PER_SET + k) * LANES:(h * GROUPS_PER_SET + k + 1) * LANES] = out[k]
        return 0

    lax.fori_loop(0, R // S5_PERM_ROWS, permute_out, 0)


def _s5(u_set, B, S, lam_re, lam_im, log_step, b_re, b_im, c_re, c_im, ssm_d):
    n_set, R, WS = u_set.shape
    G = n_set * GROUPS_PER_SET
    W = CHUNK * SSM_GROUP
    assert R == (S // CHUNK) * B and WS == CHUNK * LANES
    M, ET, DA, DB = _s5_operators(lam_re, lam_im, log_step, b_re, b_im, c_re, c_im)
    both = lambda a: jnp.concatenate([a[0], a[1]], axis=-1)
    lam = jnp.stack([both(lam_re), both(lam_im),
                     both(jnp.broadcast_to(log_step[..., None], lam_re.shape))], axis=1).astype(F32)
    dvec = jnp.tile(ssm_d.astype(F32).reshape(G, 1, SSM_GROUP), (1, CHUNK, 1)).reshape(G, 1, W)
    per_set = pl.BlockSpec((None, R, WS), lambda s: (s, 0, 0))
    blk = lambda *sh: pl.BlockSpec((GROUPS_PER_SET,) + sh, lambda s: (s,) + (0,) * len(sh))
    return pl.pallas_call(
        functools.partial(_s5_body, nb=B),
        grid=(n_set,),
        in_specs=[per_set, blk(W, W), blk(W, W), blk(W, W), blk(W, W), blk(3, W // 2), blk(1, W)],
        out_specs=per_set,
        out_shape=jax.ShapeDtypeStruct((n_set, R, WS), BF16),
        scratch_shapes=[pltpu.VMEM((GROUPS_PER_SET, R, W), BF16)] * 2
        + [pltpu.VMEM((S5_SCAN_GROUPS, R, W // 2), F32)] * 8,
        compiler_params=_cparams("arbitrary"),
        name="s5",
    )(u_set, M, ET, DA, DB, lam, dvec)


def _memkv_body(m_ref, g_ref, wk_ref, wv_ref, k_ref, v_ref):
    h = (_rms(m_ref[...]) * g_ref[...]).astype(BF16)
    k_ref[...] = _dot(h, wk_ref[...]).astype(BF16)
    v_ref[...] = _dot(h, wv_ref[...]).astype(BF16)


def _memkv(mem, g, wk, wv):
    B, M, D = mem.shape
    const = lambda b: (0, 0)
    row = pl.BlockSpec((None, M, D), lambda b: (b, 0, 0))
    return pl.pallas_call(
        _memkv_body,
        grid=(B,),
        in_specs=[row, pl.BlockSpec((1, D), const), pl.BlockSpec((D, D), const),
                  pl.BlockSpec((D, D), const)],
        out_specs=[row, row],
        out_shape=[jax.ShapeDtypeStruct((B, M, D), BF16)] * 2,
        compiler_params=_cparams("arbitrary"),
        name="memkv",
    )(mem, g[None, :], wk.astype(BF16), wv.astype(BF16))


def _mid_body(x_ref, ya_ref, z_ref, gw_ref, gb_ref, ag_ref, sg_ref, wo_ref, xg_ref,
              wq_ref, xwo_ref, k_ref, v_ref, o_ref, zs):
    D = x_ref.shape[1]
    rows = z_ref.shape[1]
    for j in range(z_ref.shape[0]):
        for t in range(CHUNK):
            zs[j, pl.ds(t, rows, stride=CHUNK), :] = z_ref[j, :, t * LANES:(t + 1) * LANES].astype(F32)
    z = jnp.concatenate([zs[j] for j in range(z_ref.shape[0])], axis=1).astype(BF16)
    s = z.astype(F32) * jax.nn.sigmoid(_dot(z, gw_ref[...]) + gb_ref[...])
    ya = _rms(ya_ref[...].astype(F32)) * ag_ref[...]
    ys = _rms(s) * sg_ref[...]
    y = jnp.concatenate([ya, ys], axis=1).astype(BF16)
    x1 = x_ref[...] + _dot(y, wo_ref[...])
    hq = (_rms(x1) * xg_ref[...]).astype(BF16)
    dh = D // XA_HEADS
    q = (_dot(hq, wq_ref[...]) * (dh ** -0.5)).astype(BF16)
    outs = []
    for h in range(XA_HEADS):
        sl = slice(h * dh, (h + 1) * dh)
        sc = _dot_nt(q[:, sl], k_ref[:, sl])
        e = jnp.exp(sc - jnp.max(sc, axis=-1, keepdims=True))
        inv = 1.0 / jnp.sum(e, axis=-1, keepdims=True)
        outs.append((_dot(e.astype(BF16), v_ref[:, sl]) * inv).astype(BF16))
    o = jnp.concatenate(outs, axis=1)
    o_ref[:, :D] = x1 + _dot(o, xwo_ref[...])
    o_ref[:, D:] = jnp.zeros((o_ref.shape[0], o_ref.shape[1] - D), F32)


def _mid(x2d, S, y_att, z, glu_w, glu_b, att_g, ssm_g, w_out, xa_g, wq, wo, kmem, vmem):
    T, D = x2d.shape
    M = kmem.shape[1]
    tm = 256
    sb = S // tm
    const = lambda i: (0, 0)
    tok = lambda w: pl.BlockSpec((tm, w), lambda i: (i, 0))
    full = lambda a: pl.BlockSpec(a.shape, const)
    vec = lambda a: a.astype(F32)[None, :]
    mem = pl.BlockSpec((None, M, D), lambda i: (i // sb, 0, 0))
    args = [x2d, y_att, z, glu_w.astype(BF16), vec(glu_b), vec(att_g), vec(ssm_g), w_out.astype(BF16),
            vec(xa_g), wq.astype(BF16), wo.astype(BF16), kmem, vmem]
    zspec = pl.BlockSpec((z.shape[0], tm // CHUNK, CHUNK * LANES), lambda i: (0, i, 0))
    specs = [tok(D), tok(y_att.shape[1]), zspec] + [full(a) for a in args[3:11]] + [mem, mem]
    return pl.pallas_call(
        _mid_body,
        grid=(T // tm,),
        in_specs=specs,
        out_specs=pl.BlockSpec((tm, D + LANES), lambda i: (i, 0)),
        out_shape=jax.ShapeDtypeStruct((T, D + LANES), F32),
        scratch_shapes=[pltpu.VMEM((z.shape[0], tm, LANES), F32)],
        compiler_params=_cparams("arbitrary"),
        name="mid",
    )(*args)


def _split_bf16(a):
    hi = a.astype(BF16)
    return hi, (a - hi.astype(F32)).astype(BF16)


ROUTE_ROWS = 32
MOE_TILE = 256


def _route_body(x_ref, g_ref, whi_ref, wlo_ref, b_ref, tri_ref, meta_ref, mt_ref, cnt_ref, carry):
    i = pl.program_id(0)
    tm, D = x_ref.shape[0], g_ref.shape[1]

    @pl.when(i == 0)
    def _():
        carry[...] = jnp.zeros_like(carry)

    h = _rms(x_ref[:, :D]) * g_ref[...]
    hhi, hlo = _split_bf16(h)
    logits = (_dot_nt(whi_ref[...], hhi) + _dot_nt(whi_ref[...], hlo) + _dot_nt(wlo_ref[...], hhi)
              + b_ref[:, 0:1])
    ninf = jnp.float32(-jnp.inf)
    cmax = lambda v: jnp.max(v, axis=0, keepdims=True)
    cmin = lambda v: jnp.min(v, axis=0, keepdims=True)
    gl = logits[N_EXPERTS:N_EXPERTS + N_EXPERT_GROUPS]
    grow = lax.broadcasted_iota(jnp.int32, gl.shape, 0)
    gmax = cmax(gl)
    g_sel = cmin(jnp.where(gl == gmax, grow, N_EXPERT_GROUPS))
    g_w = 1.0 / jnp.sum(jnp.exp(gl - gmax), axis=0, keepdims=True)
    el = logits[:N_EXPERTS]
    erow = lax.broadcasted_iota(jnp.int32, el.shape, 0)
    el = jnp.where((erow >> 2) == g_sel, el, ninf)
    v1 = cmax(el)
    i1 = cmin(jnp.where(el == v1, erow, N_EXPERTS))
    el2 = jnp.where(erow == i1, ninf, el)
    v2 = cmax(el2)
    i2 = cmin(jnp.where(el2 == v2, erow, N_EXPERTS))
    t = jnp.exp(v2 - v1)
    w1 = g_w / (1.0 + t)
    w2 = g_w * t / (1.0 + t)
    first_low = i1 < i2
    ia = jnp.minimum(i1, i2) & 3
    ib = jnp.maximum(i1, i2) & 3
    wa = jnp.where(first_low, w1, w2)
    wb = jnp.where(first_low, w2, w1)
    cls = g_sel * PAIRS_PER_GROUP + ((ia * (7 - ia)) >> 1) + ib - ia - 1
    crow = lax.broadcasted_iota(jnp.int32, (ROUTE_ROWS, tm), 0)
    onehot = (crow == cls).astype(F32)
    before = _dot(onehot.astype(BF16), tri_ref[...])
    rank = jnp.sum(onehot * (before + carry[:, 0:1]), axis=0, keepdims=True)
    carry[...] = carry[...] + jnp.sum(onehot, axis=1, keepdims=True)
    cnt_ref[...] = carry[...]

    def record(rows):
        r = lax.broadcasted_iota(jnp.int32, (rows, tm), 0)
        return jnp.where(r == META_CLS, cls.astype(F32),
                         jnp.where(r == META_RANK, rank,
                                   jnp.where(r == META_WA, wa, jnp.where(r == META_WB, wb, 0.0))))

    meta_ref[...] = record(LANES).T
    mt_ref[:, pl.ds(pl.multiple_of(i * tm, tm), tm)] = record(mt_ref.shape[0])

    @pl.when(i == pl.num_programs(0) - 1)
    def _():
        ntile = jnp.floor((carry[...] + (MOE_TILE - 1.0)) * (1.0 / MOE_TILE))
        r = lax.broadcasted_iota(jnp.int32, (ROUTE_ROWS, ROUTE_ROWS), 0)
        c = lax.broadcasted_iota(jnp.int32, (ROUTE_ROWS, ROUTE_ROWS), 1)
        off = _dot((c < r).astype(BF16), ntile.astype(BF16)) * MOE_TILE
        cls_all = mt_ref[META_CLS:META_CLS + 1, :]
        pos = mt_ref[META_RANK:META_RANK + 1, :]
        for k in range(N_CLASSES):
            pos = pos + jnp.where(cls_all == k, off[k:k + 1, 0:1], 0.0)
        mt_ref[META_POS:META_POS + 1, :] = pos


def _route(x2a, D, moe_g, grp_w, grp_b, exp_w, exp_b):
    T = x2a.shape[0]
    tm = 256
    pad = ROUTE_ROWS - N_EXPERT_GROUPS - N_EXPERTS
    w = jnp.concatenate([exp_w, grp_w, jnp.zeros((D, pad), F32)], axis=1).astype(F32).T
    b = jnp.concatenate([exp_b, grp_b, jnp.zeros((pad,), F32)]).astype(F32)
    whi, wlo = _split_bf16(w)
    tri = jnp.asarray(np.arange(tm)[:, None] < np.arange(tm)[None, :], dtype=BF16)
    const = lambda i: (0, 0)
    ncol = D // LANES
    return pl.pallas_call(
        _route_body,
        grid=(T // tm,),
        in_specs=[pl.BlockSpec((tm, D + LANES), lambda i: (i, 0)),
                  pl.BlockSpec((1, D), const), pl.BlockSpec((ROUTE_ROWS, D), const),
                  pl.BlockSpec((ROUTE_ROWS, D), const), pl.BlockSpec((ROUTE_ROWS, LANES), const),
                  pl.BlockSpec((tm, tm), const)],
        out_specs=[pl.BlockSpec((tm, LANES), lambda i: (i, ncol)),
                   pl.BlockSpec((SUBLANES, T), const),
                   pl.BlockSpec((ROUTE_ROWS, LANES), const)],
        out_shape=[jax.ShapeDtypeStruct(x2a.shape, F32), jax.ShapeDtypeStruct((SUBLANES, T), F32),
                   jax.ShapeDtypeStruct((ROUTE_ROWS, LANES), F32)],
        scratch_shapes=[pltpu.VMEM((ROUTE_ROWS, LANES), F32)],
        input_output_aliases={0: 0},
        compiler_params=_cparams("arbitrary"),
        name="route",
    )(x2a, moe_g.astype(F32)[None, :], whi, wlo, jnp.broadcast_to(b[:, None], (ROUTE_ROWS, LANES)), tri)


ROW_COPY_UNROLL = 8


def _row_copy_body(pos_ref, src_ref, *rest, rows, scatter):
    dst_ref, sem = rest[-2:]

    def copy(r):
        p = pos_ref[r]
        if scatter:
            return pltpu.make_async_copy(src_ref.at[pl.ds(r, 1)], dst_ref.at[pl.ds(p, 1)], sem)
        return pltpu.make_async_copy(src_ref.at[pl.ds(p, 1)], dst_ref.at[pl.ds(r, 1)], sem)

    def start(j, _):
        for k in range(ROW_COPY_UNROLL):
            copy(j * ROW_COPY_UNROLL + k).start(priority=k % 2)
        return 0

    def wait(j, _):
        for k in range(ROW_COPY_UNROLL):
            copy(j * ROW_COPY_UNROLL + k).wait()
        return 0

    lax.fori_loop(0, rows // ROW_COPY_UNROLL, start, 0)
    lax.fori_loop(0, rows // ROW_COPY_UNROLL, wait, 0)


def _row_copy(pos, src, *, dst_init=None, dst_rows=None, name):
    T = pos.shape[0]
    rows = 512
    scatter = dst_init is not None
    hbm = pl.BlockSpec(memory_space=pl.ANY)
    tok = pl.BlockSpec((rows, src.shape[1]), lambda i: (i, 0))
    out_rows = dst_init.shape[0] if scatter else dst_rows
    return pl.pallas_call(
        functools.partial(_row_copy_body, rows=rows, scatter=scatter),
        grid=(T // rows,),
        in_specs=[pl.BlockSpec((rows,), lambda i: (i,), memory_space=pltpu.SMEM)]
        + ([tok, hbm] if scatter else [hbm]),
        out_specs=hbm if scatter else tok,
        out_shape=jax.ShapeDtypeStruct((out_rows, src.shape[1]), src.dtype),
        scratch_shapes=[pltpu.SemaphoreType.DMA(())],
        input_output_aliases={2: 0} if scatter else {},
        compiler_params=pltpu.CompilerParams(dimension_semantics=("arbitrary",)),
        name=name,
    )(*((pos, src, dst_init) if scatter else (pos, src)))


def _ffn_body(ea_ref, eb_ref, nt_ref, xs_ref, g_ref, fg_ref, wga, wua, wda, wgb, wub, wdb, o_ref):
    del ea_ref, eb_ref
    D = g_ref.shape[1]
    valid = pl.program_id(0) < nt_ref[0]

    @pl.when(valid)
    def _():
        x = xs_ref[:, :D]
        meta = xs_ref[:, D:]
        h = (_rms(x) * g_ref[...]).astype(BF16)
        y = x
        for wg, wu, wd, lane in ((wga, wua, wda, META_WA), (wgb, wub, wdb, META_WB)):
            a = jax.nn.silu(_dot(h, wg[...])) * _dot(h, wu[...]) * meta[:, lane:lane + 1]
            y = y + _dot(a.astype(BF16), wd[...])
        o_ref[...] = _rms(y) * fg_ref[...]

    @pl.when(jnp.logical_not(valid))
    def _():
        o_ref[...] = jnp.zeros_like(o_ref)


def _ffn(xs, D, tile_ea, tile_eb, n_tiles, moe_g, final_g, w_gate, w_up, w_down):
    tm = MOE_TILE
    P = xs.shape[0]
    nt_max = P // tm
    De = w_gate.shape[2]
    wg = w_gate.astype(BF16)
    wu = w_up.astype(BF16)
    wd = w_down.astype(BF16)
    const = lambda i, ea, eb, nt: (0, 0)
    sel_a = lambda i, ea, eb, nt: (ea[i], 0, 0)
    sel_b = lambda i, ea, eb, nt: (eb[i], 0, 0)
    up = lambda sel: pl.BlockSpec((None, D, De), sel)
    down = lambda sel: pl.BlockSpec((None, De, D), sel)
    grid_spec = pltpu.PrefetchScalarGridSpec(
        num_scalar_prefetch=3,
        grid=(nt_max,),
        in_specs=[pl.BlockSpec((tm, D + LANES),
                               lambda i, ea, eb, nt: (jnp.maximum(jnp.minimum(i, nt[0] - 1), 0), 0)),
                  pl.BlockSpec((1, D), const), pl.BlockSpec((1, D), const),
                  up(sel_a), up(sel_a), down(sel_a), up(sel_b), up(sel_b), down(sel_b)],
        out_specs=pl.BlockSpec((tm, D), lambda i, ea, eb, nt: (i, 0)),
    )
    return pl.pallas_call(
        _ffn_body,
        grid_spec=grid_spec,
        out_shape=jax.ShapeDtypeStruct((P, D), F32),
        compiler_params=_cparams("arbitrary"),
        name="ffn",
    )(tile_ea, tile_eb, n_tiles, xs, moe_g.astype(F32)[None, :], final_g.astype(F32)[None, :],
      wg, wu, wd, wg, wu, wd)


def _moe(x2a, D, moe_g, grp_w, grp_b, exp_w, exp_b, w_gate, w_up, w_down, final_g):
    T = x2a.shape[0]
    tm = MOE_TILE
    x2a, record, counts = _route(x2a, D, moe_g, grp_w, grp_b, exp_w, exp_b)
    pos = record[META_POS].astype(jnp.int32)
    cnt = counts[:N_CLASSES, 0].astype(jnp.int32)
    ntile_c = (cnt + tm - 1) // tm
    tile_end = jnp.cumsum(ntile_c)
    nt_max = T // tm + N_CLASSES
    n_tiles = tile_end[-1:]
    tile_id = jnp.arange(nt_max, dtype=jnp.int32)
    tile_cls = jnp.minimum(jnp.sum((tile_end[None, :] <= tile_id[:, None]).astype(jnp.int32), axis=1),
                           N_CLASSES - 1)
    tile_cls = jnp.where(tile_id < n_tiles[0], tile_cls, tile_cls[jnp.maximum(n_tiles[0] - 1, 0)])
    pa =jnp.asarray([0, 0, 0, 1, 1, 2], jnp.int32)
    pb = jnp.asarray([1, 2, 3, 2, 3, 3], jnp.int32)
    grp = tile_cls // PAIRS_PER_GROUP
    tile_ea = grp * EXPERTS_PER_GROUP + pa[tile_cls % PAIRS_PER_GROUP]
    tile_eb = grp * EXPERTS_PER_GROUP + pb[tile_cls % PAIRS_PER_GROUP]
    xs = _row_copy(pos, x2a, dst_init=jnp.zeros((nt_max * tm, D + LANES), F32), name="dispatch")
    ys = _ffn(xs, D, tile_ea, tile_eb, n_tiles.astype(jnp.int32), moe_g, final_g, w_gate, w_up, w_down)
    return _row_copy(pos, ys, dst_rows=T, name="unsort")


def kernel(x, mem, mix_norm_g, w_in, q_norm_g, k_norm_g, ssm_lambda_re, ssm_lambda_im, ssm_log_step, ssm_b_re, ssm_b_im, ssm_c_re, ssm_c_im, ssm_d, ssm_glu_w, ssm_glu_b, att_out_g, ssm_out_g, w_out, xa_norm_g, mem_norm_g, xa_wq, xa_wk, xa_wv, xa_wo, moe_norm_g, router_grp_w, router_grp_b, router_exp_w, router_exp_b, exp_w_gate, exp_w_up, exp_w_down, final_norm_g):
    B, S, D = x.shape
    assert mix_norm_g.shape[0] == 1, "single layer"
    x2d = x.reshape(B * S, D)
    q, kd, vd, u = _inproj(x2d, S, mix_norm_g[0], w_in[0], q_norm_g[0], k_norm_g[0])
    y_att = _attention(q, kd, vd, B, S)
    z = _s5(u, B, S, ssm_lambda_re[0], ssm_lambda_im[0], ssm_log_step[0], ssm_b_re[0], ssm_b_im[0],
            ssm_c_re[0], ssm_c_im[0], ssm_d[0])
    kmem, vmem = _memkv(mem, mem_norm_g[0], xa_wk[0], xa_wv[0])
    x2a = _mid(x2d, S, y_att, z, ssm_glu_w[0], ssm_glu_b[0], att_out_g[0], ssm_out_g[0], w_out[0],
               xa_norm_g[0], xa_wq[0], xa_wo[0], kmem, vmem)
    out = _moe(x2a, D, moe_norm_g[0], router_grp_w[0], router_grp_b[0], router_exp_w[0],
               router_exp_b[0], exp_w_gate[0], exp_w_up[0], exp_w_down[0], final_norm_g)
    return out.reshape(B, S, D)
```

```python
import functools

import numpy as np
import jax
import jax.numpy as jnp
from jax import lax
from jax.experimental import pallas as pl
from jax.experimental.pallas import tpu as pltpu

F32 = jnp.float32
BF16 = jnp.bfloat16

EPS = 1e-6
GRID_W = 64
ROPE_THETA = 10000.0
ATT_HEADS = 8
ATT_KV_HEADS = 2
HEAD_DIM = 64
D_ATT = ATT_HEADS * HEAD_DIM
D_KV = ATT_KV_HEADS * HEAD_DIM
SSM_GROUP = 16
SSM_STATE = 64
XA_HEADS = 4
N_EXPERT_GROUPS = 4
EXPERTS_PER_GROUP = 4
N_EXPERTS = N_EXPERT_GROUPS * EXPERTS_PER_GROUP
PAIRS_PER_GROUP = 6
N_CLASSES = N_EXPERT_GROUPS * PAIRS_PER_GROUP

LANES = 128
SUBLANES = 8
CHUNK = 16
VMEM_LIMIT = 56 * 1024 * 1024

META_CLS, META_RANK, META_WA, META_WB, META_POS = 0, 1, 2, 3, 4


def _cparams(*sem):
    return pltpu.CompilerParams(dimension_semantics=sem, vmem_limit_bytes=VMEM_LIMIT)


def _rms(x):
    return x * lax.rsqrt(jnp.mean(x * x, axis=-1, keepdims=True) + EPS)


def _dot(a, b):
    return jnp.dot(a, b, preferred_element_type=F32)


def _dot_nt(a, b):
    return lax.dot_general(a, b, (((1,), (1,)), ((), ())), preferred_element_type=F32)


def _inproj_body(x_ref, g_ref, w_ref, qkg_ref, ones_ref, cos_ref, sin_ref,
                 q_ref, k_ref, v_ref, u_ref, us):
    n_qk = q_ref.shape[1] + k_ref.shape[1]
    h = _rms(x_ref[...]) * g_ref[...]
    proj = _dot(h.astype(BF16), w_ref[...])
    qk = proj[:, :n_qk]
    ssq = _dot((qk * qk).astype(BF16), ones_ref[...])
    qk = qk * lax.rsqrt(ssq * (1.0 / HEAD_DIM) + EPS) * qkg_ref[...]
    cos = cos_ref[...]
    sin = sin_ref[...]
    lane = lax.broadcasted_iota(jnp.int32, cos.shape, 1)
    first = (lane % 32) < 16
    nq = q_ref.shape[1] // LANES
    for c in range(n_qk // LANES):
        xc = qk[:, c * LANES:(c + 1) * LANES]
        partner = jnp.where(first, pltpu.roll(xc, LANES - 16, 1), pltpu.roll(xc, 16, 1))
        rot = xc * cos + partner * sin
        if c < nq:
            q_ref[:, c * LANES:(c + 1) * LANES] = (rot * (HEAD_DIM ** -0.5)).astype(BF16)
        else:
            k_ref[:, (c - nq) * LANES:(c - nq + 1) * LANES] = rot.astype(BF16)
    nv = n_qk + v_ref.shape[1]
    v_ref[...] = proj[:, n_qk:nv].astype(BF16)
    rows = u_ref.shape[1]
    for j in range(u_ref.shape[0]):
        us[j] = proj[:, nv + j * LANES:nv + (j + 1) * LANES]
        for t in range(CHUNK):
            u_ref[j, :, t * LANES:(t + 1) * LANES] = us[j, pl.ds(t, rows, stride=CHUNK), :].astype(BF16)


def _rope_tables(S):
    t = jnp.arange(S, dtype=jnp.int32)
    freqs = ROPE_THETA ** (-jnp.arange(16, dtype=F32) / 16)
    halves = []
    for pos in (t // GRID_W, t % GRID_W):
        ang = pos.astype(F32)[:, None] * freqs[None, :]
        halves.append((jnp.cos(ang), jnp.sin(ang)))
    cos_h = jnp.concatenate([c for c, _ in halves for _ in range(2)], axis=1)
    sin_h = jnp.concatenate([sg * s for _, s in halves for sg in (-1.0, 1.0)], axis=1)
    return jnp.concatenate([cos_h, cos_h], axis=1), jnp.concatenate([sin_h, sin_h], axis=1)


def _inproj(x2d, S, mix_g, w_in, q_g, k_g):
    T, D = x2d.shape
    tm = 512
    d_ssm = w_in.shape[1] - D_ATT - 2 * D_KV
    wq = w_in[:, :D_ATT]
    wk = w_in[:, D_ATT:D_ATT + D_KV]
    wv = w_in[:, D_ATT + D_KV:D_ATT + 2 * D_KV]
    wu = w_in[:, D_ATT + 2 * D_KV:]
    dup = lambda w: jnp.concatenate([w[:, :64], w[:, :64], w[:, 64:], w[:, 64:]], axis=1)
    w_ext = jnp.concatenate([wq, dup(wk), dup(wv), wu], axis=1).astype(BF16)
    n_qk = D_ATT + 2 * D_KV
    qkg = jnp.concatenate([jnp.tile(q_g, ATT_HEADS), jnp.tile(k_g, 2 * ATT_KV_HEADS)])[None, :]
    hid = np.arange(n_qk) // HEAD_DIM
    ones = jnp.asarray((hid[:, None] == hid[None, :]).astype(np.float32), dtype=BF16)
    cos_t, sin_t = _rope_tables(S)
    sb = S // tm
    const = lambda i: (0, 0)
    return pl.pallas_call(
        _inproj_body,
        grid=(T // tm,),
        in_specs=[
            pl.BlockSpec((tm, D), lambda i: (i, 0)),
            pl.BlockSpec((1, D), const),
            pl.BlockSpec(w_ext.shape, const),
            pl.BlockSpec((1, n_qk), const),
            pl.BlockSpec((n_qk, n_qk), const),
            pl.BlockSpec((tm, LANES), lambda i: (i % sb, 0)),
            pl.BlockSpec((tm, LANES), lambda i: (i % sb, 0)),
        ],
        out_specs=[
            pl.BlockSpec((tm, D_ATT), lambda i: (i, 0)),
            pl.BlockSpec((tm, 2 * D_KV), lambda i: (i, 0)),
            pl.BlockSpec((tm, 2 * D_KV), lambda i: (i, 0)),
            pl.BlockSpec((d_ssm // LANES, tm // CHUNK, CHUNK * LANES), lambda i: (0, i, 0)),
        ],
        out_shape=[
            jax.ShapeDtypeStruct((T, D_ATT), BF16),
            jax.ShapeDtypeStruct((T, 2 * D_KV), BF16),
            jax.ShapeDtypeStruct((T, 2 * D_KV), BF16),
            jax.ShapeDtypeStruct((d_ssm // LANES, T // CHUNK, CHUNK * LANES), BF16),
        ],
        scratch_shapes=[pltpu.VMEM((d_ssm // LANES, tm, LANES), F32)],
        compiler_params=_cparams("arbitrary"),
        name="inproj",
    )(x2d, mix_g[None, :], w_ext, qkg, ones, cos_t, sin_t)


ATT_KEY_CHUNK = 256
ATT_ROW_BLOCK = 256


def _attn_body(q_ref, k_ref, v_ref, o_ref, klo, khi, vlo, vhi, s_scr, e_scr):
    @pl.when(pl.program_id(2) == 0)
    def _():
        lane = lax.broadcasted_iota(jnp.int32, k_ref.shape, 1)
        lo = lane < HEAD_DIM
        kd = k_ref[...]
        vd = v_ref[...]
        zero = jnp.zeros_like(kd)
        klo[...] = jnp.where(lo, kd, zero)
        khi[...] = jnp.where(lo, zero, kd)
        vlo[...] = jnp.where(lo, vd, (lane == HEAD_DIM).astype(BF16))
        vhi[...] = jnp.where(lo, (lane == 0).astype(BF16), vd)

    S = klo.shape[0]
    rb = s_scr.shape[1]
    units = [(r0, p, km, vm) for r0 in range(0, q_ref.shape[0], rb)
             for p in range(2) for km, vm in ((klo, vlo), (khi, vhi))]

    def scores(i):
        r0, p, km, _ = units[i]
        qp = q_ref[r0:r0 + rb, p * LANES:(p + 1) * LANES]
        m = None
        for c in range(0, S, ATT_KEY_CHUNK):
            s = _dot_nt(qp, km[c:c + ATT_KEY_CHUNK, :])
            s_scr[i % 2, :, c:c + ATT_KEY_CHUNK] = s
            for l0 in range(0, ATT_KEY_CHUNK, LANES):
                m = s[:, l0:l0 + LANES] if m is None else jnp.maximum(m, s[:, l0:l0 + LANES])
        return jnp.max(m, axis=-1, keepdims=True)

    def values(i, m):
        vm = units[i][3]
        for c in range(0, S, ATT_KEY_CHUNK):
            e_scr[i % 2, :, c:c + ATT_KEY_CHUNK] = jnp.exp(s_scr[i % 2, :, c:c + ATT_KEY_CHUNK] - m).astype(BF16)
        acc = _dot(e_scr[i % 2], vm[...])
        ones_lane = HEAD_DIM if vm is vlo else 0
        return acc * (1.0 / acc[:, ones_lane:ones_lane + 1])

    lo = lax.broadcasted_iota(jnp.int32, (rb, LANES), 1) < HEAD_DIM
    m_prev = scores(0)
    first = None
    for i, (r0, p, _, vm) in enumerate(units):
        m_next = scores(i + 1) if i + 1 < len(units) else None
        o = values(i, m_prev)
        m_prev = m_next
        if vm is vlo:
            first = o
        else:
            o_ref[r0:r0 + rb, p * LANES:(p + 1) * LANES] = jnp.where(lo, first, o).astype(BF16)


def _attention(q, kd, vd, B, S):
    tq = 2 * ATT_ROW_BLOCK
    q3 = q.reshape(B, S, D_ATT)
    k3 = kd.reshape(B, S, 2 * D_KV)
    v3 = vd.reshape(B, S, 2 * D_KV)
    gw = D_ATT // ATT_KV_HEADS
    out = pl.pallas_call(
        _attn_body,
        grid=(B, ATT_KV_HEADS, S // tq),
        in_specs=[
            pl.BlockSpec((None, tq, gw), lambda b, j, i: (b, i, j)),
            pl.BlockSpec((None, S, LANES), lambda b, j, i: (b, 0, j)),
            pl.BlockSpec((None, S, LANES), lambda b, j, i: (b, 0, j)),
        ],
        out_specs=pl.BlockSpec((None, tq, gw), lambda b, j, i: (b, i, j)),
        out_shape=jax.ShapeDtypeStruct((B, S, D_ATT), BF16),
        scratch_shapes=[pltpu.VMEM((S, LANES), BF16)] * 4
        + [pltpu.VMEM((2, ATT_ROW_BLOCK, S), F32), pltpu.VMEM((2, ATT_ROW_BLOCK, S), BF16)],
        compiler_params=_cparams("arbitrary", "arbitrary", "arbitrary"),
        name="attn",
    )(q3, k3, v3)
    return out.reshape(B * S, D_ATT)


def _dot_tn3(a, b):
    tn = lambda x, y: lax.dot_general(x, y, (((0,), (0,)), ((), ())), preferred_element_type=F32)
    ah, al = _split_bf16(a)
    bh, bl = _split_bf16(b)
    return tn(ah, bh) + tn(al, bh) + tn(ah, bl)


def _s5op_body(lam_ref, bt_ref, ct_ref, m_ref, et_ref, da_ref, db_ref):
    W = m_ref.shape[1]
    L = CHUNK
    step = lax.broadcasted_iota(jnp.int32, (1, W), 1) // SSM_GROUP

    def direction(d):
        lam = lam_ref[d]
        lr = jnp.minimum(lam[:, 0:1], -1e-4)
        li = lam[:, 1:2]
        dt = jnp.exp(lam[:, 2:3])
        mag = jnp.exp(lr * dt)
        ab_r = mag * jnp.cos(li * dt)
        ab_i = mag * jnp.sin(li * dt)
        den = lr * lr + li * li
        nr = ab_r - 1.0
        f_r = (nr * lr + ab_i * li) / den
        f_i = (ab_i * lr - nr * li) / den
        bb_r = f_r * bt_ref[d, 0] - f_i * bt_ref[d, 1]
        bb_i = f_r * bt_ref[d, 1] + f_i * bt_ref[d, 0]
        cr, ci = ct_ref[d, 0], ct_ref[d, 1]

        def power(n):
            sq_r, sq_i = ab_r, ab_i
            pr = pi = None
            for k in range(CHUNK.bit_length() - 1):
                bit = ((n >> k) & 1) == 1
                tr = jnp.where(bit, sq_r, 1.0)
                ti = jnp.where(bit, sq_i, 0.0)
                pr, pi = (tr, ti) if pr is None else (pr * tr - pi * ti, pr * ti + pi * tr)
                sq_r, sq_i = sq_r * sq_r - sq_i * sq_i, 2.0 * sq_r * sq_i
            return pr, pi

        def times_lam(p):
            return p[0] * ab_r - p[1] * ab_i, p[0] * ab_i + p[1] * ab_r

        def a_b(p):
            return p[0] * bb_r - p[1] * bb_i, p[0] * bb_i + p[1] * bb_r

        def c_a(p):
            return cr * p[0] - ci * p[1], cr * p[1] + ci * p[0]

        up, down = power(step), power(L - 1 - step)
        return bb_r, bb_i, cr, ci, a_b, c_a, up, down, times_lam

    fbr, fbi, _, _, f_ab, f_ca, f_up, f_down, f_next = direction(0)
    bbr, bbi, bcr, bci, b_ab, b_ca, b_up, b_down, b_next = direction(1)
    etf_r, etf_i = f_ab(f_down)
    etb_r, etb_i = b_ab(b_up)
    et_ref[...] = jnp.concatenate([etf_r, etb_r, etf_i, etb_i], axis=0).astype(BF16)
    df_r, df_i = f_ca(f_next(f_up))
    db_r, db_i = b_ca(b_next(b_down))
    zero = jnp.zeros_like(df_r)
    da_ref[...] = jnp.concatenate([df_r, zero, -df_i, zero], axis=0).astype(BF16)
    db_ref[...] = jnp.concatenate([zero, db_r, zero, -db_i], axis=0).astype(BF16)
    kf_r, kf_i = f_ca(f_up)
    lag0 = step == 0
    lhs_f = jnp.concatenate([fbr, -fbi], axis=0)
    lhs_b = jnp.concatenate([bbr, -bbi], axis=0)
    q_b = _dot_tn3(lhs_b, jnp.concatenate([db_r, db_i], axis=0))
    q_f = _dot_tn3(jnp.concatenate([lhs_f, lhs_b], axis=0),
                   jnp.concatenate([kf_r, kf_i, jnp.where(lag0, bcr, 0.0), jnp.where(lag0, bci, 0.0)],
                                   axis=0))
    z = jnp.concatenate([q_b, q_f], axis=1)
    for j in range(L):
        rows = z[j * SSM_GROUP:(j + 1) * SSM_GROUP, :]
        m_ref[j * SSM_GROUP:(j + 1) * SSM_GROUP, :] = pltpu.roll(
            rows, (2 * W - (L - j) * SSM_GROUP) % (2 * W), 1)[:, :W].astype(BF16)


def _s5_operators(lam_re, lam_im, log_step, b_re, b_im, c_re, c_im):
    _, G, P = lam_re.shape
    W = CHUNK * SSM_GROUP
    lam = jnp.stack([lam_re, lam_im, jnp.broadcast_to(log_step[..., None], lam_re.shape)], axis=-1).astype(F32)
    tile = lambda a: jnp.tile(a.astype(F32), (1, 1, 1, CHUNK))
    bt = jnp.stack([tile(b_re), tile(b_im)], axis=1)
    ct = jnp.stack([tile(c_re.swapaxes(2, 3)), tile(c_im.swapaxes(2, 3))], axis=1)
    per_g = pl.BlockSpec((2, 2, None, P, W), lambda g: (0, 0, g, 0, 0))
    out = pl.BlockSpec((None, W, W), lambda g: (g, 0, 0))
    return pl.pallas_call(
        _s5op_body,
        grid=(G,),
        in_specs=[pl.BlockSpec((2, None, P, 3), lambda g: (0, g, 0, 0)), per_g, per_g],
        out_specs=[out] * 4,
        out_shape=[jax.ShapeDtypeStruct((G, W, W), BF16)] * 4,
        compiler_params=_cparams("arbitrary"),
        name="s5op",
    )(lam, bt, ct)


GROUPS_PER_SET = LANES // SSM_GROUP
S5_SCAN_GROUPS = 4
S5_PERM_ROWS = 64


def _transpose8(blocks, axis, piece):
    n = blocks[0].shape[axis]
    q = lax.broadcasted_iota(jnp.int32, blocks[0].shape, axis) // piece
    v = list(blocks)
    for d in (4, 2, 1):
        keep = (q & d) == 0
        nxt = list(v)
        for a in range(8):
            if a & d == 0:
                b = a + d
                nxt[a] = jnp.where(keep, v[a], pltpu.roll(v[b], d * piece, axis))
                nxt[b] = jnp.where(keep, pltpu.roll(v[a], n - d * piece, axis), v[b])
        v = nxt
    return v


def _piece_transpose(blocks):
    return _transpose8(blocks, 1, SSM_GROUP)


def _s5_body(u_ref, m_ref, et_ref, da_ref, db_ref, a_ref, d_ref, z_ref,
             uperm, zperm, xr_n, xi_n, xr_t, xi_t, har_s, hai_s, hbr_s, hbi_s, *, nb):
    R = u_ref.shape[0]
    nchunk = R // nb
    W = uperm.shape[2]
    half = W // 2
    halves = W // LANES

    def permute_in(j, _):
        r = pl.ds(pl.multiple_of(j * S5_PERM_ROWS, S5_PERM_ROWS), S5_PERM_ROWS)
        for h in range(halves):
            out = _piece_transpose([u_ref[r, (h * GROUPS_PER_SET + k) * LANES:(h * GROUPS_PER_SET + k + 1) * LANES]
                                    for k in range(GROUPS_PER_SET)])
            for g in range(GROUPS_PER_SET):
                uperm[g, r, h * LANES:(h + 1) * LANES] = out[g]
        return 0

    lax.fori_loop(0, R // S5_PERM_ROWS, permute_in, 0)
    fwd = lax.broadcasted_iota(jnp.int32, (nb, half), 1) < SSM_STATE

    def swap_row_order(src_a, src_b, dst_a, dst_b, to_chunk_major):
        def blocks(cb, _):
            batch_major = [pl.ds(pl.multiple_of(p * nchunk + cb * nb, nb), nb) for p in range(nb)]
            chunk_major = [pl.ds(pl.multiple_of((cb * nb + j) * nb, nb), nb) for j in range(nb)]
            rows_in, rows_out = (batch_major, chunk_major) if to_chunk_major else (chunk_major, batch_major)
            for k in range(S5_SCAN_GROUPS):
                for src, dst in ((src_a, dst_a), (src_b, dst_b)):
                    out = _transpose8([src[k, r, :] for r in rows_in], 0, 1)
                    for j in range(nb):
                        dst[k, rows_out[j], :] = out[j]
            return 0
        lax.fori_loop(0, nchunk // nb, blocks, 0)

    for g0 in range(0, GROUPS_PER_SET, S5_SCAN_GROUPS):
        coef = []
        for k in range(S5_SCAN_GROUPS):
            g = g0 + k
            x = _dot_nt(uperm[g], et_ref[g])
            xr_n[k] = x[:, :half]
            xi_n[k] = x[:, half:]
            lr = jnp.minimum(a_ref[g, 0:1, :], -1e-4)
            dt = jnp.exp(a_ref[g, 2:3, :])
            mag = jnp.exp(lr * dt * CHUNK)
            ang = a_ref[g, 1:2, :] * dt * CHUNK
            coef.append((jnp.broadcast_to(mag * jnp.cos(ang), (nb, half)),
                         jnp.broadcast_to(mag * jnp.sin(ang), (nb, half))))
        swap_row_order(xr_n, xi_n, xr_t, xi_t, True)

        def step(c, carry):
            rf = pl.ds(pl.multiple_of(c * nb, nb), nb)
            rb = pl.ds(pl.multiple_of((nchunk - 1 - c) * nb, nb), nb)
            new = []
            for k in range(S5_SCAN_GROUPS):
                sr, si = carry[2 * k], carry[2 * k + 1]
                ar, ai = coef[k]
                har_s[k, rf, :] = sr
                hai_s[k, rf, :] = si
                hbr_s[k, rb, :] = sr
                hbi_s[k, rb, :] = si
                xr = jnp.where(fwd, xr_t[k, rf, :], xr_t[k, rb, :])
                xi = jnp.where(fwd, xi_t[k, rf, :], xi_t[k, rb, :])
                new += [sr * ar - si * ai + xr, sr * ai + si * ar + xi]
            return tuple(new)

        zero = jnp.zeros((nb, half), F32)
        lax.fori_loop(0, nchunk, step, (zero,) * (2 * S5_SCAN_GROUPS))
        for k in range(S5_SCAN_GROUPS):
            g = g0 + k
            ha = jnp.concatenate([har_s[k], hai_s[k]], axis=1).astype(BF16)
            hb = jnp.concatenate([hbr_s[k], hbi_s[k]], axis=1).astype(BF16)
            y_t = _dot(ha, da_ref[g]) + _dot(hb, db_ref[g])
            xr_n[k] = y_t[:, :half]
            xi_n[k] = y_t[:, half:]
        swap_row_order(xr_n, xi_n, xr_t, xi_t, False)
        for k in range(S5_SCAN_GROUPS):
            g = g0 + k
            u = uperm[g]
            y = _dot(u, m_ref[g]) + jnp.concatenate([xr_t[k], xi_t[k]], axis=1) + d_ref[g] * u.astype(F32)
            zperm[g] = jax.nn.gelu(y).astype(BF16)

    def permute_out(j, _):
        r = pl.ds(pl.multiple_of(j * S5_PERM_ROWS, S5_PERM_ROWS), S5_PERM_ROWS)
        for h in range(halves):
            out = _piece_transpose([zperm[g, r, h * LANES:(h + 1) * LANES] for g in range(GROUPS_PER_SET)])
            for k in range(GROUPS_PER_SET):
                z_ref[r, (h * GROUPS_PER_SET + k) * LANES:(h * GROUPS_PER_SET + k + 1) * LANES] = out[k]
        return 0

    lax.fori_loop(0, R // S5_PERM_ROWS, permute_out, 0)


def _s5(u_set, B, S, lam_re, lam_im, log_step, b_re, b_im, c_re, c_im, ssm_d):
    n_set, R, WS = u_set.shape
    G = n_set * GROUPS_PER_SET
    W = CHUNK * SSM_GROUP
    assert R == (S // CHUNK) * B and WS == CHUNK * LANES
    M, ET, DA, DB = _s5_operators(lam_re, lam_im, log_step, b_re, b_im, c_re, c_im)
    both = lambda a: jnp.concatenate([a[0], a[1]], axis=-1)
    lam = jnp.stack([both(lam_re), both(lam_im),
                     both(jnp.broadcast_to(log_step[..., None], lam_re.shape))], axis=1).astype(F32)
    dvec = jnp.tile(ssm_d.astype(F32).reshape(G, 1, SSM_GROUP), (1, CHUNK, 1)).reshape(G, 1, W)
    per_set = pl.BlockSpec((None, R, WS), lambda s: (s, 0, 0))
    blk = lambda *sh: pl.BlockSpec((GROUPS_PER_SET,) + sh, lambda s: (s,) + (0,) * len(sh))
    return pl.pallas_call(
        functools.partial(_s5_body, nb=B),
        grid=(n_set,),
        in_specs=[per_set, blk(W, W), blk(W, W), blk(W, W), blk(W, W), blk(3, W // 2), blk(1, W)],
        out_specs=per_set,
        out_shape=jax.ShapeDtypeStruct((n_set, R, WS), BF16),
        scratch_shapes=[pltpu.VMEM((GROUPS_PER_SET, R, W), BF16)] * 2
        + [pltpu.VMEM((S5_SCAN_GROUPS, R, W // 2), F32)] * 8,
        compiler_params=_cparams("arbitrary"),
        name="s5",
    )(u_set, M, ET, DA, DB, lam, dvec)


def _memkv_body(m_ref, g_ref, wk_ref, wv_ref, k_ref, v_ref):
    h = (_rms(m_ref[...]) * g_ref[...]).astype(BF16)
    k_ref[...] = _dot(h, wk_ref[...]).astype(BF16)
    v_ref[...] = _dot(h, wv_ref[...]).astype(BF16)


def _memkv(mem, g, wk, wv):
    B, M, D = mem.shape
    const = lambda b: (0, 0)
    row = pl.BlockSpec((None, M, D), lambda b: (b, 0, 0))
    return pl.pallas_call(
        _memkv_body,
        grid=(B,),
        in_specs=[row, pl.BlockSpec((1, D), const), pl.BlockSpec((D, D), const),
                  pl.BlockSpec((D, D), const)],
        out_specs=[row, row],
        out_shape=[jax.ShapeDtypeStruct((B, M, D), BF16)] * 2,
        compiler_params=_cparams("arbitrary"),
        name="memkv",
    )(mem, g[None, :], wk.astype(BF16), wv.astype(BF16))


def _mid_body(x_ref, ya_ref, z_ref, gw_ref, gb_ref, ag_ref, sg_ref, wo_ref, xg_ref,
              wq_ref, xwo_ref, k_ref, v_ref, o_ref, zs):
    D = x_ref.shape[1]
    rows = z_ref.shape[1]
    for j in range(z_ref.shape[0]):
        for t in range(CHUNK):
            zs[j, pl.ds(t, rows, stride=CHUNK), :] = z_ref[j, :, t * LANES:(t + 1) * LANES].astype(F32)
    z = jnp.concatenate([zs[j] for j in range(z_ref.shape[0])], axis=1).astype(BF16)
    s = z.astype(F32) * jax.nn.sigmoid(_dot(z, gw_ref[...]) + gb_ref[...])
    ya = _rms(ya_ref[...].astype(F32)) * ag_ref[...]
    ys = _rms(s) * sg_ref[...]
    y = jnp.concatenate([ya, ys], axis=1).astype(BF16)
    x1 = x_ref[...] + _dot(y, wo_ref[...])
    hq = (_rms(x1) * xg_ref[...]).astype(BF16)
    dh = D // XA_HEADS
    q = (_dot(hq, wq_ref[...]) * (dh ** -0.5)).astype(BF16)
    outs = []
    for h in range(XA_HEADS):
        sl = slice(h * dh, (h + 1) * dh)
        sc = _dot_nt(q[:, sl], k_ref[:, sl])
        e = jnp.exp(sc - jnp.max(sc, axis=-1, keepdims=True))
        inv = 1.0 / jnp.sum(e, axis=-1, keepdims=True)
        outs.append((_dot(e.astype(BF16), v_ref[:, sl]) * inv).astype(BF16))
    o = jnp.concatenate(outs, axis=1)
    o_ref[:, :D] = x1 + _dot(o, xwo_ref[...])
    o_ref[:, D:] = jnp.zeros((o_ref.shape[0], o_ref.shape[1] - D), F32)


def _mid(x2d, S, y_att, z, glu_w, glu_b, att_g, ssm_g, w_out, xa_g, wq, wo, kmem, vmem):
    T, D = x2d.shape
    M = kmem.shape[1]
    tm = 256
    sb = S // tm
    const = lambda i: (0, 0)
    tok = lambda w: pl.BlockSpec((tm, w), lambda i: (i, 0))
    full = lambda a: pl.BlockSpec(a.shape, const)
    vec = lambda a: a.astype(F32)[None, :]
    mem = pl.BlockSpec((None, M, D), lambda i: (i // sb, 0, 0))
    args = [x2d, y_att, z, glu_w.astype(BF16), vec(glu_b), vec(att_g), vec(ssm_g), w_out.astype(BF16),
            vec(xa_g), wq.astype(BF16), wo.astype(BF16), kmem, vmem]
    zspec = pl.BlockSpec((z.shape[0], tm // CHUNK, CHUNK * LANES), lambda i: (0, i, 0))
    specs = [tok(D), tok(y_att.shape[1]), zspec] + [full(a) for a in args[3:11]] + [mem, mem]
    return pl.pallas_call(
        _mid_body,
        grid=(T // tm,),
        in_specs=specs,
        out_specs=pl.BlockSpec((tm, D + LANES), lambda i: (i, 0)),
        out_shape=jax.ShapeDtypeStruct((T, D + LANES), F32),
        scratch_shapes=[pltpu.VMEM((z.shape[0], tm, LANES), F32)],
        compiler_params=_cparams("arbitrary"),
        name="mid",
    )(*args)


def _split_bf16(a):
    hi = a.astype(BF16)
    return hi, (a - hi.astype(F32)).astype(BF16)


ROUTE_ROWS = 32
MOE_TILE = 256


def _route_body(x_ref, g_ref, whi_ref, wlo_ref, b_ref, tri_ref, meta_ref, mt_ref, cnt_ref, carry):
    i = pl.program_id(0)
    tm, D = x_ref.shape[0], g_ref.shape[1]

    @pl.when(i == 0)
    def _():
        carry[...] = jnp.zeros_like(carry)

    h = _rms(x_ref[:, :D]) * g_ref[...]
    hhi, hlo = _split_bf16(h)
    logits = (_dot_nt(whi_ref[...], hhi) + _dot_nt(whi_ref[...], hlo) + _dot_nt(wlo_ref[...], hhi)
              + b_ref[:, 0:1])
    ninf = jnp.float32(-jnp.inf)
    cmax = lambda v: jnp.max(v, axis=0, keepdims=True)
    cmin = lambda v: jnp.min(v, axis=0, keepdims=True)
    gl = logits[N_EXPERTS:N_EXPERTS + N_EXPERT_GROUPS]
    grow = lax.broadcasted_iota(jnp.int32, gl.shape, 0)
    gmax = cmax(gl)
    g_sel = cmin(jnp.where(gl == gmax, grow, N_EXPERT_GROUPS))
    g_w = 1.0 / jnp.sum(jnp.exp(gl - gmax), axis=0, keepdims=True)
    el = logits[:N_EXPERTS]
    erow = lax.broadcasted_iota(jnp.int32, el.shape, 0)
    el = jnp.where((erow >> 2) == g_sel, el, ninf)
    v1 = cmax(el)
    i1 = cmin(jnp.where(el == v1, erow, N_EXPERTS))
    el2 = jnp.where(erow == i1, ninf, el)
    v2 = cmax(el2)
    i2 = cmin(jnp.where(el2 == v2, erow, N_EXPERTS))
    t = jnp.exp(v2 - v1)
    w1 = g_w / (1.0 + t)
    w2 = g_w * t / (1.0 + t)
    first_low = i1 < i2
    ia = jnp.minimum(i1, i2) & 3
    ib = jnp.maximum(i1, i2) & 3
    wa = jnp.where(first_low, w1, w2)
    wb = jnp.where(first_low, w2, w1)
    cls = g_sel * PAIRS_PER_GROUP + ((ia * (7 - ia)) >> 1) + ib - ia - 1
    crow = lax.broadcasted_iota(jnp.int32, (ROUTE_ROWS, tm), 0)
    onehot = (crow == cls).astype(F32)
    before = _dot(onehot.astype(BF16), tri_ref[...])
    rank = jnp.sum(onehot * (before + carry[:, 0:1]), axis=0, keepdims=True)
    carry[...] = carry[...] + jnp.sum(onehot, axis=1, keepdims=True)
    cnt_ref[...] = carry[...]

    def record(rows):
        r = lax.broadcasted_iota(jnp.int32, (rows, tm), 0)
        return jnp.where(r == META_CLS, cls.astype(F32),
                         jnp.where(r == META_RANK, rank,
                                   jnp.where(r == META_WA, wa, jnp.where(r == META_WB, wb, 0.0))))

    meta_ref[...] = record(LANES).T
    mt_ref[:, pl.ds(pl.multiple_of(i * tm, tm), tm)] = record(mt_ref.shape[0])

    @pl.when(i == pl.num_programs(0) - 1)
    def _():
        ntile = jnp.floor((carry[...] + (MOE_TILE - 1.0)) * (1.0 / MOE_TILE))
        r = lax.broadcasted_iota(jnp.int32, (ROUTE_ROWS, ROUTE_ROWS), 0)
        c = lax.broadcasted_iota(jnp.int32, (ROUTE_ROWS, ROUTE_ROWS), 1)
        off = _dot((c < r).astype(BF16), ntile.astype(BF16)) * MOE_TILE
        cls_all = mt_ref[META_CLS:META_CLS + 1, :]
        pos = mt_ref[META_RANK:META_RANK + 1, :]
        for k in range(N_CLASSES):
            pos = pos + jnp.where(cls_all == k, off[k:k + 1, 0:1], 0.0)
        mt_ref[META_POS:META_POS + 1, :] = pos


def _route(x2a, D, moe_g, grp_w, grp_b, exp_w, exp_b):
    T = x2a.shape[0]
    tm = 256
    pad = ROUTE_ROWS - N_EXPERT_GROUPS - N_EXPERTS
    w = jnp.concatenate([exp_w, grp_w, jnp.zeros((D, pad), F32)], axis=1).astype(F32).T
    b = jnp.concatenate([exp_b, grp_b, jnp.zeros((pad,), F32)]).astype(F32)
    whi, wlo = _split_bf16(w)
    tri = jnp.asarray(np.arange(tm)[:, None] < np.arange(tm)[None, :], dtype=BF16)
    const = lambda i: (0, 0)
    ncol = D // LANES
    return pl.pallas_call(
        _route_body,
        grid=(T // tm,),
        in_specs=[pl.BlockSpec((tm, D + LANES), lambda i: (i, 0)),
                  pl.BlockSpec((1, D), const), pl.BlockSpec((ROUTE_ROWS, D), const),
                  pl.BlockSpec((ROUTE_ROWS, D), const), pl.BlockSpec((ROUTE_ROWS, LANES), const),
                  pl.BlockSpec((tm, tm), const)],
        out_specs=[pl.BlockSpec((tm, LANES), lambda i: (i, ncol)),
                   pl.BlockSpec((SUBLANES, T), const),
                   pl.BlockSpec((ROUTE_ROWS, LANES), const)],
        out_shape=[jax.ShapeDtypeStruct(x2a.shape, F32), jax.ShapeDtypeStruct((SUBLANES, T), F32),
                   jax.ShapeDtypeStruct((ROUTE_ROWS, LANES), F32)],
        scratch_shapes=[pltpu.VMEM((ROUTE_ROWS, LANES), F32)],
        input_output_aliases={0: 0},
        compiler_params=_cparams("arbitrary"),
        name="route",
    )(x2a, moe_g.astype(F32)[None, :], whi, wlo, jnp.broadcast_to(b[:, None], (ROUTE_ROWS, LANES)), tri)


ROW_COPY_UNROLL = 8


def _row_copy_body(pos_ref, src_ref, *rest, rows, scatter):
    dst_ref, sem = rest[-2:]

    def copy(j, k):
        p = pos_ref[j * ROW_COPY_UNROLL + k]
        tok = (j, pl.ds(k, 1))
        slot = (lax.shift_right_logical(p, 3), pl.ds(p & (SUBLANES - 1), 1))
        if scatter:
            return pltpu.make_async_copy(src_ref.at[tok], dst_ref.at[slot], sem)
        return pltpu.make_async_copy(src_ref.at[slot], dst_ref.at[tok], sem)

    def start(j, _):
        for k in range(ROW_COPY_UNROLL):
            copy(j, k).start(priority=k % 2)
        return 0

    def wait(j, _):
        for k in range(ROW_COPY_UNROLL):
            copy(j, k).wait()
        return 0

    lax.fori_loop(0, rows // ROW_COPY_UNROLL, start, 0)
    lax.fori_loop(0, rows // ROW_COPY_UNROLL, wait, 0)


def _row_copy(pos, src, *, dst_init=None, dst_rows=None, name):
    assert ROW_COPY_UNROLL == SUBLANES
    T = pos.shape[0]
    rows = 512
    width = src.shape[1]
    scatter = dst_init is not None
    hbm = pl.BlockSpec(memory_space=pl.ANY)
    tok = pl.BlockSpec((rows // SUBLANES, SUBLANES, width), lambda i: (i, 0, 0))
    out_rows = dst_init.shape[0] if scatter else dst_rows
    tiled = lambda a: a.reshape(a.shape[0] // SUBLANES, SUBLANES, width)
    out = pl.pallas_call(
        functools.partial(_row_copy_body, rows=rows, scatter=scatter),
        grid=(T // rows,),
        in_specs=[pl.BlockSpec((rows,), lambda i: (i,), memory_space=pltpu.SMEM)]
        + ([tok, hbm] if scatter else [hbm]),
        out_specs=hbm if scatter else tok,
        out_shape=jax.ShapeDtypeStruct((out_rows // SUBLANES, SUBLANES, width), src.dtype),
        scratch_shapes=[pltpu.SemaphoreType.DMA(())],
        input_output_aliases={2: 0} if scatter else {},
        compiler_params=pltpu.CompilerParams(dimension_semantics=("arbitrary",)),
        name=name,
    )(*((pos, tiled(src), tiled(dst_init)) if scatter else (pos, tiled(src))))
    return out.reshape(out_rows, width)


def _ffn_body(ea_ref, eb_ref, nt_ref, xs_ref, g_ref, fg_ref, wga, wua, wda, wgb, wub, wdb, o_ref):
    del ea_ref, eb_ref
    D = g_ref.shape[1]
    valid = pl.program_id(0) < nt_ref[0]

    @pl.when(valid)
    def _():
        x = xs_ref[:, :D]
        meta = xs_ref[:, D:]
        h = (_rms(x) * g_ref[...]).astype(BF16)
        y = x
        for wg, wu, wd, lane in ((wga, wua, wda, META_WA), (wgb, wub, wdb, META_WB)):
            a = jax.nn.silu(_dot(h, wg[...])) * _dot(h, wu[...]) * meta[:, lane:lane + 1]
            y = y + _dot(a.astype(BF16), wd[...])
        o_ref[...] = _rms(y) * fg_ref[...]

    @pl.when(jnp.logical_not(valid))
    def _():
        o_ref[...] = jnp.zeros_like(o_ref)


def _ffn(xs, D, tile_ea, tile_eb, n_tiles, moe_g, final_g, w_gate, w_up, w_down):
    tm = MOE_TILE
    P = xs.shape[0]
    nt_max = P // tm
    De = w_gate.shape[2]
    wg = w_gate.astype(BF16)
    wu = w_up.astype(BF16)
    wd = w_down.astype(BF16)
    const = lambda i, ea, eb, nt: (0, 0)
    sel_a = lambda i, ea, eb, nt: (ea[i], 0, 0)
    sel_b = lambda i, ea, eb, nt: (eb[i], 0, 0)
    up = lambda sel: pl.BlockSpec((None, D, De), sel)
    down = lambda sel: pl.BlockSpec((None, De, D), sel)
    grid_spec = pltpu.PrefetchScalarGridSpec(
        num_scalar_prefetch=3,
        grid=(nt_max,),
        in_specs=[pl.BlockSpec((tm, D + LANES),
                               lambda i, ea, eb, nt: (jnp.maximum(jnp.minimum(i, nt[0] - 1), 0), 0)),
                  pl.BlockSpec((1, D), const), pl.BlockSpec((1, D), const),
                  up(sel_a), up(sel_a), down(sel_a), up(sel_b), up(sel_b), down(sel_b)],
        out_specs=pl.BlockSpec((tm, D), lambda i, ea, eb, nt: (i, 0)),
    )
    return pl.pallas_call(
        _ffn_body,
        grid_spec=grid_spec,
        out_shape=jax.ShapeDtypeStruct((P, D), F32),
        compiler_params=_cparams("arbitrary"),
        name="ffn",
    )(tile_ea, tile_eb, n_tiles, xs, moe_g.astype(F32)[None, :], final_g.astype(F32)[None, :],
      wg, wu, wd, wg, wu, wd)


def _moe(x2a, D, moe_g, grp_w, grp_b, exp_w, exp_b, w_gate, w_up, w_down, final_g):
    T = x2a.shape[0]
    tm = MOE_TILE
    x2a, record, counts = _route(x2a, D, moe_g, grp_w, grp_b, exp_w, exp_b)
    pos = record[META_POS].astype(jnp.int32)
    cnt = counts[:N_CLASSES, 0].astype(jnp.int32)
    ntile_c = (cnt + tm - 1) // tm
    tile_end = jnp.cumsum(ntile_c)
    nt_max = T // tm + N_CLASSES
    n_tiles = tile_end[-1:]
    tile_id = jnp.arange(nt_max, dtype=jnp.int32)
    tile_cls = jnp.minimum(jnp.sum((tile_end[None, :] <= tile_id[:, None]).astype(jnp.int32), axis=1),
                           N_CLASSES - 1)
    tile_cls = jnp.where(tile_id < n_tiles[0], tile_cls, tile_cls[jnp.maximum(n_tiles[0] - 1, 0)])
    pa =jnp.asarray([0, 0, 0, 1, 1, 2], jnp.int32)
    pb = jnp.asarray([1, 2, 3, 2, 3, 3], jnp.int32)
    grp = tile_cls // PAIRS_PER_GROUP
    tile_ea = grp * EXPERTS_PER_GROUP + pa[tile_cls % PAIRS_PER_GROUP]
    tile_eb = grp * EXPERTS_PER_GROUP + pb[tile_cls % PAIRS_PER_GROUP]
    xs = _row_copy(pos, x2a, dst_init=jnp.zeros((nt_max * tm, D + LANES), F32), name="dispatch")
    ys = _ffn(xs, D, tile_ea, tile_eb, n_tiles.astype(jnp.int32), moe_g, final_g, w_gate, w_up, w_down)
    return _row_copy(pos, ys, dst_rows=T, name="unsort")


def kernel(x, mem, mix_norm_g, w_in, q_norm_g, k_norm_g, ssm_lambda_re, ssm_lambda_im, ssm_log_step, ssm_b_re, ssm_b_im, ssm_c_re, ssm_c_im, ssm_d, ssm_glu_w, ssm_glu_b, att_out_g, ssm_out_g, w_out, xa_norm_g, mem_norm_g, xa_wq, xa_wk, xa_wv, xa_wo, moe_norm_g, router_grp_w, router_grp_b, router_exp_w, router_exp_b, exp_w_gate, exp_w_up, exp_w_down, final_norm_g):
    B, S, D = x.shape
    assert mix_norm_g.shape[0] == 1, "single layer"
    x2d = x.reshape(B * S, D)
    q, kd, vd, u = _inproj(x2d, S, mix_norm_g[0], w_in[0], q_norm_g[0], k_norm_g[0])
    y_att = _attention(q, kd, vd, B, S)
    z = _s5(u, B, S, ssm_lambda_re[0], ssm_lambda_im[0], ssm_log_step[0], ssm_b_re[0], ssm_b_im[0],
            ssm_c_re[0], ssm_c_im[0], ssm_d[0])
    kmem, vmem = _memkv(mem, mem_norm_g[0], xa_wk[0], xa_wv[0])
    x2a = _mid(x2d, S, y_att, z, ssm_glu_w[0], ssm_glu_b[0], att_out_g[0], ssm_out_g[0], w_out[0],
               xa_norm_g[0], xa_wq[0], xa_wo[0], kmem, vmem)
    out = _moe(x2a, D, moe_norm_g[0], router_grp_w[0], router_grp_b[0], router_exp_w[0],
               router_exp_b[0], exp_w_gate[0], exp_w_up[0], exp_w_down[0], final_norm_g)
    return out.reshape(B, S, D)
```

```python
import functools

import numpy as np
import jax
import jax.numpy as jnp
from jax import lax
from jax.experimental import pallas as pl
from jax.experimental.pallas import tpu as pltpu

F32 = jnp.float32
BF16 = jnp.bfloat16

EPS = 1e-6
GRID_W = 64
ROPE_THETA = 10000.0
ATT_HEADS = 8
ATT_KV_HEADS = 2
HEAD_DIM = 64
D_ATT = ATT_HEADS * HEAD_DIM
D_KV = ATT_KV_HEADS * HEAD_DIM
SSM_GROUP = 16
SSM_STATE = 64
XA_HEADS = 4
N_EXPERT_GROUPS = 4
EXPERTS_PER_GROUP = 4
N_EXPERTS = N_EXPERT_GROUPS * EXPERTS_PER_GROUP
PAIRS_PER_GROUP = 6
N_CLASSES = N_EXPERT_GROUPS * PAIRS_PER_GROUP

LANES = 128
SUBLANES = 8
CHUNK = 16
VMEM_LIMIT = 56 * 1024 * 1024

META_CLS, META_RANK, META_WA, META_WB, META_POS = 0, 1, 2, 3, 4


def _cparams(*sem):
    return pltpu.CompilerParams(dimension_semantics=sem, vmem_limit_bytes=VMEM_LIMIT)


def _rms(x):
    return x * lax.rsqrt(jnp.mean(x * x, axis=-1, keepdims=True) + EPS)


def _dot(a, b):
    return jnp.dot(a, b, preferred_element_type=F32)


def _dot_nt(a, b):
    return lax.dot_general(a, b, (((1,), (1,)), ((), ())), preferred_element_type=F32)


def _inproj_body(x_ref, g_ref, w_ref, qkg_ref, ones_ref, cos_ref, sin_ref,
                 q_ref, k_ref, v_ref, u_ref, us):
    n_qk = q_ref.shape[1] + k_ref.shape[1]
    h = _rms(x_ref[...]) * g_ref[...]
    proj = _dot(h.astype(BF16), w_ref[...])
    qk = proj[:, :n_qk]
    ssq = _dot((qk * qk).astype(BF16), ones_ref[...])
    qk = qk * lax.rsqrt(ssq * (1.0 / HEAD_DIM) + EPS) * qkg_ref[...]
    cos = cos_ref[...]
    sin = sin_ref[...]
    lane = lax.broadcasted_iota(jnp.int32, cos.shape, 1)
    first = (lane % 32) < 16
    nq = q_ref.shape[1] // LANES
    for c in range(n_qk // LANES):
        xc = qk[:, c * LANES:(c + 1) * LANES]
        partner = jnp.where(first, pltpu.roll(xc, LANES - 16, 1), pltpu.roll(xc, 16, 1))
        rot = xc * cos + partner * sin
        if c < nq:
            q_ref[:, c * LANES:(c + 1) * LANES] = (rot * (HEAD_DIM ** -0.5)).astype(BF16)
        else:
            k_ref[:, (c - nq) * LANES:(c - nq + 1) * LANES] = rot.astype(BF16)
    nv = n_qk + v_ref.shape[1]
    v_ref[...] = proj[:, n_qk:nv].astype(BF16)
    rows = u_ref.shape[1]
    for j in range(u_ref.shape[0]):
        us[j] = proj[:, nv + j * LANES:nv + (j + 1) * LANES]
        for t in range(CHUNK):
            u_ref[j, :, t * LANES:(t + 1) * LANES] = us[j, pl.ds(t, rows, stride=CHUNK), :].astype(BF16)


def _rope_tables(S):
    t = np.arange(S)
    freqs = (ROPE_THETA ** (-np.arange(16, dtype=np.float32) / 16)).astype(np.float32)
    halves = []
    for pos in (t // GRID_W, t % GRID_W):
        ang = pos.astype(np.float32)[:, None] * freqs[None, :]
        halves.append((np.cos(ang), np.sin(ang)))
    cos_h = np.concatenate([c for c, _ in halves for _ in range(2)], axis=1)
    sin_h = np.concatenate([sg * s for _, s in halves for sg in (-1.0, 1.0)], axis=1)
    return (jnp.asarray(np.concatenate([cos_h, cos_h], axis=1), F32),
            jnp.asarray(np.concatenate([sin_h, sin_h], axis=1), F32))


def _inproj(x2d, S, mix_g, w_in, q_g, k_g):
    T, D = x2d.shape
    tm = 512
    d_ssm = w_in.shape[1] - D_ATT - 2 * D_KV
    wq = w_in[:, :D_ATT]
    wk = w_in[:, D_ATT:D_ATT + D_KV]
    wv = w_in[:, D_ATT + D_KV:D_ATT + 2 * D_KV]
    wu = w_in[:, D_ATT + 2 * D_KV:]
    dup = lambda w: jnp.concatenate([w[:, :64], w[:, :64], w[:, 64:], w[:, 64:]], axis=1)
    w_ext = jnp.concatenate([wq, dup(wk), dup(wv), wu], axis=1).astype(BF16)
    n_qk = D_ATT + 2 * D_KV
    qkg = jnp.concatenate([jnp.tile(q_g, ATT_HEADS), jnp.tile(k_g, 2 * ATT_KV_HEADS)])[None, :]
    hid = np.arange(n_qk) // HEAD_DIM
    ones = jnp.asarray((hid[:, None] == hid[None, :]).astype(np.float32), dtype=BF16)
    cos_t, sin_t = _rope_tables(S)
    sb = S // tm
    const = lambda i: (0, 0)
    return pl.pallas_call(
        _inproj_body,
        grid=(T // tm,),
        in_specs=[
            pl.BlockSpec((tm, D), lambda i: (i, 0)),
            pl.BlockSpec((1, D), const),
            pl.BlockSpec(w_ext.shape, const),
            pl.BlockSpec((1, n_qk), const),
            pl.BlockSpec((n_qk, n_qk), const),
            pl.BlockSpec((tm, LANES), lambda i: (i % sb, 0)),
            pl.BlockSpec((tm, LANES), lambda i: (i % sb, 0)),
        ],
        out_specs=[
            pl.BlockSpec((tm, D_ATT), lambda i: (i, 0)),
            pl.BlockSpec((tm, 2 * D_KV), lambda i: (i, 0)),
            pl.BlockSpec((tm, 2 * D_KV), lambda i: (i, 0)),
            pl.BlockSpec((d_ssm // LANES, tm // CHUNK, CHUNK * LANES), lambda i: (0, i, 0)),
        ],
        out_shape=[
            jax.ShapeDtypeStruct((T, D_ATT), BF16),
            jax.ShapeDtypeStruct((T, 2 * D_KV), BF16),
            jax.ShapeDtypeStruct((T, 2 * D_KV), BF16),
            jax.ShapeDtypeStruct((d_ssm // LANES, T // CHUNK, CHUNK * LANES), BF16),
        ],
        scratch_shapes=[pltpu.VMEM((d_ssm // LANES, tm, LANES), F32)],
        compiler_params=_cparams("arbitrary"),
        name="inproj",
    )(x2d, mix_g[None, :], w_ext, qkg, ones, cos_t, sin_t)


ATT_KEY_CHUNK = 256
ATT_ROW_BLOCK = 256


def _attn_body(q_ref, k_ref, v_ref, o_ref, klo, khi, vlo, vhi, s_scr, e_scr):
    @pl.when(pl.program_id(2) == 0)
    def _():
        lane = lax.broadcasted_iota(jnp.int32, k_ref.shape, 1)
        lo = lane < HEAD_DIM
        kd = k_ref[...]
        vd = v_ref[...]
        zero = jnp.zeros_like(kd)
        klo[...] = jnp.where(lo, kd, zero)
        khi[...] = jnp.where(lo, zero, kd)
        vlo[...] = jnp.where(lo, vd, (lane == HEAD_DIM).astype(BF16))
        vhi[...] = jnp.where(lo, (lane == 0).astype(BF16), vd)

    S = klo.shape[0]
    rb = s_scr.shape[1]
    units = [(r0, p, km, vm) for r0 in range(0, q_ref.shape[0], rb)
             for p in range(2) for km, vm in ((klo, vlo), (khi, vhi))]

    def scores(i):
        r0, p, km, _ = units[i]
        qp = q_ref[r0:r0 + rb, p * LANES:(p + 1) * LANES]
        m = None
        for c in range(0, S, ATT_KEY_CHUNK):
            s = _dot_nt(qp, km[c:c + ATT_KEY_CHUNK, :])
            s_scr[i % 2, :, c:c + ATT_KEY_CHUNK] = s
            for l0 in range(0, ATT_KEY_CHUNK, LANES):
                m = s[:, l0:l0 + LANES] if m is None else jnp.maximum(m, s[:, l0:l0 + LANES])
        return jnp.max(m, axis=-1, keepdims=True)

    def values(i, m):
        vm = units[i][3]
        for c in range(0, S, ATT_KEY_CHUNK):
            e_scr[i % 2, :, c:c + ATT_KEY_CHUNK] = jnp.exp(s_scr[i % 2, :, c:c + ATT_KEY_CHUNK] - m).astype(BF16)
        acc = _dot(e_scr[i % 2], vm[...])
        ones_lane = HEAD_DIM if vm is vlo else 0
        return acc * (1.0 / acc[:, ones_lane:ones_lane + 1])

    lo = lax.broadcasted_iota(jnp.int32, (rb, LANES), 1) < HEAD_DIM
    m_prev = scores(0)
    first = None
    for i, (r0, p, _, vm) in enumerate(units):
        m_next = scores(i + 1) if i + 1 < len(units) else None
        o = values(i, m_prev)
        m_prev = m_next
        if vm is vlo:
            first = o
        else:
            o_ref[r0:r0 + rb, p * LANES:(p + 1) * LANES] = jnp.where(lo, first, o).astype(BF16)


def _attention(q, kd, vd, B, S):
    tq = 4 * ATT_ROW_BLOCK
    q3 = q.reshape(B, S, D_ATT)
    k3 = kd.reshape(B, S, 2 * D_KV)
    v3 = vd.reshape(B, S, 2 * D_KV)
    gw = D_ATT // ATT_KV_HEADS
    out = pl.pallas_call(
        _attn_body,
        grid=(B, ATT_KV_HEADS, S // tq),
        in_specs=[
            pl.BlockSpec((None, tq, gw), lambda b, j, i: (b, i, j)),
            pl.BlockSpec((None, S, LANES), lambda b, j, i: (b, 0, j)),
            pl.BlockSpec((None, S, LANES), lambda b, j, i: (b, 0, j)),
        ],
        out_specs=pl.BlockSpec((None, tq, gw), lambda b, j, i: (b, i, j)),
        out_shape=jax.ShapeDtypeStruct((B, S, D_ATT), BF16),
        scratch_shapes=[pltpu.VMEM((S, LANES), BF16)] * 4
        + [pltpu.VMEM((2, ATT_ROW_BLOCK, S), F32), pltpu.VMEM((2, ATT_ROW_BLOCK, S), BF16)],
        compiler_params=_cparams("arbitrary", "arbitrary", "arbitrary"),
        name="attn",
    )(q3, k3, v3)
    return out.reshape(B * S, D_ATT)


def _dot_tn3(a, b):
    tn = lambda x, y: lax.dot_general(x, y, (((0,), (0,)), ((), ())), preferred_element_type=F32)
    ah, al = _split_bf16(a)
    bh, bl = _split_bf16(b)
    return tn(ah, bh) + tn(al, bh) + tn(ah, bl)


def _s5op_body(lam_ref, br_ref, bi_ref, cr_ref, ci_ref, rep_ref, m_ref, et_ref, da_ref, db_ref):
    W = m_ref.shape[1]
    L = CHUNK
    step = lax.broadcasted_iota(jnp.int32, (1, W), 1) // SSM_GROUP

    def along_steps(a, channel_axis):
        hi = a.astype(BF16)
        r1 = a - hi.astype(F32)
        mid = r1.astype(BF16)
        lo = (r1 - mid.astype(F32)).astype(BF16)
        dims = (((channel_axis,), (0,)), ((), ()))
        rep = lambda x: lax.dot_general(x, rep_ref[...], dims, preferred_element_type=F32)
        return rep(hi) + rep(mid) + rep(lo)

    def direction(d):
        lam = lam_ref[d]
        lr = jnp.minimum(lam[:, 0:1], -1e-4)
        li = lam[:, 1:2]
        dt = jnp.exp(lam[:, 2:3])
        mag = jnp.exp(lr * dt)
        ab_r = mag * jnp.cos(li * dt)
        ab_i = mag * jnp.sin(li * dt)
        den = lr * lr + li * li
        nr = ab_r - 1.0
        f_r = (nr * lr + ab_i * li) / den
        f_i = (ab_i * lr - nr * li) / den
        b_r, b_i = along_steps(br_ref[d], 1), along_steps(bi_ref[d], 1)
        bb_r = f_r * b_r - f_i * b_i
        bb_i = f_r * b_i + f_i * b_r
        cr, ci = along_steps(cr_ref[d], 0), along_steps(ci_ref[d], 0)

        def power(n):
            sq_r, sq_i = ab_r, ab_i
            pr = pi = None
            for k in range(CHUNK.bit_length() - 1):
                bit = ((n >> k) & 1) == 1
                tr = jnp.where(bit, sq_r, 1.0)
                ti = jnp.where(bit, sq_i, 0.0)
                pr, pi = (tr, ti) if pr is None else (pr * tr - pi * ti, pr * ti + pi * tr)
                sq_r, sq_i = sq_r * sq_r - sq_i * sq_i, 2.0 * sq_r * sq_i
            return pr, pi

        def times_lam(p):
            return p[0] * ab_r - p[1] * ab_i, p[0] * ab_i + p[1] * ab_r

        def a_b(p):
            return p[0] * bb_r - p[1] * bb_i, p[0] * bb_i + p[1] * bb_r

        def c_a(p):
            return cr * p[0] - ci * p[1], cr * p[1] + ci * p[0]

        up, down = power(step), power(L - 1 - step)
        return bb_r, bb_i, cr, ci, a_b, c_a, up, down, times_lam

    fbr, fbi, _, _, f_ab, f_ca, f_up, f_down, f_next = direction(0)
    bbr, bbi, bcr, bci, b_ab, b_ca, b_up, b_down, b_next = direction(1)
    etf_r, etf_i = f_ab(f_down)
    etb_r, etb_i = b_ab(b_up)
    et_ref[...] = jnp.concatenate([etf_r, etb_r, etf_i, etb_i], axis=0).astype(BF16)
    df_r, df_i = f_ca(f_next(f_up))
    db_r, db_i = b_ca(b_next(b_down))
    zero = jnp.zeros_like(df_r)
    da_ref[...] = jnp.concatenate([df_r, zero, -df_i, zero], axis=0).astype(BF16)
    db_ref[...] = jnp.concatenate([zero, db_r, zero, -db_i], axis=0).astype(BF16)
    kf_r, kf_i = f_ca(f_up)
    lag0 = step == 0
    lhs_f = jnp.concatenate([fbr, -fbi], axis=0)
    lhs_b = jnp.concatenate([bbr, -bbi], axis=0)
    q_b = _dot_tn3(lhs_b, jnp.concatenate([db_r, db_i], axis=0))
    q_f = _dot_tn3(jnp.concatenate([lhs_f, lhs_b], axis=0),
                   jnp.concatenate([kf_r, kf_i, jnp.where(lag0, bcr, 0.0), jnp.where(lag0, bci, 0.0)],
                                   axis=0))
    z = jnp.concatenate([q_b, q_f], axis=1)
    for j in range(L):
        rows = z[j * SSM_GROUP:(j + 1) * SSM_GROUP, :]
        m_ref[j * SSM_GROUP:(j + 1) * SSM_GROUP, :] = pltpu.roll(
            rows, (2 * W - (L - j) * SSM_GROUP) % (2 * W), 1)[:, :W].astype(BF16)


def _s5_operators(lam_re, lam_im, log_step, b_re, b_im, c_re, c_im):
    _, G, P = lam_re.shape
    W = CHUNK * SSM_GROUP
    lam = jnp.stack([lam_re, lam_im, jnp.broadcast_to(log_step[..., None], lam_re.shape)], axis=-1).astype(F32)
    rep = jnp.asarray(np.arange(SSM_GROUP)[:, None] == np.arange(W)[None, :] % SSM_GROUP, dtype=BF16)
    per_g = lambda a: pl.BlockSpec((2, None) + a.shape[2:], lambda g: (0, g, 0, 0))
    out = pl.BlockSpec((None, W, W), lambda g: (g, 0, 0))
    args = (lam, b_re.astype(F32), b_im.astype(F32), c_re.astype(F32), c_im.astype(F32))
    return pl.pallas_call(
        _s5op_body,
        grid=(G,),
        in_specs=[per_g(a) for a in args] + [pl.BlockSpec(rep.shape, lambda g: (0, 0))],
        out_specs=[out] * 4,
        out_shape=[jax.ShapeDtypeStruct((G, W, W), BF16)] * 4,
        compiler_params=_cparams("arbitrary"),
        name="s5op",
    )(*args, rep)


GROUPS_PER_SET = LANES // SSM_GROUP
S5_SCAN_GROUPS = 4
S5_PERM_ROWS = 64


def _transpose8(blocks, axis, piece):
    n = blocks[0].shape[axis]
    q = lax.broadcasted_iota(jnp.int32, blocks[0].shape, axis) // piece
    v = list(blocks)
    for d in (4, 2, 1):
        keep = (q & d) == 0
        nxt = list(v)
        for a in range(8):
            if a & d == 0:
                b = a + d
                nxt[a] = jnp.where(keep, v[a], pltpu.roll(v[b], d * piece, axis))
                nxt[b] = jnp.where(keep, pltpu.roll(v[a], n - d * piece, axis), v[b])
        v = nxt
    return v


def _piece_transpose(blocks):
    return _transpose8(blocks, 1, SSM_GROUP)


def _s5_body(u_ref, m_ref, et_ref, da_ref, db_ref, a_ref, d_ref, z_ref,
             uperm, zperm, xr_n, xi_n, xr_t, xi_t, har_s, hai_s, hbr_s, hbi_s, *, nb):
    R = u_ref.shape[0]
    nchunk = R // nb
    W = uperm.shape[2]
    half = W // 2
    halves = W // LANES

    def permute_in(j, _):
        r = pl.ds(pl.multiple_of(j * S5_PERM_ROWS, S5_PERM_ROWS), S5_PERM_ROWS)
        for h in range(halves):
            out = _piece_transpose([u_ref[r, (h * GROUPS_PER_SET + k) * LANES:(h * GROUPS_PER_SET + k + 1) * LANES]
                                    for k in range(GROUPS_PER_SET)])
            for g in range(GROUPS_PER_SET):
                uperm[g, r, h * LANES:(h + 1) * LANES] = out[g]
        return 0

    lax.fori_loop(0, R // S5_PERM_ROWS, permute_in, 0)
    fwd = lax.broadcasted_iota(jnp.int32, (nb, half), 1) < SSM_STATE

    def swap_row_order(src_a, src_b, dst_a, dst_b, to_chunk_major):
        def blocks(cb, _):
            batch_major = [pl.ds(pl.multiple_of(p * nchunk + cb * nb, nb), nb) for p in range(nb)]
            chunk_major = [pl.ds(pl.multiple_of((cb * nb + j) * nb, nb), nb) for j in range(nb)]
            rows_in, rows_out = (batch_major, chunk_major) if to_chunk_major else (chunk_major, batch_major)
            for k in range(S5_SCAN_GROUPS):
                for src, dst in ((src_a, dst_a), (src_b, dst_b)):
                    out = _transpose8([src[k, r, :] for r in rows_in], 0, 1)
                    for j in range(nb):
                        dst[k, rows_out[j], :] = out[j]
            return 0
        lax.fori_loop(0, nchunk // nb, blocks, 0)

    for g0 in range(0, GROUPS_PER_SET, S5_SCAN_GROUPS):
        coef = []
        for k in range(S5_SCAN_GROUPS):
            g = g0 + k
            x = _dot_nt(uperm[g], et_ref[g])
            xr_n[k] = x[:, :half]
            xi_n[k] = x[:, half:]
            lr = jnp.minimum(a_ref[g, 0:1, :], -1e-4)
            dt = jnp.exp(a_ref[g, 2:3, :])
            mag = jnp.exp(lr * dt * CHUNK)
            ang = a_ref[g, 1:2, :] * dt * CHUNK
            coef.append((jnp.broadcast_to(mag * jnp.cos(ang), (nb, half)),
                         jnp.broadcast_to(mag * jnp.sin(ang), (nb, half))))
        swap_row_order(xr_n, xi_n, xr_t, xi_t, True)

        def step(c, carry):
            rf = pl.ds(pl.multiple_of(c * nb, nb), nb)
            rb = pl.ds(pl.multiple_of((nchunk - 1 - c) * nb, nb), nb)
            new = []
            for k in range(S5_SCAN_GROUPS):
                sr, si = carry[2 * k], carry[2 * k + 1]
                ar, ai = coef[k]
                har_s[k, rf, :] = sr
                hai_s[k, rf, :] = si
                hbr_s[k, rb, :] = sr
                hbi_s[k, rb, :] = si
                xr = jnp.where(fwd, xr_t[k, rf, :], xr_t[k, rb, :])
                xi = jnp.where(fwd, xi_t[k, rf, :], xi_t[k, rb, :])
                new += [sr * ar - si * ai + xr, sr * ai + si * ar + xi]
            return tuple(new)

        zero = jnp.zeros((nb, half), F32)
        lax.fori_loop(0, nchunk, step, (zero,) * (2 * S5_SCAN_GROUPS))
        for k in range(S5_SCAN_GROUPS):
            g = g0 + k
            ha = jnp.concatenate([har_s[k], hai_s[k]], axis=1).astype(BF16)
            hb = jnp.concatenate([hbr_s[k], hbi_s[k]], axis=1).astype(BF16)
            y_t = _dot(ha, da_ref[g]) + _dot(hb, db_ref[g])
            xr_n[k] = y_t[:, :half]
            xi_n[k] = y_t[:, half:]
        swap_row_order(xr_n, xi_n, xr_t, xi_t, False)
        for k in range(S5_SCAN_GROUPS):
            g = g0 + k
            u = uperm[g]
            y = _dot(u, m_ref[g]) + jnp.concatenate([xr_t[k], xi_t[k]], axis=1) + d_ref[g] * u.astype(F32)
            zperm[g] = jax.nn.gelu(y).astype(BF16)

    def permute_out(j, _):
        r = pl.ds(pl.multiple_of(j * S5_PERM_ROWS, S5_PERM_ROWS), S5_PERM_ROWS)
        for h in range(halves):
            out = _piece_transpose([zperm[g, r, h * LANES:(h + 1) * LANES] for g in range(GROUPS_PER_SET)])
            for k in range(GROUPS_PER_SET):
                z_ref[r, (h * GROUPS_PER_SET + k) * LANES:(h * GROUPS_PER_SET + k + 1) * LANES] = out[k]
        return 0

    lax.fori_loop(0, R // S5_PERM_ROWS, permute_out, 0)


def _s5(u_set, B, S, lam_re, lam_im, log_step, b_re, b_im, c_re, c_im, ssm_d):
    n_set, R, WS = u_set.shape
    G = n_set * GROUPS_PER_SET
    W = CHUNK * SSM_GROUP
    assert R == (S // CHUNK) * B and WS == CHUNK * LANES
    M, ET, DA, DB = _s5_operators(lam_re, lam_im, log_step, b_re, b_im, c_re, c_im)
    both = lambda a: jnp.concatenate([a[0], a[1]], axis=-1)
    lam = jnp.stack([both(lam_re), both(lam_im),
                     both(jnp.broadcast_to(log_step[..., None], lam_re.shape))], axis=1).astype(F32)
    dvec = jnp.tile(ssm_d.astype(F32).reshape(G, 1, SSM_GROUP), (1, CHUNK, 1)).reshape(G, 1, W)
    per_set = pl.BlockSpec((None, R, WS), lambda s: (s, 0, 0))
    blk = lambda *sh: pl.BlockSpec((GROUPS_PER_SET,) + sh, lambda s: (s,) + (0,) * len(sh))
    return pl.pallas_call(
        functools.partial(_s5_body, nb=B),
        grid=(n_set,),
        in_specs=[per_set, blk(W, W), blk(W, W), blk(W, W), blk(W, W), blk(3, W // 2), blk(1, W)],
        out_specs=per_set,
        out_shape=jax.ShapeDtypeStruct((n_set, R, WS), BF16),
        scratch_shapes=[pltpu.VMEM((GROUPS_PER_SET, R, W), BF16)] * 2
        + [pltpu.VMEM((S5_SCAN_GROUPS, R, W // 2), F32)] * 8,
        compiler_params=_cparams("arbitrary"),
        name="s5",
    )(u_set, M, ET, DA, DB, lam, dvec)


def _memkv_body(m_ref, g_ref, wk_ref, wv_ref, k_ref, v_ref):
    h = (_rms(m_ref[...]) * g_ref[...]).astype(BF16)
    k_ref[...] = _dot(h, wk_ref[...]).astype(BF16)
    v_ref[...] = _dot(h, wv_ref[...]).astype(BF16)


def _memkv(mem, g, wk, wv):
    B, M, D = mem.shape
    const = lambda b: (0, 0)
    row = pl.BlockSpec((None, M, D), lambda b: (b, 0, 0))
    return pl.pallas_call(
        _memkv_body,
        grid=(B,),
        in_specs=[row, pl.BlockSpec((1, D), const), pl.BlockSpec((D, D), const),
                  pl.BlockSpec((D, D), const)],
        out_specs=[row, row],
        out_shape=[jax.ShapeDtypeStruct((B, M, D), BF16)] * 2,
        compiler_params=_cparams("arbitrary"),
        name="memkv",
    )(mem, g[None, :], wk.astype(BF16), wv.astype(BF16))


MID_ROW_BLOCK = 512


def _mid_body(x_ref, ya_ref, z_ref, gw_ref, gb_ref, ag_ref, sg_ref, wo_ref, xg_ref,
              wq_ref, xwo_ref, k_ref, v_ref, o_ref, zs):
    D = x_ref.shape[1]
    dh = D // XA_HEADS
    n_set = z_ref.shape[0]
    crows = MID_ROW_BLOCK // CHUNK

    def glu(r0, st):
        c0 = r0 // CHUNK
        for j in range(n_set):
            for t in range(CHUNK):
                zs[j, pl.ds(r0 + t, crows, stride=CHUNK), :] = (
                    z_ref[j, c0:c0 + crows, t * LANES:(t + 1) * LANES].astype(F32))
        z = jnp.concatenate([zs[j, r0:r0 + MID_ROW_BLOCK, :] for j in range(n_set)], axis=1).astype(BF16)
        st['s'] = z.astype(F32) * jax.nn.sigmoid(_dot(z, gw_ref[...]) + gb_ref[...])

    def out_proj(r0, st):
        rows = slice(r0, r0 + MID_ROW_BLOCK)
        ya = _rms(ya_ref[rows, :].astype(F32)) * ag_ref[...]
        ys = _rms(st.pop('s')) * sg_ref[...]
        y = jnp.concatenate([ya, ys], axis=1).astype(BF16)
        st['x1'] = x_ref[rows, :] + _dot(y, wo_ref[...])

    def query(r0, st):
        hq = (_rms(st['x1']) * xg_ref[...]).astype(BF16)
        st['q'] = (_dot(hq, wq_ref[...]) * (dh ** -0.5)).astype(BF16)

    def attend(r0, st):
        q = st.pop('q')
        outs = []
        for h in range(XA_HEADS):
            sl = slice(h * dh, (h + 1) * dh)
            sc = _dot_nt(q[:, sl], k_ref[:, sl])
            e = jnp.exp(sc - jnp.max(sc, axis=-1, keepdims=True))
            inv = 1.0 / jnp.sum(e, axis=-1, keepdims=True)
            outs.append((_dot(e.astype(BF16), v_ref[:, sl]) * inv).astype(BF16))
        st['o'] = jnp.concatenate(outs, axis=1)

    def finish(r0, st):
        rows = slice(r0, r0 + MID_ROW_BLOCK)
        o_ref[rows, :D] = st.pop('x1') + _dot(st.pop('o'), xwo_ref[...])
        o_ref[rows, D:] = jnp.zeros((MID_ROW_BLOCK, o_ref.shape[1] - D), F32)

    blocks = [(r0, {}) for r0 in range(0, x_ref.shape[0], MID_ROW_BLOCK)]
    for stage in (glu, out_proj, query, attend, finish):
        for r0, st in blocks:
            stage(r0, st)


def _mid(x2d, S, y_att, z, glu_w, glu_b, att_g, ssm_g, w_out, xa_g, wq, wo, kmem, vmem):
    T, D = x2d.shape
    M = kmem.shape[1]
    tm = 2 * MID_ROW_BLOCK
    sb = S // tm
    const = lambda i: (0, 0)
    tok = lambda w: pl.BlockSpec((tm, w), lambda i: (i, 0))
    full = lambda a: pl.BlockSpec(a.shape, const)
    vec = lambda a: a.astype(F32)[None, :]
    mem = pl.BlockSpec((None, M, D), lambda i: (i // sb, 0, 0))
    args = [x2d, y_att, z, glu_w.astype(BF16), vec(glu_b), vec(att_g), vec(ssm_g), w_out.astype(BF16),
            vec(xa_g), wq.astype(BF16), wo.astype(BF16), kmem, vmem]
    zspec = pl.BlockSpec((z.shape[0], tm // CHUNK, CHUNK * LANES), lambda i: (0, i, 0))
    specs = [tok(D), tok(y_att.shape[1]), zspec] + [full(a) for a in args[3:11]] + [mem, mem]
    return pl.pallas_call(
        _mid_body,
        grid=(T // tm,),
        in_specs=specs,
        out_specs=pl.BlockSpec((tm, D + LANES), lambda i: (i, 0)),
        out_shape=jax.ShapeDtypeStruct((T, D + LANES), F32),
        scratch_shapes=[pltpu.VMEM((z.shape[0], tm, LANES), F32)],
        compiler_params=_cparams("arbitrary"),
        name="mid",
    )(*args)


def _split_bf16(a):
    hi = a.astype(BF16)
    return hi, (a - hi.astype(F32)).astype(BF16)


ROUTE_ROWS = 32
MOE_TILE = 256


def _route_body(x_ref, g_ref, whi_ref, wlo_ref, b_ref, tri_ref, meta_ref, mt_ref, cnt_ref, carry):
    i = pl.program_id(0)
    tm, D = x_ref.shape[0], g_ref.shape[1]

    @pl.when(i == 0)
    def _():
        carry[...] = jnp.zeros_like(carry)

    h = _rms(x_ref[:, :D]) * g_ref[...]
    hhi, hlo = _split_bf16(h)
    logits = (_dot_nt(whi_ref[...], hhi) + _dot_nt(whi_ref[...], hlo) + _dot_nt(wlo_ref[...], hhi)
              + b_ref[:, 0:1])
    ninf = jnp.float32(-jnp.inf)
    cmax = lambda v: jnp.max(v, axis=0, keepdims=True)
    cmin = lambda v: jnp.min(v, axis=0, keepdims=True)
    gl = logits[N_EXPERTS:N_EXPERTS + N_EXPERT_GROUPS]
    grow = lax.broadcasted_iota(jnp.int32, gl.shape, 0)
    gmax = cmax(gl)
    g_sel = cmin(jnp.where(gl == gmax, grow, N_EXPERT_GROUPS))
    g_w = 1.0 / jnp.sum(jnp.exp(gl - gmax), axis=0, keepdims=True)
    el = logits[:N_EXPERTS]
    erow = lax.broadcasted_iota(jnp.int32, el.shape, 0)
    el = jnp.where((erow >> 2) == g_sel, el, ninf)
    v1 = cmax(el)
    i1 = cmin(jnp.where(el == v1, erow, N_EXPERTS))
    el2 = jnp.where(erow == i1, ninf, el)
    v2 = cmax(el2)
    i2 = cmin(jnp.where(el2 == v2, erow, N_EXPERTS))
    t = jnp.exp(v2 - v1)
    w1 = g_w / (1.0 + t)
    w2 = g_w * t / (1.0 + t)
    first_low = i1 < i2
    ia = jnp.minimum(i1, i2) & 3
    ib = jnp.maximum(i1, i2) & 3
    wa = jnp.where(first_low, w1, w2)
    wb = jnp.where(first_low, w2, w1)
    cls = g_sel * PAIRS_PER_GROUP + ((ia * (7 - ia)) >> 1) + ib - ia - 1
    crow = lax.broadcasted_iota(jnp.int32, (ROUTE_ROWS, tm), 0)
    onehot = (crow == cls).astype(F32)
    before = _dot(onehot.astype(BF16), tri_ref[...])
    rank = jnp.sum(onehot * (before + carry[:, 0:1]), axis=0, keepdims=True)
    carry[...] = carry[...] + jnp.sum(onehot, axis=1, keepdims=True)
    cnt_ref[...] = carry[...]

    def record(rows):
        r = lax.broadcasted_iota(jnp.int32, (rows, tm), 0)
        return jnp.where(r == META_CLS, cls.astype(F32),
                         jnp.where(r == META_RANK, rank,
                                   jnp.where(r == META_WA, wa, jnp.where(r == META_WB, wb, 0.0))))

    meta_ref[...] = record(LANES).T
    mt_ref[:, pl.ds(pl.multiple_of(i * tm, tm), tm)] = record(mt_ref.shape[0])

    @pl.when(i == pl.num_programs(0) - 1)
    def _():
        ntile = jnp.floor((carry[...] + (MOE_TILE - 1.0)) * (1.0 / MOE_TILE))
        r = lax.broadcasted_iota(jnp.int32, (ROUTE_ROWS, ROUTE_ROWS), 0)
        c = lax.broadcasted_iota(jnp.int32, (ROUTE_ROWS, ROUTE_ROWS), 1)
        off = _dot((c < r).astype(BF16), ntile.astype(BF16)) * MOE_TILE
        cls_all = mt_ref[META_CLS:META_CLS + 1, :]
        pos = mt_ref[META_RANK:META_RANK + 1, :]
        for k in range(N_CLASSES):
            pos = pos + jnp.where(cls_all == k, off[k:k + 1, 0:1], 0.0)
        mt_ref[META_POS:META_POS + 1, :] = pos


def _route(x2a, D, moe_g, grp_w, grp_b, exp_w, exp_b):
    T = x2a.shape[0]
    tm = 256
    pad = ROUTE_ROWS - N_EXPERT_GROUPS - N_EXPERTS
    w = jnp.concatenate([exp_w, grp_w, jnp.zeros((D, pad), F32)], axis=1).astype(F32).T
    b = jnp.concatenate([exp_b, grp_b, jnp.zeros((pad,), F32)]).astype(F32)
    whi, wlo = _split_bf16(w)
    tri = jnp.asarray(np.arange(tm)[:, None] < np.arange(tm)[None, :], dtype=BF16)
    const = lambda i: (0, 0)
    ncol = D // LANES
    return pl.pallas_call(
        _route_body,
        grid=(T // tm,),
        in_specs=[pl.BlockSpec((tm, D + LANES), lambda i: (i, 0)),
                  pl.BlockSpec((1, D), const), pl.BlockSpec((ROUTE_ROWS, D), const),
                  pl.BlockSpec((ROUTE_ROWS, D), const), pl.BlockSpec((ROUTE_ROWS, LANES), const),
                  pl.BlockSpec((tm, tm), const)],
        out_specs=[pl.BlockSpec((tm, LANES), lambda i: (i, ncol)),
                   pl.BlockSpec((SUBLANES, T), const),
                   pl.BlockSpec((ROUTE_ROWS, LANES), const)],
        out_shape=[jax.ShapeDtypeStruct(x2a.shape, F32), jax.ShapeDtypeStruct((SUBLANES, T), F32),
                   jax.ShapeDtypeStruct((ROUTE_ROWS, LANES), F32)],
        scratch_shapes=[pltpu.VMEM((ROUTE_ROWS, LANES), F32)],
        input_output_aliases={0: 0},
        compiler_params=_cparams("arbitrary"),
        name="route",
    )(x2a, moe_g.astype(F32)[None, :], whi, wlo, jnp.broadcast_to(b[:, None], (ROUTE_ROWS, LANES)), tri)


ROW_COPY_UNROLL = 8


def _row_copy_body(pos_ref, src_ref, *rest, rows, scatter):
    dst_ref, sem = rest[-2:]

    def copy(j, k):
        p = pos_ref[j * ROW_COPY_UNROLL + k]
        tok = (j, pl.ds(k, 1))
        slot = (lax.shift_right_logical(p, 3), pl.ds(p & (SUBLANES - 1), 1))
        if scatter:
            return pltpu.make_async_copy(src_ref.at[tok], dst_ref.at[slot], sem)
        return pltpu.make_async_copy(src_ref.at[slot], dst_ref.at[tok], sem)

    def start(j, _):
        for k in range(ROW_COPY_UNROLL):
            copy(j, k).start(priority=k % 2)
        return 0

    def wait(j, _):
        for k in range(ROW_COPY_UNROLL):
            copy(j, k).wait()
        return 0

    lax.fori_loop(0, rows // ROW_COPY_UNROLL, start, 0)
    lax.fori_loop(0, rows // ROW_COPY_UNROLL, wait, 0)


def _row_copy(pos, src, *, dst_init=None, dst_rows=None, name):
    assert ROW_COPY_UNROLL == SUBLANES
    T = pos.shape[0]
    rows = 512
    width = src.shape[1]
    scatter = dst_init is not None
    hbm = pl.BlockSpec(memory_space=pl.ANY)
    tok = pl.BlockSpec((rows // SUBLANES, SUBLANES, width), lambda i: (i, 0, 0))
    out_rows = dst_init.shape[0] if scatter else dst_rows
    tiled = lambda a: a.reshape(a.shape[0] // SUBLANES, SUBLANES, width)
    out = pl.pallas_call(
        functools.partial(_row_copy_body, rows=rows, scatter=scatter),
        grid=(T // rows,),
        in_specs=[pl.BlockSpec((rows,), lambda i: (i,), memory_space=pltpu.SMEM)]
        + ([tok, hbm] if scatter else [hbm]),
        out_specs=hbm if scatter else tok,
        out_shape=jax.ShapeDtypeStruct((out_rows // SUBLANES, SUBLANES, width), src.dtype),
        scratch_shapes=[pltpu.SemaphoreType.DMA(())],
        input_output_aliases={2: 0} if scatter else {},
        compiler_params=pltpu.CompilerParams(dimension_semantics=("arbitrary",)),
        name=name,
    )(*((pos, tiled(src), tiled(dst_init)) if scatter else (pos, tiled(src))))
    return out.reshape(out_rows, width)


def _ffn_body(ea_ref, eb_ref, nt_ref, xs_ref, g_ref, fg_ref, wga, wua, wda, wgb, wub, wdb, o_ref):
    del ea_ref, eb_ref
    D = g_ref.shape[1]
    valid = pl.program_id(0) < nt_ref[0]

    @pl.when(valid)
    def _():
        x = xs_ref[:, :D]
        meta = xs_ref[:, D:]
        h = (_rms(x) * g_ref[...]).astype(BF16)
        y = x
        for wg, wu, wd, lane in ((wga, wua, wda, META_WA), (wgb, wub, wdb, META_WB)):
            a = jax.nn.silu(_dot(h, wg[...])) * _dot(h, wu[...]) * meta[:, lane:lane + 1]
            y = y + _dot(a.astype(BF16), wd[...])
        o_ref[...] = _rms(y) * fg_ref[...]

    @pl.when(jnp.logical_not(valid))
    def _():
        o_ref[...] = jnp.zeros_like(o_ref)


def _ffn(xs, D, tile_ea, tile_eb, n_tiles, moe_g, final_g, w_gate, w_up, w_down):
    tm = MOE_TILE
    P = xs.shape[0]
    nt_max = P // tm
    De = w_gate.shape[2]
    wg = w_gate.astype(BF16)
    wu = w_up.astype(BF16)
    wd = w_down.astype(BF16)
    const = lambda i, ea, eb, nt: (0, 0)
    sel_a = lambda i, ea, eb, nt: (ea[i], 0, 0)
    sel_b = lambda i, ea, eb, nt: (eb[i], 0, 0)
    up = lambda sel: pl.BlockSpec((None, D, De), sel)
    down = lambda sel: pl.BlockSpec((None, De, D), sel)
    grid_spec = pltpu.PrefetchScalarGridSpec(
        num_scalar_prefetch=3,
        grid=(nt_max,),
        in_specs=[pl.BlockSpec((tm, D + LANES),
                               lambda i, ea, eb, nt: (jnp.maximum(jnp.minimum(i, nt[0] - 1), 0), 0)),
                  pl.BlockSpec((1, D), const), pl.BlockSpec((1, D), const),
                  up(sel_a), up(sel_a), down(sel_a), up(sel_b), up(sel_b), down(sel_b)],
        out_specs=pl.BlockSpec((tm, D), lambda i, ea, eb, nt: (i, 0)),
    )
    return pl.pallas_call(
        _ffn_body,
        grid_spec=grid_spec,
        out_shape=jax.ShapeDtypeStruct((P, D), F32),
        compiler_params=_cparams("arbitrary"),
        name="ffn",
    )(tile_ea, tile_eb, n_tiles, xs, moe_g.astype(F32)[None, :], final_g.astype(F32)[None, :],
      wg, wu, wd, wg, wu, wd)


def _moe(x2a, D, moe_g, grp_w, grp_b, exp_w, exp_b, w_gate, w_up, w_down, final_g):
    T = x2a.shape[0]
    tm = MOE_TILE
    x2a, record, counts = _route(x2a, D, moe_g, grp_w, grp_b, exp_w, exp_b)
    pos = record[META_POS].astype(jnp.int32)
    cnt = counts[:N_CLASSES, 0].astype(jnp.int32)
    ntile_c = (cnt + tm - 1) // tm
    tile_end = jnp.cumsum(ntile_c)
    nt_max = T // tm + N_CLASSES
    n_tiles = tile_end[-1:]
    tile_id = jnp.arange(nt_max, dtype=jnp.int32)
    tile_cls = jnp.minimum(jnp.sum((tile_end[None, :] <= tile_id[:, None]).astype(jnp.int32), axis=1),
                           N_CLASSES - 1)
    tile_cls = jnp.where(tile_id < n_tiles[0], tile_cls, tile_cls[jnp.maximum(n_tiles[0] - 1, 0)])
    pa = jnp.asarray([0, 0, 0, 1, 1, 2], jnp.int32)
    pb = jnp.asarray([1, 2, 3, 2, 3, 3], jnp.int32)
    grp = tile_cls // PAIRS_PER_GROUP
    tile_ea = grp * EXPERTS_PER_GROUP + pa[tile_cls % PAIRS_PER_GROUP]
    tile_eb = grp * EXPERTS_PER_GROUP + pb[tile_cls % PAIRS_PER_GROUP]
    xs = _row_copy(pos, x2a, dst_init=jnp.zeros((nt_max * tm, D + LANES), F32), name="dispatch")
    ys = _ffn(xs, D, tile_ea, tile_eb, n_tiles.astype(jnp.int32), moe_g, final_g, w_gate, w_up, w_down)
    return _row_copy(pos, ys, dst_rows=T, name="unsort")


def kernel(x, mem, mix_norm_g, w_in, q_norm_g, k_norm_g, ssm_lambda_re, ssm_lambda_im, ssm_log_step, ssm_b_re, ssm_b_im, ssm_c_re, ssm_c_im, ssm_d, ssm_glu_w, ssm_glu_b, att_out_g, ssm_out_g, w_out, xa_norm_g, mem_norm_g, xa_wq, xa_wk, xa_wv, xa_wo, moe_norm_g, router_grp_w, router_grp_b, router_exp_w, router_exp_b, exp_w_gate, exp_w_up, exp_w_down, final_norm_g):
    B, S, D = x.shape
    assert mix_norm_g.shape[0] == 1, "single layer"
    x2d = x.reshape(B * S, D)
    q, kd, vd, u = _inproj(x2d, S, mix_norm_g[0], w_in[0], q_norm_g[0], k_norm_g[0])
    y_att = _attention(q, kd, vd, B, S)
    z = _s5(u, B, S, ssm_lambda_re[0], ssm_lambda_im[0], ssm_log_step[0], ssm_b_re[0], ssm_b_im[0],
            ssm_c_re[0], ssm_c_im[0], ssm_d[0])
    kmem, vmem = _memkv(mem, mem_norm_g[0], xa_wk[0], xa_wv[0])
    x2a = _mid(x2d, S, y_att, z, ssm_glu_w[0], ssm_glu_b[0], att_out_g[0], ssm_out_g[0], w_out[0],
               xa_norm_g[0], xa_wq[0], xa_wo[0], kmem, vmem)
    out = _moe(x2a, D, moe_norm_g[0], router_grp_w[0], router_grp_b[0], router_exp_w[0],
               router_exp_b[0], exp_w_gate[0], exp_w_up[0], exp_w_down[0], final_norm_g)
    return out.reshape(B, S, D)
```

```python
import functools

import numpy as np
import jax
import jax.numpy as jnp
from jax import lax
from jax.experimental import pallas as pl
from jax.experimental.pallas import tpu as pltpu

F32 = jnp.float32
BF16 = jnp.bfloat16

EPS = 1e-6
GRID_W = 64
ROPE_THETA = 10000.0
ATT_HEADS = 8
ATT_KV_HEADS = 2
HEAD_DIM = 64
D_ATT = ATT_HEADS * HEAD_DIM
D_KV = ATT_KV_HEADS * HEAD_DIM
SSM_GROUP = 16
SSM_STATE = 64
XA_HEADS = 4
N_EXPERT_GROUPS = 4
EXPERTS_PER_GROUP = 4
N_EXPERTS = N_EXPERT_GROUPS * EXPERTS_PER_GROUP
PAIRS_PER_GROUP = 6
N_CLASSES = N_EXPERT_GROUPS * PAIRS_PER_GROUP

LANES = 128
SUBLANES = 8
CHUNK = 16
VMEM_LIMIT = 56 * 1024 * 1024

META_CLS, META_RANK, META_WA, META_WB, META_POS = 0, 1, 2, 3, 4


def _cparams(*sem):
    return pltpu.CompilerParams(dimension_semantics=sem, vmem_limit_bytes=VMEM_LIMIT)


def _rms(x):
    return x * lax.rsqrt(jnp.mean(x * x, axis=-1, keepdims=True) + EPS)


def _dot(a, b):
    return jnp.dot(a, b, preferred_element_type=F32)


def _dot_nt(a, b):
    return lax.dot_general(a, b, (((1,), (1,)), ((), ())), preferred_element_type=F32)


def _inproj_body(x_ref, g_ref, w_ref, qkg_ref, ones_ref, cos_ref, sin_ref,
                 q_ref, k_ref, v_ref, u_ref, us):
    n_qk = q_ref.shape[1] + k_ref.shape[1]
    h = _rms(x_ref[...]) * g_ref[...]
    proj = _dot(h.astype(BF16), w_ref[...])
    qk = proj[:, :n_qk]
    ssq = _dot((qk * qk).astype(BF16), ones_ref[...])
    qk = qk * lax.rsqrt(ssq * (1.0 / HEAD_DIM) + EPS) * qkg_ref[...]
    cos = cos_ref[...]
    sin = sin_ref[...]
    lane = lax.broadcasted_iota(jnp.int32, cos.shape, 1)
    first = (lane % 32) < 16
    nq = q_ref.shape[1] // LANES
    for c in range(n_qk // LANES):
        xc = qk[:, c * LANES:(c + 1) * LANES]
        partner = jnp.where(first, pltpu.roll(xc, LANES - 16, 1), pltpu.roll(xc, 16, 1))
        rot = xc * cos + partner * sin
        if c < nq:
            q_ref[:, c * LANES:(c + 1) * LANES] = (rot * (HEAD_DIM ** -0.5)).astype(BF16)
        else:
            k_ref[:, (c - nq) * LANES:(c - nq + 1) * LANES] = rot.astype(BF16)
    nv = n_qk + v_ref.shape[1]
    v_ref[...] = proj[:, n_qk:nv].astype(BF16)
    rows = u_ref.shape[1]
    for j in range(u_ref.shape[0]):
        us[j] = proj[:, nv + j * LANES:nv + (j + 1) * LANES]
        for t in range(CHUNK):
            u_ref[j, :, t * LANES:(t + 1) * LANES] = us[j, pl.ds(t, rows, stride=CHUNK), :].astype(BF16)


def _rope_tables(S):
    t = np.arange(S)
    freqs = (ROPE_THETA ** (-np.arange(16, dtype=np.float32) / 16)).astype(np.float32)
    halves = []
    for pos in (t // GRID_W, t % GRID_W):
        ang = pos.astype(np.float32)[:, None] * freqs[None, :]
        halves.append((np.cos(ang), np.sin(ang)))
    cos_h = np.concatenate([c for c, _ in halves for _ in range(2)], axis=1)
    sin_h = np.concatenate([sg * s for _, s in halves for sg in (-1.0, 1.0)], axis=1)
    return (jnp.asarray(np.concatenate([cos_h, cos_h], axis=1), F32),
            jnp.asarray(np.concatenate([sin_h, sin_h], axis=1), F32))


def _inproj(x2d, S, mix_g, w_in, q_g, k_g):
    T, D = x2d.shape
    tm = 512
    d_ssm = w_in.shape[1] - D_ATT - 2 * D_KV
    wq = w_in[:, :D_ATT]
    wk = w_in[:, D_ATT:D_ATT + D_KV]
    wv = w_in[:, D_ATT + D_KV:D_ATT + 2 * D_KV]
    wu = w_in[:, D_ATT + 2 * D_KV:]
    dup = lambda w: jnp.concatenate([w[:, :64], w[:, :64], w[:, 64:], w[:, 64:]], axis=1)
    w_ext = jnp.concatenate([wq, dup(wk), dup(wv), wu], axis=1).astype(BF16)
    n_qk = D_ATT + 2 * D_KV
    qkg = jnp.concatenate([jnp.tile(q_g, ATT_HEADS), jnp.tile(k_g, 2 * ATT_KV_HEADS)])[None, :]
    hid = np.arange(n_qk) // HEAD_DIM
    ones = jnp.asarray((hid[:, None] == hid[None, :]).astype(np.float32), dtype=BF16)
    cos_t, sin_t = _rope_tables(S)
    sb = S // tm
    const = lambda i: (0, 0)
    return pl.pallas_call(
        _inproj_body,
        grid=(T // tm,),
        in_specs=[
            pl.BlockSpec((tm, D), lambda i: (i, 0)),
            pl.BlockSpec((1, D), const),
            pl.BlockSpec(w_ext.shape, const),
            pl.BlockSpec((1, n_qk), const),
            pl.BlockSpec((n_qk, n_qk), const),
            pl.BlockSpec((tm, LANES), lambda i: (i % sb, 0)),
            pl.BlockSpec((tm, LANES), lambda i: (i % sb, 0)),
        ],
        out_specs=[
            pl.BlockSpec((tm, D_ATT), lambda i: (i, 0)),
            pl.BlockSpec((tm, 2 * D_KV), lambda i: (i, 0)),
            pl.BlockSpec((tm, 2 * D_KV), lambda i: (i, 0)),
            pl.BlockSpec((d_ssm // LANES, tm // CHUNK, CHUNK * LANES), lambda i: (0, i, 0)),
        ],
        out_shape=[
            jax.ShapeDtypeStruct((T, D_ATT), BF16),
            jax.ShapeDtypeStruct((T, 2 * D_KV), BF16),
            jax.ShapeDtypeStruct((T, 2 * D_KV), BF16),
            jax.ShapeDtypeStruct((d_ssm // LANES, T // CHUNK, CHUNK * LANES), BF16),
        ],
        scratch_shapes=[pltpu.VMEM((d_ssm // LANES, tm, LANES), F32)],
        compiler_params=_cparams("arbitrary"),
        name="inproj",
    )(x2d, mix_g[None, :], w_ext, qkg, ones, cos_t, sin_t)


ATT_KEY_CHUNK = 256
ATT_ROW_BLOCK = 256


def _attn_body(q_ref, k_ref, v_ref, o_ref, klo, khi, vlo, vhi, s_scr, e_scr):
    @pl.when(pl.program_id(2) == 0)
    def _():
        lane = lax.broadcasted_iota(jnp.int32, k_ref.shape, 1)
        lo = lane < HEAD_DIM
        kd = k_ref[...]
        vd = v_ref[...]
        zero = jnp.zeros_like(kd)
        klo[...] = jnp.where(lo, kd, zero)
        khi[...] = jnp.where(lo, zero, kd)
        vlo[...] = jnp.where(lo, vd, (lane == HEAD_DIM).astype(BF16))
        vhi[...] = jnp.where(lo, (lane == 0).astype(BF16), vd)

    S = klo.shape[0]
    rb = s_scr.shape[1]
    units = [(r0, p, km, vm) for r0 in range(0, q_ref.shape[0], rb)
             for p in range(2) for km, vm in ((klo, vlo), (khi, vhi))]

    def scores(i):
        r0, p, km, _ = units[i]
        qp = q_ref[r0:r0 + rb, p * LANES:(p + 1) * LANES]
        m = None
        for c in range(0, S, ATT_KEY_CHUNK):
            s = _dot_nt(qp, km[c:c + ATT_KEY_CHUNK, :])
            s_scr[i % 2, :, c:c + ATT_KEY_CHUNK] = s
            for l0 in range(0, ATT_KEY_CHUNK, LANES):
                m = s[:, l0:l0 + LANES] if m is None else jnp.maximum(m, s[:, l0:l0 + LANES])
        return jnp.max(m, axis=-1, keepdims=True)

    def values(i, m):
        vm = units[i][3]
        for c in range(0, S, ATT_KEY_CHUNK):
            e_scr[i % 2, :, c:c + ATT_KEY_CHUNK] = jnp.exp(s_scr[i % 2, :, c:c + ATT_KEY_CHUNK] - m).astype(BF16)
        acc = _dot(e_scr[i % 2], vm[...])
        ones_lane = HEAD_DIM if vm is vlo else 0
        return acc * (1.0 / acc[:, ones_lane:ones_lane + 1])

    lo = lax.broadcasted_iota(jnp.int32, (rb, LANES), 1) < HEAD_DIM
    m_prev = scores(0)
    first = None
    for i, (r0, p, _, vm) in enumerate(units):
        m_next = scores(i + 1) if i + 1 < len(units) else None
        o = values(i, m_prev)
        m_prev = m_next
        if vm is vlo:
            first = o
        else:
            o_ref[r0:r0 + rb, p * LANES:(p + 1) * LANES] = jnp.where(lo, first, o).astype(BF16)


def _attention(q, kd, vd, B, S):
    tq = 4 * ATT_ROW_BLOCK
    q3 = q.reshape(B, S, D_ATT)
    k3 = kd.reshape(B, S, 2 * D_KV)
    v3 = vd.reshape(B, S, 2 * D_KV)
    gw = D_ATT // ATT_KV_HEADS
    out = pl.pallas_call(
        _attn_body,
        grid=(B, ATT_KV_HEADS, S // tq),
        in_specs=[
            pl.BlockSpec((None, tq, gw), lambda b, j, i: (b, i, j)),
            pl.BlockSpec((None, S, LANES), lambda b, j, i: (b, 0, j)),
            pl.BlockSpec((None, S, LANES), lambda b, j, i: (b, 0, j)),
        ],
        out_specs=pl.BlockSpec((None, tq, gw), lambda b, j, i: (b, i, j)),
        out_shape=jax.ShapeDtypeStruct((B, S, D_ATT), BF16),
        scratch_shapes=[pltpu.VMEM((S, LANES), BF16)] * 4
        + [pltpu.VMEM((2, ATT_ROW_BLOCK, S), F32), pltpu.VMEM((2, ATT_ROW_BLOCK, S), BF16)],
        compiler_params=_cparams("arbitrary", "arbitrary", "arbitrary"),
        name="attn",
    )(q3, k3, v3)
    return out.reshape(B * S, D_ATT)


def _dot_tn3(a, b):
    tn = lambda x, y: lax.dot_general(x, y, (((0,), (0,)), ((), ())), preferred_element_type=F32)
    ah, al = _split_bf16(a)
    bh, bl = _split_bf16(b)
    return tn(ah, bh) + tn(al, bh) + tn(ah, bl)


def _s5op_body(lam_ref, br_ref, bi_ref, cr_ref, ci_ref, rep_ref, m_ref, et_ref, da_ref, db_ref):
    W = m_ref.shape[1]
    L = CHUNK
    step = lax.broadcasted_iota(jnp.int32, (1, W), 1) // SSM_GROUP

    def along_steps(a, channel_axis):
        hi = a.astype(BF16)
        r1 = a - hi.astype(F32)
        mid = r1.astype(BF16)
        lo = (r1 - mid.astype(F32)).astype(BF16)
        dims = (((channel_axis,), (0,)), ((), ()))
        rep = lambda x: lax.dot_general(x, rep_ref[...], dims, preferred_element_type=F32)
        return rep(hi) + rep(mid) + rep(lo)

    def direction(d):
        lam = lam_ref[d]
        lr = jnp.minimum(lam[:, 0:1], -1e-4)
        li = lam[:, 1:2]
        dt = jnp.exp(lam[:, 2:3])
        mag = jnp.exp(lr * dt)
        ab_r = mag * jnp.cos(li * dt)
        ab_i = mag * jnp.sin(li * dt)
        den = lr * lr + li * li
        nr = ab_r - 1.0
        f_r = (nr * lr + ab_i * li) / den
        f_i = (ab_i * lr - nr * li) / den
        b_r, b_i = along_steps(br_ref[d], 1), along_steps(bi_ref[d], 1)
        bb_r = f_r * b_r - f_i * b_i
        bb_i = f_r * b_i + f_i * b_r
        cr, ci = along_steps(cr_ref[d], 0), along_steps(ci_ref[d], 0)

        def power(n):
            sq_r, sq_i = ab_r, ab_i
            pr = pi = None
            for k in range(CHUNK.bit_length() - 1):
                bit = ((n >> k) & 1) == 1
                tr = jnp.where(bit, sq_r, 1.0)
                ti = jnp.where(bit, sq_i, 0.0)
                pr, pi = (tr, ti) if pr is None else (pr * tr - pi * ti, pr * ti + pi * tr)
                sq_r, sq_i = sq_r * sq_r - sq_i * sq_i, 2.0 * sq_r * sq_i
            return pr, pi

        def times_lam(p):
            return p[0] * ab_r - p[1] * ab_i, p[0] * ab_i + p[1] * ab_r

        def a_b(p):
            return p[0] * bb_r - p[1] * bb_i, p[0] * bb_i + p[1] * bb_r

        def c_a(p):
            return cr * p[0] - ci * p[1], cr * p[1] + ci * p[0]

        up, down = power(step), power(L - 1 - step)
        return bb_r, bb_i, cr, ci, a_b, c_a, up, down, times_lam

    fbr, fbi, _, _, f_ab, f_ca, f_up, f_down, f_next = direction(0)
    bbr, bbi, bcr, bci, b_ab, b_ca, b_up, b_down, b_next = direction(1)
    etf_r, etf_i = f_ab(f_down)
    etb_r, etb_i = b_ab(b_up)
    et_ref[...] = jnp.concatenate([etf_r, etb_r, etf_i, etb_i], axis=0).astype(BF16)
    df_r, df_i = f_ca(f_next(f_up))
    db_r, db_i = b_ca(b_next(b_down))
    zero = jnp.zeros_like(df_r)
    da_ref[...] = jnp.concatenate([df_r, zero, -df_i, zero], axis=0).astype(BF16)
    db_ref[...] = jnp.concatenate([zero, db_r, zero, -db_i], axis=0).astype(BF16)
    kf_r, kf_i = f_ca(f_up)
    lag0 = step == 0
    lhs_f = jnp.concatenate([fbr, -fbi], axis=0)
    lhs_b = jnp.concatenate([bbr, -bbi], axis=0)
    q_b = _dot_tn3(lhs_b, jnp.concatenate([db_r, db_i], axis=0))
    q_f = _dot_tn3(jnp.concatenate([lhs_f, lhs_b], axis=0),
                   jnp.concatenate([kf_r, kf_i, jnp.where(lag0, bcr, 0.0), jnp.where(lag0, bci, 0.0)],
                                   axis=0))
    z = jnp.concatenate([q_b, q_f], axis=1)
    for j in range(L):
        rows = z[j * SSM_GROUP:(j + 1) * SSM_GROUP, :]
        m_ref[j * SSM_GROUP:(j + 1) * SSM_GROUP, :] = pltpu.roll(
            rows, (2 * W - (L - j) * SSM_GROUP) % (2 * W), 1)[:, :W].astype(BF16)


def _s5_operators(lam_re, lam_im, log_step, b_re, b_im, c_re, c_im):
    _, G, P = lam_re.shape
    W = CHUNK * SSM_GROUP
    lam = jnp.stack([lam_re, lam_im, jnp.broadcast_to(log_step[..., None], lam_re.shape)], axis=-1).astype(F32)
    rep = jnp.asarray(np.arange(SSM_GROUP)[:, None] == np.arange(W)[None, :] % SSM_GROUP, dtype=BF16)
    per_g = lambda a: pl.BlockSpec((2, None) + a.shape[2:], lambda g: (0, g, 0, 0))
    out = pl.BlockSpec((None, W, W), lambda g: (g, 0, 0))
    args = (lam, b_re.astype(F32), b_im.astype(F32), c_re.astype(F32), c_im.astype(F32))
    return pl.pallas_call(
        _s5op_body,
        grid=(G,),
        in_specs=[per_g(a) for a in args] + [pl.BlockSpec(rep.shape, lambda g: (0, 0))],
        out_specs=[out] * 4,
        out_shape=[jax.ShapeDtypeStruct((G, W, W), BF16)] * 4,
        compiler_params=_cparams("arbitrary"),
        name="s5op",
    )(*args, rep)


GROUPS_PER_SET = LANES // SSM_GROUP
S5_SCAN_GROUPS = 4
S5_PERM_ROWS = 128


def _transpose8(blocks, axis, piece):
    n = blocks[0].shape[axis]
    q = lax.broadcasted_iota(jnp.int32, blocks[0].shape, axis) // piece
    v = list(blocks)
    for d in (4, 2, 1):
        keep = (q & d) == 0
        nxt = list(v)
        for a in range(8):
            if a & d == 0:
                b = a + d
                nxt[a] = jnp.where(keep, v[a], pltpu.roll(v[b], d * piece, axis))
                nxt[b] = jnp.where(keep, pltpu.roll(v[a], n - d * piece, axis), v[b])
        v = nxt
    return v


def _piece_transpose(blocks):
    return _transpose8(blocks, 1, SSM_GROUP)


def _s5_body(u_ref, m_ref, et_ref, da_ref, db_ref, a_ref, d_ref, z_ref,
             uperm, zperm, xr_n, xi_n, xr_t, xi_t, har_s, hai_s, hbr_s, hbi_s, *, nb):
    R = u_ref.shape[0]
    nchunk = R // nb
    W = uperm.shape[2]
    half = W // 2
    halves = W // LANES

    def permute_in(j, _):
        r = pl.ds(pl.multiple_of(j * S5_PERM_ROWS, S5_PERM_ROWS), S5_PERM_ROWS)
        for h in range(halves):
            out = _piece_transpose([u_ref[r, (h * GROUPS_PER_SET + k) * LANES:(h * GROUPS_PER_SET + k + 1) * LANES]
                                    for k in range(GROUPS_PER_SET)])
            for g in range(GROUPS_PER_SET):
                uperm[g, r, h * LANES:(h + 1) * LANES] = out[g]
        return 0

    lax.fori_loop(0, R // S5_PERM_ROWS, permute_in, 0)
    fwd = lax.broadcasted_iota(jnp.int32, (nb, half), 1) < SSM_STATE

    def swap_row_order(src_a, src_b, dst_a, dst_b, to_chunk_major):
        def blocks(cb, _):
            batch_major = [pl.ds(pl.multiple_of(p * nchunk + cb * nb, nb), nb) for p in range(nb)]
            chunk_major = [pl.ds(pl.multiple_of((cb * nb + j) * nb, nb), nb) for j in range(nb)]
            rows_in, rows_out = (batch_major, chunk_major) if to_chunk_major else (chunk_major, batch_major)
            for k in range(S5_SCAN_GROUPS):
                for src, dst in ((src_a, dst_a), (src_b, dst_b)):
                    out = _transpose8([src[k, r, :] for r in rows_in], 0, 1)
                    for j in range(nb):
                        dst[k, rows_out[j], :] = out[j]
            return 0
        lax.fori_loop(0, nchunk // nb, blocks, 0)

    for g0 in range(0, GROUPS_PER_SET, S5_SCAN_GROUPS):
        coef = []
        for k in range(S5_SCAN_GROUPS):
            g = g0 + k
            x = _dot_nt(uperm[g], et_ref[g])
            xr_n[k] = x[:, :half]
            xi_n[k] = x[:, half:]
            lr = jnp.minimum(a_ref[g, 0:1, :], -1e-4)
            dt = jnp.exp(a_ref[g, 2:3, :])
            mag = jnp.exp(lr * dt * CHUNK)
            ang = a_ref[g, 1:2, :] * dt * CHUNK
            coef.append((jnp.broadcast_to(mag * jnp.cos(ang), (nb, half)),
                         jnp.broadcast_to(mag * jnp.sin(ang), (nb, half))))
        swap_row_order(xr_n, xi_n, xr_t, xi_t, True)

        def step(c, carry):
            rf = pl.ds(pl.multiple_of(c * nb, nb), nb)
            rb = pl.ds(pl.multiple_of((nchunk - 1 - c) * nb, nb), nb)
            new = []
            for k in range(S5_SCAN_GROUPS):
                sr, si = carry[2 * k], carry[2 * k + 1]
                ar, ai = coef[k]
                har_s[k, rf, :] = sr
                hai_s[k, rf, :] = si
                hbr_s[k, rb, :] = sr
                hbi_s[k, rb, :] = si
                xr = jnp.where(fwd, xr_t[k, rf, :], xr_t[k, rb, :])
                xi = jnp.where(fwd, xi_t[k, rf, :], xi_t[k, rb, :])
                new += [sr * ar - si * ai + xr, sr * ai + si * ar + xi]
            return tuple(new)

        zero = jnp.zeros((nb, half), F32)
        lax.fori_loop(0, nchunk, step, (zero,) * (2 * S5_SCAN_GROUPS))
        for k in range(S5_SCAN_GROUPS):
            g = g0 + k
            ha = jnp.concatenate([har_s[k], hai_s[k]], axis=1).astype(BF16)
            hb = jnp.concatenate([hbr_s[k], hbi_s[k]], axis=1).astype(BF16)
            y_t = _dot(ha, da_ref[g]) + _dot(hb, db_ref[g])
            xr_n[k] = y_t[:, :half]
            xi_n[k] = y_t[:, half:]
        swap_row_order(xr_n, xi_n, xr_t, xi_t, False)
        for k in range(S5_SCAN_GROUPS):
            g = g0 + k
            u = uperm[g]
            y = _dot(u, m_ref[g]) + jnp.concatenate([xr_t[k], xi_t[k]], axis=1) + d_ref[g] * u.astype(F32)
            zperm[g] = jax.nn.gelu(y).astype(BF16)

    def permute_out(j, _):
        r = pl.ds(pl.multiple_of(j * S5_PERM_ROWS, S5_PERM_ROWS), S5_PERM_ROWS)
        for h in range(halves):
            out = _piece_transpose([zperm[g, r, h * LANES:(h + 1) * LANES] for g in range(GROUPS_PER_SET)])
            for k in range(GROUPS_PER_SET):
                z_ref[r, (h * GROUPS_PER_SET + k) * LANES:(h * GROUPS_PER_SET + k + 1) * LANES] = out[k]
        return 0

    lax.fori_loop(0, R // S5_PERM_ROWS, permute_out, 0)


def _s5(u_set, B, S, lam_re, lam_im, log_step, b_re, b_im, c_re, c_im, ssm_d):
    n_set, R, WS = u_set.shape
    G = n_set * GROUPS_PER_SET
    W = CHUNK * SSM_GROUP
    assert R == (S // CHUNK) * B and WS == CHUNK * LANES
    M, ET, DA, DB = _s5_operators(lam_re, lam_im, log_step, b_re, b_im, c_re, c_im)
    both = lambda a: jnp.concatenate([a[0], a[1]], axis=-1)
    lam = jnp.stack([both(lam_re), both(lam_im),
                     both(jnp.broadcast_to(log_step[..., None], lam_re.shape))], axis=1).astype(F32)
    dvec = jnp.tile(ssm_d.astype(F32).reshape(G, 1, SSM_GROUP), (1, CHUNK, 1)).reshape(G, 1, W)
    per_set = pl.BlockSpec((None, R, WS), lambda s: (s, 0, 0))
    blk = lambda *sh: pl.BlockSpec((GROUPS_PER_SET,) + sh, lambda s: (s,) + (0,) * len(sh))
    return pl.pallas_call(
        functools.partial(_s5_body, nb=B),
        grid=(n_set,),
        in_specs=[per_set, blk(W, W), blk(W, W), blk(W, W), blk(W, W), blk(3, W // 2), blk(1, W)],
        out_specs=per_set,
        out_shape=jax.ShapeDtypeStruct((n_set, R, WS), BF16),
        scratch_shapes=[pltpu.VMEM((GROUPS_PER_SET, R, W), BF16)] * 2
        + [pltpu.VMEM((S5_SCAN_GROUPS, R, W // 2), F32)] * 8,
        compiler_params=_cparams("arbitrary"),
        name="s5",
    )(u_set, M, ET, DA, DB, lam, dvec)


def _memkv_body(m_ref, g_ref, wk_ref, wv_ref, k_ref, v_ref):
    h = (_rms(m_ref[...]) * g_ref[...]).astype(BF16)
    k_ref[...] = _dot(h, wk_ref[...]).astype(BF16)
    v_ref[...] = _dot(h, wv_ref[...]).astype(BF16)


def _memkv(mem, g, wk, wv):
    B, M, D = mem.shape
    const = lambda b: (0, 0)
    row = pl.BlockSpec((None, M, D), lambda b: (b, 0, 0))
    return pl.pallas_call(
        _memkv_body,
        grid=(B,),
        in_specs=[row, pl.BlockSpec((1, D), const), pl.BlockSpec((D, D), const),
                  pl.BlockSpec((D, D), const)],
        out_specs=[row, row],
        out_shape=[jax.ShapeDtypeStruct((B, M, D), BF16)] * 2,
        compiler_params=_cparams("arbitrary"),
        name="memkv",
    )(mem, g[None, :], wk.astype(BF16), wv.astype(BF16))


MID_ROW_BLOCK = 512


def _mid_body(x_ref, ya_ref, z_ref, gw_ref, gb_ref, ag_ref, sg_ref, wo_ref, xg_ref,
              wq_ref, xwo_ref, k_ref, v_ref, o_ref, zs):
    D = x_ref.shape[1]
    dh = D // XA_HEADS
    n_set = z_ref.shape[0]
    crows = MID_ROW_BLOCK // CHUNK

    def glu(r0, st):
        c0 = r0 // CHUNK
        for j in range(n_set):
            for t in range(CHUNK):
                zs[j, pl.ds(r0 + t, crows, stride=CHUNK), :] = (
                    z_ref[j, c0:c0 + crows, t * LANES:(t + 1) * LANES].astype(F32))
        z = jnp.concatenate([zs[j, r0:r0 + MID_ROW_BLOCK, :] for j in range(n_set)], axis=1).astype(BF16)
        st['s'] = z.astype(F32) * jax.nn.sigmoid(_dot(z, gw_ref[...]) + gb_ref[...])

    def out_proj(r0, st):
        rows = slice(r0, r0 + MID_ROW_BLOCK)
        ya = _rms(ya_ref[rows, :].astype(F32)) * ag_ref[...]
        ys = _rms(st.pop('s')) * sg_ref[...]
        y = jnp.concatenate([ya, ys], axis=1).astype(BF16)
        st['x1'] = x_ref[rows, :] + _dot(y, wo_ref[...])

    def query(r0, st):
        hq = (_rms(st['x1']) * xg_ref[...]).astype(BF16)
        st['q'] = (_dot(hq, wq_ref[...]) * (dh ** -0.5)).astype(BF16)

    def attend(r0, st):
        q = st.pop('q')
        outs = []
        for h in range(XA_HEADS):
            sl = slice(h * dh, (h + 1) * dh)
            sc = _dot_nt(q[:, sl], k_ref[:, sl])
            e = jnp.exp(sc - jnp.max(sc, axis=-1, keepdims=True))
            inv = 1.0 / jnp.sum(e, axis=-1, keepdims=True)
            outs.append((_dot(e.astype(BF16), v_ref[:, sl]) * inv).astype(BF16))
        st['o'] = jnp.concatenate(outs, axis=1)

    def finish(r0, st):
        rows = slice(r0, r0 + MID_ROW_BLOCK)
        o_ref[rows, :D] = st.pop('x1') + _dot(st.pop('o'), xwo_ref[...])
        o_ref[rows, D:] = jnp.zeros((MID_ROW_BLOCK, o_ref.shape[1] - D), F32)

    blocks = [(r0, {}) for r0 in range(0, x_ref.shape[0], MID_ROW_BLOCK)]
    for stage in (glu, out_proj, query, attend, finish):
        for r0, st in blocks:
            stage(r0, st)


def _mid(x2d, S, y_att, z, glu_w, glu_b, att_g, ssm_g, w_out, xa_g, wq, wo, kmem, vmem):
    T, D = x2d.shape
    M = kmem.shape[1]
    tm = 2 * MID_ROW_BLOCK
    sb = S // tm
    const = lambda i: (0, 0)
    tok = lambda w: pl.BlockSpec((tm, w), lambda i: (i, 0))
    full = lambda a: pl.BlockSpec(a.shape, const)
    vec = lambda a: a.astype(F32)[None, :]
    mem = pl.BlockSpec((None, M, D), lambda i: (i // sb, 0, 0))
    args = [x2d, y_att, z, glu_w.astype(BF16), vec(glu_b), vec(att_g), vec(ssm_g), w_out.astype(BF16),
            vec(xa_g), wq.astype(BF16), wo.astype(BF16), kmem, vmem]
    zspec = pl.BlockSpec((z.shape[0], tm // CHUNK, CHUNK * LANES), lambda i: (0, i, 0))
    specs = [tok(D), tok(y_att.shape[1]), zspec] + [full(a) for a in args[3:11]] + [mem, mem]
    return pl.pallas_call(
        _mid_body,
        grid=(T // tm,),
        in_specs=specs,
        out_specs=pl.BlockSpec((tm, D + LANES), lambda i: (i, 0)),
        out_shape=jax.ShapeDtypeStruct((T, D + LANES), F32),
        scratch_shapes=[pltpu.VMEM((z.shape[0], tm, LANES), F32)],
        compiler_params=_cparams("arbitrary"),
        name="mid",
    )(*args)


def _split_bf16(a):
    hi = a.astype(BF16)
    return hi, (a - hi.astype(F32)).astype(BF16)


ROUTE_ROWS = 32
MOE_TILE = 256


def _route_body(x_ref, g_ref, whi_ref, wlo_ref, b_ref, tri_ref, meta_ref, mt_ref, cnt_ref, carry):
    i = pl.program_id(0)
    tm, D = x_ref.shape[0], g_ref.shape[1]

    @pl.when(i == 0)
    def _():
        carry[...] = jnp.zeros_like(carry)

    h = _rms(x_ref[:, :D]) * g_ref[...]
    hhi, hlo = _split_bf16(h)
    logits = (_dot_nt(whi_ref[...], hhi) + _dot_nt(whi_ref[...], hlo) + _dot_nt(wlo_ref[...], hhi)
              + b_ref[:, 0:1])
    ninf = jnp.float32(-jnp.inf)
    cmax = lambda v: jnp.max(v, axis=0, keepdims=True)
    cmin = lambda v: jnp.min(v, axis=0, keepdims=True)
    gl = logits[N_EXPERTS:N_EXPERTS + N_EXPERT_GROUPS]
    grow = lax.broadcasted_iota(jnp.int32, gl.shape, 0)
    gmax = cmax(gl)
    g_sel = cmin(jnp.where(gl == gmax, grow, N_EXPERT_GROUPS))
    g_w = 1.0 / jnp.sum(jnp.exp(gl - gmax), axis=0, keepdims=True)
    el = logits[:N_EXPERTS]
    erow = lax.broadcasted_iota(jnp.int32, el.shape, 0)
    el = jnp.where((erow >> 2) == g_sel, el, ninf)
    v1 = cmax(el)
    i1 = cmin(jnp.where(el == v1, erow, N_EXPERTS))
    el2 = jnp.where(erow == i1, ninf, el)
    v2 = cmax(el2)
    i2 = cmin(jnp.where(el2 == v2, erow, N_EXPERTS))
    t = jnp.exp(v2 - v1)
    w1 = g_w / (1.0 + t)
    w2 = g_w * t / (1.0 + t)
    first_low = i1 < i2
    ia = jnp.minimum(i1, i2) & 3
    ib = jnp.maximum(i1, i2) & 3
    wa = jnp.where(first_low, w1, w2)
    wb = jnp.where(first_low, w2, w1)
    cls = g_sel * PAIRS_PER_GROUP + ((ia * (7 - ia)) >> 1) + ib - ia - 1
    crow = lax.broadcasted_iota(jnp.int32, (ROUTE_ROWS, tm), 0)
    onehot = (crow == cls).astype(F32)
    before = _dot(onehot.astype(BF16), tri_ref[...])
    rank = jnp.sum(onehot * (before + carry[:, 0:1]), axis=0, keepdims=True)
    carry[...] = carry[...] + jnp.sum(onehot, axis=1, keepdims=True)
    cnt_ref[...] = carry[...]

    def record(rows):
        r = lax.broadcasted_iota(jnp.int32, (rows, tm), 0)
        return jnp.where(r == META_CLS, cls.astype(F32),
                         jnp.where(r == META_RANK, rank,
                                   jnp.where(r == META_WA, wa, jnp.where(r == META_WB, wb, 0.0))))

    meta_ref[...] = record(LANES).T
    mt_ref[:, pl.ds(pl.multiple_of(i * tm, tm), tm)] = record(mt_ref.shape[0])

    @pl.when(i == pl.num_programs(0) - 1)
    def _():
        ntile = jnp.floor((carry[...] + (MOE_TILE - 1.0)) * (1.0 / MOE_TILE))
        r = lax.broadcasted_iota(jnp.int32, (ROUTE_ROWS, ROUTE_ROWS), 0)
        c = lax.broadcasted_iota(jnp.int32, (ROUTE_ROWS, ROUTE_ROWS), 1)
        off = _dot((c < r).astype(BF16), ntile.astype(BF16)) * MOE_TILE
        cls_all = mt_ref[META_CLS:META_CLS + 1, :]
        pos = mt_ref[META_RANK:META_RANK + 1, :]
        for k in range(N_CLASSES):
            pos = pos + jnp.where(cls_all == k, off[k:k + 1, 0:1], 0.0)
        mt_ref[META_POS:META_POS + 1, :] = pos


def _route(x2a, D, moe_g, grp_w, grp_b, exp_w, exp_b):
    T = x2a.shape[0]
    tm = 1024
    pad = ROUTE_ROWS - N_EXPERT_GROUPS - N_EXPERTS
    w = jnp.concatenate([exp_w, grp_w, jnp.zeros((D, pad), F32)], axis=1).astype(F32).T
    b = jnp.concatenate([exp_b, grp_b, jnp.zeros((pad,), F32)]).astype(F32)
    whi, wlo = _split_bf16(w)
    tri = jnp.asarray(np.arange(tm)[:, None] < np.arange(tm)[None, :], dtype=BF16)
    const = lambda i: (0, 0)
    ncol = D // LANES
    return pl.pallas_call(
        _route_body,
        grid=(T // tm,),
        in_specs=[pl.BlockSpec((tm, D + LANES), lambda i: (i, 0)),
                  pl.BlockSpec((1, D), const), pl.BlockSpec((ROUTE_ROWS, D), const),
                  pl.BlockSpec((ROUTE_ROWS, D), const), pl.BlockSpec((ROUTE_ROWS, LANES), const),
                  pl.BlockSpec((tm, tm), const)],
        out_specs=[pl.BlockSpec((tm, LANES), lambda i: (i, ncol)),
                   pl.BlockSpec((SUBLANES, T), const),
                   pl.BlockSpec((ROUTE_ROWS, LANES), const)],
        out_shape=[jax.ShapeDtypeStruct(x2a.shape, F32), jax.ShapeDtypeStruct((SUBLANES, T), F32),
                   jax.ShapeDtypeStruct((ROUTE_ROWS, LANES), F32)],
        scratch_shapes=[pltpu.VMEM((ROUTE_ROWS, LANES), F32)],
        input_output_aliases={0: 0},
        compiler_params=_cparams("arbitrary"),
        name="route",
    )(x2a, moe_g.astype(F32)[None, :], whi, wlo, jnp.broadcast_to(b[:, None], (ROUTE_ROWS, LANES)), tri)


ROW_COPY_UNROLL = 8


def _row_copy_body(pos_ref, src_ref, *rest, rows, scatter):
    dst_ref, sem = rest[-2:]

    def copy(j, k):
        p = pos_ref[j * ROW_COPY_UNROLL + k]
        tok = (j, pl.ds(k, 1))
        slot = (lax.shift_right_logical(p, 3), pl.ds(p & (SUBLANES - 1), 1))
        if scatter:
            return pltpu.make_async_copy(src_ref.at[tok], dst_ref.at[slot], sem)
        return pltpu.make_async_copy(src_ref.at[slot], dst_ref.at[tok], sem)

    def start(j, _):
        for k in range(ROW_COPY_UNROLL):
            copy(j, k).start(priority=k % 2)
        return 0

    def wait(j, _):
        for k in range(ROW_COPY_UNROLL):
            copy(j, k).wait()
        return 0

    lax.fori_loop(0, rows // ROW_COPY_UNROLL, start, 0)
    lax.fori_loop(0, rows // ROW_COPY_UNROLL, wait, 0)


def _row_copy(pos, src, *, dst_init=None, dst_rows=None, name):
    assert ROW_COPY_UNROLL == SUBLANES
    T = pos.shape[0]
    rows = 512
    width = src.shape[1]
    scatter = dst_init is not None
    hbm = pl.BlockSpec(memory_space=pl.ANY)
    tok = pl.BlockSpec((rows // SUBLANES, SUBLANES, width), lambda i: (i, 0, 0))
    out_rows = dst_init.shape[0] if scatter else dst_rows
    tiled = lambda a: a.reshape(a.shape[0] // SUBLANES, SUBLANES, width)
    out = pl.pallas_call(
        functools.partial(_row_copy_body, rows=rows, scatter=scatter),
        grid=(T // rows,),
        in_specs=[pl.BlockSpec((rows,), lambda i: (i,), memory_space=pltpu.SMEM)]
        + ([tok, hbm] if scatter else [hbm]),
        out_specs=hbm if scatter else tok,
        out_shape=jax.ShapeDtypeStruct((out_rows // SUBLANES, SUBLANES, width), src.dtype),
        scratch_shapes=[pltpu.SemaphoreType.DMA(())],
        input_output_aliases={2: 0} if scatter else {},
        compiler_params=pltpu.CompilerParams(dimension_semantics=("arbitrary",)),
        name=name,
    )(*((pos, tiled(src), tiled(dst_init)) if scatter else (pos, tiled(src))))
    return out.reshape(out_rows, width)


def _ffn_body(ea_ref, eb_ref, nt_ref, xs_ref, g_ref, fg_ref, wga, wua, wda, wgb, wub, wdb, o_ref,
              ga, ua, da, gb, ub, db):
    D = g_ref.shape[1]
    i = pl.program_id(0)
    valid = i < nt_ref[0]
    prev = jnp.maximum(i - 1, 0)
    fresh = (i == 0) | (ea_ref[i] != ea_ref[prev]) | (eb_ref[i] != eb_ref[prev])

    @pl.when(valid & fresh)
    def _():
        for src, dst in ((wga, ga), (wua, ua), (wda, da), (wgb, gb), (wub, ub), (wdb, db)):
            dst[...] = src[...].astype(BF16)

    @pl.when(valid)
    def _():
        x = xs_ref[:, :D]
        meta = xs_ref[:, D:]
        h = (_rms(x) * g_ref[...]).astype(BF16)
        y = x
        for wg, wu, wd, lane in ((ga, ua, da, META_WA), (gb, ub, db, META_WB)):
            a = jax.nn.silu(_dot(h, wg[...])) * _dot(h, wu[...]) * meta[:, lane:lane + 1]
            y = y + _dot(a.astype(BF16), wd[...])
        o_ref[...] = _rms(y) * fg_ref[...]

    @pl.when(jnp.logical_not(valid))
    def _():
        o_ref[...] = jnp.zeros_like(o_ref)


def _ffn(xs, D, tile_ea, tile_eb, n_tiles, moe_g, final_g, w_gate, w_up, w_down):
    tm = MOE_TILE
    P = xs.shape[0]
    nt_max = P // tm
    De = w_gate.shape[2]
    wg, wu, wd = w_gate.astype(F32), w_up.astype(F32), w_down.astype(F32)
    const = lambda i, ea, eb, nt: (0, 0)
    sel_a = lambda i, ea, eb, nt: (ea[i], 0, 0)
    sel_b = lambda i, ea, eb, nt: (eb[i], 0, 0)
    up = lambda sel: pl.BlockSpec((None, D, De), sel)
    down = lambda sel: pl.BlockSpec((None, De, D), sel)
    grid_spec = pltpu.PrefetchScalarGridSpec(
        num_scalar_prefetch=3,
        grid=(nt_max,),
        in_specs=[pl.BlockSpec((tm, D + LANES),
                               lambda i, ea, eb, nt: (jnp.maximum(jnp.minimum(i, nt[0] - 1), 0), 0)),
                  pl.BlockSpec((1, D), const), pl.BlockSpec((1, D), const),
                  up(sel_a), up(sel_a), down(sel_a), up(sel_b), up(sel_b), down(sel_b)],
        out_specs=pl.BlockSpec((tm, D), lambda i, ea, eb, nt: (i, 0)),
        scratch_shapes=[pltpu.VMEM((D, De), BF16), pltpu.VMEM((D, De), BF16), pltpu.VMEM((De, D), BF16)] * 2,
    )
    return pl.pallas_call(
        _ffn_body,
        grid_spec=grid_spec,
        out_shape=jax.ShapeDtypeStruct((P, D), F32),
        compiler_params=_cparams("arbitrary"),
        name="ffn",
    )(tile_ea, tile_eb, n_tiles, xs, moe_g.astype(F32)[None, :], final_g.astype(F32)[None, :],
      wg, wu, wd, wg, wu, wd)


def _moe(x2a, D, moe_g, grp_w, grp_b, exp_w, exp_b, w_gate, w_up, w_down, final_g):
    T = x2a.shape[0]
    tm = MOE_TILE
    x2a, record, counts = _route(x2a, D, moe_g, grp_w, grp_b, exp_w, exp_b)
    pos = record[META_POS].astype(jnp.int32)
    cnt = counts[:N_CLASSES, 0].astype(jnp.int32)
    ntile_c = (cnt + tm - 1) // tm
    tile_end = jnp.cumsum(ntile_c)
    nt_max = T // tm + N_CLASSES
    n_tiles = tile_end[-1:]
    tile_id = jnp.arange(nt_max, dtype=jnp.int32)
    tile_cls = jnp.minimum(jnp.sum((tile_end[None, :] <= tile_id[:, None]).astype(jnp.int32), axis=1),
                           N_CLASSES - 1)
    tile_cls = jnp.where(tile_id < n_tiles[0], tile_cls, tile_cls[jnp.maximum(n_tiles[0] - 1, 0)])
    pa = jnp.asarray([0, 0, 0, 1, 1, 2], jnp.int32)
    pb = jnp.asarray([1, 2, 3, 2, 3, 3], jnp.int32)
    grp = tile_cls // PAIRS_PER_GROUP
    tile_ea = grp * EXPERTS_PER_GROUP + pa[tile_cls % PAIRS_PER_GROUP]
    tile_eb = grp * EXPERTS_PER_GROUP + pb[tile_cls % PAIRS_PER_GROUP]
    xs = _row_copy(pos, x2a, dst_init=jnp.zeros((nt_max * tm, D + LANES), F32), name="dispatch")
    ys = _ffn(xs, D, tile_ea, tile_eb, n_tiles.astype(jnp.int32), moe_g, final_g, w_gate, w_up, w_down)
    return _row_copy(pos, ys, dst_rows=T, name="unsort")


def kernel(x, mem, mix_norm_g, w_in, q_norm_g, k_norm_g, ssm_lambda_re, ssm_lambda_im, ssm_log_step, ssm_b_re, ssm_b_im, ssm_c_re, ssm_c_im, ssm_d, ssm_glu_w, ssm_glu_b, att_out_g, ssm_out_g, w_out, xa_norm_g, mem_norm_g, xa_wq, xa_wk, xa_wv, xa_wo, moe_norm_g, router_grp_w, router_grp_b, router_exp_w, router_exp_b, exp_w_gate, exp_w_up, exp_w_down, final_norm_g):
    B, S, D = x.shape
    assert mix_norm_g.shape[0] == 1, "single layer"
    x2d = x.reshape(B * S, D)
    q, kd, vd, u = _inproj(x2d, S, mix_norm_g[0], w_in[0], q_norm_g[0], k_norm_g[0])
    y_att = _attention(q, kd, vd, B, S)
    z = _s5(u, B, S, ssm_lambda_re[0], ssm_lambda_im[0], ssm_log_step[0], ssm_b_re[0], ssm_b_im[0],
            ssm_c_re[0], ssm_c_im[0], ssm_d[0])
    kmem, vmem = _memkv(mem, mem_norm_g[0], xa_wk[0], xa_wv[0])
    x2a = _mid(x2d, S, y_att, z, ssm_glu_w[0], ssm_glu_b[0], att_out_g[0], ssm_out_g[0], w_out[0],
               xa_norm_g[0], xa_wq[0], xa_wo[0], kmem, vmem)
    out = _moe(x2a, D, moe_norm_g[0], router_grp_w[0], router_grp_b[0], router_exp_w[0],
               router_exp_b[0], exp_w_gate[0], exp_w_up[0], exp_w_down[0], final_norm_g)
    return out.reshape(B, S, D)
```

```python
import functools

import numpy as np
import jax
import jax.numpy as jnp
from jax import lax
from jax.experimental import pallas as pl
from jax.experimental.pallas import tpu as pltpu

F32 = jnp.float32
BF16 = jnp.bfloat16

EPS = 1e-6
GRID_W = 64
ROPE_THETA = 10000.0
ATT_HEADS = 8
ATT_KV_HEADS = 2
HEAD_DIM = 64
D_ATT = ATT_HEADS * HEAD_DIM
D_KV = ATT_KV_HEADS * HEAD_DIM
SSM_GROUP = 16
SSM_STATE = 64
XA_HEADS = 4
N_EXPERT_GROUPS = 4
EXPERTS_PER_GROUP = 4
N_EXPERTS = N_EXPERT_GROUPS * EXPERTS_PER_GROUP
PAIRS_PER_GROUP = 6
N_CLASSES = N_EXPERT_GROUPS * PAIRS_PER_GROUP

LANES = 128
SUBLANES = 8
CHUNK = 16
VMEM_LIMIT = 56 * 1024 * 1024

META_CLS, META_RANK, META_WA, META_WB, META_POS = 0, 1, 2, 3, 4


def _cparams(*sem):
    return pltpu.CompilerParams(dimension_semantics=sem, vmem_limit_bytes=VMEM_LIMIT)


def _rms(x):
    return x * lax.rsqrt(jnp.mean(x * x, axis=-1, keepdims=True) + EPS)


def _dot(a, b):
    return jnp.dot(a, b, preferred_element_type=F32)


def _dot_nt(a, b):
    return lax.dot_general(a, b, (((1,), (1,)), ((), ())), preferred_element_type=F32)


def _inproj_body(x_ref, g_ref, w_ref, qkg_ref, ones_ref, cos_ref, sin_ref,
                 q_ref, k_ref, v_ref, u_ref, us):
    n_qk = q_ref.shape[1] + k_ref.shape[1]
    h = _rms(x_ref[...]) * g_ref[...]
    proj = _dot(h.astype(BF16), w_ref[...])
    qk = proj[:, :n_qk]
    ssq = _dot((qk * qk).astype(BF16), ones_ref[...])
    qk = qk * lax.rsqrt(ssq * (1.0 / HEAD_DIM) + EPS) * qkg_ref[...]
    cos = cos_ref[...]
    sin = sin_ref[...]
    lane = lax.broadcasted_iota(jnp.int32, cos.shape, 1)
    first = (lane % 32) < 16
    nq = q_ref.shape[1] // LANES
    for c in range(n_qk // LANES):
        xc = qk[:, c * LANES:(c + 1) * LANES]
        partner = jnp.where(first, pltpu.roll(xc, LANES - 16, 1), pltpu.roll(xc, 16, 1))
        rot = xc * cos + partner * sin
        if c < nq:
            q_ref[:, c * LANES:(c + 1) * LANES] = (rot * (HEAD_DIM ** -0.5)).astype(BF16)
        else:
            k_ref[:, (c - nq) * LANES:(c - nq + 1) * LANES] = rot.astype(BF16)
    nv = n_qk + v_ref.shape[1]
    v_ref[...] = proj[:, n_qk:nv].astype(BF16)
    rows = u_ref.shape[1]
    W = CHUNK * SSM_GROUP
    for j in range(u_ref.shape[0]):
        us[j] = proj[:, nv + j * LANES:nv + (j + 1) * LANES]
        steps = [us[j, pl.ds(t, rows, stride=CHUNK), :] for t in range(CHUNK)]
        for h in range(W // LANES):
            per_group = _piece_transpose(steps[h * GROUPS_PER_SET:(h + 1) * GROUPS_PER_SET])
            for g in range(GROUPS_PER_SET):
                u_ref[j, :, g * W + h * LANES:g * W + (h + 1) * LANES] = per_group[g].astype(BF16)


def _rope_tables(S):
    t = np.arange(S)
    freqs = (ROPE_THETA ** (-np.arange(16, dtype=np.float32) / 16)).astype(np.float32)
    halves = []
    for pos in (t // GRID_W, t % GRID_W):
        ang = pos.astype(np.float32)[:, None] * freqs[None, :]
        halves.append((np.cos(ang), np.sin(ang)))
    cos_h = np.concatenate([c for c, _ in halves for _ in range(2)], axis=1)
    sin_h = np.concatenate([sg * s for _, s in halves for sg in (-1.0, 1.0)], axis=1)
    return (jnp.asarray(np.concatenate([cos_h, cos_h], axis=1), F32),
            jnp.asarray(np.concatenate([sin_h, sin_h], axis=1), F32))


def _inproj(x2d, S, mix_g, w_in, q_g, k_g):
    T, D = x2d.shape
    tm = 512
    d_ssm = w_in.shape[1] - D_ATT - 2 * D_KV
    wq = w_in[:, :D_ATT]
    wk = w_in[:, D_ATT:D_ATT + D_KV]
    wv = w_in[:, D_ATT + D_KV:D_ATT + 2 * D_KV]
    wu = w_in[:, D_ATT + 2 * D_KV:]
    dup = lambda w: jnp.concatenate([w[:, :64], w[:, :64], w[:, 64:], w[:, 64:]], axis=1)
    w_ext = jnp.concatenate([wq, dup(wk), dup(wv), wu], axis=1).astype(BF16)
    n_qk = D_ATT + 2 * D_KV
    qkg = jnp.concatenate([jnp.tile(q_g, ATT_HEADS), jnp.tile(k_g, 2 * ATT_KV_HEADS)])[None, :]
    hid = np.arange(n_qk) // HEAD_DIM
    ones = jnp.asarray((hid[:, None] == hid[None, :]).astype(np.float32), dtype=BF16)
    cos_t, sin_t = _rope_tables(S)
    sb = S // tm
    const = lambda i: (0, 0)
    return pl.pallas_call(
        _inproj_body,
        grid=(T // tm,),
        in_specs=[
            pl.BlockSpec((tm, D), lambda i: (i, 0)),
            pl.BlockSpec((1, D), const),
            pl.BlockSpec(w_ext.shape, const),
            pl.BlockSpec((1, n_qk), const),
            pl.BlockSpec((n_qk, n_qk), const),
            pl.BlockSpec((tm, LANES), lambda i: (i % sb, 0)),
            pl.BlockSpec((tm, LANES), lambda i: (i % sb, 0)),
        ],
        out_specs=[
            pl.BlockSpec((tm, D_ATT), lambda i: (i, 0)),
            pl.BlockSpec((tm, 2 * D_KV), lambda i: (i, 0)),
            pl.BlockSpec((tm, 2 * D_KV), lambda i: (i, 0)),
            pl.BlockSpec((d_ssm // LANES, tm // CHUNK, CHUNK * LANES), lambda i: (0, i, 0)),
        ],
        out_shape=[
            jax.ShapeDtypeStruct((T, D_ATT), BF16),
            jax.ShapeDtypeStruct((T, 2 * D_KV), BF16),
            jax.ShapeDtypeStruct((T, 2 * D_KV), BF16),
            jax.ShapeDtypeStruct((d_ssm // LANES, T // CHUNK, CHUNK * LANES), BF16),
        ],
        scratch_shapes=[pltpu.VMEM((d_ssm // LANES, tm, LANES), F32)],
        compiler_params=_cparams("arbitrary"),
        name="inproj",
    )(x2d, mix_g[None, :], w_ext, qkg, ones, cos_t, sin_t)


ATT_KEY_CHUNK = 256
ATT_ROW_BLOCK = 256


def _attn_body(q_ref, k_ref, v_ref, o_ref, klo, khi, vlo, vhi, s_scr, e_scr):
    @pl.when(pl.program_id(2) == 0)
    def _():
        lane = lax.broadcasted_iota(jnp.int32, k_ref.shape, 1)
        lo = lane < HEAD_DIM
        kd = k_ref[...]
        vd = v_ref[...]
        zero = jnp.zeros_like(kd)
        klo[...] = jnp.where(lo, kd, zero)
        khi[...] = jnp.where(lo, zero, kd)
        vlo[...] = jnp.where(lo, vd, (lane == HEAD_DIM).astype(BF16))
        vhi[...] = jnp.where(lo, (lane == 0).astype(BF16), vd)

    S = klo.shape[0]
    rb = s_scr.shape[1]
    units = [(r0, p, km, vm) for r0 in range(0, q_ref.shape[0], rb)
             for p in range(2) for km, vm in ((klo, vlo), (khi, vhi))]

    def scores(i):
        r0, p, km, _ = units[i]
        qp = q_ref[r0:r0 + rb, p * LANES:(p + 1) * LANES]
        m = None
        for c in range(0, S, ATT_KEY_CHUNK):
            s = _dot_nt(qp, km[c:c + ATT_KEY_CHUNK, :])
            s_scr[i % 2, :, c:c + ATT_KEY_CHUNK] = s
            for l0 in range(0, ATT_KEY_CHUNK, LANES):
                m = s[:, l0:l0 + LANES] if m is None else jnp.maximum(m, s[:, l0:l0 + LANES])
        return jnp.max(m, axis=-1, keepdims=True)

    def values(i, m):
        vm = units[i][3]
        for c in range(0, S, ATT_KEY_CHUNK):
            e_scr[i % 2, :, c:c + ATT_KEY_CHUNK] = jnp.exp(s_scr[i % 2, :, c:c + ATT_KEY_CHUNK] - m).astype(BF16)
        acc = _dot(e_scr[i % 2], vm[...])
        ones_lane = HEAD_DIM if vm is vlo else 0
        return acc * (1.0 / acc[:, ones_lane:ones_lane + 1])

    lo = lax.broadcasted_iota(jnp.int32, (rb, LANES), 1) < HEAD_DIM
    m_prev = scores(0)
    first = None
    for i, (r0, p, _, vm) in enumerate(units):
        m_next = scores(i + 1) if i + 1 < len(units) else None
        o = values(i, m_prev)
        m_prev = m_next
        if vm is vlo:
            first = o
        else:
            o_ref[r0:r0 + rb, p * LANES:(p + 1) * LANES] = jnp.where(lo, first, o).astype(BF16)


def _attention(q, kd, vd, B, S):
    tq = 4 * ATT_ROW_BLOCK
    q3 = q.reshape(B, S, D_ATT)
    k3 = kd.reshape(B, S, 2 * D_KV)
    v3 = vd.reshape(B, S, 2 * D_KV)
    gw = D_ATT // ATT_KV_HEADS
    out = pl.pallas_call(
        _attn_body,
        grid=(B, ATT_KV_HEADS, S // tq),
        in_specs=[
            pl.BlockSpec((None, tq, gw), lambda b, j, i: (b, i, j)),
            pl.BlockSpec((None, S, LANES), lambda b, j, i: (b, 0, j)),
            pl.BlockSpec((None, S, LANES), lambda b, j, i: (b, 0, j)),
        ],
        out_specs=pl.BlockSpec((None, tq, gw), lambda b, j, i: (b, i, j)),
        out_shape=jax.ShapeDtypeStruct((B, S, D_ATT), BF16),
        scratch_shapes=[pltpu.VMEM((S, LANES), BF16)] * 4
        + [pltpu.VMEM((2, ATT_ROW_BLOCK, S), F32), pltpu.VMEM((2, ATT_ROW_BLOCK, S), BF16)],
        compiler_params=_cparams("arbitrary", "arbitrary", "arbitrary"),
        name="attn",
    )(q3, k3, v3)
    return out.reshape(B * S, D_ATT)


def _dot_tn3(a, b):
    tn = lambda x, y: lax.dot_general(x, y, (((0,), (0,)), ((), ())), preferred_element_type=F32)
    ah, al = _split_bf16(a)
    bh, bl = _split_bf16(b)
    return tn(ah, bh) + tn(al, bh) + tn(ah, bl)


def _s5op_body(lam_ref, br_ref, bi_ref, cr_ref, ci_ref, rep_ref, m_ref, et_ref, da_ref, db_ref):
    W = m_ref.shape[1]
    L = CHUNK
    step = lax.broadcasted_iota(jnp.int32, (1, W), 1) // SSM_GROUP

    def along_steps(a, channel_axis):
        hi = a.astype(BF16)
        r1 = a - hi.astype(F32)
        mid = r1.astype(BF16)
        lo = (r1 - mid.astype(F32)).astype(BF16)
        dims = (((channel_axis,), (0,)), ((), ()))
        rep = lambda x: lax.dot_general(x, rep_ref[...], dims, preferred_element_type=F32)
        return rep(hi) + rep(mid) + rep(lo)

    def direction(d):
        lam = lam_ref[d]
        lr = jnp.minimum(lam[:, 0:1], -1e-4)
        li = lam[:, 1:2]
        dt = jnp.exp(lam[:, 2:3])
        mag = jnp.exp(lr * dt)
        ab_r = mag * jnp.cos(li * dt)
        ab_i = mag * jnp.sin(li * dt)
        den = lr * lr + li * li
        nr = ab_r - 1.0
        f_r = (nr * lr + ab_i * li) / den
        f_i = (ab_i * lr - nr * li) / den
        b_r, b_i = along_steps(br_ref[d], 1), along_steps(bi_ref[d], 1)
        bb_r = f_r * b_r - f_i * b_i
        bb_i = f_r * b_i + f_i * b_r
        cr, ci = along_steps(cr_ref[d], 0), along_steps(ci_ref[d], 0)

        def power(n):
            sq_r, sq_i = ab_r, ab_i
            pr = pi = None
            for k in range(CHUNK.bit_length() - 1):
                bit = ((n >> k) & 1) == 1
                tr = jnp.where(bit, sq_r, 1.0)
                ti = jnp.where(bit, sq_i, 0.0)
                pr, pi = (tr, ti) if pr is None else (pr * tr - pi * ti, pr * ti + pi * tr)
                sq_r, sq_i = sq_r * sq_r - sq_i * sq_i, 2.0 * sq_r * sq_i
            return pr, pi

        def times_lam(p):
            return p[0] * ab_r - p[1] * ab_i, p[0] * ab_i + p[1] * ab_r

        def a_b(p):
            return p[0] * bb_r - p[1] * bb_i, p[0] * bb_i + p[1] * bb_r

        def c_a(p):
            return cr * p[0] - ci * p[1], cr * p[1] + ci * p[0]

        up, down = power(step), power(L - 1 - step)
        return bb_r, bb_i, cr, ci, a_b, c_a, up, down, times_lam

    fbr, fbi, _, _, f_ab, f_ca, f_up, f_down, f_next = direction(0)
    bbr, bbi, bcr, bci, b_ab, b_ca, b_up, b_down, b_next = direction(1)
    etf_r, etf_i = f_ab(f_down)
    etb_r, etb_i = b_ab(b_up)
    et_ref[...] = jnp.concatenate([etf_r, etb_r, etf_i, etb_i], axis=0).astype(BF16)
    df_r, df_i = f_ca(f_next(f_up))
    db_r, db_i = b_ca(b_next(b_down))
    zero = jnp.zeros_like(df_r)
    da_ref[...] = jnp.concatenate([df_r, zero, -df_i, zero], axis=0).astype(BF16)
    db_ref[...] = jnp.concatenate([zero, db_r, zero, -db_i], axis=0).astype(BF16)
    kf_r, kf_i = f_ca(f_up)
    lag0 = step == 0
    lhs_f = jnp.concatenate([fbr, -fbi], axis=0)
    lhs_b = jnp.concatenate([bbr, -bbi], axis=0)
    q_b = _dot_tn3(lhs_b, jnp.concatenate([db_r, db_i], axis=0))
    q_f = _dot_tn3(jnp.concatenate([lhs_f, lhs_b], axis=0),
                   jnp.concatenate([kf_r, kf_i, jnp.where(lag0, bcr, 0.0), jnp.where(lag0, bci, 0.0)],
                                   axis=0))
    z = jnp.concatenate([q_b, q_f], axis=1)
    for j in range(L):
        rows = z[j * SSM_GROUP:(j + 1) * SSM_GROUP, :]
        m_ref[j * SSM_GROUP:(j + 1) * SSM_GROUP, :] = pltpu.roll(
            rows, (2 * W - (L - j) * SSM_GROUP) % (2 * W), 1)[:, :W].astype(BF16)


def _s5_operators(lam_re, lam_im, log_step, b_re, b_im, c_re, c_im):
    _, G, P = lam_re.shape
    W = CHUNK * SSM_GROUP
    lam = jnp.stack([lam_re, lam_im, jnp.broadcast_to(log_step[..., None], lam_re.shape)], axis=-1).astype(F32)
    rep = jnp.asarray(np.arange(SSM_GROUP)[:, None] == np.arange(W)[None, :] % SSM_GROUP, dtype=BF16)
    per_g = lambda a: pl.BlockSpec((2, None) + a.shape[2:], lambda g: (0, g, 0, 0))
    out = pl.BlockSpec((None, W, W), lambda g: (g, 0, 0))
    args = (lam, b_re.astype(F32), b_im.astype(F32), c_re.astype(F32), c_im.astype(F32))
    return pl.pallas_call(
        _s5op_body,
        grid=(G,),
        in_specs=[per_g(a) for a in args] + [pl.BlockSpec(rep.shape, lambda g: (0, 0))],
        out_specs=[out] * 4,
        out_shape=[jax.ShapeDtypeStruct((G, W, W), BF16)] * 4,
        compiler_params=_cparams("arbitrary"),
        name="s5op",
    )(*args, rep)


GROUPS_PER_SET = LANES // SSM_GROUP
S5_SCAN_GROUPS = 4


def _transpose8(blocks, axis, piece):
    n = blocks[0].shape[axis]
    q = lax.broadcasted_iota(jnp.int32, blocks[0].shape, axis) // piece
    v = list(blocks)
    for d in (4, 2, 1):
        keep = (q & d) == 0
        nxt = list(v)
        for a in range(8):
            if a & d == 0:
                b = a + d
                nxt[a] = jnp.where(keep, v[a], pltpu.roll(v[b], d * piece, axis))
                nxt[b] = jnp.where(keep, pltpu.roll(v[a], n - d * piece, axis), v[b])
        v = nxt
    return v


def _piece_transpose(blocks):
    return _transpose8(blocks, 1, SSM_GROUP)


def _s5_body(u_ref, m_ref, et_ref, da_ref, db_ref, a_ref, d_ref, z_ref,
             xr_n, xi_n, xr_t, xi_t, har_s, hai_s, hbr_s, hbi_s, *, nb):
    R = u_ref.shape[0]
    nchunk = R // nb
    W = m_ref.shape[1]
    half = W // 2
    group = lambda g: slice(g * W, (g + 1) * W)
    fwd = lax.broadcasted_iota(jnp.int32, (nb, half), 1) < SSM_STATE

    def swap_row_order(src_a, src_b, dst_a, dst_b, to_chunk_major):
        def blocks(cb, _):
            batch_major = [pl.ds(pl.multiple_of(p * nchunk + cb * nb, nb), nb) for p in range(nb)]
            chunk_major = [pl.ds(pl.multiple_of((cb * nb + j) * nb, nb), nb) for j in range(nb)]
            rows_in, rows_out = (batch_major, chunk_major) if to_chunk_major else (chunk_major, batch_major)
            for k in range(S5_SCAN_GROUPS):
                for src, dst in ((src_a, dst_a), (src_b, dst_b)):
                    out = _transpose8([src[k, r, :] for r in rows_in], 0, 1)
                    for j in range(nb):
                        dst[k, rows_out[j], :] = out[j]
            return 0
        lax.fori_loop(0, nchunk // nb, blocks, 0)

    for g0 in range(0, GROUPS_PER_SET, S5_SCAN_GROUPS):
        coef = []
        for k in range(S5_SCAN_GROUPS):
            g = g0 + k
            x = _dot_nt(u_ref[:, group(g)], et_ref[g])
            xr_n[k] = x[:, :half]
            xi_n[k] = x[:, half:]
            lr = jnp.minimum(a_ref[g, 0:1, :], -1e-4)
            dt = jnp.exp(a_ref[g, 2:3, :])
            mag = jnp.exp(lr * dt * CHUNK)
            ang = a_ref[g, 1:2, :] * dt * CHUNK
            coef.append((jnp.broadcast_to(mag * jnp.cos(ang), (nb, half)),
                         jnp.broadcast_to(mag * jnp.sin(ang), (nb, half))))
        swap_row_order(xr_n, xi_n, xr_t, xi_t, True)

        def step(c, carry):
            rf = pl.ds(pl.multiple_of(c * nb, nb), nb)
            rb = pl.ds(pl.multiple_of((nchunk - 1 - c) * nb, nb), nb)
            new = []
            for k in range(S5_SCAN_GROUPS):
                sr, si = carry[2 * k], carry[2 * k + 1]
                ar, ai = coef[k]
                har_s[k, rf, :] = sr
                hai_s[k, rf, :] = si
                hbr_s[k, rb, :] = sr
                hbi_s[k, rb, :] = si
                xr = jnp.where(fwd, xr_t[k, rf, :], xr_t[k, rb, :])
                xi = jnp.where(fwd, xi_t[k, rf, :], xi_t[k, rb, :])
                new += [sr * ar - si * ai + xr, sr * ai + si * ar + xi]
            return tuple(new)

        zero = jnp.zeros((nb, half), F32)
        lax.fori_loop(0, nchunk, step, (zero,) * (2 * S5_SCAN_GROUPS))
        for k in range(S5_SCAN_GROUPS):
            g = g0 + k
            ha = jnp.concatenate([har_s[k], hai_s[k]], axis=1).astype(BF16)
            hb = jnp.concatenate([hbr_s[k], hbi_s[k]], axis=1).astype(BF16)
            y_t = _dot(ha, da_ref[g]) + _dot(hb, db_ref[g])
            xr_n[k] = y_t[:, :half]
            xi_n[k] = y_t[:, half:]
        swap_row_order(xr_n, xi_n, xr_t, xi_t, False)
        for k in range(S5_SCAN_GROUPS):
            g = g0 + k
            u = u_ref[:, group(g)]
            y = _dot(u, m_ref[g]) + jnp.concatenate([xr_t[k], xi_t[k]], axis=1) + d_ref[g] * u.astype(F32)
            z_ref[:, group(g)] = jax.nn.gelu(y).astype(BF16)


def _s5(u_set, B, S, lam_re, lam_im, log_step, b_re, b_im, c_re, c_im, ssm_d):
    n_set, R, WS = u_set.shape
    G = n_set * GROUPS_PER_SET
    W = CHUNK * SSM_GROUP
    assert R == (S // CHUNK) * B and WS == CHUNK * LANES
    M, ET, DA, DB = _s5_operators(lam_re, lam_im, log_step, b_re, b_im, c_re, c_im)
    both = lambda a: jnp.concatenate([a[0], a[1]], axis=-1)
    lam = jnp.stack([both(lam_re), both(lam_im),
                     both(jnp.broadcast_to(log_step[..., None], lam_re.shape))], axis=1).astype(F32)
    dvec = jnp.tile(ssm_d.astype(F32).reshape(G, 1, SSM_GROUP), (1, CHUNK, 1)).reshape(G, 1, W)
    per_set = pl.BlockSpec((None, R, WS), lambda s: (s, 0, 0))
    blk = lambda *sh: pl.BlockSpec((GROUPS_PER_SET,) + sh, lambda s: (s,) + (0,) * len(sh))
    return pl.pallas_call(
        functools.partial(_s5_body, nb=B),
        grid=(n_set,),
        in_specs=[per_set, blk(W, W), blk(W, W), blk(W, W), blk(W, W), blk(3, W // 2), blk(1, W)],
        out_specs=per_set,
        out_shape=jax.ShapeDtypeStruct((n_set, R, WS), BF16),
        scratch_shapes=[pltpu.VMEM((S5_SCAN_GROUPS, R, W // 2), F32)] * 8,
        compiler_params=_cparams("arbitrary"),
        name="s5",
    )(u_set, M, ET, DA, DB, lam, dvec)


def _memkv_body(m_ref, g_ref, wk_ref, wv_ref, k_ref, v_ref):
    h = (_rms(m_ref[...]) * g_ref[...]).astype(BF16)
    k_ref[...] = _dot(h, wk_ref[...]).astype(BF16)
    v_ref[...] = _dot(h, wv_ref[...]).astype(BF16)


def _memkv(mem, g, wk, wv):
    B, M, D = mem.shape
    const = lambda b: (0, 0)
    row = pl.BlockSpec((None, M, D), lambda b: (b, 0, 0))
    return pl.pallas_call(
        _memkv_body,
        grid=(B,),
        in_specs=[row, pl.BlockSpec((1, D), const), pl.BlockSpec((D, D), const),
                  pl.BlockSpec((D, D), const)],
        out_specs=[row, row],
        out_shape=[jax.ShapeDtypeStruct((B, M, D), BF16)] * 2,
        compiler_params=_cparams("arbitrary"),
        name="memkv",
    )(mem, g[None, :], wk.astype(BF16), wv.astype(BF16))


MID_ROW_BLOCK = 512


def _mid_body(x_ref, ya_ref, z_ref, gw_ref, gb_ref, ag_ref, sg_ref, wo_ref, xg_ref,
              wq_ref, xwo_ref, k_ref, v_ref, o_ref, zs):
    D = x_ref.shape[1]
    dh = D // XA_HEADS
    n_set = z_ref.shape[0]
    crows = MID_ROW_BLOCK // CHUNK

    def glu(r0, st):
        c0 = r0 // CHUNK
        W = CHUNK * SSM_GROUP
        for j in range(n_set):
            for h in range(W // LANES):
                steps = _piece_transpose([z_ref[j, c0:c0 + crows, g * W + h * LANES:g * W + (h + 1) * LANES].astype(F32)
                                          for g in range(GROUPS_PER_SET)])
                for k in range(GROUPS_PER_SET):
                    zs[j, pl.ds(r0 + h * GROUPS_PER_SET + k, crows, stride=CHUNK), :] = steps[k]
        z = jnp.concatenate([zs[j, r0:r0 + MID_ROW_BLOCK, :] for j in range(n_set)], axis=1).astype(BF16)
        st['s'] = z.astype(F32) * jax.nn.sigmoid(_dot(z, gw_ref[...]) + gb_ref[...])

    def out_proj(r0, st):
        rows = slice(r0, r0 + MID_ROW_BLOCK)
        ya = _rms(ya_ref[rows, :].astype(F32)) * ag_ref[...]
        ys = _rms(st.pop('s')) * sg_ref[...]
        y = jnp.concatenate([ya, ys], axis=1).astype(BF16)
        st['x1'] = x_ref[rows, :] + _dot(y, wo_ref[...])

    def query(r0, st):
        hq = (_rms(st['x1']) * xg_ref[...]).astype(BF16)
        st['q'] = (_dot(hq, wq_ref[...]) * (dh ** -0.5)).astype(BF16)

    def attend(r0, st):
        q = st.pop('q')
        outs = []
        for h in range(XA_HEADS):
            sl = slice(h * dh, (h + 1) * dh)
            sc = _dot_nt(q[:, sl], k_ref[:, sl])
            e = jnp.exp(sc - jnp.max(sc, axis=-1, keepdims=True))
            inv = 1.0 / jnp.sum(e, axis=-1, keepdims=True)
            outs.append((_dot(e.astype(BF16), v_ref[:, sl]) * inv).astype(BF16))
        st['o'] = jnp.concatenate(outs, axis=1)

    def finish(r0, st):
        rows = slice(r0, r0 + MID_ROW_BLOCK)
        o_ref[rows, :D] = st.pop('x1') + _dot(st.pop('o'), xwo_ref[...])
        o_ref[rows, D:] = jnp.zeros((MID_ROW_BLOCK, o_ref.shape[1] - D), F32)

    blocks = [(r0, {}) for r0 in range(0, x_ref.shape[0], MID_ROW_BLOCK)]
    for stage in (glu, out_proj, query, attend, finish):
        for r0, st in blocks:
            stage(r0, st)


def _mid(x2d, S, y_att, z, glu_w, glu_b, att_g, ssm_g, w_out, xa_g, wq, wo, kmem, vmem):
    T, D = x2d.shape
    M = kmem.shape[1]
    tm = 2 * MID_ROW_BLOCK
    sb = S // tm
    const = lambda i: (0, 0)
    tok = lambda w: pl.BlockSpec((tm, w), lambda i: (i, 0))
    full = lambda a: pl.BlockSpec(a.shape, const)
    vec = lambda a: a.astype(F32)[None, :]
    mem = pl.BlockSpec((None, M, D), lambda i: (i // sb, 0, 0))
    args = [x2d, y_att, z, glu_w.astype(BF16), vec(glu_b), vec(att_g), vec(ssm_g), w_out.astype(BF16),
            vec(xa_g), wq.astype(BF16), wo.astype(BF16), kmem, vmem]
    zspec = pl.BlockSpec((z.shape[0], tm // CHUNK, CHUNK * LANES), lambda i: (0, i, 0))
    specs = [tok(D), tok(y_att.shape[1]), zspec] + [full(a) for a in args[3:11]] + [mem, mem]
    return pl.pallas_call(
        _mid_body,
        grid=(T // tm,),
        in_specs=specs,
        out_specs=pl.BlockSpec((tm, D + LANES), lambda i: (i, 0)),
        out_shape=jax.ShapeDtypeStruct((T, D + LANES), F32),
        scratch_shapes=[pltpu.VMEM((z.shape[0], tm, LANES), F32)],
        compiler_params=_cparams("arbitrary"),
        name="mid",
    )(*args)


def _split_bf16(a):
    hi = a.astype(BF16)
    return hi, (a - hi.astype(F32)).astype(BF16)


ROUTE_ROWS = 32
PAIR_SLOTS = ((0, 1), (2, 1), (2, 0), (3, 0), (3, 1), (3, 2))
MOE_TILE = 256


def _route_body(x_ref, g_ref, whi_ref, wlo_ref, b_ref, tri_ref, meta_ref, mt_ref, cnt_ref, carry):
    i = pl.program_id(0)
    tm, D = x_ref.shape[0], g_ref.shape[1]

    @pl.when(i == 0)
    def _():
        carry[...] = jnp.zeros_like(carry)

    h = _rms(x_ref[:, :D]) * g_ref[...]
    hhi, hlo = _split_bf16(h)
    logits = (_dot_nt(whi_ref[...], hhi) + _dot_nt(whi_ref[...], hlo) + _dot_nt(wlo_ref[...], hhi)
              + b_ref[:, 0:1])
    ninf = jnp.float32(-jnp.inf)
    cmax = lambda v: jnp.max(v, axis=0, keepdims=True)
    cmin = lambda v: jnp.min(v, axis=0, keepdims=True)
    gl = logits[N_EXPERTS:N_EXPERTS + N_EXPERT_GROUPS]
    grow = lax.broadcasted_iota(jnp.int32, gl.shape, 0)
    gmax = cmax(gl)
    g_sel = cmin(jnp.where(gl == gmax, grow, N_EXPERT_GROUPS))
    g_w = 1.0 / jnp.sum(jnp.exp(gl - gmax), axis=0, keepdims=True)
    el = logits[:N_EXPERTS]
    erow = lax.broadcasted_iota(jnp.int32, el.shape, 0)
    el = jnp.where((erow >> 2) == g_sel, el, ninf)
    v1 = cmax(el)
    i1 = cmin(jnp.where(el == v1, erow, N_EXPERTS))
    el2 = jnp.where(erow == i1, ninf, el)
    v2 = cmax(el2)
    i2 = cmin(jnp.where(el2 == v2, erow, N_EXPERTS))
    t = jnp.exp(v2 - v1)
    w1 = g_w / (1.0 + t)
    w2 = g_w * t / (1.0 + t)
    first_low = i1 < i2
    ia = jnp.minimum(i1, i2) & 3
    ib = jnp.maximum(i1, i2) & 3
    wa = jnp.where(first_low, w1, w2)
    wb = jnp.where(first_low, w2, w1)
    pair = ((ia * (7 - ia)) >> 1) + ib - ia - 1
    order = jnp.where(pair == 3, 1, jnp.where(pair == 1, 2, jnp.where(pair == 2, 3, pair)))
    cls = g_sel * PAIRS_PER_GROUP + order
    crow = lax.broadcasted_iota(jnp.int32, (ROUTE_ROWS, tm), 0)
    onehot = (crow == cls).astype(F32)
    before = _dot(onehot.astype(BF16), tri_ref[...])
    rank = jnp.sum(onehot * (before + carry[:, 0:1]), axis=0, keepdims=True)
    carry[...] = carry[...] + jnp.sum(onehot, axis=1, keepdims=True)
    cnt_ref[...] = carry[...]

    def record(rows):
        r = lax.broadcasted_iota(jnp.int32, (rows, tm), 0)
        return jnp.where(r == META_CLS, cls.astype(F32),
                         jnp.where(r == META_RANK, rank,
                                   jnp.where(r == META_WA, wa, jnp.where(r == META_WB, wb, 0.0))))

    meta_ref[...] = record(LANES).T
    mt_ref[:, pl.ds(pl.multiple_of(i * tm, tm), tm)] = record(mt_ref.shape[0])

    @pl.when(i == pl.num_programs(0) - 1)
    def _():
        ntile = jnp.floor((carry[...] + (MOE_TILE - 1.0)) * (1.0 / MOE_TILE))
        r = lax.broadcasted_iota(jnp.int32, (ROUTE_ROWS, ROUTE_ROWS), 0)
        c = lax.broadcasted_iota(jnp.int32, (ROUTE_ROWS, ROUTE_ROWS), 1)
        off = _dot((c < r).astype(BF16), ntile.astype(BF16)) * MOE_TILE
        cls_all = mt_ref[META_CLS:META_CLS + 1, :]
        pos = mt_ref[META_RANK:META_RANK + 1, :]
        for k in range(N_CLASSES):
            pos = pos + jnp.where(cls_all == k, off[k:k + 1, 0:1], 0.0)
        mt_ref[META_POS:META_POS + 1, :] = pos


def _route(x2a, D, moe_g, grp_w, grp_b, exp_w, exp_b):
    T = x2a.shape[0]
    tm = 1024
    pad = ROUTE_ROWS - N_EXPERT_GROUPS - N_EXPERTS
    w = jnp.concatenate([exp_w, grp_w, jnp.zeros((D, pad), F32)], axis=1).astype(F32).T
    b = jnp.concatenate([exp_b, grp_b, jnp.zeros((pad,), F32)]).astype(F32)
    whi, wlo = _split_bf16(w)
    tri = jnp.asarray(np.arange(tm)[:, None] < np.arange(tm)[None, :], dtype=BF16)
    const = lambda i: (0, 0)
    ncol = D // LANES
    return pl.pallas_call(
        _route_body,
        grid=(T // tm,),
        in_specs=[pl.BlockSpec((tm, D + LANES), lambda i: (i, 0)),
                  pl.BlockSpec((1, D), const), pl.BlockSpec((ROUTE_ROWS, D), const),
                  pl.BlockSpec((ROUTE_ROWS, D), const), pl.BlockSpec((ROUTE_ROWS, LANES), const),
                  pl.BlockSpec((tm, tm), const)],
        out_specs=[pl.BlockSpec((tm, LANES), lambda i: (i, ncol)),
                   pl.BlockSpec((SUBLANES, T), const),
                   pl.BlockSpec((ROUTE_ROWS, LANES), const)],
        out_shape=[jax.ShapeDtypeStruct(x2a.shape, F32), jax.ShapeDtypeStruct((SUBLANES, T), F32),
                   jax.ShapeDtypeStruct((ROUTE_ROWS, LANES), F32)],
        scratch_shapes=[pltpu.VMEM((ROUTE_ROWS, LANES), F32)],
        input_output_aliases={0: 0},
        compiler_params=_cparams("arbitrary"),
        name="route",
    )(x2a, moe_g.astype(F32)[None, :], whi, wlo, jnp.broadcast_to(b[:, None], (ROUTE_ROWS, LANES)), tri)


ROW_COPY_UNROLL = 8


def _row_copy_body(pos_ref, src_ref, *rest, rows, scatter):
    dst_ref, sem = rest[-2:]

    def copy(j, k):
        p = pos_ref[j * ROW_COPY_UNROLL + k]
        tok = (j, pl.ds(k, 1))
        slot = (lax.shift_right_logical(p, 3), pl.ds(p & (SUBLANES - 1), 1))
        if scatter:
            return pltpu.make_async_copy(src_ref.at[tok], dst_ref.at[slot], sem)
        return pltpu.make_async_copy(src_ref.at[slot], dst_ref.at[tok], sem)

    def start(j, _):
        for k in range(ROW_COPY_UNROLL):
            copy(j, k).start(priority=k % 2)
        return 0

    def wait(j, _):
        for k in range(ROW_COPY_UNROLL):
            copy(j, k).wait()
        return 0

    lax.fori_loop(0, rows // ROW_COPY_UNROLL, start, 0)
    lax.fori_loop(0, rows // ROW_COPY_UNROLL, wait, 0)


def _row_copy(pos, src, *, dst_init=None, dst_rows=None, name):
    assert ROW_COPY_UNROLL == SUBLANES
    T = pos.shape[0]
    rows = 512
    width = src.shape[1]
    scatter = dst_init is not None
    hbm = pl.BlockSpec(memory_space=pl.ANY)
    tok = pl.BlockSpec((rows // SUBLANES, SUBLANES, width), lambda i: (i, 0, 0))
    out_rows = dst_init.shape[0] if scatter else dst_rows
    tiled = lambda a: a.reshape(a.shape[0] // SUBLANES, SUBLANES, width)
    out = pl.pallas_call(
        functools.partial(_row_copy_body, rows=rows, scatter=scatter),
        grid=(T // rows,),
        in_specs=[pl.BlockSpec((rows,), lambda i: (i,), memory_space=pltpu.SMEM)]
        + ([tok, hbm] if scatter else [hbm]),
        out_specs=hbm if scatter else tok,
        out_shape=jax.ShapeDtypeStruct((out_rows // SUBLANES, SUBLANES, width), src.dtype),
        scratch_shapes=[pltpu.SemaphoreType.DMA(())],
        input_output_aliases={2: 0} if scatter else {},
        compiler_params=pltpu.CompilerParams(dimension_semantics=("arbitrary",)),
        name=name,
    )(*((pos, tiled(src), tiled(dst_init)) if scatter else (pos, tiled(src))))
    return out.reshape(out_rows, width)


def _ffn_body(ea_ref, eb_ref, nt_ref, xs_ref, g_ref, fg_ref, wga, wua, wda, wgb, wub, wdb, o_ref,
              ga, ua, da, gb, ub, db):
    D = g_ref.shape[1]
    i = pl.program_id(0)
    valid = i < nt_ref[0]
    prev = jnp.maximum(i - 1, 0)

    for e_ref, blocks in ((ea_ref, ((wga, ga), (wua, ua), (wda, da))), (eb_ref, ((wgb, gb), (wub, ub), (wdb, db)))):
        @pl.when(valid & ((i == 0) | (e_ref[i] != e_ref[prev])))
        def _(blocks=blocks):
            for src, dst in blocks:
                dst[...] = src[...].astype(BF16)

    @pl.when(valid)
    def _():
        x = xs_ref[:, :D]
        meta = xs_ref[:, D:]
        lower_in_a = ea_ref[i] < eb_ref[i]
        g_lo, g_hi = meta[:, META_WA:META_WA + 1], meta[:, META_WB:META_WB + 1]
        gates = (jnp.where(lower_in_a, g_lo, g_hi), jnp.where(lower_in_a, g_hi, g_lo))
        h = (_rms(x) * g_ref[...]).astype(BF16)
        y = x
        for (wg, wu, wd), gate in zip(((ga, ua, da), (gb, ub, db)), gates):
            a = jax.nn.silu(_dot(h, wg[...])) * _dot(h, wu[...]) * gate
            y = y + _dot(a.astype(BF16), wd[...])
        o_ref[...] = _rms(y) * fg_ref[...]

    @pl.when(jnp.logical_not(valid))
    def _():
        o_ref[...] = jnp.zeros_like(o_ref)


def _ffn(xs, D, tile_ea, tile_eb, n_tiles, moe_g, final_g, w_gate, w_up, w_down):
    tm = MOE_TILE
    P = xs.shape[0]
    nt_max = P // tm
    De = w_gate.shape[2]
    wg, wu, wd = w_gate.astype(F32), w_up.astype(F32), w_down.astype(F32)
    const = lambda i, ea, eb, nt: (0, 0)
    sel_a = lambda i, ea, eb, nt: (ea[i], 0, 0)
    sel_b = lambda i, ea, eb, nt: (eb[i], 0, 0)
    up = lambda sel: pl.BlockSpec((None, D, De), sel)
    down = lambda sel: pl.BlockSpec((None, De, D), sel)
    grid_spec = pltpu.PrefetchScalarGridSpec(
        num_scalar_prefetch=3,
        grid=(nt_max,),
        in_specs=[pl.BlockSpec((tm, D + LANES),
                               lambda i, ea, eb, nt: (jnp.maximum(jnp.minimum(i, nt[0] - 1), 0), 0)),
                  pl.BlockSpec((1, D), const), pl.BlockSpec((1, D), const),
                  up(sel_a), up(sel_a), down(sel_a), up(sel_b), up(sel_b), down(sel_b)],
        out_specs=pl.BlockSpec((tm, D), lambda i, ea, eb, nt: (i, 0)),
        scratch_shapes=[pltpu.VMEM((D, De), BF16), pltpu.VMEM((D, De), BF16), pltpu.VMEM((De, D), BF16)] * 2,
    )
    return pl.pallas_call(
        _ffn_body,
        grid_spec=grid_spec,
        out_shape=jax.ShapeDtypeStruct((P, D), F32),
        compiler_params=_cparams("arbitrary"),
        name="ffn",
    )(tile_ea, tile_eb, n_tiles, xs, moe_g.astype(F32)[None, :], final_g.astype(F32)[None, :],
      wg, wu, wd, wg, wu, wd)


def _moe(x2a, D, moe_g, grp_w, grp_b, exp_w, exp_b, w_gate, w_up, w_down, final_g):
    T = x2a.shape[0]
    tm = MOE_TILE
    x2a, record, counts = _route(x2a, D, moe_g, grp_w, grp_b, exp_w, exp_b)
    pos = record[META_POS].astype(jnp.int32)
    cnt = counts[:N_CLASSES, 0].astype(jnp.int32)
    ntile_c = (cnt + tm - 1) // tm
    tile_end = jnp.cumsum(ntile_c)
    nt_max = T // tm + N_CLASSES
    n_tiles = tile_end[-1:]
    tile_id = jnp.arange(nt_max, dtype=jnp.int32)
    tile_cls = jnp.minimum(jnp.sum((tile_end[None, :] <= tile_id[:, None]).astype(jnp.int32), axis=1),
                           N_CLASSES - 1)
    tile_cls = jnp.where(tile_id < n_tiles[0], tile_cls, tile_cls[jnp.maximum(n_tiles[0] - 1, 0)])
    slot_a = jnp.asarray([a for a, _ in PAIR_SLOTS], jnp.int32)
    slot_b = jnp.asarray([b for _, b in PAIR_SLOTS], jnp.int32)
    grp = tile_cls // PAIRS_PER_GROUP
    tile_ea = grp * EXPERTS_PER_GROUP + slot_a[tile_cls % PAIRS_PER_GROUP]
    tile_eb = grp * EXPERTS_PER_GROUP + slot_b[tile_cls % PAIRS_PER_GROUP]
    xs = _row_copy(pos, x2a, dst_init=jnp.zeros((nt_max * tm, D + LANES), F32), name="dispatch")
    ys = _ffn(xs, D, tile_ea, tile_eb, n_tiles.astype(jnp.int32), moe_g, final_g, w_gate, w_up, w_down)
    return _row_copy(pos, ys, dst_rows=T, name="unsort")


def kernel(x, mem, mix_norm_g, w_in, q_norm_g, k_norm_g, ssm_lambda_re, ssm_lambda_im, ssm_log_step, ssm_b_re, ssm_b_im, ssm_c_re, ssm_c_im, ssm_d, ssm_glu_w, ssm_glu_b, att_out_g, ssm_out_g, w_out, xa_norm_g, mem_norm_g, xa_wq, xa_wk, xa_wv, xa_wo, moe_norm_g, router_grp_w, router_grp_b, router_exp_w, router_exp_b, exp_w_gate, exp_w_up, exp_w_down, final_norm_g):
    B, S, D = x.shape
    assert mix_norm_g.shape[0] == 1, "single layer"
    x2d = x.reshape(B * S, D)
    q, kd, vd, u = _inproj(x2d, S, mix_norm_g[0], w_in[0], q_norm_g[0], k_norm_g[0])
    y_att = _attention(q, kd, vd, B, S)
    z = _s5(u, B, S, ssm_lambda_re[0], ssm_lambda_im[0], ssm_log_step[0], ssm_b_re[0], ssm_b_im[0],
            ssm_c_re[0], ssm_c_im[0], ssm_d[0])
    kmem, vmem = _memkv(mem, mem_norm_g[0], xa_wk[0], xa_wv[0])
    x2a = _mid(x2d, S, y_att, z, ssm_glu_w[0], ssm_glu_b[0], att_out_g[0], ssm_out_g[0], w_out[0],
               xa_norm_g[0], xa_wq[0], xa_wo[0], kmem, vmem)
    out = _moe(x2a, D, moe_norm_g[0], router_grp_w[0], router_grp_b[0], router_exp_w[0],
               router_exp_b[0], exp_w_gate[0], exp_w_up[0], exp_w_down[0], final_norm_g)
    return out.reshape(B, S, D)
```

```python
import functools

import numpy as np
import jax
import jax.numpy as jnp
from jax import lax
from jax.experimental import pallas as pl
from jax.experimental.pallas import tpu as pltpu

F32 = jnp.float32
BF16 = jnp.bfloat16

EPS = 1e-6
GRID_W = 64
ROPE_THETA = 10000.0
ATT_HEADS = 8
ATT_KV_HEADS = 2
HEAD_DIM = 64
D_ATT = ATT_HEADS * HEAD_DIM
D_KV = ATT_KV_HEADS * HEAD_DIM
SSM_GROUP = 16
SSM_STATE = 64
XA_HEADS = 4
N_EXPERT_GROUPS = 4
EXPERTS_PER_GROUP = 4
N_EXPERTS = N_EXPERT_GROUPS * EXPERTS_PER_GROUP
PAIRS_PER_GROUP = 6
N_CLASSES = N_EXPERT_GROUPS * PAIRS_PER_GROUP

LANES = 128
SUBLANES = 8
CHUNK = 16
VMEM_LIMIT = 56 * 1024 * 1024

META_CLS, META_RANK, META_WA, META_WB, META_POS = 0, 1, 2, 3, 4


def _cparams(*sem):
    return pltpu.CompilerParams(dimension_semantics=sem, vmem_limit_bytes=VMEM_LIMIT)


def _rms(x):
    return x * lax.rsqrt(jnp.mean(x * x, axis=-1, keepdims=True) + EPS)


def _dot(a, b):
    return jnp.dot(a, b, preferred_element_type=F32)


def _dot_nt(a, b):
    return lax.dot_general(a, b, (((1,), (1,)), ((), ())), preferred_element_type=F32)


def _inproj_body(x_ref, g_ref, w_ref, qkg_ref, ones_ref, cos_ref, sin_ref,
                 q_ref, k_ref, v_ref, u_ref, us):
    n_qk = q_ref.shape[1] + k_ref.shape[1]
    h = _rms(x_ref[...]) * g_ref[...]
    proj = _dot(h.astype(BF16), w_ref[...])
    qk = proj[:, :n_qk]
    ssq = _dot((qk * qk).astype(BF16), ones_ref[...])
    qk = qk * lax.rsqrt(ssq * (1.0 / HEAD_DIM) + EPS) * qkg_ref[...]
    cos = cos_ref[...]
    sin = sin_ref[...]
    lane = lax.broadcasted_iota(jnp.int32, cos.shape, 1)
    first = (lane % 32) < 16
    nq = q_ref.shape[1] // LANES
    for c in range(n_qk // LANES):
        xc = qk[:, c * LANES:(c + 1) * LANES]
        partner = jnp.where(first, pltpu.roll(xc, LANES - 16, 1), pltpu.roll(xc, 16, 1))
        rot = xc * cos + partner * sin
        if c < nq:
            q_ref[:, c * LANES:(c + 1) * LANES] = (rot * (HEAD_DIM ** -0.5)).astype(BF16)
        else:
            k_ref[:, (c - nq) * LANES:(c - nq + 1) * LANES] = rot.astype(BF16)
    nv = n_qk + v_ref.shape[1]
    v_ref[...] = proj[:, n_qk:nv].astype(BF16)
    rows = u_ref.shape[1]
    W = CHUNK * SSM_GROUP
    for j in range(u_ref.shape[0]):
        us[j] = proj[:, nv + j * LANES:nv + (j + 1) * LANES]
        steps = [us[j, pl.ds(t, rows, stride=CHUNK), :] for t in range(CHUNK)]
        for h in range(W // LANES):
            per_group = _piece_transpose(steps[h * GROUPS_PER_SET:(h + 1) * GROUPS_PER_SET])
            for g in range(GROUPS_PER_SET):
                u_ref[j, :, g * W + h * LANES:g * W + (h + 1) * LANES] = per_group[g].astype(BF16)


def _rope_tables(S):
    t = np.arange(S)
    freqs = (ROPE_THETA ** (-np.arange(16, dtype=np.float32) / 16)).astype(np.float32)
    halves = []
    for pos in (t // GRID_W, t % GRID_W):
        ang = pos.astype(np.float32)[:, None] * freqs[None, :]
        halves.append((np.cos(ang), np.sin(ang)))
    cos_h = np.concatenate([c for c, _ in halves for _ in range(2)], axis=1)
    sin_h = np.concatenate([sg * s for _, s in halves for sg in (-1.0, 1.0)], axis=1)
    return (jnp.asarray(np.concatenate([cos_h, cos_h], axis=1), F32),
            jnp.asarray(np.concatenate([sin_h, sin_h], axis=1), F32))


def _inproj(x2d, S, mix_g, w_in, q_g, k_g):
    T, D = x2d.shape
    tm = 512
    d_ssm = w_in.shape[1] - D_ATT - 2 * D_KV
    wq = w_in[:, :D_ATT]
    wk = w_in[:, D_ATT:D_ATT + D_KV]
    wv = w_in[:, D_ATT + D_KV:D_ATT + 2 * D_KV]
    wu = w_in[:, D_ATT + 2 * D_KV:]
    dup = lambda w: jnp.concatenate([w[:, :64], w[:, :64], w[:, 64:], w[:, 64:]], axis=1)
    w_ext = jnp.concatenate([wq, dup(wk), dup(wv), wu], axis=1).astype(BF16)
    n_qk = D_ATT + 2 * D_KV
    qkg = jnp.concatenate([jnp.tile(q_g, ATT_HEADS), jnp.tile(k_g, 2 * ATT_KV_HEADS)])[None, :]
    hid = np.arange(n_qk) // HEAD_DIM
    ones = jnp.asarray((hid[:, None] == hid[None, :]).astype(np.float32), dtype=BF16)
    cos_t, sin_t = _rope_tables(S)
    sb = S // tm
    const = lambda i: (0, 0)
    return pl.pallas_call(
        _inproj_body,
        grid=(T // tm,),
        in_specs=[
            pl.BlockSpec((tm, D), lambda i: (i, 0)),
            pl.BlockSpec((1, D), const),
            pl.BlockSpec(w_ext.shape, const),
            pl.BlockSpec((1, n_qk), const),
            pl.BlockSpec((n_qk, n_qk), const),
            pl.BlockSpec((tm, LANES), lambda i: (i % sb, 0)),
            pl.BlockSpec((tm, LANES), lambda i: (i % sb, 0)),
        ],
        out_specs=[
            pl.BlockSpec((tm, D_ATT), lambda i: (i, 0)),
            pl.BlockSpec((tm, 2 * D_KV), lambda i: (i, 0)),
            pl.BlockSpec((tm, 2 * D_KV), lambda i: (i, 0)),
            pl.BlockSpec((d_ssm // LANES, tm // CHUNK, CHUNK * LANES), lambda i: (0, i, 0)),
        ],
        out_shape=[
            jax.ShapeDtypeStruct((T, D_ATT), BF16),
            jax.ShapeDtypeStruct((T, 2 * D_KV), BF16),
            jax.ShapeDtypeStruct((T, 2 * D_KV), BF16),
            jax.ShapeDtypeStruct((d_ssm // LANES, T // CHUNK, CHUNK * LANES), BF16),
        ],
        scratch_shapes=[pltpu.VMEM((d_ssm // LANES, tm, LANES), F32)],
        compiler_params=_cparams("arbitrary"),
        name="inproj",
    )(x2d, mix_g[None, :], w_ext, qkg, ones, cos_t, sin_t)


ATT_KEY_CHUNK = 256
ATT_ROW_BLOCK = 256


def _attn_body(q_ref, k_ref, v_ref, o_ref, klo, khi, vlo, vhi, s_scr, e_scr):
    @pl.when(pl.program_id(2) == 0)
    def _():
        lane = lax.broadcasted_iota(jnp.int32, k_ref.shape, 1)
        lo = lane < HEAD_DIM
        kd = k_ref[...]
        vd = v_ref[...]
        zero = jnp.zeros_like(kd)
        klo[...] = jnp.where(lo, kd, zero)
        khi[...] = jnp.where(lo, zero, kd)
        vlo[...] = jnp.where(lo, vd, (lane == HEAD_DIM).astype(BF16))
        vhi[...] = jnp.where(lo, (lane == 0).astype(BF16), vd)

    S = klo.shape[0]
    rb = s_scr.shape[1]
    units = [(r0, p, km, vm) for r0 in range(0, q_ref.shape[0], rb)
             for p in range(2) for km, vm in ((klo, vlo), (khi, vhi))]

    def scores(i):
        r0, p, km, _ = units[i]
        qp = q_ref[r0:r0 + rb, p * LANES:(p + 1) * LANES]
        m = None
        for c in range(0, S, ATT_KEY_CHUNK):
            s = _dot_nt(qp, km[c:c + ATT_KEY_CHUNK, :])
            s_scr[i % 2, :, c:c + ATT_KEY_CHUNK] = s
            for l0 in range(0, ATT_KEY_CHUNK, LANES):
                m = s[:, l0:l0 + LANES] if m is None else jnp.maximum(m, s[:, l0:l0 + LANES])
        return jnp.max(m, axis=-1, keepdims=True)

    def values(i, m):
        vm = units[i][3]
        for c in range(0, S, ATT_KEY_CHUNK):
            e_scr[i % 2, :, c:c + ATT_KEY_CHUNK] = jnp.exp(s_scr[i % 2, :, c:c + ATT_KEY_CHUNK] - m).astype(BF16)
        acc = _dot(e_scr[i % 2], vm[...])
        ones_lane = HEAD_DIM if vm is vlo else 0
        return acc * (1.0 / acc[:, ones_lane:ones_lane + 1])

    lo = lax.broadcasted_iota(jnp.int32, (rb, LANES), 1) < HEAD_DIM
    m_prev = scores(0)
    first = None
    for i, (r0, p, _, vm) in enumerate(units):
        m_next = scores(i + 1) if i + 1 < len(units) else None
        o = values(i, m_prev)
        m_prev = m_next
        if vm is vlo:
            first = o
        else:
            o_ref[r0:r0 + rb, p * LANES:(p + 1) * LANES] = jnp.where(lo, first, o).astype(BF16)


def _attention(q, kd, vd, B, S):
    tq = 4 * ATT_ROW_BLOCK
    q3 = q.reshape(B, S, D_ATT)
    k3 = kd.reshape(B, S, 2 * D_KV)
    v3 = vd.reshape(B, S, 2 * D_KV)
    gw = D_ATT // ATT_KV_HEADS
    out = pl.pallas_call(
        _attn_body,
        grid=(B, ATT_KV_HEADS, S // tq),
        in_specs=[
            pl.BlockSpec((None, tq, gw), lambda b, j, i: (b, i, j)),
            pl.BlockSpec((None, S, LANES), lambda b, j, i: (b, 0, j)),
            pl.BlockSpec((None, S, LANES), lambda b, j, i: (b, 0, j)),
        ],
        out_specs=pl.BlockSpec((None, tq, gw), lambda b, j, i: (b, i, j)),
        out_shape=jax.ShapeDtypeStruct((B, S, D_ATT), BF16),
        scratch_shapes=[pltpu.VMEM((S, LANES), BF16)] * 4
        + [pltpu.VMEM((2, ATT_ROW_BLOCK, S), F32), pltpu.VMEM((2, ATT_ROW_BLOCK, S), BF16)],
        compiler_params=_cparams("arbitrary", "arbitrary", "arbitrary"),
        name="attn",
    )(q3, k3, v3)
    return out.reshape(B * S, D_ATT)


def _dot_tn3(a, b):
    tn = lambda x, y: lax.dot_general(x, y, (((0,), (0,)), ((), ())), preferred_element_type=F32)
    ah, al = _split_bf16(a)
    bh, bl = _split_bf16(b)
    return tn(ah, bh) + tn(al, bh) + tn(ah, bl)


def _s5op_body(lam_ref, br_ref, bi_ref, cr_ref, ci_ref, rep_ref, m_ref, et_ref, da_ref, db_ref):
    W = m_ref.shape[1]
    L = CHUNK
    step = lax.broadcasted_iota(jnp.int32, (1, W), 1) // SSM_GROUP

    def along_steps(a, channel_axis):
        hi = a.astype(BF16)
        r1 = a - hi.astype(F32)
        mid = r1.astype(BF16)
        lo = (r1 - mid.astype(F32)).astype(BF16)
        dims = (((channel_axis,), (0,)), ((), ()))
        rep = lambda x: lax.dot_general(x, rep_ref[...], dims, preferred_element_type=F32)
        return rep(hi) + rep(mid) + rep(lo)

    def direction(d):
        lam = lam_ref[d]
        lr = jnp.minimum(lam[:, 0:1], -1e-4)
        li = lam[:, 1:2]
        dt = jnp.exp(lam[:, 2:3])
        mag = jnp.exp(lr * dt)
        ab_r = mag * jnp.cos(li * dt)
        ab_i = mag * jnp.sin(li * dt)
        den = lr * lr + li * li
        nr = ab_r - 1.0
        f_r = (nr * lr + ab_i * li) / den
        f_i = (ab_i * lr - nr * li) / den
        b_r, b_i = along_steps(br_ref[d], 1), along_steps(bi_ref[d], 1)
        bb_r = f_r * b_r - f_i * b_i
        bb_i = f_r * b_i + f_i * b_r
        cr, ci = along_steps(cr_ref[d], 0), along_steps(ci_ref[d], 0)

        def power(n):
            sq_r, sq_i = ab_r, ab_i
            pr = pi = None
            for k in range(CHUNK.bit_length() - 1):
                bit = ((n >> k) & 1) == 1
                tr = jnp.where(bit, sq_r, 1.0)
                ti = jnp.where(bit, sq_i, 0.0)
                pr, pi = (tr, ti) if pr is None else (pr * tr - pi * ti, pr * ti + pi * tr)
                sq_r, sq_i = sq_r * sq_r - sq_i * sq_i, 2.0 * sq_r * sq_i
            return pr, pi

        def times_lam(p):
            return p[0] * ab_r - p[1] * ab_i, p[0] * ab_i + p[1] * ab_r

        def a_b(p):
            return p[0] * bb_r - p[1] * bb_i, p[0] * bb_i + p[1] * bb_r

        def c_a(p):
            return cr * p[0] - ci * p[1], cr * p[1] + ci * p[0]

        up, down = power(step), power(L - 1 - step)
        return bb_r, bb_i, cr, ci, a_b, c_a, up, down, times_lam

    fbr, fbi, _, _, f_ab, f_ca, f_up, f_down, f_next = direction(0)
    bbr, bbi, bcr, bci, b_ab, b_ca, b_up, b_down, b_next = direction(1)
    etf_r, etf_i = f_ab(f_down)
    etb_r, etb_i = b_ab(b_up)
    et_ref[...] = jnp.concatenate([etf_r, etb_r, etf_i, etb_i], axis=0).astype(BF16)
    df_r, df_i = f_ca(f_next(f_up))
    db_r, db_i = b_ca(b_next(b_down))
    zero = jnp.zeros_like(df_r)
    da_ref[...] = jnp.concatenate([df_r, zero, -df_i, zero], axis=0).astype(BF16)
    db_ref[...] = jnp.concatenate([zero, db_r, zero, -db_i], axis=0).astype(BF16)
    kf_r, kf_i = f_ca(f_up)
    lag0 = step == 0
    lhs_f = jnp.concatenate([fbr, -fbi], axis=0)
    lhs_b = jnp.concatenate([bbr, -bbi], axis=0)
    q_b = _dot_tn3(lhs_b, jnp.concatenate([db_r, db_i], axis=0))
    q_f = _dot_tn3(jnp.concatenate([lhs_f, lhs_b], axis=0),
                   jnp.concatenate([kf_r, kf_i, jnp.where(lag0, bcr, 0.0), jnp.where(lag0, bci, 0.0)],
                                   axis=0))
    z = jnp.concatenate([q_b, q_f], axis=1)
    for j in range(L):
        rows = z[j * SSM_GROUP:(j + 1) * SSM_GROUP, :]
        m_ref[j * SSM_GROUP:(j + 1) * SSM_GROUP, :] = pltpu.roll(
            rows, (2 * W - (L - j) * SSM_GROUP) % (2 * W), 1)[:, :W].astype(BF16)


def _s5_operators(lam_re, lam_im, log_step, b_re, b_im, c_re, c_im):
    _, G, P = lam_re.shape
    W = CHUNK * SSM_GROUP
    lam = jnp.stack([lam_re, lam_im, jnp.broadcast_to(log_step[..., None], lam_re.shape)], axis=-1).astype(F32)
    rep = jnp.asarray(np.arange(SSM_GROUP)[:, None] == np.arange(W)[None, :] % SSM_GROUP, dtype=BF16)
    per_g = lambda a: pl.BlockSpec((2, None) + a.shape[2:], lambda g: (0, g, 0, 0))
    out = pl.BlockSpec((None, W, W), lambda g: (g, 0, 0))
    args = (lam, b_re.astype(F32), b_im.astype(F32), c_re.astype(F32), c_im.astype(F32))
    return pl.pallas_call(
        _s5op_body,
        grid=(G,),
        in_specs=[per_g(a) for a in args] + [pl.BlockSpec(rep.shape, lambda g: (0, 0))],
        out_specs=[out] * 4,
        out_shape=[jax.ShapeDtypeStruct((G, W, W), BF16)] * 4,
        compiler_params=_cparams("arbitrary"),
        name="s5op",
    )(*args, rep)


GROUPS_PER_SET = LANES // SSM_GROUP
S5_SCAN_GROUPS = 4


def _transpose8(blocks, axis, piece):
    n = blocks[0].shape[axis]
    q = lax.broadcasted_iota(jnp.int32, blocks[0].shape, axis) // piece
    v = list(blocks)
    for d in (4, 2, 1):
        keep = (q & d) == 0
        nxt = list(v)
        for a in range(8):
            if a & d == 0:
                b = a + d
                nxt[a] = jnp.where(keep, v[a], pltpu.roll(v[b], d * piece, axis))
                nxt[b] = jnp.where(keep, pltpu.roll(v[a], n - d * piece, axis), v[b])
        v = nxt
    return v


def _piece_transpose(blocks):
    return _transpose8(blocks, 1, SSM_GROUP)


def _s5_body(u_ref, m_ref, et_ref, da_ref, db_ref, a_ref, d_ref, z_ref,
             xr_n, xi_n, xr_t, xi_t, har_s, hai_s, hbr_s, hbi_s, *, nb):
    R = u_ref.shape[0]
    nchunk = R // nb
    W = m_ref.shape[1]
    half = W // 2
    group = lambda g: slice(g * W, (g + 1) * W)
    fwd = lax.broadcasted_iota(jnp.int32, (nb, half), 1) < SSM_STATE

    def swap_row_order(src_a, src_b, dst_a, dst_b, to_chunk_major):
        def blocks(cb, _):
            batch_major = [pl.ds(pl.multiple_of(p * nchunk + cb * nb, nb), nb) for p in range(nb)]
            chunk_major = [pl.ds(pl.multiple_of((cb * nb + j) * nb, nb), nb) for j in range(nb)]
            rows_in, rows_out = (batch_major, chunk_major) if to_chunk_major else (chunk_major, batch_major)
            for k in range(S5_SCAN_GROUPS):
                for src, dst in ((src_a, dst_a), (src_b, dst_b)):
                    out = _transpose8([src[k, r, :] for r in rows_in], 0, 1)
                    for j in range(nb):
                        dst[k, rows_out[j], :] = out[j]
            return 0
        lax.fori_loop(0, nchunk // nb, blocks, 0)

    for g0 in range(0, GROUPS_PER_SET, S5_SCAN_GROUPS):
        coef = []
        for k in range(S5_SCAN_GROUPS):
            g = g0 + k
            x = _dot_nt(u_ref[:, group(g)], et_ref[g])
            xr_n[k] = x[:, :half]
            xi_n[k] = x[:, half:]
            lr = jnp.minimum(a_ref[g, 0:1, :], -1e-4)
            dt = jnp.exp(a_ref[g, 2:3, :])
            mag = jnp.exp(lr * dt * CHUNK)
            ang = a_ref[g, 1:2, :] * dt * CHUNK
            coef.append((jnp.broadcast_to(mag * jnp.cos(ang), (nb, half)),
                         jnp.broadcast_to(mag * jnp.sin(ang), (nb, half))))
        swap_row_order(xr_n, xi_n, xr_t, xi_t, True)

        def step(c, carry):
            rf = pl.ds(pl.multiple_of(c * nb, nb), nb)
            rb = pl.ds(pl.multiple_of((nchunk - 1 - c) * nb, nb), nb)
            new = []
            for k in range(S5_SCAN_GROUPS):
                sr, si = carry[2 * k], carry[2 * k + 1]
                ar, ai = coef[k]
                har_s[k, rf, :] = sr
                hai_s[k, rf, :] = si
                hbr_s[k, rb, :] = sr
                hbi_s[k, rb, :] = si
                xr = jnp.where(fwd, xr_t[k, rf, :], xr_t[k, rb, :])
                xi = jnp.where(fwd, xi_t[k, rf, :], xi_t[k, rb, :])
                new += [sr * ar - si * ai + xr, sr * ai + si * ar + xi]
            return tuple(new)

        zero = jnp.zeros((nb, half), F32)
        lax.fori_loop(0, nchunk, step, (zero,) * (2 * S5_SCAN_GROUPS))
        for k in range(S5_SCAN_GROUPS):
            g = g0 + k
            ha = jnp.concatenate([har_s[k], hai_s[k]], axis=1).astype(BF16)
            hb = jnp.concatenate([hbr_s[k], hbi_s[k]], axis=1).astype(BF16)
            y_t = _dot(ha, da_ref[g]) + _dot(hb, db_ref[g])
            xr_n[k] = y_t[:, :half]
            xi_n[k] = y_t[:, half:]
        swap_row_order(xr_n, xi_n, xr_t, xi_t, False)
        for k in range(S5_SCAN_GROUPS):
            g = g0 + k
            u = u_ref[:, group(g)]
            y = _dot(u, m_ref[g]) + jnp.concatenate([xr_t[k], xi_t[k]], axis=1) + d_ref[g] * u.astype(F32)
            z_ref[:, group(g)] = jax.nn.gelu(y).astype(BF16)


def _s5(u_set, B, S, lam_re, lam_im, log_step, b_re, b_im, c_re, c_im, ssm_d):
    n_set, R, WS = u_set.shape
    G = n_set * GROUPS_PER_SET
    W = CHUNK * SSM_GROUP
    assert R == (S // CHUNK) * B and WS == CHUNK * LANES
    M, ET, DA, DB = _s5_operators(lam_re, lam_im, log_step, b_re, b_im, c_re, c_im)
    both = lambda a: jnp.concatenate([a[0], a[1]], axis=-1)
    lam = jnp.stack([both(lam_re), both(lam_im),
                     both(jnp.broadcast_to(log_step[..., None], lam_re.shape))], axis=1).astype(F32)
    dvec = jnp.tile(ssm_d.astype(F32).reshape(G, 1, SSM_GROUP), (1, CHUNK, 1)).reshape(G, 1, W)
    per_set = pl.BlockSpec((None, R, WS), lambda s: (s, 0, 0))
    blk = lambda *sh: pl.BlockSpec((GROUPS_PER_SET,) + sh, lambda s: (s,) + (0,) * len(sh))
    return pl.pallas_call(
        functools.partial(_s5_body, nb=B),
        grid=(n_set,),
        in_specs=[per_set, blk(W, W), blk(W, W), blk(W, W), blk(W, W), blk(3, W // 2), blk(1, W)],
        out_specs=per_set,
        out_shape=jax.ShapeDtypeStruct((n_set, R, WS), BF16),
        scratch_shapes=[pltpu.VMEM((S5_SCAN_GROUPS, R, W // 2), F32)] * 8,
        compiler_params=_cparams("arbitrary"),
        name="s5",
    )(u_set, M, ET, DA, DB, lam, dvec)


def _memkv_body(m_ref, g_ref, wk_ref, wv_ref, k_ref, v_ref):
    h = (_rms(m_ref[...]) * g_ref[...]).astype(BF16)
    k_ref[...] = _dot(h, wk_ref[...]).astype(BF16)
    v_ref[...] = _dot(h, wv_ref[...]).astype(BF16)


def _memkv(mem, g, wk, wv):
    B, M, D = mem.shape
    const = lambda b: (0, 0)
    row = pl.BlockSpec((None, M, D), lambda b: (b, 0, 0))
    return pl.pallas_call(
        _memkv_body,
        grid=(B,),
        in_specs=[row, pl.BlockSpec((1, D), const), pl.BlockSpec((D, D), const),
                  pl.BlockSpec((D, D), const)],
        out_specs=[row, row],
        out_shape=[jax.ShapeDtypeStruct((B, M, D), BF16)] * 2,
        compiler_params=_cparams("arbitrary"),
        name="memkv",
    )(mem, g[None, :], wk.astype(BF16), wv.astype(BF16))


MID_ROW_BLOCK = 512


def _mid_body(x_ref, ya_ref, z_ref, gw_ref, gb_ref, ag_ref, sg_ref, wo_ref, xg_ref,
              wq_ref, xwo_ref, k_ref, v_ref, o_ref, zs):
    D = x_ref.shape[1]
    dh = D // XA_HEADS
    n_set = z_ref.shape[0]
    crows = MID_ROW_BLOCK // CHUNK

    def glu(r0, st):
        c0 = r0 // CHUNK
        W = CHUNK * SSM_GROUP
        for j in range(n_set):
            for h in range(W // LANES):
                steps = _piece_transpose([z_ref[j, c0:c0 + crows, g * W + h * LANES:g * W + (h + 1) * LANES].astype(F32)
                                          for g in range(GROUPS_PER_SET)])
                for k in range(GROUPS_PER_SET):
                    zs[j, pl.ds(r0 + h * GROUPS_PER_SET + k, crows, stride=CHUNK), :] = steps[k]
        z = jnp.concatenate([zs[j, r0:r0 + MID_ROW_BLOCK, :] for j in range(n_set)], axis=1).astype(BF16)
        st['s'] = z.astype(F32) * jax.nn.sigmoid(_dot(z, gw_ref[...]) + gb_ref[...])

    def out_proj(r0, st):
        rows = slice(r0, r0 + MID_ROW_BLOCK)
        ya = _rms(ya_ref[rows, :].astype(F32)) * ag_ref[...]
        ys = _rms(st.pop('s')) * sg_ref[...]
        y = jnp.concatenate([ya, ys], axis=1).astype(BF16)
        st['x1'] = x_ref[rows, :] + _dot(y, wo_ref[...])

    def query(r0, st):
        hq = (_rms(st['x1']) * xg_ref[...]).astype(BF16)
        st['q'] = (_dot(hq, wq_ref[...]) * (dh ** -0.5)).astype(BF16)

    def attend(r0, st):
        q = st.pop('q')
        outs = []
        for h in range(XA_HEADS):
            sl = slice(h * dh, (h + 1) * dh)
            sc = _dot_nt(q[:, sl], k_ref[:, sl])
            e = jnp.exp(sc - jnp.max(sc, axis=-1, keepdims=True))
            inv = 1.0 / jnp.sum(e, axis=-1, keepdims=True)
            outs.append((_dot(e.astype(BF16), v_ref[:, sl]) * inv).astype(BF16))
        st['o'] = jnp.concatenate(outs, axis=1)

    def finish(r0, st):
        rows = slice(r0, r0 + MID_ROW_BLOCK)
        o_ref[rows, :D] = st.pop('x1') + _dot(st.pop('o'), xwo_ref[...])
        o_ref[rows, D:] = jnp.zeros((MID_ROW_BLOCK, o_ref.shape[1] - D), F32)

    blocks = [(r0, {}) for r0 in range(0, x_ref.shape[0], MID_ROW_BLOCK)]
    for stage in (glu, out_proj, query, attend, finish):
        for r0, st in blocks:
            stage(r0, st)


def _mid(x2d, S, y_att, z, glu_w, glu_b, att_g, ssm_g, w_out, xa_g, wq, wo, kmem, vmem):
    T, D = x2d.shape
    M = kmem.shape[1]
    tm = 2 * MID_ROW_BLOCK
    sb = S // tm
    const = lambda i: (0, 0)
    tok = lambda w: pl.BlockSpec((tm, w), lambda i: (i, 0))
    full = lambda a: pl.BlockSpec(a.shape, const)
    vec = lambda a: a.astype(F32)[None, :]
    mem = pl.BlockSpec((None, M, D), lambda i: (i // sb, 0, 0))
    args = [x2d, y_att, z, glu_w.astype(BF16), vec(glu_b), vec(att_g), vec(ssm_g), w_out.astype(BF16),
            vec(xa_g), wq.astype(BF16), wo.astype(BF16), kmem, vmem]
    zspec = pl.BlockSpec((z.shape[0], tm // CHUNK, CHUNK * LANES), lambda i: (0, i, 0))
    specs = [tok(D), tok(y_att.shape[1]), zspec] + [full(a) for a in args[3:11]] + [mem, mem]
    return pl.pallas_call(
        _mid_body,
        grid=(T // tm,),
        in_specs=specs,
        out_specs=pl.BlockSpec((tm, D + LANES), lambda i: (i, 0)),
        out_shape=jax.ShapeDtypeStruct((T, D + LANES), F32),
        scratch_shapes=[pltpu.VMEM((z.shape[0], tm, LANES), F32)],
        compiler_params=_cparams("arbitrary"),
        name="mid",
    )(*args)


def _split_bf16(a):
    hi = a.astype(BF16)
    return hi, (a - hi.astype(F32)).astype(BF16)


ROUTE_ROWS = 32
PAIR_SLOTS = ((0, 1), (2, 1), (2, 0), (3, 0), (3, 1), (3, 2))
MOE_TILE = 256


def _route_body(x_ref, g_ref, whi_ref, wlo_ref, b_ref, tri_ref, meta_ref, mt_ref, cnt_ref, carry):
    i = pl.program_id(0)
    tm, D = x_ref.shape[0], g_ref.shape[1]

    @pl.when(i == 0)
    def _():
        carry[...] = jnp.zeros_like(carry)

    h = _rms(x_ref[:, :D]) * g_ref[...]
    hhi, hlo = _split_bf16(h)
    logits = (_dot_nt(whi_ref[...], hhi) + _dot_nt(whi_ref[...], hlo) + _dot_nt(wlo_ref[...], hhi)
              + b_ref[:, 0:1])
    ninf = jnp.float32(-jnp.inf)
    cmax = lambda v: jnp.max(v, axis=0, keepdims=True)
    cmin = lambda v: jnp.min(v, axis=0, keepdims=True)
    gl = logits[N_EXPERTS:N_EXPERTS + N_EXPERT_GROUPS]
    grow = lax.broadcasted_iota(jnp.int32, gl.shape, 0)
    gmax = cmax(gl)
    g_sel = cmin(jnp.where(gl == gmax, grow, N_EXPERT_GROUPS))
    g_w = 1.0 / jnp.sum(jnp.exp(gl - gmax), axis=0, keepdims=True)
    el = logits[:N_EXPERTS]
    erow = lax.broadcasted_iota(jnp.int32, el.shape, 0)
    el = jnp.where((erow >> 2) == g_sel, el, ninf)
    v1 = cmax(el)
    i1 = cmin(jnp.where(el == v1, erow, N_EXPERTS))
    el2 = jnp.where(erow == i1, ninf, el)
    v2 = cmax(el2)
    i2 = cmin(jnp.where(el2 == v2, erow, N_EXPERTS))
    t = jnp.exp(v2 - v1)
    w1 = g_w / (1.0 + t)
    w2 = g_w * t / (1.0 + t)
    first_low = i1 < i2
    ia = jnp.minimum(i1, i2) & 3
    ib = jnp.maximum(i1, i2) & 3
    wa = jnp.where(first_low, w1, w2)
    wb = jnp.where(first_low, w2, w1)
    pair = ((ia * (7 - ia)) >> 1) + ib - ia - 1
    order = jnp.where(pair == 3, 1, jnp.where(pair == 1, 2, jnp.where(pair == 2, 3, pair)))
    cls = g_sel * PAIRS_PER_GROUP + order
    crow = lax.broadcasted_iota(jnp.int32, (ROUTE_ROWS, tm), 0)
    onehot = (crow == cls).astype(F32)
    before = _dot(onehot.astype(BF16), tri_ref[...])
    rank = jnp.sum(onehot * (before + carry[:, 0:1]), axis=0, keepdims=True)
    carry[...] = carry[...] + jnp.sum(onehot, axis=1, keepdims=True)
    cnt_ref[...] = carry[...]

    def record(rows):
        r = lax.broadcasted_iota(jnp.int32, (rows, tm), 0)
        return jnp.where(r == META_CLS, cls.astype(F32),
                         jnp.where(r == META_RANK, rank,
                                   jnp.where(r == META_WA, wa, jnp.where(r == META_WB, wb, 0.0))))

    meta_ref[...] = record(LANES).T
    mt_ref[:, pl.ds(pl.multiple_of(i * tm, tm), tm)] = record(mt_ref.shape[0])

    @pl.when(i == pl.num_programs(0) - 1)
    def _():
        ntile = jnp.floor((carry[...] + (MOE_TILE - 1.0)) * (1.0 / MOE_TILE))
        r = lax.broadcasted_iota(jnp.int32, (ROUTE_ROWS, ROUTE_ROWS), 0)
        c = lax.broadcasted_iota(jnp.int32, (ROUTE_ROWS, ROUTE_ROWS), 1)
        off = _dot((c < r).astype(BF16), ntile.astype(BF16)) * MOE_TILE
        cls_all = mt_ref[META_CLS:META_CLS + 1, :]
        pos = mt_ref[META_RANK:META_RANK + 1, :]
        for k in range(N_CLASSES):
            pos = pos + jnp.where(cls_all == k, off[k:k + 1, 0:1], 0.0)
        mt_ref[META_POS:META_POS + 1, :] = pos


def _route(x2a, D, moe_g, grp_w, grp_b, exp_w, exp_b):
    T = x2a.shape[0]
    tm = 1024
    pad = ROUTE_ROWS - N_EXPERT_GROUPS - N_EXPERTS
    w = jnp.concatenate([exp_w, grp_w, jnp.zeros((D, pad), F32)], axis=1).astype(F32).T
    b = jnp.concatenate([exp_b, grp_b, jnp.zeros((pad,), F32)]).astype(F32)
    whi, wlo = _split_bf16(w)
    tri = jnp.asarray(np.arange(tm)[:, None] < np.arange(tm)[None, :], dtype=BF16)
    const = lambda i: (0, 0)
    ncol = D // LANES
    return pl.pallas_call(
        _route_body,
        grid=(T // tm,),
        in_specs=[pl.BlockSpec((tm, D + LANES), lambda i: (i, 0)),
                  pl.BlockSpec((1, D), const), pl.BlockSpec((ROUTE_ROWS, D), const),
                  pl.BlockSpec((ROUTE_ROWS, D), const), pl.BlockSpec((ROUTE_ROWS, LANES), const),
                  pl.BlockSpec((tm, tm), const)],
        out_specs=[pl.BlockSpec((tm, LANES), lambda i: (i, ncol)),
                   pl.BlockSpec((SUBLANES, T), const),
                   pl.BlockSpec((ROUTE_ROWS, LANES), const)],
        out_shape=[jax.ShapeDtypeStruct(x2a.shape, F32), jax.ShapeDtypeStruct((SUBLANES, T), F32),
                   jax.ShapeDtypeStruct((ROUTE_ROWS, LANES), F32)],
        scratch_shapes=[pltpu.VMEM((ROUTE_ROWS, LANES), F32)],
        input_output_aliases={0: 0},
        compiler_params=_cparams("arbitrary"),
        name="route",
    )(x2a, moe_g.astype(F32)[None, :], whi, wlo, jnp.broadcast_to(b[:, None], (ROUTE_ROWS, LANES)), tri)


ROW_COPY_UNROLL = 8


def _row_copy_body(pos_ref, src_ref, *rest, rows, scatter):
    dst_ref, sem = rest[-2:]

    def copy(j, k):
        p = pos_ref[j * ROW_COPY_UNROLL + k]
        tok = (j, pl.ds(k, 1))
        slot = (lax.shift_right_logical(p, 3), pl.ds(p & (SUBLANES - 1), 1))
        if scatter:
            return pltpu.make_async_copy(src_ref.at[tok], dst_ref.at[slot], sem)
        return pltpu.make_async_copy(src_ref.at[slot], dst_ref.at[tok], sem)

    def start(j, _):
        for k in range(ROW_COPY_UNROLL):
            copy(j, k).start(priority=k % 2)
        return 0

    def wait(j, _):
        for k in range(ROW_COPY_UNROLL):
            copy(j, k).wait()
        return 0

    lax.fori_loop(0, rows // ROW_COPY_UNROLL, start, 0)
    lax.fori_loop(0, rows // ROW_COPY_UNROLL, wait, 0)


def _row_copy(pos, src, *, dst_init=None, dst_rows=None, name):
    assert ROW_COPY_UNROLL == SUBLANES
    T = pos.shape[0]
    rows = 512
    width = src.shape[1]
    scatter = dst_init is not None
    hbm = pl.BlockSpec(memory_space=pl.ANY)
    tok = pl.BlockSpec((rows // SUBLANES, SUBLANES, width), lambda i: (i, 0, 0))
    out_rows = dst_init.shape[0] if scatter else dst_rows
    tiled = lambda a: a.reshape(a.shape[0] // SUBLANES, SUBLANES, width)
    out = pl.pallas_call(
        functools.partial(_row_copy_body, rows=rows, scatter=scatter),
        grid=(T // rows,),
        in_specs=[pl.BlockSpec((rows,), lambda i: (i,), memory_space=pltpu.SMEM)]
        + ([tok, hbm] if scatter else [hbm]),
        out_specs=hbm if scatter else tok,
        out_shape=jax.ShapeDtypeStruct((out_rows // SUBLANES, SUBLANES, width), src.dtype),
        scratch_shapes=[pltpu.SemaphoreType.DMA(())],
        input_output_aliases={2: 0} if scatter else {},
        compiler_params=pltpu.CompilerParams(dimension_semantics=("arbitrary",)),
        name=name,
    )(*((pos, tiled(src), tiled(dst_init)) if scatter else (pos, tiled(src))))
    return out.reshape(out_rows, width)


def _ffn_body(ea_ref, eb_ref, na_ref, nb_ref, nt_ref, xs_ref, g_ref, fg_ref, wg_hbm, wu_hbm, wd_hbm, o_ref,
              sga, sua, sda, sgb, sub, sdb, ga, ua, da, gb, ub, db, sem):
    D = g_ref.shape[1]
    i = pl.program_id(0)
    valid = i < nt_ref[0]
    prev = jnp.maximum(i - 1, 0)
    slots = ((0, ea_ref, na_ref, (sga, sua, sda), (ga, ua, da)), (1, eb_ref, nb_ref, (sgb, sub, sdb), (gb, ub, db)))

    def fetch(e, stage, s):
        return [pltpu.make_async_copy(w.at[e], buf, sem.at[s])
                for w, buf in zip((wg_hbm, wu_hbm, wd_hbm), stage)]

    for s, e_ref, n_ref, stage, work in slots:
        @pl.when(i == 0)
        def _(s=s, e_ref=e_ref, stage=stage):
            for c in fetch(e_ref[0], stage, s):
                c.start()

        @pl.when(valid & ((i == 0) | (e_ref[i] != e_ref[prev])))
        def _(s=s, e_ref=e_ref, n_ref=n_ref, stage=stage, work=work):
            for c in fetch(e_ref[i], stage, s):
                c.wait()
            for src, dst in zip(stage, work):
                dst[...] = src[...].astype(BF16)

            @pl.when(n_ref[i] >= 0)
            def _():
                for c in fetch(n_ref[i], stage, s):
                    c.start()

    @pl.when(valid)
    def _():
        x = xs_ref[:, :D]
        meta = xs_ref[:, D:]
        lower_in_a = ea_ref[i] < eb_ref[i]
        g_lo, g_hi = meta[:, META_WA:META_WA + 1], meta[:, META_WB:META_WB + 1]
        gates = (jnp.where(lower_in_a, g_lo, g_hi), jnp.where(lower_in_a, g_hi, g_lo))
        h = (_rms(x) * g_ref[...]).astype(BF16)
        y = x
        for (wg, wu, wd), gate in zip(((ga, ua, da), (gb, ub, db)), gates):
            a = jax.nn.silu(_dot(h, wg[...])) * _dot(h, wu[...]) * gate
            y = y + _dot(a.astype(BF16), wd[...])
        o_ref[...] = _rms(y) * fg_ref[...]

    @pl.when(jnp.logical_not(valid))
    def _():
        o_ref[...] = jnp.zeros_like(o_ref)


def _ffn(xs, D, tile_ea, tile_eb, n_tiles, moe_g, final_g, w_gate, w_up, w_down):
    tm = MOE_TILE
    P = xs.shape[0]
    nt_max = P // tm
    De = w_gate.shape[2]

    def next_expert(e):
        j = jnp.arange(nt_max, dtype=jnp.int32)
        later = (j[None, :] > j[:, None]) & (j[None, :] < n_tiles[0]) & (e[None, :] != e[:, None])
        first = jnp.min(jnp.where(later, j[None, :], nt_max), axis=1)
        return jnp.where(first < nt_max, e[jnp.minimum(first, nt_max - 1)], -1).astype(jnp.int32)

    const = lambda i, ea, eb, na, nb, nt: (0, 0)
    hbm = pl.BlockSpec(memory_space=pl.ANY)
    expert = lambda dt: [pltpu.VMEM((D, De), dt), pltpu.VMEM((D, De), dt), pltpu.VMEM((De, D), dt)]
    grid_spec = pltpu.PrefetchScalarGridSpec(
        num_scalar_prefetch=5,
        grid=(nt_max,),
        in_specs=[pl.BlockSpec((tm, D + LANES),
                               lambda i, ea, eb, na, nb, nt: (jnp.maximum(jnp.minimum(i, nt[0] - 1), 0), 0)),
                  pl.BlockSpec((1, D), const), pl.BlockSpec((1, D), const), hbm, hbm, hbm],
        out_specs=pl.BlockSpec((tm, D), lambda i, ea, eb, na, nb, nt: (i, 0)),
        scratch_shapes=expert(F32) * 2 + expert(BF16) * 2 + [pltpu.SemaphoreType.DMA((2,))],
    )
    return pl.pallas_call(
        _ffn_body,
        grid_spec=grid_spec,
        out_shape=jax.ShapeDtypeStruct((P, D), F32),
        compiler_params=_cparams("arbitrary"),
        name="ffn",
    )(tile_ea, tile_eb, next_expert(tile_ea), next_expert(tile_eb), n_tiles, xs,
      moe_g.astype(F32)[None, :], final_g.astype(F32)[None, :],
      w_gate.astype(F32), w_up.astype(F32), w_down.astype(F32))


def _moe(x2a, D, moe_g, grp_w, grp_b, exp_w, exp_b, w_gate, w_up, w_down, final_g):
    T = x2a.shape[0]
    tm = MOE_TILE
    x2a, record, counts = _route(x2a, D, moe_g, grp_w, grp_b, exp_w, exp_b)
    pos = record[META_POS].astype(jnp.int32)
    cnt = counts[:N_CLASSES, 0].astype(jnp.int32)
    ntile_c = (cnt + tm - 1) // tm
    tile_end = jnp.cumsum(ntile_c)
    nt_max = T // tm + N_CLASSES
    n_tiles = tile_end[-1:]
    tile_id = jnp.arange(nt_max, dtype=jnp.int32)
    tile_cls = jnp.minimum(jnp.sum((tile_end[None, :] <= tile_id[:, None]).astype(jnp.int32), axis=1),
                           N_CLASSES - 1)
    tile_cls = jnp.where(tile_id < n_tiles[0], tile_cls, tile_cls[jnp.maximum(n_tiles[0] - 1, 0)])
    slot_a = jnp.asarray([a for a, _ in PAIR_SLOTS], jnp.int32)
    slot_b = jnp.asarray([b for _, b in PAIR_SLOTS], jnp.int32)
    grp = tile_cls // PAIRS_PER_GROUP
    tile_ea = grp * EXPERTS_PER_GROUP + slot_a[tile_cls % PAIRS_PER_GROUP]
    tile_eb = grp * EXPERTS_PER_GROUP + slot_b[tile_cls % PAIRS_PER_GROUP]
    xs = _row_copy(pos, x2a, dst_init=jnp.zeros((nt_max * tm, D + LANES), F32), name="dispatch")
    ys = _ffn(xs, D, tile_ea, tile_eb, n_tiles.astype(jnp.int32), moe_g, final_g, w_gate, w_up, w_down)
    return _row_copy(pos, ys, dst_rows=T, name="unsort")


def kernel(x, mem, mix_norm_g, w_in, q_norm_g, k_norm_g, ssm_lambda_re, ssm_lambda_im, ssm_log_step, ssm_b_re, ssm_b_im, ssm_c_re, ssm_c_im, ssm_d, ssm_glu_w, ssm_glu_b, att_out_g, ssm_out_g, w_out, xa_norm_g, mem_norm_g, xa_wq, xa_wk, xa_wv, xa_wo, moe_norm_g, router_grp_w, router_grp_b, router_exp_w, router_exp_b, exp_w_gate, exp_w_up, exp_w_down, final_norm_g):
    B, S, D = x.shape
    assert mix_norm_g.shape[0] == 1, "single layer"
    x2d = x.reshape(B * S, D)
    q, kd, vd, u = _inproj(x2d, S, mix_norm_g[0], w_in[0], q_norm_g[0], k_norm_g[0])
    y_att = _attention(q, kd, vd, B, S)
    z = _s5(u, B, S, ssm_lambda_re[0], ssm_lambda_im[0], ssm_log_step[0], ssm_b_re[0], ssm_b_im[0],
            ssm_c_re[0], ssm_c_im[0], ssm_d[0])
    kmem, vmem = _memkv(mem, mem_norm_g[0], xa_wk[0], xa_wv[0])
    x2a = _mid(x2d, S, y_att, z, ssm_glu_w[0], ssm_glu_b[0], att_out_g[0], ssm_out_g[0], w_out[0],
               xa_norm_g[0], xa_wq[0], xa_wo[0], kmem, vmem)
    out = _moe(x2a, D, moe_norm_g[0], router_grp_w[0], router_grp_b[0], router_exp_w[0],
               router_exp_b[0], exp_w_gate[0], exp_w_up[0], exp_w_down[0], final_norm_g)
    return out.reshape(B, S, D)
```

```python
import functools

import numpy as np
import jax
import jax.numpy as jnp
from jax import lax
from jax.experimental import pallas as pl
from jax.experimental.pallas import tpu as pltpu

F32 = jnp.float32
BF16 = jnp.bfloat16

EPS = 1e-6
GRID_W = 64
ROPE_THETA = 10000.0
ATT_HEADS = 8
ATT_KV_HEADS = 2
HEAD_DIM = 64
D_ATT = ATT_HEADS * HEAD_DIM
D_KV = ATT_KV_HEADS * HEAD_DIM
SSM_GROUP = 16
SSM_STATE = 64
XA_HEADS = 4
N_EXPERT_GROUPS = 4
EXPERTS_PER_GROUP = 4
N_EXPERTS = N_EXPERT_GROUPS * EXPERTS_PER_GROUP
PAIRS_PER_GROUP = 6
N_CLASSES = N_EXPERT_GROUPS * PAIRS_PER_GROUP

LANES = 128
SUBLANES = 8
CHUNK = 16
VMEM_LIMIT = 56 * 1024 * 1024

META_CLS, META_RANK, META_WA, META_WB, META_POS = 0, 1, 2, 3, 4


def _cparams(*sem):
    return pltpu.CompilerParams(dimension_semantics=sem, vmem_limit_bytes=VMEM_LIMIT)


def _rms(x):
    return x * lax.rsqrt(jnp.mean(x * x, axis=-1, keepdims=True) + EPS)


def _dot(a, b):
    return jnp.dot(a, b, preferred_element_type=F32)


def _dot_nt(a, b):
    return lax.dot_general(a, b, (((1,), (1,)), ((), ())), preferred_element_type=F32)


def _inproj_body(x_ref, g_ref, w_ref, qkg_ref, ones_ref, cos_ref, sin_ref,
                 q_ref, k_ref, v_ref, u_ref, us):
    n_qk = q_ref.shape[1] + k_ref.shape[1]
    h = _rms(x_ref[...]) * g_ref[...]
    proj = _dot(h.astype(BF16), w_ref[...])
    qk = proj[:, :n_qk]
    ssq = _dot((qk * qk).astype(BF16), ones_ref[...])
    qk = qk * lax.rsqrt(ssq * (1.0 / HEAD_DIM) + EPS) * qkg_ref[...]
    cos = cos_ref[...]
    sin = sin_ref[...]
    lane = lax.broadcasted_iota(jnp.int32, cos.shape, 1)
    first = (lane % 32) < 16
    nq = q_ref.shape[1] // LANES
    for c in range(n_qk // LANES):
        xc = qk[:, c * LANES:(c + 1) * LANES]
        partner = jnp.where(first, pltpu.roll(xc, LANES - 16, 1), pltpu.roll(xc, 16, 1))
        rot = xc * cos + partner * sin
        if c < nq:
            q_ref[:, c * LANES:(c + 1) * LANES] = (rot * (HEAD_DIM ** -0.5)).astype(BF16)
        else:
            k_ref[:, (c - nq) * LANES:(c - nq + 1) * LANES] = rot.astype(BF16)
    nv = n_qk + v_ref.shape[1]
    v_ref[...] = proj[:, n_qk:nv].astype(BF16)
    rows = u_ref.shape[1]
    W = CHUNK * SSM_GROUP
    for j in range(u_ref.shape[0]):
        us[j] = proj[:, nv + j * LANES:nv + (j + 1) * LANES]
        steps = [us[j, pl.ds(t, rows, stride=CHUNK), :] for t in range(CHUNK)]
        for h in range(W // LANES):
            per_group = _piece_transpose(steps[h * GROUPS_PER_SET:(h + 1) * GROUPS_PER_SET])
            for g in range(GROUPS_PER_SET):
                u_ref[j, :, g * W + h * LANES:g * W + (h + 1) * LANES] = per_group[g].astype(BF16)


def _rope_tables(S):
    t = np.arange(S)
    freqs = (ROPE_THETA ** (-np.arange(16, dtype=np.float32) / 16)).astype(np.float32)
    halves = []
    for pos in (t // GRID_W, t % GRID_W):
        ang = pos.astype(np.float32)[:, None] * freqs[None, :]
        halves.append((np.cos(ang), np.sin(ang)))
    cos_h = np.concatenate([c for c, _ in halves for _ in range(2)], axis=1)
    sin_h = np.concatenate([sg * s for _, s in halves for sg in (-1.0, 1.0)], axis=1)
    return (jnp.asarray(np.concatenate([cos_h, cos_h], axis=1), F32),
            jnp.asarray(np.concatenate([sin_h, sin_h], axis=1), F32))


def _inproj(x2d, S, mix_g, w_in, q_g, k_g):
    T, D = x2d.shape
    tm = 512
    d_ssm = w_in.shape[1] - D_ATT - 2 * D_KV
    wq = w_in[:, :D_ATT]
    wk = w_in[:, D_ATT:D_ATT + D_KV]
    wv = w_in[:, D_ATT + D_KV:D_ATT + 2 * D_KV]
    wu = w_in[:, D_ATT + 2 * D_KV:]
    dup = lambda w: jnp.concatenate([w[:, :64], w[:, :64], w[:, 64:], w[:, 64:]], axis=1)
    w_ext = jnp.concatenate([wq, dup(wk), dup(wv), wu], axis=1).astype(BF16)
    n_qk = D_ATT + 2 * D_KV
    qkg = jnp.concatenate([jnp.tile(q_g, ATT_HEADS), jnp.tile(k_g, 2 * ATT_KV_HEADS)])[None, :]
    hid = np.arange(n_qk) // HEAD_DIM
    ones = jnp.asarray((hid[:, None] == hid[None, :]).astype(np.float32), dtype=BF16)
    cos_t, sin_t = _rope_tables(S)
    sb = S // tm
    const = lambda i: (0, 0)
    return pl.pallas_call(
        _inproj_body,
        grid=(T // tm,),
        in_specs=[
            pl.BlockSpec((tm, D), lambda i: (i, 0)),
            pl.BlockSpec((1, D), const),
            pl.BlockSpec(w_ext.shape, const),
            pl.BlockSpec((1, n_qk), const),
            pl.BlockSpec((n_qk, n_qk), const),
            pl.BlockSpec((tm, LANES), lambda i: (i % sb, 0)),
            pl.BlockSpec((tm, LANES), lambda i: (i % sb, 0)),
        ],
        out_specs=[
            pl.BlockSpec((tm, D_ATT), lambda i: (i, 0)),
            pl.BlockSpec((tm, 2 * D_KV), lambda i: (i, 0)),
            pl.BlockSpec((tm, 2 * D_KV), lambda i: (i, 0)),
            pl.BlockSpec((d_ssm // LANES, tm // CHUNK, CHUNK * LANES), lambda i: (0, i, 0)),
        ],
        out_shape=[
            jax.ShapeDtypeStruct((T, D_ATT), BF16),
            jax.ShapeDtypeStruct((T, 2 * D_KV), BF16),
            jax.ShapeDtypeStruct((T, 2 * D_KV), BF16),
            jax.ShapeDtypeStruct((d_ssm // LANES, T // CHUNK, CHUNK * LANES), BF16),
        ],
        scratch_shapes=[pltpu.VMEM((d_ssm // LANES, tm, LANES), F32)],
        compiler_params=_cparams("arbitrary"),
        name="inproj",
    )(x2d, mix_g[None, :], w_ext, qkg, ones, cos_t, sin_t)


ATT_KEY_CHUNK = 256
ATT_ROW_BLOCK = 256


def _attn_body(q_ref, k_ref, v_ref, o_ref, klo, khi, vlo, vhi, s_scr, e_scr):
    @pl.when(pl.program_id(2) == 0)
    def _():
        lane = lax.broadcasted_iota(jnp.int32, k_ref.shape, 1)
        lo = lane < HEAD_DIM
        kd = k_ref[...]
        vd = v_ref[...]
        zero = jnp.zeros_like(kd)
        klo[...] = jnp.where(lo, kd, zero)
        khi[...] = jnp.where(lo, zero, kd)
        vlo[...] = jnp.where(lo, vd, (lane == HEAD_DIM).astype(BF16))
        vhi[...] = jnp.where(lo, (lane == 0).astype(BF16), vd)

    S = klo.shape[0]
    rb = s_scr.shape[1]
    units = [(r0, p, km, vm) for r0 in range(0, q_ref.shape[0], rb)
             for p in range(2) for km, vm in ((klo, vlo), (khi, vhi))]

    def scores(i):
        r0, p, km, _ = units[i]
        qp = q_ref[r0:r0 + rb, p * LANES:(p + 1) * LANES]
        m = None
        for c in range(0, S, ATT_KEY_CHUNK):
            s = _dot_nt(qp, km[c:c + ATT_KEY_CHUNK, :])
            s_scr[i % 2, :, c:c + ATT_KEY_CHUNK] = s
            for l0 in range(0, ATT_KEY_CHUNK, LANES):
                m = s[:, l0:l0 + LANES] if m is None else jnp.maximum(m, s[:, l0:l0 + LANES])
        return jnp.max(m, axis=-1, keepdims=True)

    def values(i, m):
        vm = units[i][3]
        for c in range(0, S, ATT_KEY_CHUNK):
            e_scr[i % 2, :, c:c + ATT_KEY_CHUNK] = jnp.exp(s_scr[i % 2, :, c:c + ATT_KEY_CHUNK] - m).astype(BF16)
        acc = _dot(e_scr[i % 2], vm[...])
        ones_lane = HEAD_DIM if vm is vlo else 0
        return acc * (1.0 / acc[:, ones_lane:ones_lane + 1])

    lo = lax.broadcasted_iota(jnp.int32, (rb, LANES), 1) < HEAD_DIM
    m_prev = scores(0)
    first = None
    for i, (r0, p, _, vm) in enumerate(units):
        m_next = scores(i + 1) if i + 1 < len(units) else None
        o = values(i, m_prev)
        m_prev = m_next
        if vm is vlo:
            first = o
        else:
            o_ref[r0:r0 + rb, p * LANES:(p + 1) * LANES] = jnp.where(lo, first, o).astype(BF16)


def _attention(q, kd, vd, B, S):
    tq = 4 * ATT_ROW_BLOCK
    q3 = q.reshape(B, S, D_ATT)
    k3 = kd.reshape(B, S, 2 * D_KV)
    v3 = vd.reshape(B, S, 2 * D_KV)
    gw = D_ATT // ATT_KV_HEADS
    out = pl.pallas_call(
        _attn_body,
        grid=(B, ATT_KV_HEADS, S // tq),
        in_specs=[
            pl.BlockSpec((None, tq, gw), lambda b, j, i: (b, i, j)),
            pl.BlockSpec((None, S, LANES), lambda b, j, i: (b, 0, j)),
            pl.BlockSpec((None, S, LANES), lambda b, j, i: (b, 0, j)),
        ],
        out_specs=pl.BlockSpec((None, tq, gw), lambda b, j, i: (b, i, j)),
        out_shape=jax.ShapeDtypeStruct((B, S, D_ATT), BF16),
        scratch_shapes=[pltpu.VMEM((S, LANES), BF16)] * 4
        + [pltpu.VMEM((2, ATT_ROW_BLOCK, S), F32), pltpu.VMEM((2, ATT_ROW_BLOCK, S), BF16)],
        compiler_params=_cparams("arbitrary", "arbitrary", "arbitrary"),
        name="attn",
    )(q3, k3, v3)
    return out.reshape(B * S, D_ATT)


def _dot_tn3(a, b):
    tn = lambda x, y: lax.dot_general(x, y, (((0,), (0,)), ((), ())), preferred_element_type=F32)
    ah, al = _split_bf16(a)
    bh, bl = _split_bf16(b)
    return tn(ah, bh) + tn(al, bh) + tn(ah, bl)


def _s5op_body(lam_ref, br_ref, bi_ref, cr_ref, ci_ref, rep_ref, m_ref, et_ref, da_ref, db_ref):
    W = m_ref.shape[1]
    L = CHUNK
    step = lax.broadcasted_iota(jnp.int32, (1, W), 1) // SSM_GROUP

    def along_steps(a, channel_axis):
        hi = a.astype(BF16)
        r1 = a - hi.astype(F32)
        mid = r1.astype(BF16)
        lo = (r1 - mid.astype(F32)).astype(BF16)
        dims = (((channel_axis,), (0,)), ((), ()))
        rep = lambda x: lax.dot_general(x, rep_ref[...], dims, preferred_element_type=F32)
        return rep(hi) + rep(mid) + rep(lo)

    def direction(d):
        lam = lam_ref[d]
        lr = jnp.minimum(lam[:, 0:1], -1e-4)
        li = lam[:, 1:2]
        dt = jnp.exp(lam[:, 2:3])
        mag = jnp.exp(lr * dt)
        ab_r = mag * jnp.cos(li * dt)
        ab_i = mag * jnp.sin(li * dt)
        den = lr * lr + li * li
        nr = ab_r - 1.0
        f_r = (nr * lr + ab_i * li) / den
        f_i = (ab_i * lr - nr * li) / den
        b_r, b_i = along_steps(br_ref[d], 1), along_steps(bi_ref[d], 1)
        bb_r = f_r * b_r - f_i * b_i
        bb_i = f_r * b_i + f_i * b_r
        cr, ci = along_steps(cr_ref[d], 0), along_steps(ci_ref[d], 0)

        def power(n):
            sq_r, sq_i = ab_r, ab_i
            pr = pi = None
            for k in range(CHUNK.bit_length() - 1):
                bit = ((n >> k) & 1) == 1
                tr = jnp.where(bit, sq_r, 1.0)
                ti = jnp.where(bit, sq_i, 0.0)
                pr, pi = (tr, ti) if pr is None else (pr * tr - pi * ti, pr * ti + pi * tr)
                sq_r, sq_i = sq_r * sq_r - sq_i * sq_i, 2.0 * sq_r * sq_i
            return pr, pi

        def times_lam(p):
            return p[0] * ab_r - p[1] * ab_i, p[0] * ab_i + p[1] * ab_r

        def a_b(p):
            return p[0] * bb_r - p[1] * bb_i, p[0] * bb_i + p[1] * bb_r

        def c_a(p):
            return cr * p[0] - ci * p[1], cr * p[1] + ci * p[0]

        up, down = power(step), power(L - 1 - step)
        return bb_r, bb_i, cr, ci, a_b, c_a, up, down, times_lam

    fbr, fbi, _, _, f_ab, f_ca, f_up, f_down, f_next = direction(0)
    bbr, bbi, bcr, bci, b_ab, b_ca, b_up, b_down, b_next = direction(1)
    etf_r, etf_i = f_ab(f_down)
    etb_r, etb_i = b_ab(b_up)
    et_ref[...] = jnp.concatenate([etf_r, etb_r, etf_i, etb_i], axis=0).astype(BF16)
    df_r, df_i = f_ca(f_next(f_up))
    db_r, db_i = b_ca(b_next(b_down))
    zero = jnp.zeros_like(df_r)
    da_ref[...] = jnp.concatenate([df_r, zero, -df_i, zero], axis=0).astype(BF16)
    db_ref[...] = jnp.concatenate([zero, db_r, zero, -db_i], axis=0).astype(BF16)
    kf_r, kf_i = f_ca(f_up)
    lag0 = step == 0
    lhs_f = jnp.concatenate([fbr, -fbi], axis=0)
    lhs_b = jnp.concatenate([bbr, -bbi], axis=0)
    q_b = _dot_tn3(lhs_b, jnp.concatenate([db_r, db_i], axis=0))
    q_f = _dot_tn3(jnp.concatenate([lhs_f, lhs_b], axis=0),
                   jnp.concatenate([kf_r, kf_i, jnp.where(lag0, bcr, 0.0), jnp.where(lag0, bci, 0.0)],
                                   axis=0))
    z = jnp.concatenate([q_b, q_f], axis=1)
    for j in range(L):
        rows = z[j * SSM_GROUP:(j + 1) * SSM_GROUP, :]
        m_ref[j * SSM_GROUP:(j + 1) * SSM_GROUP, :] = pltpu.roll(
            rows, (2 * W - (L - j) * SSM_GROUP) % (2 * W), 1)[:, :W].astype(BF16)


def _s5_operators(lam_re, lam_im, log_step, b_re, b_im, c_re, c_im):
    _, G, P = lam_re.shape
    W = CHUNK * SSM_GROUP
    lam = jnp.stack([lam_re, lam_im, jnp.broadcast_to(log_step[..., None], lam_re.shape)], axis=-1).astype(F32)
    rep = jnp.asarray(np.arange(SSM_GROUP)[:, None] == np.arange(W)[None, :] % SSM_GROUP, dtype=BF16)
    per_g = lambda a: pl.BlockSpec((2, None) + a.shape[2:], lambda g: (0, g, 0, 0))
    out = pl.BlockSpec((None, W, W), lambda g: (g, 0, 0))
    args = (lam, b_re.astype(F32), b_im.astype(F32), c_re.astype(F32), c_im.astype(F32))
    return pl.pallas_call(
        _s5op_body,
        grid=(G,),
        in_specs=[per_g(a) for a in args] + [pl.BlockSpec(rep.shape, lambda g: (0, 0))],
        out_specs=[out] * 4,
        out_shape=[jax.ShapeDtypeStruct((G, W, W), BF16)] * 4,
        compiler_params=_cparams("arbitrary"),
        name="s5op",
    )(*args, rep)


GROUPS_PER_SET = LANES // SSM_GROUP
S5_SCAN_GROUPS = 4


def _transpose8(blocks, axis, piece):
    n = blocks[0].shape[axis]
    q = lax.broadcasted_iota(jnp.int32, blocks[0].shape, axis) // piece
    v = list(blocks)
    for d in (4, 2, 1):
        keep = (q & d) == 0
        nxt = list(v)
        for a in range(8):
            if a & d == 0:
                b = a + d
                nxt[a] = jnp.where(keep, v[a], pltpu.roll(v[b], d * piece, axis))
                nxt[b] = jnp.where(keep, pltpu.roll(v[a], n - d * piece, axis), v[b])
        v = nxt
    return v


def _piece_transpose(blocks):
    return _transpose8(blocks, 1, SSM_GROUP)


def _s5_body(u_ref, m_ref, et_ref, da_ref, db_ref, a_ref, d_ref, z_ref,
             xr_n, xi_n, xr_t, xi_t, har_s, hai_s, hbr_s, hbi_s, *, nb):
    R = u_ref.shape[0]
    nchunk = R // nb
    W = m_ref.shape[1]
    half = W // 2
    group = lambda g: slice(g * W, (g + 1) * W)
    fwd = lax.broadcasted_iota(jnp.int32, (nb, half), 1) < SSM_STATE

    def swap_row_order(src_a, src_b, dst_a, dst_b, to_chunk_major):
        def blocks(cb, _):
            batch_major = [pl.ds(pl.multiple_of(p * nchunk + cb * nb, nb), nb) for p in range(nb)]
            chunk_major = [pl.ds(pl.multiple_of((cb * nb + j) * nb, nb), nb) for j in range(nb)]
            rows_in, rows_out = (batch_major, chunk_major) if to_chunk_major else (chunk_major, batch_major)
            for k in range(S5_SCAN_GROUPS):
                for src, dst in ((src_a, dst_a), (src_b, dst_b)):
                    out = _transpose8([src[k, r, :] for r in rows_in], 0, 1)
                    for j in range(nb):
                        dst[k, rows_out[j], :] = out[j]
            return 0
        lax.fori_loop(0, nchunk // nb, blocks, 0)

    for g0 in range(0, GROUPS_PER_SET, S5_SCAN_GROUPS):
        coef = []
        for k in range(S5_SCAN_GROUPS):
            g = g0 + k
            x = _dot_nt(u_ref[:, group(g)], et_ref[g])
            xr_n[k] = x[:, :half]
            xi_n[k] = x[:, half:]
            lr = jnp.minimum(a_ref[g, 0:1, :], -1e-4)
            dt = jnp.exp(a_ref[g, 2:3, :])
            mag = jnp.exp(lr * dt * CHUNK)
            ang = a_ref[g, 1:2, :] * dt * CHUNK
            coef.append((jnp.broadcast_to(mag * jnp.cos(ang), (nb, half)),
                         jnp.broadcast_to(mag * jnp.sin(ang), (nb, half))))
        swap_row_order(xr_n, xi_n, xr_t, xi_t, True)

        def step(c, carry):
            rf = pl.ds(pl.multiple_of(c * nb, nb), nb)
            rb = pl.ds(pl.multiple_of((nchunk - 1 - c) * nb, nb), nb)
            new = []
            for k in range(S5_SCAN_GROUPS):
                sr, si = carry[2 * k], carry[2 * k + 1]
                ar, ai = coef[k]
                har_s[k, rf, :] = sr
                hai_s[k, rf, :] = si
                hbr_s[k, rb, :] = sr
                hbi_s[k, rb, :] = si
                xr = jnp.where(fwd, xr_t[k, rf, :], xr_t[k, rb, :])
                xi = jnp.where(fwd, xi_t[k, rf, :], xi_t[k, rb, :])
                new += [sr * ar - si * ai + xr, sr * ai + si * ar + xi]
            return tuple(new)

        zero = jnp.zeros((nb, half), F32)
        lax.fori_loop(0, nchunk, step, (zero,) * (2 * S5_SCAN_GROUPS))
        for k in range(S5_SCAN_GROUPS):
            g = g0 + k
            ha = jnp.concatenate([har_s[k], hai_s[k]], axis=1).astype(BF16)
            hb = jnp.concatenate([hbr_s[k], hbi_s[k]], axis=1).astype(BF16)
            y_t = _dot(ha, da_ref[g]) + _dot(hb, db_ref[g])
            xr_n[k] = y_t[:, :half]
            xi_n[k] = y_t[:, half:]
        swap_row_order(xr_n, xi_n, xr_t, xi_t, False)
        for k in range(S5_SCAN_GROUPS):
            g = g0 + k
            u = u_ref[:, group(g)]
            y = _dot(u, m_ref[g]) + jnp.concatenate([xr_t[k], xi_t[k]], axis=1) + d_ref[g] * u.astype(F32)
            z_ref[:, group(g)] = jax.nn.gelu(y).astype(BF16)


def _s5(u_set, B, S, lam_re, lam_im, log_step, b_re, b_im, c_re, c_im, ssm_d):
    n_set, R, WS = u_set.shape
    G = n_set * GROUPS_PER_SET
    W = CHUNK * SSM_GROUP
    assert R == (S // CHUNK) * B and WS == CHUNK * LANES
    M, ET, DA, DB = _s5_operators(lam_re, lam_im, log_step, b_re, b_im, c_re, c_im)
    both = lambda a: jnp.concatenate([a[0], a[1]], axis=-1)
    lam = jnp.stack([both(lam_re), both(lam_im),
                     both(jnp.broadcast_to(log_step[..., None], lam_re.shape))], axis=1).astype(F32)
    dvec = jnp.tile(ssm_d.astype(F32).reshape(G, 1, SSM_GROUP), (1, CHUNK, 1)).reshape(G, 1, W)
    per_set = pl.BlockSpec((None, R, WS), lambda s: (s, 0, 0))
    blk = lambda *sh: pl.BlockSpec((GROUPS_PER_SET,) + sh, lambda s: (s,) + (0,) * len(sh))
    return pl.pallas_call(
        functools.partial(_s5_body, nb=B),
        grid=(n_set,),
        in_specs=[per_set, blk(W, W), blk(W, W), blk(W, W), blk(W, W), blk(3, W // 2), blk(1, W)],
        out_specs=per_set,
        out_shape=jax.ShapeDtypeStruct((n_set, R, WS), BF16),
        scratch_shapes=[pltpu.VMEM((S5_SCAN_GROUPS, R, W // 2), F32)] * 8,
        compiler_params=_cparams("arbitrary"),
        name="s5",
    )(u_set, M, ET, DA, DB, lam, dvec)


def _memkv_body(m_ref, g_ref, wk_ref, wv_ref, k_ref, v_ref):
    h = (_rms(m_ref[...]) * g_ref[...]).astype(BF16)
    k_ref[...] = _dot(h, wk_ref[...]).astype(BF16)
    v_ref[...] = _dot(h, wv_ref[...]).astype(BF16)


def _memkv(mem, g, wk, wv):
    B, M, D = mem.shape
    const = lambda b: (0, 0)
    row = pl.BlockSpec((None, M, D), lambda b: (b, 0, 0))
    return pl.pallas_call(
        _memkv_body,
        grid=(B,),
        in_specs=[row, pl.BlockSpec((1, D), const), pl.BlockSpec((D, D), const),
                  pl.BlockSpec((D, D), const)],
        out_specs=[row, row],
        out_shape=[jax.ShapeDtypeStruct((B, M, D), BF16)] * 2,
        compiler_params=_cparams("arbitrary"),
        name="memkv",
    )(mem, g[None, :], wk.astype(BF16), wv.astype(BF16))


MID_ROW_BLOCK = 512


def _mid_body(x_ref, ya_ref, z_ref, gw_ref, gb_ref, ag_ref, sg_ref, wo_ref, xg_ref,
              wq_ref, xwo_ref, k_ref, v_ref, o_ref, zs):
    D = x_ref.shape[1]
    dh = D // XA_HEADS
    n_set = z_ref.shape[0]
    crows = MID_ROW_BLOCK // CHUNK

    def glu(r0, st):
        c0 = r0 // CHUNK
        W = CHUNK * SSM_GROUP
        for j in range(n_set):
            for h in range(W // LANES):
                steps = _piece_transpose([z_ref[j, c0:c0 + crows, g * W + h * LANES:g * W + (h + 1) * LANES].astype(F32)
                                          for g in range(GROUPS_PER_SET)])
                for k in range(GROUPS_PER_SET):
                    zs[j, pl.ds(r0 + h * GROUPS_PER_SET + k, crows, stride=CHUNK), :] = steps[k]
        z = jnp.concatenate([zs[j, r0:r0 + MID_ROW_BLOCK, :] for j in range(n_set)], axis=1).astype(BF16)
        st['s'] = z.astype(F32) * jax.nn.sigmoid(_dot(z, gw_ref[...]) + gb_ref[...])

    def out_proj(r0, st):
        rows = slice(r0, r0 + MID_ROW_BLOCK)
        ya = _rms(ya_ref[rows, :].astype(F32)) * ag_ref[...]
        ys = _rms(st.pop('s')) * sg_ref[...]
        y = jnp.concatenate([ya, ys], axis=1).astype(BF16)
        st['x1'] = x_ref[rows, :] + _dot(y, wo_ref[...])

    def query(r0, st):
        hq = (_rms(st['x1']) * xg_ref[...]).astype(BF16)
        st['q'] = (_dot(hq, wq_ref[...]) * (dh ** -0.5)).astype(BF16)

    def attend(r0, st):
        q = st.pop('q')
        outs = []
        for h in range(XA_HEADS):
            sl = slice(h * dh, (h + 1) * dh)
            sc = _dot_nt(q[:, sl], k_ref[:, sl])
            e = jnp.exp(sc - jnp.max(sc, axis=-1, keepdims=True))
            inv = 1.0 / jnp.sum(e, axis=-1, keepdims=True)
            outs.append((_dot(e.astype(BF16), v_ref[:, sl]) * inv).astype(BF16))
        st['o'] = jnp.concatenate(outs, axis=1)

    def finish(r0, st):
        rows = slice(r0, r0 + MID_ROW_BLOCK)
        o_ref[rows, :D] = st.pop('x1') + _dot(st.pop('o'), xwo_ref[...])
        o_ref[rows, D:] = jnp.zeros((MID_ROW_BLOCK, o_ref.shape[1] - D), F32)

    blocks = [(r0, {}) for r0 in range(0, x_ref.shape[0], MID_ROW_BLOCK)]
    for stage in (glu, out_proj, query, attend, finish):
        for r0, st in blocks:
            stage(r0, st)


def _mid(x2d, S, y_att, z, glu_w, glu_b, att_g, ssm_g, w_out, xa_g, wq, wo, kmem, vmem):
    T, D = x2d.shape
    M = kmem.shape[1]
    tm = 2 * MID_ROW_BLOCK
    sb = S // tm
    const = lambda i: (0, 0)
    tok = lambda w: pl.BlockSpec((tm, w), lambda i: (i, 0))
    full = lambda a: pl.BlockSpec(a.shape, const)
    vec = lambda a: a.astype(F32)[None, :]
    mem = pl.BlockSpec((None, M, D), lambda i: (i // sb, 0, 0))
    args = [x2d, y_att, z, glu_w.astype(BF16), vec(glu_b), vec(att_g), vec(ssm_g), w_out.astype(BF16),
            vec(xa_g), wq.astype(BF16), wo.astype(BF16), kmem, vmem]
    zspec = pl.BlockSpec((z.shape[0], tm // CHUNK, CHUNK * LANES), lambda i: (0, i, 0))
    specs = [tok(D), tok(y_att.shape[1]), zspec] + [full(a) for a in args[3:11]] + [mem, mem]
    return pl.pallas_call(
        _mid_body,
        grid=(T // tm,),
        in_specs=specs,
        out_specs=pl.BlockSpec((tm, D + LANES), lambda i: (i, 0)),
        out_shape=jax.ShapeDtypeStruct((T, D + LANES), F32),
        scratch_shapes=[pltpu.VMEM((z.shape[0], tm, LANES), F32)],
        compiler_params=_cparams("arbitrary"),
        name="mid",
    )(*args)


def _split_bf16(a):
    hi = a.astype(BF16)
    return hi, (a - hi.astype(F32)).astype(BF16)


ROUTE_ROWS = 32
PAIR_SLOTS = ((0, 1), (2, 1), (2, 0), (3, 0), (3, 1), (3, 2))
MOE_TILE = 256


def _route_body(x_ref, g_ref, whi_ref, wlo_ref, b_ref, tri_ref, meta_ref, mt_ref, cnt_ref, carry):
    i = pl.program_id(0)
    tm, D = x_ref.shape[0], g_ref.shape[1]

    @pl.when(i == 0)
    def _():
        carry[...] = jnp.zeros_like(carry)

    h = _rms(x_ref[:, :D]) * g_ref[...]
    hhi, hlo = _split_bf16(h)
    logits = (_dot_nt(whi_ref[...], hhi) + _dot_nt(whi_ref[...], hlo) + _dot_nt(wlo_ref[...], hhi)
              + b_ref[:, 0:1])
    ninf = jnp.float32(-jnp.inf)
    cmax = lambda v: jnp.max(v, axis=0, keepdims=True)
    cmin = lambda v: jnp.min(v, axis=0, keepdims=True)
    gl = logits[N_EXPERTS:N_EXPERTS + N_EXPERT_GROUPS]
    grow = lax.broadcasted_iota(jnp.int32, gl.shape, 0)
    gmax = cmax(gl)
    g_sel = cmin(jnp.where(gl == gmax, grow, N_EXPERT_GROUPS))
    g_w = 1.0 / jnp.sum(jnp.exp(gl - gmax), axis=0, keepdims=True)
    el = logits[:N_EXPERTS]
    erow = lax.broadcasted_iota(jnp.int32, el.shape, 0)
    el = jnp.where((erow >> 2) == g_sel, el, ninf)
    v1 = cmax(el)
    i1 = cmin(jnp.where(el == v1, erow, N_EXPERTS))
    el2 = jnp.where(erow == i1, ninf, el)
    v2 = cmax(el2)
    i2 = cmin(jnp.where(el2 == v2, erow, N_EXPERTS))
    t = jnp.exp(v2 - v1)
    w1 = g_w / (1.0 + t)
    w2 = g_w * t / (1.0 + t)
    first_low = i1 < i2
    ia = jnp.minimum(i1, i2) & 3
    ib = jnp.maximum(i1, i2) & 3
    wa = jnp.where(first_low, w1, w2)
    wb = jnp.where(first_low, w2, w1)
    pair = ((ia * (7 - ia)) >> 1) + ib - ia - 1
    order = jnp.where(pair == 3, 1, jnp.where(pair == 1, 2, jnp.where(pair == 2, 3, pair)))
    cls = g_sel * PAIRS_PER_GROUP + order
    crow = lax.broadcasted_iota(jnp.int32, (ROUTE_ROWS, tm), 0)
    onehot = (crow == cls).astype(F32)
    before = _dot(onehot.astype(BF16), tri_ref[...])
    rank = jnp.sum(onehot * (before + carry[:, 0:1]), axis=0, keepdims=True)
    carry[...] = carry[...] + jnp.sum(onehot, axis=1, keepdims=True)
    cnt_ref[...] = carry[...]

    def record(rows):
        r = lax.broadcasted_iota(jnp.int32, (rows, tm), 0)
        return jnp.where(r == META_CLS, cls.astype(F32),
                         jnp.where(r == META_RANK, rank,
                                   jnp.where(r == META_WA, wa, jnp.where(r == META_WB, wb, 0.0))))

    meta_ref[...] = record(LANES).T
    mt_ref[:, pl.ds(pl.multiple_of(i * tm, tm), tm)] = record(mt_ref.shape[0])

    @pl.when(i == pl.num_programs(0) - 1)
    def _():
        ntile = jnp.floor((carry[...] + (MOE_TILE - 1.0)) * (1.0 / MOE_TILE))
        r = lax.broadcasted_iota(jnp.int32, (ROUTE_ROWS, ROUTE_ROWS), 0)
        c = lax.broadcasted_iota(jnp.int32, (ROUTE_ROWS, ROUTE_ROWS), 1)
        off = _dot((c < r).astype(BF16), ntile.astype(BF16)) * MOE_TILE
        cls_all = mt_ref[META_CLS:META_CLS + 1, :]
        pos = mt_ref[META_RANK:META_RANK + 1, :]
        for k in range(N_CLASSES):
            pos = pos + jnp.where(cls_all == k, off[k:k + 1, 0:1], 0.0)
        mt_ref[META_POS:META_POS + 1, :] = pos


def _route(x2a, D, moe_g, grp_w, grp_b, exp_w, exp_b):
    T = x2a.shape[0]
    tm = 1024
    pad = ROUTE_ROWS - N_EXPERT_GROUPS - N_EXPERTS
    w = jnp.concatenate([exp_w, grp_w, jnp.zeros((D, pad), F32)], axis=1).astype(F32).T
    b = jnp.concatenate([exp_b, grp_b, jnp.zeros((pad,), F32)]).astype(F32)
    whi, wlo = _split_bf16(w)
    tri = jnp.asarray(np.arange(tm)[:, None] < np.arange(tm)[None, :], dtype=BF16)
    const = lambda i: (0, 0)
    ncol = D // LANES
    return pl.pallas_call(
        _route_body,
        grid=(T // tm,),
        in_specs=[pl.BlockSpec((tm, D + LANES), lambda i: (i, 0)),
                  pl.BlockSpec((1, D), const), pl.BlockSpec((ROUTE_ROWS, D), const),
                  pl.BlockSpec((ROUTE_ROWS, D), const), pl.BlockSpec((ROUTE_ROWS, LANES), const),
                  pl.BlockSpec((tm, tm), const)],
        out_specs=[pl.BlockSpec((tm, LANES), lambda i: (i, ncol)),
                   pl.BlockSpec((SUBLANES, T), const),
                   pl.BlockSpec((ROUTE_ROWS, LANES), const)],
        out_shape=[jax.ShapeDtypeStruct(x2a.shape, F32), jax.ShapeDtypeStruct((SUBLANES, T), F32),
                   jax.ShapeDtypeStruct((ROUTE_ROWS, LANES), F32)],
        scratch_shapes=[pltpu.VMEM((ROUTE_ROWS, LANES), F32)],
        input_output_aliases={0: 0},
        compiler_params=_cparams("arbitrary"),
        name="route",
    )(x2a, moe_g.astype(F32)[None, :], whi, wlo, jnp.broadcast_to(b[:, None], (ROUTE_ROWS, LANES)), tri)


ROW_COPY_UNROLL = 8


def _zero_fill(fill_ref, dst_ref, zbuf, sem):
    tile_groups = MOE_TILE // SUBLANES
    zbuf[...] = jnp.zeros_like(zbuf)
    pieces = []
    for c in range(N_CLASSES):
        s, n = fill_ref[2 * c], fill_ref[2 * c + 1]
        lead = jnp.minimum(n, (-s) & (SUBLANES - 1))
        for j in range(SUBLANES - 1):
            r = s + j
            dst = dst_ref.at[lax.shift_right_logical(r, 3), pl.ds(r & (SUBLANES - 1), 1)]
            pieces.append((j < lead, pltpu.make_async_copy(zbuf.at[0, pl.ds(0, 1)], dst, sem)))
        g0 = lax.shift_right_logical(s + lead, 3)
        groups = lax.shift_right_logical(n - lead, 3)
        k = tile_groups // 2
        while k >= 1:
            pieces.append(((groups & k) != 0,
                           pltpu.make_async_copy(zbuf.at[pl.ds(0, k)], dst_ref.at[pl.ds(g0, k)], sem)))
            g0 = g0 + (groups & k)
            k //= 2
    first_unused = fill_ref[2 * N_CLASSES]
    for t in range(N_CLASSES):
        tile = first_unused + t
        pieces.append((tile < dst_ref.shape[0] // tile_groups,
                       pltpu.make_async_copy(zbuf, dst_ref.at[pl.ds(tile * tile_groups, tile_groups)], sem)))
    for cond, cp in pieces:
        pl.when(cond)(cp.start)
    for cond, cp in pieces:
        pl.when(cond)(cp.wait)


def _row_copy_body(*refs, rows, scatter):
    if scatter:
        fill_ref, pos_ref, src_ref, dst_ref, zbuf, sem = refs

        @pl.when(pl.program_id(0) == 0)
        def _():
            _zero_fill(fill_ref, dst_ref, zbuf, sem)
    else:
        pos_ref, src_ref, dst_ref, sem = refs

    def copy(j, k):
        p = pos_ref[j * ROW_COPY_UNROLL + k]
        tok = (j, pl.ds(k, 1))
        slot = (lax.shift_right_logical(p, 3), pl.ds(p & (SUBLANES - 1), 1))
        if scatter:
            return pltpu.make_async_copy(src_ref.at[tok], dst_ref.at[slot], sem)
        return pltpu.make_async_copy(src_ref.at[slot], dst_ref.at[tok], sem)

    def start(j, _):
        for k in range(ROW_COPY_UNROLL):
            copy(j, k).start(priority=k % 2)
        return 0

    def wait(j, _):
        for k in range(ROW_COPY_UNROLL):
            copy(j, k).wait()
        return 0

    lax.fori_loop(0, rows // ROW_COPY_UNROLL, start, 0)
    lax.fori_loop(0, rows // ROW_COPY_UNROLL, wait, 0)


def _row_copy(pos, src, *, dst_rows, fill=None, name):
    assert ROW_COPY_UNROLL == SUBLANES
    T = pos.shape[0]
    rows = 512
    width = src.shape[1]
    scatter = fill is not None
    hbm = pl.BlockSpec(memory_space=pl.ANY)
    tok = pl.BlockSpec((rows // SUBLANES, SUBLANES, width), lambda i, *_: (i, 0, 0))
    idx = pl.BlockSpec((rows,), lambda i, *_: (i,), memory_space=pltpu.SMEM)
    scratch = [pltpu.SemaphoreType.DMA(())]
    if scatter:
        scratch = [pltpu.VMEM((MOE_TILE // SUBLANES, SUBLANES, width), src.dtype)] + scratch
    grid_spec = pltpu.PrefetchScalarGridSpec(
        num_scalar_prefetch=1 if scatter else 0,
        grid=(T // rows,),
        in_specs=[idx, tok if scatter else hbm],
        out_specs=hbm if scatter else tok,
        scratch_shapes=scratch,
    )
    out = pl.pallas_call(
        functools.partial(_row_copy_body, rows=rows, scatter=scatter),
        grid_spec=grid_spec,
        out_shape=jax.ShapeDtypeStruct((dst_rows // SUBLANES, SUBLANES, width), src.dtype),
        compiler_params=pltpu.CompilerParams(dimension_semantics=("arbitrary",)),
        name=name,
    )(*(((fill,) if scatter else ()) + (pos, src.reshape(src.shape[0] // SUBLANES, SUBLANES, width))))
    return out.reshape(dst_rows, width)


def _ffn_body(ea_ref, eb_ref, na_ref, nb_ref, nt_ref, xs_ref, g_ref, fg_ref, wg_hbm, wu_hbm, wd_hbm, o_ref,
              sga, sua, sda, sgb, sub, sdb, ga, ua, da, gb, ub, db, sem):
    D = g_ref.shape[1]
    i = pl.program_id(0)
    valid = i < nt_ref[0]
    prev = jnp.maximum(i - 1, 0)
    slots = ((0, ea_ref, na_ref, (sga, sua, sda), (ga, ua, da)), (1, eb_ref, nb_ref, (sgb, sub, sdb), (gb, ub, db)))

    def fetch(e, stage, s):
        return [pltpu.make_async_copy(w.at[e], buf, sem.at[s])
                for w, buf in zip((wg_hbm, wu_hbm, wd_hbm), stage)]

    for s, e_ref, n_ref, stage, work in slots:
        @pl.when(i == 0)
        def _(s=s, e_ref=e_ref, stage=stage):
            for c in fetch(e_ref[0], stage, s):
                c.start()

        @pl.when(valid & ((i == 0) | (e_ref[i] != e_ref[prev])))
        def _(s=s, e_ref=e_ref, n_ref=n_ref, stage=stage, work=work):
            for c in fetch(e_ref[i], stage, s):
                c.wait()
            for src, dst in zip(stage, work):
                dst[...] = src[...].astype(BF16)

            @pl.when(n_ref[i] >= 0)
            def _():
                for c in fetch(n_ref[i], stage, s):
                    c.start()

    @pl.when(valid)
    def _():
        x = xs_ref[:, :D]
        meta = xs_ref[:, D:]
        lower_in_a = ea_ref[i] < eb_ref[i]
        g_lo, g_hi = meta[:, META_WA:META_WA + 1], meta[:, META_WB:META_WB + 1]
        gates = (jnp.where(lower_in_a, g_lo, g_hi), jnp.where(lower_in_a, g_hi, g_lo))
        h = (_rms(x) * g_ref[...]).astype(BF16)
        y = x
        for (wg, wu, wd), gate in zip(((ga, ua, da), (gb, ub, db)), gates):
            a = jax.nn.silu(_dot(h, wg[...])) * _dot(h, wu[...]) * gate
            y = y + _dot(a.astype(BF16), wd[...])
        o_ref[...] = _rms(y) * fg_ref[...]

    @pl.when(jnp.logical_not(valid))
    def _():
        o_ref[...] = jnp.zeros_like(o_ref)


def _ffn(xs, D, tile_ea, tile_eb, n_tiles, moe_g, final_g, w_gate, w_up, w_down):
    tm = MOE_TILE
    P = xs.shape[0]
    nt_max = P // tm
    De = w_gate.shape[2]

    def next_expert(e):
        j = jnp.arange(nt_max, dtype=jnp.int32)
        later = (j[None, :] > j[:, None]) & (j[None, :] < n_tiles[0]) & (e[None, :] != e[:, None])
        first = jnp.min(jnp.where(later, j[None, :], nt_max), axis=1)
        return jnp.where(first < nt_max, e[jnp.minimum(first, nt_max - 1)], -1).astype(jnp.int32)

    const = lambda i, ea, eb, na, nb, nt: (0, 0)
    hbm = pl.BlockSpec(memory_space=pl.ANY)
    expert = lambda dt: [pltpu.VMEM((D, De), dt), pltpu.VMEM((D, De), dt), pltpu.VMEM((De, D), dt)]
    grid_spec = pltpu.PrefetchScalarGridSpec(
        num_scalar_prefetch=5,
        grid=(nt_max,),
        in_specs=[pl.BlockSpec((tm, D + LANES),
                               lambda i, ea, eb, na, nb, nt: (jnp.maximum(jnp.minimum(i, nt[0] - 1), 0), 0)),
                  pl.BlockSpec((1, D), const), pl.BlockSpec((1, D), const), hbm, hbm, hbm],
        out_specs=pl.BlockSpec((tm, D), lambda i, ea, eb, na, nb, nt: (i, 0)),
        scratch_shapes=expert(F32) * 2 + expert(BF16) * 2 + [pltpu.SemaphoreType.DMA((2,))],
    )
    return pl.pallas_call(
        _ffn_body,
        grid_spec=grid_spec,
        out_shape=jax.ShapeDtypeStruct((P, D), F32),
        compiler_params=_cparams("arbitrary"),
        name="ffn",
    )(tile_ea, tile_eb, next_expert(tile_ea), next_expert(tile_eb), n_tiles, xs,
      moe_g.astype(F32)[None, :], final_g.astype(F32)[None, :],
      w_gate.astype(F32), w_up.astype(F32), w_down.astype(F32))


def _moe(x2a, D, moe_g, grp_w, grp_b, exp_w, exp_b, w_gate, w_up, w_down, final_g):
    T = x2a.shape[0]
    tm = MOE_TILE
    x2a, record, counts = _route(x2a, D, moe_g, grp_w, grp_b, exp_w, exp_b)
    pos = record[META_POS].astype(jnp.int32)
    cnt = counts[:N_CLASSES, 0].astype(jnp.int32)
    ntile_c = (cnt + tm - 1) // tm
    tile_end = jnp.cumsum(ntile_c)
    nt_max = T // tm + N_CLASSES
    n_tiles = tile_end[-1:]
    tile_id = jnp.arange(nt_max, dtype=jnp.int32)
    tile_cls = jnp.minimum(jnp.sum((tile_end[None, :] <= tile_id[:, None]).astype(jnp.int32), axis=1),
                           N_CLASSES - 1)
    tile_cls = jnp.where(tile_id < n_tiles[0], tile_cls, tile_cls[jnp.maximum(n_tiles[0] - 1, 0)])
    slot_a = jnp.asarray([a for a, _ in PAIR_SLOTS], jnp.int32)
    slot_b = jnp.asarray([b for _, b in PAIR_SLOTS], jnp.int32)
    grp = tile_cls // PAIRS_PER_GROUP
    tile_ea = grp * EXPERTS_PER_GROUP + slot_a[tile_cls % PAIRS_PER_GROUP]
    tile_eb = grp * EXPERTS_PER_GROUP + slot_b[tile_cls % PAIRS_PER_GROUP]
    seg_start = (tile_end - ntile_c) * tm
    fill = jnp.concatenate([jnp.stack([seg_start + cnt, ntile_c * tm - cnt], axis=1).reshape(-1), n_tiles])
    xs = _row_copy(pos, x2a, dst_rows=nt_max * tm, fill=fill.astype(jnp.int32), name="dispatch")
    ys = _ffn(xs, D, tile_ea, tile_eb, n_tiles.astype(jnp.int32), moe_g, final_g, w_gate, w_up, w_down)
    return _row_copy(pos, ys, dst_rows=T, name="unsort")


def kernel(x, mem, mix_norm_g, w_in, q_norm_g, k_norm_g, ssm_lambda_re, ssm_lambda_im, ssm_log_step, ssm_b_re, ssm_b_im, ssm_c_re, ssm_c_im, ssm_d, ssm_glu_w, ssm_glu_b, att_out_g, ssm_out_g, w_out, xa_norm_g, mem_norm_g, xa_wq, xa_wk, xa_wv, xa_wo, moe_norm_g, router_grp_w, router_grp_b, router_exp_w, router_exp_b, exp_w_gate, exp_w_up, exp_w_down, final_norm_g):
    B, S, D = x.shape
    assert mix_norm_g.shape[0] == 1, "single layer"
    x2d = x.reshape(B * S, D)
    q, kd, vd, u = _inproj(x2d, S, mix_norm_g[0], w_in[0], q_norm_g[0], k_norm_g[0])
    y_att = _attention(q, kd, vd, B, S)
    z = _s5(u, B, S, ssm_lambda_re[0], ssm_lambda_im[0], ssm_log_step[0], ssm_b_re[0], ssm_b_im[0],
            ssm_c_re[0], ssm_c_im[0], ssm_d[0])
    kmem, vmem = _memkv(mem, mem_norm_g[0], xa_wk[0], xa_wv[0])
    x2a = _mid(x2d, S, y_att, z, ssm_glu_w[0], ssm_glu_b[0], att_out_g[0], ssm_out_g[0], w_out[0],
               xa_norm_g[0], xa_wq[0], xa_wo[0], kmem, vmem)
    out = _moe(x2a, D, moe_norm_g[0], router_grp_w[0], router_grp_b[0], router_exp_w[0],
               router_exp_b[0], exp_w_gate[0], exp_w_up[0], exp_w_down[0], final_norm_g)
    return out.reshape(B, S, D)
```

```python
import functools

import numpy as np
import jax
import jax.numpy as jnp
from jax import lax
from jax.experimental import pallas as pl
from jax.experimental.pallas import tpu as pltpu

F32 = jnp.float32
BF16 = jnp.bfloat16

EPS = 1e-6
GRID_W = 64
ROPE_THETA = 10000.0
ATT_HEADS = 8
ATT_KV_HEADS = 2
HEAD_DIM = 64
D_ATT = ATT_HEADS * HEAD_DIM
D_KV = ATT_KV_HEADS * HEAD_DIM
SSM_GROUP = 16
SSM_STATE = 64
XA_HEADS = 4
N_EXPERT_GROUPS = 4
EXPERTS_PER_GROUP = 4
N_EXPERTS = N_EXPERT_GROUPS * EXPERTS_PER_GROUP
PAIRS_PER_GROUP = 6
N_CLASSES = N_EXPERT_GROUPS * PAIRS_PER_GROUP

LANES = 128
SUBLANES = 8
CHUNK = 16
VMEM_LIMIT = 56 * 1024 * 1024

META_CLS, META_RANK, META_WA, META_WB, META_POS = 0, 1, 2, 3, 4


def _cparams(*sem):
    return pltpu.CompilerParams(dimension_semantics=sem, vmem_limit_bytes=VMEM_LIMIT)


def _rms(x):
    return x * lax.rsqrt(jnp.mean(x * x, axis=-1, keepdims=True) + EPS)


def _dot(a, b):
    return jnp.dot(a, b, preferred_element_type=F32)


def _dot_nt(a, b):
    return lax.dot_general(a, b, (((1,), (1,)), ((), ())), preferred_element_type=F32)


def _inproj_body(x_ref, g_ref, w_ref, qkg_ref, ones_ref, cos_ref, sin_ref,
                 q_ref, k_ref, v_ref, u_ref, us):
    n_qk = q_ref.shape[1] + k_ref.shape[1]
    h = _rms(x_ref[...]) * g_ref[...]
    proj = _dot(h.astype(BF16), w_ref[...])
    qk = proj[:, :n_qk]
    ssq = _dot((qk * qk).astype(BF16), ones_ref[...])
    qk = qk * lax.rsqrt(ssq * (1.0 / HEAD_DIM) + EPS) * qkg_ref[...]
    cos = cos_ref[...]
    sin = sin_ref[...]
    lane = lax.broadcasted_iota(jnp.int32, cos.shape, 1)
    first = (lane % 32) < 16
    nq = q_ref.shape[1] // LANES
    for c in range(n_qk // LANES):
        xc = qk[:, c * LANES:(c + 1) * LANES]
        partner = jnp.where(first, pltpu.roll(xc, LANES - 16, 1), pltpu.roll(xc, 16, 1))
        rot = xc * cos + partner * sin
        if c < nq:
            q_ref[:, c * LANES:(c + 1) * LANES] = (rot * (HEAD_DIM ** -0.5)).astype(BF16)
        else:
            k_ref[:, (c - nq) * LANES:(c - nq + 1) * LANES] = rot.astype(BF16)
    nv = n_qk + v_ref.shape[1]
    v_ref[...] = proj[:, n_qk:nv].astype(BF16)
    rows = u_ref.shape[1]
    W = CHUNK * SSM_GROUP
    for j in range(u_ref.shape[0]):
        us[j] = proj[:, nv + j * LANES:nv + (j + 1) * LANES]
        steps = [us[j, pl.ds(t, rows, stride=CHUNK), :] for t in range(CHUNK)]
        for h in range(W // LANES):
            per_group = _piece_transpose(steps[h * GROUPS_PER_SET:(h + 1) * GROUPS_PER_SET])
            for g in range(GROUPS_PER_SET):
                u_ref[j, :, g * W + h * LANES:g * W + (h + 1) * LANES] = per_group[g].astype(BF16)


def _rope_tables(S):
    t = np.arange(S)
    freqs = (ROPE_THETA ** (-np.arange(16, dtype=np.float32) / 16)).astype(np.float32)
    halves = []
    for pos in (t // GRID_W, t % GRID_W):
        ang = pos.astype(np.float32)[:, None] * freqs[None, :]
        halves.append((np.cos(ang), np.sin(ang)))
    cos_h = np.concatenate([c for c, _ in halves for _ in range(2)], axis=1)
    sin_h = np.concatenate([sg * s for _, s in halves for sg in (-1.0, 1.0)], axis=1)
    return (jnp.asarray(np.concatenate([cos_h, cos_h], axis=1), F32),
            jnp.asarray(np.concatenate([sin_h, sin_h], axis=1), F32))


def _inproj(x2d, S, mix_g, w_in, q_g, k_g):
    T, D = x2d.shape
    tm = 512
    d_ssm = w_in.shape[1] - D_ATT - 2 * D_KV
    wq = w_in[:, :D_ATT]
    wk = w_in[:, D_ATT:D_ATT + D_KV]
    wv = w_in[:, D_ATT + D_KV:D_ATT + 2 * D_KV]
    wu = w_in[:, D_ATT + 2 * D_KV:]
    dup = lambda w: jnp.concatenate([w[:, :64], w[:, :64], w[:, 64:], w[:, 64:]], axis=1)
    w_ext = jnp.concatenate([wq, dup(wk), dup(wv), wu], axis=1).astype(BF16)
    n_qk = D_ATT + 2 * D_KV
    qkg = jnp.concatenate([jnp.tile(q_g, ATT_HEADS), jnp.tile(k_g, 2 * ATT_KV_HEADS)])[None, :]
    hid = np.arange(n_qk) // HEAD_DIM
    ones = jnp.asarray((hid[:, None] == hid[None, :]).astype(np.float32), dtype=BF16)
    cos_t, sin_t = _rope_tables(S)
    sb = S // tm
    const = lambda i: (0, 0)
    return pl.pallas_call(
        _inproj_body,
        grid=(T // tm,),
        in_specs=[
            pl.BlockSpec((tm, D), lambda i: (i, 0)),
            pl.BlockSpec((1, D), const),
            pl.BlockSpec(w_ext.shape, const),
            pl.BlockSpec((1, n_qk), const),
            pl.BlockSpec((n_qk, n_qk), const),
            pl.BlockSpec((tm, LANES), lambda i: (i % sb, 0)),
            pl.BlockSpec((tm, LANES), lambda i: (i % sb, 0)),
        ],
        out_specs=[
            pl.BlockSpec((tm, D_ATT), lambda i: (i, 0)),
            pl.BlockSpec((tm, 2 * D_KV), lambda i: (i, 0)),
            pl.BlockSpec((tm, 2 * D_KV), lambda i: (i, 0)),
            pl.BlockSpec((d_ssm // LANES, tm // CHUNK, CHUNK * LANES), lambda i: (0, i, 0)),
        ],
        out_shape=[
            jax.ShapeDtypeStruct((T, D_ATT), BF16),
            jax.ShapeDtypeStruct((T, 2 * D_KV), BF16),
            jax.ShapeDtypeStruct((T, 2 * D_KV), BF16),
            jax.ShapeDtypeStruct((d_ssm // LANES, T // CHUNK, CHUNK * LANES), BF16),
        ],
        scratch_shapes=[pltpu.VMEM((d_ssm // LANES, tm, LANES), F32)],
        compiler_params=_cparams("arbitrary"),
        name="inproj",
    )(x2d, mix_g[None, :], w_ext, qkg, ones, cos_t, sin_t)


ATT_KEY_CHUNK = 256
ATT_ROW_BLOCK = 512


def _attn_body(q_ref, k_ref, v_ref, o_ref, klo, khi, vlo, vhi, s_scr, e_scr):
    @pl.when(pl.program_id(2) == 0)
    def _():
        lane = lax.broadcasted_iota(jnp.int32, k_ref.shape, 1)
        lo = lane < HEAD_DIM
        kd = k_ref[...]
        vd = v_ref[...]
        zero = jnp.zeros_like(kd)
        klo[...] = jnp.where(lo, kd, zero)
        khi[...] = jnp.where(lo, zero, kd)
        vlo[...] = jnp.where(lo, vd, (lane == HEAD_DIM).astype(BF16))
        vhi[...] = jnp.where(lo, (lane == 0).astype(BF16), vd)

    S = klo.shape[0]
    rb = s_scr.shape[1]
    units = [(r0, p, km, vm) for r0 in range(0, q_ref.shape[0], rb)
             for p in range(2) for km, vm in ((klo, vlo), (khi, vhi))]

    def scores(i):
        r0, p, km, _ = units[i]
        qp = q_ref[r0:r0 + rb, p * LANES:(p + 1) * LANES]
        m = None
        for c in range(0, S, ATT_KEY_CHUNK):
            s = _dot_nt(qp, km[c:c + ATT_KEY_CHUNK, :])
            s_scr[i % 2, :, c:c + ATT_KEY_CHUNK] = s
            for l0 in range(0, ATT_KEY_CHUNK, LANES):
                m = s[:, l0:l0 + LANES] if m is None else jnp.maximum(m, s[:, l0:l0 + LANES])
        return jnp.max(m, axis=-1, keepdims=True)

    def values(i, m):
        vm = units[i][3]
        for c in range(0, S, ATT_KEY_CHUNK):
            e_scr[i % 2, :, c:c + ATT_KEY_CHUNK] = jnp.exp(s_scr[i % 2, :, c:c + ATT_KEY_CHUNK] - m).astype(BF16)
        acc = _dot(e_scr[i % 2], vm[...])
        ones_lane = HEAD_DIM if vm is vlo else 0
        return acc * (1.0 / acc[:, ones_lane:ones_lane + 1])

    lo = lax.broadcasted_iota(jnp.int32, (rb, LANES), 1) < HEAD_DIM
    m_prev = scores(0)
    first = None
    for i, (r0, p, _, vm) in enumerate(units):
        m_next = scores(i + 1) if i + 1 < len(units) else None
        o = values(i, m_prev)
        m_prev = m_next
        if vm is vlo:
            first = o
        else:
            o_ref[r0:r0 + rb, p * LANES:(p + 1) * LANES] = jnp.where(lo, first, o).astype(BF16)


def _attention(q, kd, vd, B, S):
    tq = 2 * ATT_ROW_BLOCK
    q3 = q.reshape(B, S, D_ATT)
    k3 = kd.reshape(B, S, 2 * D_KV)
    v3 = vd.reshape(B, S, 2 * D_KV)
    gw = D_ATT // ATT_KV_HEADS
    out = pl.pallas_call(
        _attn_body,
        grid=(B, ATT_KV_HEADS, S // tq),
        in_specs=[
            pl.BlockSpec((None, tq, gw), lambda b, j, i: (b, i, j)),
            pl.BlockSpec((None, S, LANES), lambda b, j, i: (b, 0, j)),
            pl.BlockSpec((None, S, LANES), lambda b, j, i: (b, 0, j)),
        ],
        out_specs=pl.BlockSpec((None, tq, gw), lambda b, j, i: (b, i, j)),
        out_shape=jax.ShapeDtypeStruct((B, S, D_ATT), BF16),
        scratch_shapes=[pltpu.VMEM((S, LANES), BF16)] * 4
        + [pltpu.VMEM((2, ATT_ROW_BLOCK, S), F32), pltpu.VMEM((2, ATT_ROW_BLOCK, S), BF16)],
        compiler_params=_cparams("arbitrary", "arbitrary", "arbitrary"),
        name="attn",
    )(q3, k3, v3)
    return out.reshape(B * S, D_ATT)


def _dot_tn3(a, b):
    tn = lambda x, y: lax.dot_general(x, y, (((0,), (0,)), ((), ())), preferred_element_type=F32)
    ah, al = _split_bf16(a)
    bh, bl = _split_bf16(b)
    return tn(ah, bh) + tn(al, bh) + tn(ah, bl)


def _s5op_body(lam_ref, br_ref, bi_ref, cr_ref, ci_ref, rep_ref, m_ref, et_ref, da_ref, db_ref):
    W = m_ref.shape[1]
    L = CHUNK
    step = lax.broadcasted_iota(jnp.int32, (1, W), 1) // SSM_GROUP

    def along_steps(a, channel_axis):
        hi = a.astype(BF16)
        r1 = a - hi.astype(F32)
        mid = r1.astype(BF16)
        lo = (r1 - mid.astype(F32)).astype(BF16)
        dims = (((channel_axis,), (0,)), ((), ()))
        rep = lambda x: lax.dot_general(x, rep_ref[...], dims, preferred_element_type=F32)
        return rep(hi) + rep(mid) + rep(lo)

    def direction(d):
        lam = lam_ref[d]
        lr = jnp.minimum(lam[:, 0:1], -1e-4)
        li = lam[:, 1:2]
        dt = jnp.exp(lam[:, 2:3])
        mag = jnp.exp(lr * dt)
        ab_r = mag * jnp.cos(li * dt)
        ab_i = mag * jnp.sin(li * dt)
        den = lr * lr + li * li
        nr = ab_r - 1.0
        f_r = (nr * lr + ab_i * li) / den
        f_i = (ab_i * lr - nr * li) / den
        b_r, b_i = along_steps(br_ref[d], 1), along_steps(bi_ref[d], 1)
        bb_r = f_r * b_r - f_i * b_i
        bb_i = f_r * b_i + f_i * b_r
        cr, ci = along_steps(cr_ref[d], 0), along_steps(ci_ref[d], 0)

        def power(n):
            sq_r, sq_i = ab_r, ab_i
            pr = pi = None
            for k in range(CHUNK.bit_length() - 1):
                bit = ((n >> k) & 1) == 1
                tr = jnp.where(bit, sq_r, 1.0)
                ti = jnp.where(bit, sq_i, 0.0)
                pr, pi = (tr, ti) if pr is None else (pr * tr - pi * ti, pr * ti + pi * tr)
                sq_r, sq_i = sq_r * sq_r - sq_i * sq_i, 2.0 * sq_r * sq_i
            return pr, pi

        def times_lam(p):
            return p[0] * ab_r - p[1] * ab_i, p[0] * ab_i + p[1] * ab_r

        def a_b(p):
            return p[0] * bb_r - p[1] * bb_i, p[0] * bb_i + p[1] * bb_r

        def c_a(p):
            return cr * p[0] - ci * p[1], cr * p[1] + ci * p[0]

        up, down = power(step), power(L - 1 - step)
        return bb_r, bb_i, cr, ci, a_b, c_a, up, down, times_lam

    fbr, fbi, _, _, f_ab, f_ca, f_up, f_down, f_next = direction(0)
    bbr, bbi, bcr, bci, b_ab, b_ca, b_up, b_down, b_next = direction(1)
    etf_r, etf_i = f_ab(f_down)
    etb_r, etb_i = b_ab(b_up)
    et_ref[...] = jnp.concatenate([etf_r, etb_r, etf_i, etb_i], axis=0).astype(BF16)
    df_r, df_i = f_ca(f_next(f_up))
    db_r, db_i = b_ca(b_next(b_down))
    zero = jnp.zeros_like(df_r)
    da_ref[...] = jnp.concatenate([df_r, zero, -df_i, zero], axis=0).astype(BF16)
    db_ref[...] = jnp.concatenate([zero, db_r, zero, -db_i], axis=0).astype(BF16)
    kf_r, kf_i = f_ca(f_up)
    lag0 = step == 0
    lhs_f = jnp.concatenate([fbr, -fbi], axis=0)
    lhs_b = jnp.concatenate([bbr, -bbi], axis=0)
    q_b = _dot_tn3(lhs_b, jnp.concatenate([db_r, db_i], axis=0))
    q_f = _dot_tn3(jnp.concatenate([lhs_f, lhs_b], axis=0),
                   jnp.concatenate([kf_r, kf_i, jnp.where(lag0, bcr, 0.0), jnp.where(lag0, bci, 0.0)],
                                   axis=0))
    z = jnp.concatenate([q_b, q_f], axis=1)
    for j in range(L):
        rows = z[j * SSM_GROUP:(j + 1) * SSM_GROUP, :]
        m_ref[j * SSM_GROUP:(j + 1) * SSM_GROUP, :] = pltpu.roll(
            rows, (2 * W - (L - j) * SSM_GROUP) % (2 * W), 1)[:, :W].astype(BF16)


def _s5_operators(lam_re, lam_im, log_step, b_re, b_im, c_re, c_im):
    _, G, P = lam_re.shape
    W = CHUNK * SSM_GROUP
    lam = jnp.stack([lam_re, lam_im, jnp.broadcast_to(log_step[..., None], lam_re.shape)], axis=-1).astype(F32)
    rep = jnp.asarray(np.arange(SSM_GROUP)[:, None] == np.arange(W)[None, :] % SSM_GROUP, dtype=BF16)
    per_g = lambda a: pl.BlockSpec((2, None) + a.shape[2:], lambda g: (0, g, 0, 0))
    out = pl.BlockSpec((None, W, W), lambda g: (g, 0, 0))
    args = (lam, b_re.astype(F32), b_im.astype(F32), c_re.astype(F32), c_im.astype(F32))
    return pl.pallas_call(
        _s5op_body,
        grid=(G,),
        in_specs=[per_g(a) for a in args] + [pl.BlockSpec(rep.shape, lambda g: (0, 0))],
        out_specs=[out] * 4,
        out_shape=[jax.ShapeDtypeStruct((G, W, W), BF16)] * 4,
        compiler_params=_cparams("arbitrary"),
        name="s5op",
    )(*args, rep)


GROUPS_PER_SET = LANES // SSM_GROUP
S5_SCAN_GROUPS = 4


def _transpose8(blocks, axis, piece):
    n = blocks[0].shape[axis]
    q = lax.broadcasted_iota(jnp.int32, blocks[0].shape, axis) // piece
    v = list(blocks)
    for d in (4, 2, 1):
        keep = (q & d) == 0
        nxt = list(v)
        for a in range(8):
            if a & d == 0:
                b = a + d
                nxt[a] = jnp.where(keep, v[a], pltpu.roll(v[b], d * piece, axis))
                nxt[b] = jnp.where(keep, pltpu.roll(v[a], n - d * piece, axis), v[b])
        v = nxt
    return v


def _piece_transpose(blocks):
    return _transpose8(blocks, 1, SSM_GROUP)


def _s5_body(u_ref, m_ref, et_ref, da_ref, db_ref, a_ref, d_ref, z_ref,
             xr_n, xi_n, xr_t, xi_t, har_s, hai_s, hbr_s, hbi_s, *, nb):
    R = u_ref.shape[0]
    nchunk = R // nb
    W = m_ref.shape[1]
    half = W // 2
    group = lambda g: slice(g * W, (g + 1) * W)
    fwd = lax.broadcasted_iota(jnp.int32, (nb, half), 1) < SSM_STATE

    def swap_row_order(src_a, src_b, dst_a, dst_b, to_chunk_major):
        def blocks(cb, _):
            batch_major = [pl.ds(pl.multiple_of(p * nchunk + cb * nb, nb), nb) for p in range(nb)]
            chunk_major = [pl.ds(pl.multiple_of((cb * nb + j) * nb, nb), nb) for j in range(nb)]
            rows_in, rows_out = (batch_major, chunk_major) if to_chunk_major else (chunk_major, batch_major)
            for k in range(S5_SCAN_GROUPS):
                for src, dst in ((src_a, dst_a), (src_b, dst_b)):
                    out = _transpose8([src[k, r, :] for r in rows_in], 0, 1)
                    for j in range(nb):
                        dst[k, rows_out[j], :] = out[j]
            return 0
        lax.fori_loop(0, nchunk // nb, blocks, 0)

    for g0 in range(0, GROUPS_PER_SET, S5_SCAN_GROUPS):
        coef = []
        for k in range(S5_SCAN_GROUPS):
            g = g0 + k
            x = _dot_nt(u_ref[:, group(g)], et_ref[g])
            xr_n[k] = x[:, :half]
            xi_n[k] = x[:, half:]
            lr = jnp.minimum(a_ref[g, 0:1, :], -1e-4)
            dt = jnp.exp(a_ref[g, 2:3, :])
            mag = jnp.exp(lr * dt * CHUNK)
            ang = a_ref[g, 1:2, :] * dt * CHUNK
            coef.append((jnp.broadcast_to(mag * jnp.cos(ang), (nb, half)),
                         jnp.broadcast_to(mag * jnp.sin(ang), (nb, half))))
        swap_row_order(xr_n, xi_n, xr_t, xi_t, True)

        def step(c, carry):
            rf = pl.ds(pl.multiple_of(c * nb, nb), nb)
            rb = pl.ds(pl.multiple_of((nchunk - 1 - c) * nb, nb), nb)
            new = []
            for k in range(S5_SCAN_GROUPS):
                sr, si = carry[2 * k], carry[2 * k + 1]
                ar, ai = coef[k]
                har_s[k, rf, :] = sr
                hai_s[k, rf, :] = si
                hbr_s[k, rb, :] = sr
                hbi_s[k, rb, :] = si
                xr = jnp.where(fwd, xr_t[k, rf, :], xr_t[k, rb, :])
                xi = jnp.where(fwd, xi_t[k, rf, :], xi_t[k, rb, :])
                new += [sr * ar - si * ai + xr, sr * ai + si * ar + xi]
            return tuple(new)

        zero = jnp.zeros((nb, half), F32)
        lax.fori_loop(0, nchunk, step, (zero,) * (2 * S5_SCAN_GROUPS))
        for k in range(S5_SCAN_GROUPS):
            g = g0 + k
            ha = jnp.concatenate([har_s[k], hai_s[k]], axis=1).astype(BF16)
            hb = jnp.concatenate([hbr_s[k], hbi_s[k]], axis=1).astype(BF16)
            y_t = _dot(ha, da_ref[g]) + _dot(hb, db_ref[g])
            xr_n[k] = y_t[:, :half]
            xi_n[k] = y_t[:, half:]
        swap_row_order(xr_n, xi_n, xr_t, xi_t, False)
        for k in range(S5_SCAN_GROUPS):
            g = g0 + k
            u = u_ref[:, group(g)]
            y = _dot(u, m_ref[g]) + jnp.concatenate([xr_t[k], xi_t[k]], axis=1) + d_ref[g] * u.astype(F32)
            z_ref[:, group(g)] = jax.nn.gelu(y).astype(BF16)


def _s5(u_set, B, S, lam_re, lam_im, log_step, b_re, b_im, c_re, c_im, ssm_d):
    n_set, R, WS = u_set.shape
    G = n_set * GROUPS_PER_SET
    W = CHUNK * SSM_GROUP
    assert R == (S // CHUNK) * B and WS == CHUNK * LANES
    M, ET, DA, DB = _s5_operators(lam_re, lam_im, log_step, b_re, b_im, c_re, c_im)
    both = lambda a: jnp.concatenate([a[0], a[1]], axis=-1)
    lam = jnp.stack([both(lam_re), both(lam_im),
                     both(jnp.broadcast_to(log_step[..., None], lam_re.shape))], axis=1).astype(F32)
    dvec = jnp.tile(ssm_d.astype(F32).reshape(G, 1, SSM_GROUP), (1, CHUNK, 1)).reshape(G, 1, W)
    per_set = pl.BlockSpec((None, R, WS), lambda s: (s, 0, 0))
    blk = lambda *sh: pl.BlockSpec((GROUPS_PER_SET,) + sh, lambda s: (s,) + (0,) * len(sh))
    return pl.pallas_call(
        functools.partial(_s5_body, nb=B),
        grid=(n_set,),
        in_specs=[per_set, blk(W, W), blk(W, W), blk(W, W), blk(W, W), blk(3, W // 2), blk(1, W)],
        out_specs=per_set,
        out_shape=jax.ShapeDtypeStruct((n_set, R, WS), BF16),
        scratch_shapes=[pltpu.VMEM((S5_SCAN_GROUPS, R, W // 2), F32)] * 8,
        compiler_params=_cparams("arbitrary"),
        name="s5",
    )(u_set, M, ET, DA, DB, lam, dvec)


def _memkv_body(m_ref, g_ref, wk_ref, wv_ref, k_ref, v_ref):
    h = (_rms(m_ref[...]) * g_ref[...]).astype(BF16)
    k_ref[...] = _dot(h, wk_ref[...]).astype(BF16)
    v_ref[...] = _dot(h, wv_ref[...]).astype(BF16)


def _memkv(mem, g, wk, wv):
    B, M, D = mem.shape
    const = lambda b: (0, 0)
    row = pl.BlockSpec((None, M, D), lambda b: (b, 0, 0))
    return pl.pallas_call(
        _memkv_body,
        grid=(B,),
        in_specs=[row, pl.BlockSpec((1, D), const), pl.BlockSpec((D, D), const),
                  pl.BlockSpec((D, D), const)],
        out_specs=[row, row],
        out_shape=[jax.ShapeDtypeStruct((B, M, D), BF16)] * 2,
        compiler_params=_cparams("arbitrary"),
        name="memkv",
    )(mem, g[None, :], wk.astype(BF16), wv.astype(BF16))


MID_ROW_BLOCK = 512


def _mid_body(x_ref, ya_ref, z_ref, gw_ref, gb_ref, ag_ref, sg_ref, wo_ref, xg_ref,
              wq_ref, xwo_ref, k_ref, v_ref, o_ref, zs):
    D = x_ref.shape[1]
    dh = D // XA_HEADS
    n_set = z_ref.shape[0]
    crows = MID_ROW_BLOCK // CHUNK

    def glu(r0, st):
        c0 = r0 // CHUNK
        W = CHUNK * SSM_GROUP
        for j in range(n_set):
            for h in range(W // LANES):
                steps = _piece_transpose([z_ref[j, c0:c0 + crows, g * W + h * LANES:g * W + (h + 1) * LANES].astype(F32)
                                          for g in range(GROUPS_PER_SET)])
                for k in range(GROUPS_PER_SET):
                    zs[j, pl.ds(r0 + h * GROUPS_PER_SET + k, crows, stride=CHUNK), :] = steps[k]
        z = jnp.concatenate([zs[j, r0:r0 + MID_ROW_BLOCK, :] for j in range(n_set)], axis=1).astype(BF16)
        st['s'] = z.astype(F32) * jax.nn.sigmoid(_dot(z, gw_ref[...]) + gb_ref[...])

    def out_proj(r0, st):
        rows = slice(r0, r0 + MID_ROW_BLOCK)
        ya = _rms(ya_ref[rows, :].astype(F32)) * ag_ref[...]
        ys = _rms(st.pop('s')) * sg_ref[...]
        y = jnp.concatenate([ya, ys], axis=1).astype(BF16)
        st['x1'] = x_ref[rows, :] + _dot(y, wo_ref[...])

    def query(r0, st):
        hq = (_rms(st['x1']) * xg_ref[...]).astype(BF16)
        st['q'] = (_dot(hq, wq_ref[...]) * (dh ** -0.5)).astype(BF16)

    def attend(r0, st):
        q = st.pop('q')
        outs = []
        for h in range(XA_HEADS):
            sl = slice(h * dh, (h + 1) * dh)
            sc = _dot_nt(q[:, sl], k_ref[:, sl])
            e = jnp.exp(sc - jnp.max(sc, axis=-1, keepdims=True))
            inv = 1.0 / jnp.sum(e, axis=-1, keepdims=True)
            outs.append((_dot(e.astype(BF16), v_ref[:, sl]) * inv).astype(BF16))
        st['o'] = jnp.concatenate(outs, axis=1)

    def finish(r0, st):
        rows = slice(r0, r0 + MID_ROW_BLOCK)
        o_ref[rows, :D] = st.pop('x1') + _dot(st.pop('o'), xwo_ref[...])
        o_ref[rows, D:] = jnp.zeros((MID_ROW_BLOCK, o_ref.shape[1] - D), F32)

    blocks = [(r0, {}) for r0 in range(0, x_ref.shape[0], MID_ROW_BLOCK)]
    for stage in (glu, out_proj, query, attend, finish):
        for r0, st in blocks:
            stage(r0, st)


def _mid(x2d, S, y_att, z, glu_w, glu_b, att_g, ssm_g, w_out, xa_g, wq, wo, kmem, vmem):
    T, D = x2d.shape
    M = kmem.shape[1]
    tm = 2 * MID_ROW_BLOCK
    sb = S // tm
    const = lambda i: (0, 0)
    tok = lambda w: pl.BlockSpec((tm, w), lambda i: (i, 0))
    full = lambda a: pl.BlockSpec(a.shape, const)
    vec = lambda a: a.astype(F32)[None, :]
    mem = pl.BlockSpec((None, M, D), lambda i: (i // sb, 0, 0))
    args = [x2d, y_att, z, glu_w.astype(BF16), vec(glu_b), vec(att_g), vec(ssm_g), w_out.astype(BF16),
            vec(xa_g), wq.astype(BF16), wo.astype(BF16), kmem, vmem]
    zspec = pl.BlockSpec((z.shape[0], tm // CHUNK, CHUNK * LANES), lambda i: (0, i, 0))
    specs = [tok(D), tok(y_att.shape[1]), zspec] + [full(a) for a in args[3:11]] + [mem, mem]
    return pl.pallas_call(
        _mid_body,
        grid=(T // tm,),
        in_specs=specs,
        out_specs=pl.BlockSpec((tm, D + LANES), lambda i: (i, 0)),
        out_shape=jax.ShapeDtypeStruct((T, D + LANES), F32),
        scratch_shapes=[pltpu.VMEM((z.shape[0], tm, LANES), F32)],
        compiler_params=_cparams("arbitrary"),
        name="mid",
    )(*args)


def _split_bf16(a):
    hi = a.astype(BF16)
    return hi, (a - hi.astype(F32)).astype(BF16)


ROUTE_ROWS = 32
PAIR_SLOTS = ((0, 1), (2, 1), (2, 0), (3, 0), (3, 1), (3, 2))
MOE_TILE = 256


def _route_body(x_ref, g_ref, whi_ref, wlo_ref, b_ref, tri_ref, meta_ref, mt_ref, cnt_ref, carry):
    i = pl.program_id(0)
    tm, D = x_ref.shape[0], g_ref.shape[1]

    @pl.when(i == 0)
    def _():
        carry[...] = jnp.zeros_like(carry)

    h = _rms(x_ref[:, :D]) * g_ref[...]
    hhi, hlo = _split_bf16(h)
    logits = (_dot_nt(whi_ref[...], hhi) + _dot_nt(whi_ref[...], hlo) + _dot_nt(wlo_ref[...], hhi)
              + b_ref[:, 0:1])
    ninf = jnp.float32(-jnp.inf)
    cmax = lambda v: jnp.max(v, axis=0, keepdims=True)
    cmin = lambda v: jnp.min(v, axis=0, keepdims=True)
    gl = logits[N_EXPERTS:N_EXPERTS + N_EXPERT_GROUPS]
    grow = lax.broadcasted_iota(jnp.int32, gl.shape, 0)
    gmax = cmax(gl)
    g_sel = cmin(jnp.where(gl == gmax, grow, N_EXPERT_GROUPS))
    g_w = 1.0 / jnp.sum(jnp.exp(gl - gmax), axis=0, keepdims=True)
    el = logits[:N_EXPERTS]
    erow = lax.broadcasted_iota(jnp.int32, el.shape, 0)
    el = jnp.where((erow >> 2) == g_sel, el, ninf)
    v1 = cmax(el)
    i1 = cmin(jnp.where(el == v1, erow, N_EXPERTS))
    el2 = jnp.where(erow == i1, ninf, el)
    v2 = cmax(el2)
    i2 = cmin(jnp.where(el2 == v2, erow, N_EXPERTS))
    t = jnp.exp(v2 - v1)
    w1 = g_w / (1.0 + t)
    w2 = g_w * t / (1.0 + t)
    first_low = i1 < i2
    ia = jnp.minimum(i1, i2) & 3
    ib = jnp.maximum(i1, i2) & 3
    wa = jnp.where(first_low, w1, w2)
    wb = jnp.where(first_low, w2, w1)
    pair = ((ia * (7 - ia)) >> 1) + ib - ia - 1
    order = jnp.where(pair == 3, 1, jnp.where(pair == 1, 2, jnp.where(pair == 2, 3, pair)))
    cls = g_sel * PAIRS_PER_GROUP + order
    crow = lax.broadcasted_iota(jnp.int32, (ROUTE_ROWS, tm), 0)
    onehot = (crow == cls).astype(F32)
    before = _dot(onehot.astype(BF16), tri_ref[...])
    rank = jnp.sum(onehot * (before + carry[:, 0:1]), axis=0, keepdims=True)
    carry[...] = carry[...] + jnp.sum(onehot, axis=1, keepdims=True)
    cnt_ref[...] = carry[...]

    def record(rows):
        r = lax.broadcasted_iota(jnp.int32, (rows, tm), 0)
        return jnp.where(r == META_CLS, cls.astype(F32),
                         jnp.where(r == META_RANK, rank,
                                   jnp.where(r == META_WA, wa, jnp.where(r == META_WB, wb, 0.0))))

    meta_ref[...] = record(LANES).T
    mt_ref[:, pl.ds(pl.multiple_of(i * tm, tm), tm)] = record(mt_ref.shape[0])

    @pl.when(i == pl.num_programs(0) - 1)
    def _():
        ntile = jnp.floor((carry[...] + (MOE_TILE - 1.0)) * (1.0 / MOE_TILE))
        r = lax.broadcasted_iota(jnp.int32, (ROUTE_ROWS, ROUTE_ROWS), 0)
        c = lax.broadcasted_iota(jnp.int32, (ROUTE_ROWS, ROUTE_ROWS), 1)
        off = _dot((c < r).astype(BF16), ntile.astype(BF16)) * MOE_TILE
        cls_all = mt_ref[META_CLS:META_CLS + 1, :]
        pos = mt_ref[META_RANK:META_RANK + 1, :]
        for k in range(N_CLASSES):
            pos = pos + jnp.where(cls_all == k, off[k:k + 1, 0:1], 0.0)
        mt_ref[META_POS:META_POS + 1, :] = pos


def _route(x2a, D, moe_g, grp_w, grp_b, exp_w, exp_b):
    T = x2a.shape[0]
    tm = 1024
    pad = ROUTE_ROWS - N_EXPERT_GROUPS - N_EXPERTS
    w = jnp.concatenate([exp_w, grp_w, jnp.zeros((D, pad), F32)], axis=1).astype(F32).T
    b = jnp.concatenate([exp_b, grp_b, jnp.zeros((pad,), F32)]).astype(F32)
    whi, wlo = _split_bf16(w)
    tri = jnp.asarray(np.arange(tm)[:, None] < np.arange(tm)[None, :], dtype=BF16)
    const = lambda i: (0, 0)
    ncol = D // LANES
    return pl.pallas_call(
        _route_body,
        grid=(T // tm,),
        in_specs=[pl.BlockSpec((tm, D + LANES), lambda i: (i, 0)),
                  pl.BlockSpec((1, D), const), pl.BlockSpec((ROUTE_ROWS, D), const),
                  pl.BlockSpec((ROUTE_ROWS, D), const), pl.BlockSpec((ROUTE_ROWS, LANES), const),
                  pl.BlockSpec((tm, tm), const)],
        out_specs=[pl.BlockSpec((tm, LANES), lambda i: (i, ncol)),
                   pl.BlockSpec((SUBLANES, T), const),
                   pl.BlockSpec((ROUTE_ROWS, LANES), const)],
        out_shape=[jax.ShapeDtypeStruct(x2a.shape, F32), jax.ShapeDtypeStruct((SUBLANES, T), F32),
                   jax.ShapeDtypeStruct((ROUTE_ROWS, LANES), F32)],
        scratch_shapes=[pltpu.VMEM((ROUTE_ROWS, LANES), F32)],
        input_output_aliases={0: 0},
        compiler_params=_cparams("arbitrary"),
        name="route",
    )(x2a, moe_g.astype(F32)[None, :], whi, wlo, jnp.broadcast_to(b[:, None], (ROUTE_ROWS, LANES)), tri)


ROW_COPY_UNROLL = 8


def _zero_fill(fill_ref, dst_ref, zbuf, sem):
    tile_groups = MOE_TILE // SUBLANES
    zbuf[...] = jnp.zeros_like(zbuf)
    pieces = []
    for c in range(N_CLASSES):
        s, n = fill_ref[2 * c], fill_ref[2 * c + 1]
        lead = jnp.minimum(n, (-s) & (SUBLANES - 1))
        for j in range(SUBLANES - 1):
            r = s + j
            dst = dst_ref.at[lax.shift_right_logical(r, 3), pl.ds(r & (SUBLANES - 1), 1)]
            pieces.append((j < lead, pltpu.make_async_copy(zbuf.at[0, pl.ds(0, 1)], dst, sem)))
        g0 = lax.shift_right_logical(s + lead, 3)
        groups = lax.shift_right_logical(n - lead, 3)
        k = tile_groups // 2
        while k >= 1:
            pieces.append(((groups & k) != 0,
                           pltpu.make_async_copy(zbuf.at[pl.ds(0, k)], dst_ref.at[pl.ds(g0, k)], sem)))
            g0 = g0 + (groups & k)
            k //= 2
    first_unused = fill_ref[2 * N_CLASSES]
    for t in range(N_CLASSES):
        tile = first_unused + t
        pieces.append((tile < dst_ref.shape[0] // tile_groups,
                       pltpu.make_async_copy(zbuf, dst_ref.at[pl.ds(tile * tile_groups, tile_groups)], sem)))
    for cond, cp in pieces:
        pl.when(cond)(cp.start)
    for cond, cp in pieces:
        pl.when(cond)(cp.wait)


def _row_copy_body(*refs, rows, scatter):
    if scatter:
        fill_ref, pos_ref, src_ref, dst_ref, zbuf, sem = refs

        @pl.when(pl.program_id(0) == 0)
        def _():
            _zero_fill(fill_ref, dst_ref, zbuf, sem)
    else:
        pos_ref, src_ref, dst_ref, sem = refs

    def copy(j, k):
        p = pos_ref[j * ROW_COPY_UNROLL + k]
        tok = (j, pl.ds(k, 1))
        slot = (lax.shift_right_logical(p, 3), pl.ds(p & (SUBLANES - 1), 1))
        if scatter:
            return pltpu.make_async_copy(src_ref.at[tok], dst_ref.at[slot], sem)
        return pltpu.make_async_copy(src_ref.at[slot], dst_ref.at[tok], sem)

    def start(j, _):
        for k in range(ROW_COPY_UNROLL):
            copy(j, k).start(priority=k % 2)
        return 0

    def wait(j, _):
        for k in range(ROW_COPY_UNROLL):
            copy(j, k).wait()
        return 0

    lax.fori_loop(0, rows // ROW_COPY_UNROLL, start, 0)
    lax.fori_loop(0, rows // ROW_COPY_UNROLL, wait, 0)


def _row_copy(pos, src, *, dst_rows, fill=None, name):
    assert ROW_COPY_UNROLL == SUBLANES
    T = pos.shape[0]
    rows = 1024
    width = src.shape[1]
    scatter = fill is not None
    hbm = pl.BlockSpec(memory_space=pl.ANY)
    tok = pl.BlockSpec((rows // SUBLANES, SUBLANES, width), lambda i, *_: (i, 0, 0))
    idx = pl.BlockSpec((rows,), lambda i, *_: (i,), memory_space=pltpu.SMEM)
    scratch = [pltpu.SemaphoreType.DMA(())]
    if scatter:
        scratch = [pltpu.VMEM((MOE_TILE // SUBLANES, SUBLANES, width), src.dtype)] + scratch
    grid_spec = pltpu.PrefetchScalarGridSpec(
        num_scalar_prefetch=1 if scatter else 0,
        grid=(T // rows,),
        in_specs=[idx, tok if scatter else hbm],
        out_specs=hbm if scatter else tok,
        scratch_shapes=scratch,
    )
    out = pl.pallas_call(
        functools.partial(_row_copy_body, rows=rows, scatter=scatter),
        grid_spec=grid_spec,
        out_shape=jax.ShapeDtypeStruct((dst_rows // SUBLANES, SUBLANES, width), src.dtype),
        compiler_params=pltpu.CompilerParams(dimension_semantics=("arbitrary",)),
        name=name,
    )(*(((fill,) if scatter else ()) + (pos, src.reshape(src.shape[0] // SUBLANES, SUBLANES, width))))
    return out.reshape(dst_rows, width)


def _ffn_body(ea_ref, eb_ref, na_ref, nb_ref, nt_ref, xs_ref, g_ref, fg_ref, wg_hbm, wu_hbm, wd_hbm, o_ref,
              sga, sua, sda, sgb, sub, sdb, ga, ua, da, gb, ub, db, sem):
    D = g_ref.shape[1]
    i = pl.program_id(0)
    valid = i < nt_ref[0]
    prev = jnp.maximum(i - 1, 0)
    slots = ((0, ea_ref, na_ref, (sga, sua, sda), (ga, ua, da)), (1, eb_ref, nb_ref, (sgb, sub, sdb), (gb, ub, db)))

    def fetch(e, stage, s):
        return [pltpu.make_async_copy(w.at[e], buf, sem.at[s])
                for w, buf in zip((wg_hbm, wu_hbm, wd_hbm), stage)]

    for s, e_ref, n_ref, stage, work in slots:
        @pl.when(i == 0)
        def _(s=s, e_ref=e_ref, stage=stage):
            for c in fetch(e_ref[0], stage, s):
                c.start()

        @pl.when(valid & ((i == 0) | (e_ref[i] != e_ref[prev])))
        def _(s=s, e_ref=e_ref, n_ref=n_ref, stage=stage, work=work):
            for c in fetch(e_ref[i], stage, s):
                c.wait()
            for src, dst in zip(stage, work):
                dst[...] = src[...].astype(BF16)

            @pl.when(n_ref[i] >= 0)
            def _():
                for c in fetch(n_ref[i], stage, s):
                    c.start()

    @pl.when(valid)
    def _():
        x = xs_ref[:, :D]
        meta = xs_ref[:, D:]
        lower_in_a = ea_ref[i] < eb_ref[i]
        g_lo, g_hi = meta[:, META_WA:META_WA + 1], meta[:, META_WB:META_WB + 1]
        gates = (jnp.where(lower_in_a, g_lo, g_hi), jnp.where(lower_in_a, g_hi, g_lo))
        h = (_rms(x) * g_ref[...]).astype(BF16)
        y = x
        for (wg, wu, wd), gate in zip(((ga, ua, da), (gb, ub, db)), gates):
            a = jax.nn.silu(_dot(h, wg[...])) * _dot(h, wu[...]) * gate
            y = y + _dot(a.astype(BF16), wd[...])
        o_ref[...] = _rms(y) * fg_ref[...]

    @pl.when(jnp.logical_not(valid))
    def _():
        o_ref[...] = jnp.zeros_like(o_ref)


def _ffn(xs, D, tile_ea, tile_eb, n_tiles, moe_g, final_g, w_gate, w_up, w_down):
    tm = MOE_TILE
    P = xs.shape[0]
    nt_max = P // tm
    De = w_gate.shape[2]

    def next_expert(e):
        j = jnp.arange(nt_max, dtype=jnp.int32)
        later = (j[None, :] > j[:, None]) & (j[None, :] < n_tiles[0]) & (e[None, :] != e[:, None])
        first = jnp.min(jnp.where(later, j[None, :], nt_max), axis=1)
        return jnp.where(first < nt_max, e[jnp.minimum(first, nt_max - 1)], -1).astype(jnp.int32)

    const = lambda i, ea, eb, na, nb, nt: (0, 0)
    hbm = pl.BlockSpec(memory_space=pl.ANY)
    expert = lambda dt: [pltpu.VMEM((D, De), dt), pltpu.VMEM((D, De), dt), pltpu.VMEM((De, D), dt)]
    grid_spec = pltpu.PrefetchScalarGridSpec(
        num_scalar_prefetch=5,
        grid=(nt_max,),
        in_specs=[pl.BlockSpec((tm, D + LANES),
                               lambda i, ea, eb, na, nb, nt: (jnp.maximum(jnp.minimum(i, nt[0] - 1), 0), 0)),
                  pl.BlockSpec((1, D), const), pl.BlockSpec((1, D), const), hbm, hbm, hbm],
        out_specs=pl.BlockSpec((tm, D), lambda i, ea, eb, na, nb, nt: (i, 0)),
        scratch_shapes=expert(F32) * 2 + expert(BF16) * 2 + [pltpu.SemaphoreType.DMA((2,))],
    )
    return pl.pallas_call(
        _ffn_body,
        grid_spec=grid_spec,
        out_shape=jax.ShapeDtypeStruct((P, D), F32),
        compiler_params=_cparams("arbitrary"),
        name="ffn",
    )(tile_ea, tile_eb, next_expert(tile_ea), next_expert(tile_eb), n_tiles, xs,
      moe_g.astype(F32)[None, :], final_g.astype(F32)[None, :],
      w_gate.astype(F32), w_up.astype(F32), w_down.astype(F32))


def _moe(x2a, D, moe_g, grp_w, grp_b, exp_w, exp_b, w_gate, w_up, w_down, final_g):
    T = x2a.shape[0]
    tm = MOE_TILE
    x2a, record, counts = _route(x2a, D, moe_g, grp_w, grp_b, exp_w, exp_b)
    pos = record[META_POS].astype(jnp.int32)
    cnt = counts[:N_CLASSES, 0].astype(jnp.int32)
    ntile_c = (cnt + tm - 1) // tm
    tile_end = jnp.cumsum(ntile_c)
    nt_max = T // tm + N_CLASSES
    n_tiles = tile_end[-1:]
    tile_id = jnp.arange(nt_max, dtype=jnp.int32)
    tile_cls = jnp.minimum(jnp.sum((tile_end[None, :] <= tile_id[:, None]).astype(jnp.int32), axis=1),
                           N_CLASSES - 1)
    tile_cls = jnp.where(tile_id < n_tiles[0], tile_cls, tile_cls[jnp.maximum(n_tiles[0] - 1, 0)])
    slot_a = jnp.asarray([a for a, _ in PAIR_SLOTS], jnp.int32)
    slot_b = jnp.asarray([b for _, b in PAIR_SLOTS], jnp.int32)
    grp = tile_cls // PAIRS_PER_GROUP
    tile_ea = grp * EXPERTS_PER_GROUP + slot_a[tile_cls % PAIRS_PER_GROUP]
    tile_eb = grp * EXPERTS_PER_GROUP + slot_b[tile_cls % PAIRS_PER_GROUP]
    seg_start = (tile_end - ntile_c) * tm
    fill = jnp.concatenate([jnp.stack([seg_start + cnt, ntile_c * tm - cnt], axis=1).reshape(-1), n_tiles])
    xs = _row_copy(pos, x2a, dst_rows=nt_max * tm, fill=fill.astype(jnp.int32), name="dispatch")
    ys = _ffn(xs, D, tile_ea, tile_eb, n_tiles.astype(jnp.int32), moe_g, final_g, w_gate, w_up, w_down)
    return _row_copy(pos, ys, dst_rows=T, name="unsort")


def kernel(x, mem, mix_norm_g, w_in, q_norm_g, k_norm_g, ssm_lambda_re, ssm_lambda_im, ssm_log_step, ssm_b_re, ssm_b_im, ssm_c_re, ssm_c_im, ssm_d, ssm_glu_w, ssm_glu_b, att_out_g, ssm_out_g, w_out, xa_norm_g, mem_norm_g, xa_wq, xa_wk, xa_wv, xa_wo, moe_norm_g, router_grp_w, router_grp_b, router_exp_w, router_exp_b, exp_w_gate, exp_w_up, exp_w_down, final_norm_g):
    B, S, D = x.shape
    assert mix_norm_g.shape[0] == 1, "single layer"
    x2d = x.reshape(B * S, D)
    q, kd, vd, u = _inproj(x2d, S, mix_norm_g[0], w_in[0], q_norm_g[0], k_norm_g[0])
    y_att = _attention(q, kd, vd, B, S)
    z = _s5(u, B, S, ssm_lambda_re[0], ssm_lambda_im[0], ssm_log_step[0], ssm_b_re[0], ssm_b_im[0],
            ssm_c_re[0], ssm_c_im[0], ssm_d[0])
    kmem, vmem = _memkv(mem, mem_norm_g[0], xa_wk[0], xa_wv[0])
    x2a = _mid(x2d, S, y_att, z, ssm_glu_w[0], ssm_glu_b[0], att_out_g[0], ssm_out_g[0], w_out[0],
               xa_norm_g[0], xa_wq[0], xa_wo[0], kmem, vmem)
    out = _moe(x2a, D, moe_norm_g[0], router_grp_w[0], router_grp_b[0], router_exp_w[0],
               router_exp_b[0], exp_w_gate[0], exp_w_up[0], exp_w_down[0], final_norm_g)
    return out.reshape(B, S, D)
```

```python
import functools

import numpy as np
import jax
import jax.numpy as jnp
from jax import lax
from jax.experimental import pallas as pl
from jax.experimental.pallas import tpu as pltpu

F32 = jnp.float32
BF16 = jnp.bfloat16

EPS = 1e-6
GRID_W = 64
ROPE_THETA = 10000.0
ATT_HEADS = 8
ATT_KV_HEADS = 2
HEAD_DIM = 64
D_ATT = ATT_HEADS * HEAD_DIM
D_KV = ATT_KV_HEADS * HEAD_DIM
SSM_GROUP = 16
SSM_STATE = 64
XA_HEADS = 4
N_EXPERT_GROUPS = 4
EXPERTS_PER_GROUP = 4
N_EXPERTS = N_EXPERT_GROUPS * EXPERTS_PER_GROUP
PAIRS_PER_GROUP = 6
N_CLASSES = N_EXPERT_GROUPS * PAIRS_PER_GROUP

LANES = 128
SUBLANES = 8
CHUNK = 16
VMEM_LIMIT = 56 * 1024 * 1024

META_CLS, META_RANK, META_WA, META_WB, META_POS = 0, 1, 2, 3, 4


def _cparams(*sem):
    return pltpu.CompilerParams(dimension_semantics=sem, vmem_limit_bytes=VMEM_LIMIT)


def _rms(x):
    return x * lax.rsqrt(jnp.mean(x * x, axis=-1, keepdims=True) + EPS)


def _dot(a, b):
    return jnp.dot(a, b, preferred_element_type=F32)


def _dot_nt(a, b):
    return lax.dot_general(a, b, (((1,), (1,)), ((), ())), preferred_element_type=F32)


def _inproj_body(x_ref, g_ref, w_ref, qkg_ref, ones_ref, cos_ref, sin_ref,
                 q_ref, k_ref, v_ref, u_ref, us):
    n_qk = q_ref.shape[1] + k_ref.shape[1]
    h = _rms(x_ref[...]) * g_ref[...]
    proj = _dot(h.astype(BF16), w_ref[...])
    qk = proj[:, :n_qk]
    ssq = _dot((qk * qk).astype(BF16), ones_ref[...])
    qk = qk * lax.rsqrt(ssq * (1.0 / HEAD_DIM) + EPS) * qkg_ref[...]
    cos = cos_ref[...]
    sin = sin_ref[...]
    lane = lax.broadcasted_iota(jnp.int32, cos.shape, 1)
    first = (lane % 32) < 16
    nq = q_ref.shape[1] // LANES
    for c in range(n_qk // LANES):
        xc = qk[:, c * LANES:(c + 1) * LANES]
        partner = jnp.where(first, pltpu.roll(xc, LANES - 16, 1), pltpu.roll(xc, 16, 1))
        rot = xc * cos + partner * sin
        if c < nq:
            q_ref[:, c * LANES:(c + 1) * LANES] = (rot * (HEAD_DIM ** -0.5)).astype(BF16)
        else:
            k_ref[:, (c - nq) * LANES:(c - nq + 1) * LANES] = rot.astype(BF16)
    nv = n_qk + v_ref.shape[1]
    v_ref[...] = proj[:, n_qk:nv].astype(BF16)
    rows = u_ref.shape[1]
    W = CHUNK * SSM_GROUP
    for j in range(u_ref.shape[0]):
        us[j] = proj[:, nv + j * LANES:nv + (j + 1) * LANES]
        steps = [us[j, pl.ds(t, rows, stride=CHUNK), :] for t in range(CHUNK)]
        for h in range(W // LANES):
            per_group = _piece_transpose(steps[h * GROUPS_PER_SET:(h + 1) * GROUPS_PER_SET])
            for g in range(GROUPS_PER_SET):
                u_ref[j, :, g * W + h * LANES:g * W + (h + 1) * LANES] = per_group[g].astype(BF16)


def _rope_tables(S):
    t = np.arange(S)
    freqs = (ROPE_THETA ** (-np.arange(16, dtype=np.float32) / 16)).astype(np.float32)
    halves = []
    for pos in (t // GRID_W, t % GRID_W):
        ang = pos.astype(np.float32)[:, None] * freqs[None, :]
        halves.append((np.cos(ang), np.sin(ang)))
    cos_h = np.concatenate([c for c, _ in halves for _ in range(2)], axis=1)
    sin_h = np.concatenate([sg * s for _, s in halves for sg in (-1.0, 1.0)], axis=1)
    return (jnp.asarray(np.concatenate([cos_h, cos_h], axis=1), F32),
            jnp.asarray(np.concatenate([sin_h, sin_h], axis=1), F32))


def _inproj(x2d, S, mix_g, w_in, q_g, k_g):
    T, D = x2d.shape
    tm = 1024
    d_ssm = w_in.shape[1] - D_ATT - 2 * D_KV
    wq = w_in[:, :D_ATT]
    wk = w_in[:, D_ATT:D_ATT + D_KV]
    wv = w_in[:, D_ATT + D_KV:D_ATT + 2 * D_KV]
    wu = w_in[:, D_ATT + 2 * D_KV:]
    dup = lambda w: jnp.concatenate([w[:, :64], w[:, :64], w[:, 64:], w[:, 64:]], axis=1)
    w_ext = jnp.concatenate([wq, dup(wk), dup(wv), wu], axis=1).astype(BF16)
    n_qk = D_ATT + 2 * D_KV
    qkg = jnp.concatenate([jnp.tile(q_g, ATT_HEADS), jnp.tile(k_g, 2 * ATT_KV_HEADS)])[None, :]
    hid = np.arange(n_qk) // HEAD_DIM
    ones = jnp.asarray((hid[:, None] == hid[None, :]).astype(np.float32), dtype=BF16)
    cos_t, sin_t = _rope_tables(S)
    sb = S // tm
    const = lambda i: (0, 0)
    return pl.pallas_call(
        _inproj_body,
        grid=(T // tm,),
        in_specs=[
            pl.BlockSpec((tm, D), lambda i: (i, 0)),
            pl.BlockSpec((1, D), const),
            pl.BlockSpec(w_ext.shape, const),
            pl.BlockSpec((1, n_qk), const),
            pl.BlockSpec((n_qk, n_qk), const),
            pl.BlockSpec((tm, LANES), lambda i: (i % sb, 0)),
            pl.BlockSpec((tm, LANES), lambda i: (i % sb, 0)),
        ],
        out_specs=[
            pl.BlockSpec((tm, D_ATT), lambda i: (i, 0)),
            pl.BlockSpec((tm, 2 * D_KV), lambda i: (i, 0)),
            pl.BlockSpec((tm, 2 * D_KV), lambda i: (i, 0)),
            pl.BlockSpec((d_ssm // LANES, tm // CHUNK, CHUNK * LANES), lambda i: (0, i, 0)),
        ],
        out_shape=[
            jax.ShapeDtypeStruct((T, D_ATT), BF16),
            jax.ShapeDtypeStruct((T, 2 * D_KV), BF16),
            jax.ShapeDtypeStruct((T, 2 * D_KV), BF16),
            jax.ShapeDtypeStruct((d_ssm // LANES, T // CHUNK, CHUNK * LANES), BF16),
        ],
        scratch_shapes=[pltpu.VMEM((d_ssm // LANES, tm, LANES), F32)],
        compiler_params=_cparams("arbitrary"),
        name="inproj",
    )(x2d, mix_g[None, :], w_ext, qkg, ones, cos_t, sin_t)


ATT_KEY_CHUNK = 256
ATT_ROW_BLOCK = 512


def _attn_body(q_ref, k_ref, v_ref, o_ref, klo, khi, vlo, vhi, s_scr, e_scr):
    @pl.when(pl.program_id(2) == 0)
    def _():
        lane = lax.broadcasted_iota(jnp.int32, k_ref.shape, 1)
        lo = lane < HEAD_DIM
        kd = k_ref[...]
        vd = v_ref[...]
        zero = jnp.zeros_like(kd)
        klo[...] = jnp.where(lo, kd, zero)
        khi[...] = jnp.where(lo, zero, kd)
        vlo[...] = jnp.where(lo, vd, (lane == HEAD_DIM).astype(BF16))
        vhi[...] = jnp.where(lo, (lane == 0).astype(BF16), vd)

    S = klo.shape[0]
    rb = s_scr.shape[1]
    units = [(r0, p, km, vm) for r0 in range(0, q_ref.shape[0], rb)
             for p in range(2) for km, vm in ((klo, vlo), (khi, vhi))]

    def scores(i):
        r0, p, km, _ = units[i]
        qp = q_ref[r0:r0 + rb, p * LANES:(p + 1) * LANES]
        m = None
        for c in range(0, S, ATT_KEY_CHUNK):
            s = _dot_nt(qp, km[c:c + ATT_KEY_CHUNK, :])
            s_scr[i % 2, :, c:c + ATT_KEY_CHUNK] = s
            for l0 in range(0, ATT_KEY_CHUNK, LANES):
                m = s[:, l0:l0 + LANES] if m is None else jnp.maximum(m, s[:, l0:l0 + LANES])
        return jnp.max(m, axis=-1, keepdims=True)

    def values(i, m):
        vm = units[i][3]
        for c in range(0, S, ATT_KEY_CHUNK):
            e_scr[i % 2, :, c:c + ATT_KEY_CHUNK] = jnp.exp(s_scr[i % 2, :, c:c + ATT_KEY_CHUNK] - m).astype(BF16)
        acc = _dot(e_scr[i % 2], vm[...])
        ones_lane = HEAD_DIM if vm is vlo else 0
        return acc * (1.0 / acc[:, ones_lane:ones_lane + 1])

    lo = lax.broadcasted_iota(jnp.int32, (rb, LANES), 1) < HEAD_DIM
    m_prev = scores(0)
    first = None
    for i, (r0, p, _, vm) in enumerate(units):
        m_next = scores(i + 1) if i + 1 < len(units) else None
        o = values(i, m_prev)
        m_prev = m_next
        if vm is vlo:
            first = o
        else:
            o_ref[r0:r0 + rb, p * LANES:(p + 1) * LANES] = jnp.where(lo, first, o).astype(BF16)


def _attention(q, kd, vd, B, S):
    tq = 2 * ATT_ROW_BLOCK
    q3 = q.reshape(B, S, D_ATT)
    k3 = kd.reshape(B, S, 2 * D_KV)
    v3 = vd.reshape(B, S, 2 * D_KV)
    gw = D_ATT // ATT_KV_HEADS
    out = pl.pallas_call(
        _attn_body,
        grid=(B, ATT_KV_HEADS, S // tq),
        in_specs=[
            pl.BlockSpec((None, tq, gw), lambda b, j, i: (b, i, j)),
            pl.BlockSpec((None, S, LANES), lambda b, j, i: (b, 0, j)),
            pl.BlockSpec((None, S, LANES), lambda b, j, i: (b, 0, j)),
        ],
        out_specs=pl.BlockSpec((None, tq, gw), lambda b, j, i: (b, i, j)),
        out_shape=jax.ShapeDtypeStruct((B, S, D_ATT), BF16),
        scratch_shapes=[pltpu.VMEM((S, LANES), BF16)] * 4
        + [pltpu.VMEM((2, ATT_ROW_BLOCK, S), F32), pltpu.VMEM((2, ATT_ROW_BLOCK, S), BF16)],
        compiler_params=_cparams("arbitrary", "arbitrary", "arbitrary"),
        name="attn",
    )(q3, k3, v3)
    return out.reshape(B * S, D_ATT)


def _dot_tn3(a, b):
    tn = lambda x, y: lax.dot_general(x, y, (((0,), (0,)), ((), ())), preferred_element_type=F32)
    ah, al = _split_bf16(a)
    bh, bl = _split_bf16(b)
    return tn(ah, bh) + tn(al, bh) + tn(ah, bl)


def _s5op_body(lam_ref, br_ref, bi_ref, cr_ref, ci_ref, rep_ref, m_ref, et_ref, da_ref, db_ref):
    W = m_ref.shape[1]
    L = CHUNK
    step = lax.broadcasted_iota(jnp.int32, (1, W), 1) // SSM_GROUP

    def along_steps(a, channel_axis):
        hi = a.astype(BF16)
        r1 = a - hi.astype(F32)
        mid = r1.astype(BF16)
        lo = (r1 - mid.astype(F32)).astype(BF16)
        dims = (((channel_axis,), (0,)), ((), ()))
        rep = lambda x: lax.dot_general(x, rep_ref[...], dims, preferred_element_type=F32)
        return rep(hi) + rep(mid) + rep(lo)

    def direction(d):
        lam = lam_ref[d]
        lr = jnp.minimum(lam[:, 0:1], -1e-4)
        li = lam[:, 1:2]
        dt = jnp.exp(lam[:, 2:3])
        mag = jnp.exp(lr * dt)
        ab_r = mag * jnp.cos(li * dt)
        ab_i = mag * jnp.sin(li * dt)
        den = lr * lr + li * li
        nr = ab_r - 1.0
        f_r = (nr * lr + ab_i * li) / den
        f_i = (ab_i * lr - nr * li) / den
        b_r, b_i = along_steps(br_ref[d], 1), along_steps(bi_ref[d], 1)
        bb_r = f_r * b_r - f_i * b_i
        bb_i = f_r * b_i + f_i * b_r
        cr, ci = along_steps(cr_ref[d], 0), along_steps(ci_ref[d], 0)

        def power(n):
            sq_r, sq_i = ab_r, ab_i
            pr = pi = None
            for k in range(CHUNK.bit_length() - 1):
                bit = ((n >> k) & 1) == 1
                tr = jnp.where(bit, sq_r, 1.0)
                ti = jnp.where(bit, sq_i, 0.0)
                pr, pi = (tr, ti) if pr is None else (pr * tr - pi * ti, pr * ti + pi * tr)
                sq_r, sq_i = sq_r * sq_r - sq_i * sq_i, 2.0 * sq_r * sq_i
            return pr, pi

        def times_lam(p):
            return p[0] * ab_r - p[1] * ab_i, p[0] * ab_i + p[1] * ab_r

        def a_b(p):
            return p[0] * bb_r - p[1] * bb_i, p[0] * bb_i + p[1] * bb_r

        def c_a(p):
            return cr * p[0] - ci * p[1], cr * p[1] + ci * p[0]

        up, down = power(step), power(L - 1 - step)
        return bb_r, bb_i, cr, ci, a_b, c_a, up, down, times_lam

    fbr, fbi, _, _, f_ab, f_ca, f_up, f_down, f_next = direction(0)
    bbr, bbi, bcr, bci, b_ab, b_ca, b_up, b_down, b_next = direction(1)
    etf_r, etf_i = f_ab(f_down)
    etb_r, etb_i = b_ab(b_up)
    et_ref[...] = jnp.concatenate([etf_r, etb_r, etf_i, etb_i], axis=0).astype(BF16)
    df_r, df_i = f_ca(f_next(f_up))
    db_r, db_i = b_ca(b_next(b_down))
    zero = jnp.zeros_like(df_r)
    da_ref[...] = jnp.concatenate([df_r, zero, -df_i, zero], axis=0).astype(BF16)
    db_ref[...] = jnp.concatenate([zero, db_r, zero, -db_i], axis=0).astype(BF16)
    kf_r, kf_i = f_ca(f_up)
    lag0 = step == 0
    lhs_f = jnp.concatenate([fbr, -fbi], axis=0)
    lhs_b = jnp.concatenate([bbr, -bbi], axis=0)
    q_b = _dot_tn3(lhs_b, jnp.concatenate([db_r, db_i], axis=0))
    q_f = _dot_tn3(jnp.concatenate([lhs_f, lhs_b], axis=0),
                   jnp.concatenate([kf_r, kf_i, jnp.where(lag0, bcr, 0.0), jnp.where(lag0, bci, 0.0)],
                                   axis=0))
    z = jnp.concatenate([q_b, q_f], axis=1)
    for j in range(L):
        rows = z[j * SSM_GROUP:(j + 1) * SSM_GROUP, :]
        m_ref[j * SSM_GROUP:(j + 1) * SSM_GROUP, :] = pltpu.roll(
            rows, (2 * W - (L - j) * SSM_GROUP) % (2 * W), 1)[:, :W].astype(BF16)


def _s5_operators(lam_re, lam_im, log_step, b_re, b_im, c_re, c_im):
    _, G, P = lam_re.shape
    W = CHUNK * SSM_GROUP
    lam = jnp.stack([lam_re, lam_im, jnp.broadcast_to(log_step[..., None], lam_re.shape)], axis=-1).astype(F32)
    rep = jnp.asarray(np.arange(SSM_GROUP)[:, None] == np.arange(W)[None, :] % SSM_GROUP, dtype=BF16)
    per_g = lambda a: pl.BlockSpec((2, None) + a.shape[2:], lambda g: (0, g, 0, 0))
    out = pl.BlockSpec((None, W, W), lambda g: (g, 0, 0))
    args = (lam, b_re.astype(F32), b_im.astype(F32), c_re.astype(F32), c_im.astype(F32))
    return pl.pallas_call(
        _s5op_body,
        grid=(G,),
        in_specs=[per_g(a) for a in args] + [pl.BlockSpec(rep.shape, lambda g: (0, 0))],
        out_specs=[out] * 4,
        out_shape=[jax.ShapeDtypeStruct((G, W, W), BF16)] * 4,
        compiler_params=_cparams("arbitrary"),
        name="s5op",
    )(*args, rep)


GROUPS_PER_SET = LANES // SSM_GROUP
S5_SCAN_GROUPS = 4


def _transpose8(blocks, axis, piece):
    n = blocks[0].shape[axis]
    q = lax.broadcasted_iota(jnp.int32, blocks[0].shape, axis) // piece
    v = list(blocks)
    for d in (4, 2, 1):
        keep = (q & d) == 0
        nxt = list(v)
        for a in range(8):
            if a & d == 0:
                b = a + d
                nxt[a] = jnp.where(keep, v[a], pltpu.roll(v[b], d * piece, axis))
                nxt[b] = jnp.where(keep, pltpu.roll(v[a], n - d * piece, axis), v[b])
        v = nxt
    return v


def _piece_transpose(blocks):
    return _transpose8(blocks, 1, SSM_GROUP)


def _s5_body(u_ref, m_ref, et_ref, da_ref, db_ref, a_ref, d_ref, z_ref,
             xr_n, xi_n, xr_t, xi_t, har_s, hai_s, hbr_s, hbi_s, *, nb):
    R = u_ref.shape[0]
    nchunk = R // nb
    W = m_ref.shape[1]
    half = W // 2
    group = lambda g: slice(g * W, (g + 1) * W)
    fwd = lax.broadcasted_iota(jnp.int32, (nb, half), 1) < SSM_STATE

    def swap_row_order(src_a, src_b, dst_a, dst_b, to_chunk_major):
        def blocks(cb, _):
            batch_major = [pl.ds(pl.multiple_of(p * nchunk + cb * nb, nb), nb) for p in range(nb)]
            chunk_major = [pl.ds(pl.multiple_of((cb * nb + j) * nb, nb), nb) for j in range(nb)]
            rows_in, rows_out = (batch_major, chunk_major) if to_chunk_major else (chunk_major, batch_major)
            for k in range(S5_SCAN_GROUPS):
                for src, dst in ((src_a, dst_a), (src_b, dst_b)):
                    out = _transpose8([src[k, r, :] for r in rows_in], 0, 1)
                    for j in range(nb):
                        dst[k, rows_out[j], :] = out[j]
            return 0
        lax.fori_loop(0, nchunk // nb, blocks, 0)

    for g0 in range(0, GROUPS_PER_SET, S5_SCAN_GROUPS):
        coef = []
        for k in range(S5_SCAN_GROUPS):
            g = g0 + k
            x = _dot_nt(u_ref[:, group(g)], et_ref[g])
            xr_n[k] = x[:, :half]
            xi_n[k] = x[:, half:]
            lr = jnp.minimum(a_ref[g, 0:1, :], -1e-4)
            dt = jnp.exp(a_ref[g, 2:3, :])
            mag = jnp.exp(lr * dt * CHUNK)
            ang = a_ref[g, 1:2, :] * dt * CHUNK
            coef.append((jnp.broadcast_to(mag * jnp.cos(ang), (nb, half)),
                         jnp.broadcast_to(mag * jnp.sin(ang), (nb, half))))
        swap_row_order(xr_n, xi_n, xr_t, xi_t, True)

        def step(c, carry):
            rf = pl.ds(pl.multiple_of(c * nb, nb), nb)
            rb = pl.ds(pl.multiple_of((nchunk - 1 - c) * nb, nb), nb)
            new = []
            for k in range(S5_SCAN_GROUPS):
                sr, si = carry[2 * k], carry[2 * k + 1]
                ar, ai = coef[k]
                har_s[k, rf, :] = sr
                hai_s[k, rf, :] = si
                hbr_s[k, rb, :] = sr
                hbi_s[k, rb, :] = si
                xr = jnp.where(fwd, xr_t[k, rf, :], xr_t[k, rb, :])
                xi = jnp.where(fwd, xi_t[k, rf, :], xi_t[k, rb, :])
                new += [sr * ar - si * ai + xr, sr * ai + si * ar + xi]
            return tuple(new)

        zero = jnp.zeros((nb, half), F32)
        lax.fori_loop(0, nchunk, step, (zero,) * (2 * S5_SCAN_GROUPS))
        for k in range(S5_SCAN_GROUPS):
            g = g0 + k
            ha = jnp.concatenate([har_s[k], hai_s[k]], axis=1).astype(BF16)
            hb = jnp.concatenate([hbr_s[k], hbi_s[k]], axis=1).astype(BF16)
            y_t = _dot(ha, da_ref[g]) + _dot(hb, db_ref[g])
            xr_n[k] = y_t[:, :half]
            xi_n[k] = y_t[:, half:]
        swap_row_order(xr_n, xi_n, xr_t, xi_t, False)
        for k in range(S5_SCAN_GROUPS):
            g = g0 + k
            u = u_ref[:, group(g)]
            y = _dot(u, m_ref[g]) + jnp.concatenate([xr_t[k], xi_t[k]], axis=1) + d_ref[g] * u.astype(F32)
            z_ref[:, group(g)] = jax.nn.gelu(y).astype(BF16)


def _s5(u_set, B, S, lam_re, lam_im, log_step, b_re, b_im, c_re, c_im, ssm_d):
    n_set, R, WS = u_set.shape
    G = n_set * GROUPS_PER_SET
    W = CHUNK * SSM_GROUP
    assert R == (S // CHUNK) * B and WS == CHUNK * LANES
    M, ET, DA, DB = _s5_operators(lam_re, lam_im, log_step, b_re, b_im, c_re, c_im)
    both = lambda a: jnp.concatenate([a[0], a[1]], axis=-1)
    lam = jnp.stack([both(lam_re), both(lam_im),
                     both(jnp.broadcast_to(log_step[..., None], lam_re.shape))], axis=1).astype(F32)
    dvec = jnp.tile(ssm_d.astype(F32).reshape(G, 1, SSM_GROUP), (1, CHUNK, 1)).reshape(G, 1, W)
    per_set = pl.BlockSpec((None, R, WS), lambda s: (s, 0, 0))
    blk = lambda *sh: pl.BlockSpec((GROUPS_PER_SET,) + sh, lambda s: (s,) + (0,) * len(sh))
    return pl.pallas_call(
        functools.partial(_s5_body, nb=B),
        grid=(n_set,),
        in_specs=[per_set, blk(W, W), blk(W, W), blk(W, W), blk(W, W), blk(3, W // 2), blk(1, W)],
        out_specs=per_set,
        out_shape=jax.ShapeDtypeStruct((n_set, R, WS), BF16),
        scratch_shapes=[pltpu.VMEM((S5_SCAN_GROUPS, R, W // 2), F32)] * 8,
        compiler_params=_cparams("arbitrary"),
        name="s5",
    )(u_set, M, ET, DA, DB, lam, dvec)


def _memkv_body(m_ref, g_ref, wk_ref, wv_ref, k_ref, v_ref):
    h = (_rms(m_ref[...]) * g_ref[...]).astype(BF16)
    k_ref[...] = _dot(h, wk_ref[...]).astype(BF16)
    v_ref[...] = _dot(h, wv_ref[...]).astype(BF16)


def _memkv(mem, g, wk, wv):
    B, M, D = mem.shape
    const = lambda b: (0, 0)
    row = pl.BlockSpec((None, M, D), lambda b: (b, 0, 0))
    return pl.pallas_call(
        _memkv_body,
        grid=(B,),
        in_specs=[row, pl.BlockSpec((1, D), const), pl.BlockSpec((D, D), const),
                  pl.BlockSpec((D, D), const)],
        out_specs=[row, row],
        out_shape=[jax.ShapeDtypeStruct((B, M, D), BF16)] * 2,
        compiler_params=_cparams("arbitrary"),
        name="memkv",
    )(mem, g[None, :], wk.astype(BF16), wv.astype(BF16))


MID_ROW_BLOCK = 512


def _mid_body(x_ref, ya_ref, z_ref, gw_ref, gb_ref, ag_ref, sg_ref, wo_ref, xg_ref,
              wq_ref, xwo_ref, k_ref, v_ref, o_ref, zs):
    D = x_ref.shape[1]
    dh = D // XA_HEADS
    n_set = z_ref.shape[0]
    crows = MID_ROW_BLOCK // CHUNK

    def glu(r0, st):
        c0 = r0 // CHUNK
        W = CHUNK * SSM_GROUP
        for j in range(n_set):
            for h in range(W // LANES):
                steps = _piece_transpose([z_ref[j, c0:c0 + crows, g * W + h * LANES:g * W + (h + 1) * LANES].astype(F32)
                                          for g in range(GROUPS_PER_SET)])
                for k in range(GROUPS_PER_SET):
                    zs[j, pl.ds(r0 + h * GROUPS_PER_SET + k, crows, stride=CHUNK), :] = steps[k]
        z = jnp.concatenate([zs[j, r0:r0 + MID_ROW_BLOCK, :] for j in range(n_set)], axis=1).astype(BF16)
        st['s'] = z.astype(F32) * jax.nn.sigmoid(_dot(z, gw_ref[...]) + gb_ref[...])

    def out_proj(r0, st):
        rows = slice(r0, r0 + MID_ROW_BLOCK)
        ya = _rms(ya_ref[rows, :].astype(F32)) * ag_ref[...]
        ys = _rms(st.pop('s')) * sg_ref[...]
        y = jnp.concatenate([ya, ys], axis=1).astype(BF16)
        st['x1'] = x_ref[rows, :] + _dot(y, wo_ref[...])

    def query(r0, st):
        hq = (_rms(st['x1']) * xg_ref[...]).astype(BF16)
        st['q'] = (_dot(hq, wq_ref[...]) * (dh ** -0.5)).astype(BF16)

    def attend(r0, st):
        q = st.pop('q')
        outs = []
        for h in range(XA_HEADS):
            sl = slice(h * dh, (h + 1) * dh)
            sc = _dot_nt(q[:, sl], k_ref[:, sl])
            e = jnp.exp(sc - jnp.max(sc, axis=-1, keepdims=True))
            inv = 1.0 / jnp.sum(e, axis=-1, keepdims=True)
            outs.append((_dot(e.astype(BF16), v_ref[:, sl]) * inv).astype(BF16))
        st['o'] = jnp.concatenate(outs, axis=1)

    def finish(r0, st):
        rows = slice(r0, r0 + MID_ROW_BLOCK)
        o_ref[rows, :D] = st.pop('x1') + _dot(st.pop('o'), xwo_ref[...])
        o_ref[rows, D:] = jnp.zeros((MID_ROW_BLOCK, o_ref.shape[1] - D), F32)

    blocks = [(r0, {}) for r0 in range(0, x_ref.shape[0], MID_ROW_BLOCK)]
    for stage in (glu, out_proj, query, attend, finish):
        for r0, st in blocks:
            stage(r0, st)


def _mid(x2d, S, y_att, z, glu_w, glu_b, att_g, ssm_g, w_out, xa_g, wq, wo, kmem, vmem):
    T, D = x2d.shape
    M = kmem.shape[1]
    tm = 2 * MID_ROW_BLOCK
    sb = S // tm
    const = lambda i: (0, 0)
    tok = lambda w: pl.BlockSpec((tm, w), lambda i: (i, 0))
    full = lambda a: pl.BlockSpec(a.shape, const)
    vec = lambda a: a.astype(F32)[None, :]
    mem = pl.BlockSpec((None, M, D), lambda i: (i // sb, 0, 0))
    args = [x2d, y_att, z, glu_w.astype(BF16), vec(glu_b), vec(att_g), vec(ssm_g), w_out.astype(BF16),
            vec(xa_g), wq.astype(BF16), wo.astype(BF16), kmem, vmem]
    zspec = pl.BlockSpec((z.shape[0], tm // CHUNK, CHUNK * LANES), lambda i: (0, i, 0))
    specs = [tok(D), tok(y_att.shape[1]), zspec] + [full(a) for a in args[3:11]] + [mem, mem]
    return pl.pallas_call(
        _mid_body,
        grid=(T // tm,),
        in_specs=specs,
        out_specs=pl.BlockSpec((tm, D + LANES), lambda i: (i, 0)),
        out_shape=jax.ShapeDtypeStruct((T, D + LANES), F32),
        scratch_shapes=[pltpu.VMEM((z.shape[0], tm, LANES), F32)],
        compiler_params=_cparams("arbitrary"),
        name="mid",
    )(*args)


def _split_bf16(a):
    hi = a.astype(BF16)
    return hi, (a - hi.astype(F32)).astype(BF16)


ROUTE_ROWS = 32
PAIR_SLOTS = ((0, 1), (2, 1), (2, 0), (3, 0), (3, 1), (3, 2))
MOE_TILE = 256


def _route_body(x_ref, g_ref, whi_ref, wlo_ref, b_ref, tri_ref, meta_ref, mt_ref, cnt_ref, carry):
    i = pl.program_id(0)
    tm, D = x_ref.shape[0], g_ref.shape[1]

    @pl.when(i == 0)
    def _():
        carry[...] = jnp.zeros_like(carry)

    h = _rms(x_ref[:, :D]) * g_ref[...]
    hhi, hlo = _split_bf16(h)
    logits = (_dot_nt(whi_ref[...], hhi) + _dot_nt(whi_ref[...], hlo) + _dot_nt(wlo_ref[...], hhi)
              + b_ref[:, 0:1])
    ninf = jnp.float32(-jnp.inf)
    cmax = lambda v: jnp.max(v, axis=0, keepdims=True)
    cmin = lambda v: jnp.min(v, axis=0, keepdims=True)
    gl = logits[N_EXPERTS:N_EXPERTS + N_EXPERT_GROUPS]
    grow = lax.broadcasted_iota(jnp.int32, gl.shape, 0)
    gmax = cmax(gl)
    g_sel = cmin(jnp.where(gl == gmax, grow, N_EXPERT_GROUPS))
    g_w = 1.0 / jnp.sum(jnp.exp(gl - gmax), axis=0, keepdims=True)
    el = logits[:N_EXPERTS]
    erow = lax.broadcasted_iota(jnp.int32, el.shape, 0)
    el = jnp.where((erow >> 2) == g_sel, el, ninf)
    v1 = cmax(el)
    i1 = cmin(jnp.where(el == v1, erow, N_EXPERTS))
    el2 = jnp.where(erow == i1, ninf, el)
    v2 = cmax(el2)
    i2 = cmin(jnp.where(el2 == v2, erow, N_EXPERTS))
    t = jnp.exp(v2 - v1)
    w1 = g_w / (1.0 + t)
    w2 = g_w * t / (1.0 + t)
    first_low = i1 < i2
    ia = jnp.minimum(i1, i2) & 3
    ib = jnp.maximum(i1, i2) & 3
    wa = jnp.where(first_low, w1, w2)
    wb = jnp.where(first_low, w2, w1)
    pair = ((ia * (7 - ia)) >> 1) + ib - ia - 1
    order = jnp.where(pair == 3, 1, jnp.where(pair == 1, 2, jnp.where(pair == 2, 3, pair)))
    cls = g_sel * PAIRS_PER_GROUP + order
    crow = lax.broadcasted_iota(jnp.int32, (ROUTE_ROWS, tm), 0)
    onehot = (crow == cls).astype(F32)
    before = _dot(onehot.astype(BF16), tri_ref[...])
    rank = jnp.sum(onehot * (before + carry[:, 0:1]), axis=0, keepdims=True)
    carry[...] = carry[...] + jnp.sum(onehot, axis=1, keepdims=True)
    cnt_ref[...] = carry[...]

    def record(rows):
        r = lax.broadcasted_iota(jnp.int32, (rows, tm), 0)
        return jnp.where(r == META_CLS, cls.astype(F32),
                         jnp.where(r == META_RANK, rank,
                                   jnp.where(r == META_WA, wa, jnp.where(r == META_WB, wb, 0.0))))

    meta_ref[...] = record(LANES).T
    mt_ref[:, pl.ds(pl.multiple_of(i * tm, tm), tm)] = record(mt_ref.shape[0])

    @pl.when(i == pl.num_programs(0) - 1)
    def _():
        ntile = jnp.floor((carry[...] + (MOE_TILE - 1.0)) * (1.0 / MOE_TILE))
        r = lax.broadcasted_iota(jnp.int32, (ROUTE_ROWS, ROUTE_ROWS), 0)
        c = lax.broadcasted_iota(jnp.int32, (ROUTE_ROWS, ROUTE_ROWS), 1)
        off = _dot((c < r).astype(BF16), ntile.astype(BF16)) * MOE_TILE
        cls_all = mt_ref[META_CLS:META_CLS + 1, :]
        pos = mt_ref[META_RANK:META_RANK + 1, :]
        for k in range(N_CLASSES):
            pos = pos + jnp.where(cls_all == k, off[k:k + 1, 0:1], 0.0)
        mt_ref[META_POS:META_POS + 1, :] = pos


def _route(x2a, D, moe_g, grp_w, grp_b, exp_w, exp_b):
    T = x2a.shape[0]
    tm = 1024
    pad = ROUTE_ROWS - N_EXPERT_GROUPS - N_EXPERTS
    w = jnp.concatenate([exp_w, grp_w, jnp.zeros((D, pad), F32)], axis=1).astype(F32).T
    b = jnp.concatenate([exp_b, grp_b, jnp.zeros((pad,), F32)]).astype(F32)
    whi, wlo = _split_bf16(w)
    tri = jnp.asarray(np.arange(tm)[:, None] < np.arange(tm)[None, :], dtype=BF16)
    const = lambda i: (0, 0)
    ncol = D // LANES
    return pl.pallas_call(
        _route_body,
        grid=(T // tm,),
        in_specs=[pl.BlockSpec((tm, D + LANES), lambda i: (i, 0)),
                  pl.BlockSpec((1, D), const), pl.BlockSpec((ROUTE_ROWS, D), const),
                  pl.BlockSpec((ROUTE_ROWS, D), const), pl.BlockSpec((ROUTE_ROWS, LANES), const),
                  pl.BlockSpec((tm, tm), const)],
        out_specs=[pl.BlockSpec((tm, LANES), lambda i: (i, ncol)),
                   pl.BlockSpec((SUBLANES, T), const),
                   pl.BlockSpec((ROUTE_ROWS, LANES), const)],
        out_shape=[jax.ShapeDtypeStruct(x2a.shape, F32), jax.ShapeDtypeStruct((SUBLANES, T), F32),
                   jax.ShapeDtypeStruct((ROUTE_ROWS, LANES), F32)],
        scratch_shapes=[pltpu.VMEM((ROUTE_ROWS, LANES), F32)],
        input_output_aliases={0: 0},
        compiler_params=_cparams("arbitrary"),
        name="route",
    )(x2a, moe_g.astype(F32)[None, :], whi, wlo, jnp.broadcast_to(b[:, None], (ROUTE_ROWS, LANES)), tri)


ROW_COPY_UNROLL = 8


def _zero_fill(fill_ref, dst_ref, zbuf, sem):
    tile_groups = MOE_TILE // SUBLANES
    zbuf[...] = jnp.zeros_like(zbuf)
    pieces = []
    for c in range(N_CLASSES):
        s, n = fill_ref[2 * c], fill_ref[2 * c + 1]
        lead = jnp.minimum(n, (-s) & (SUBLANES - 1))
        for j in range(SUBLANES - 1):
            r = s + j
            dst = dst_ref.at[lax.shift_right_logical(r, 3), pl.ds(r & (SUBLANES - 1), 1)]
            pieces.append((j < lead, pltpu.make_async_copy(zbuf.at[0, pl.ds(0, 1)], dst, sem)))
        g0 = lax.shift_right_logical(s + lead, 3)
        groups = lax.shift_right_logical(n - lead, 3)
        k = tile_groups // 2
        while k >= 1:
            pieces.append(((groups & k) != 0,
                           pltpu.make_async_copy(zbuf.at[pl.ds(0, k)], dst_ref.at[pl.ds(g0, k)], sem)))
            g0 = g0 + (groups & k)
            k //= 2
    first_unused = fill_ref[2 * N_CLASSES]
    for t in range(N_CLASSES):
        tile = first_unused + t
        pieces.append((tile < dst_ref.shape[0] // tile_groups,
                       pltpu.make_async_copy(zbuf, dst_ref.at[pl.ds(tile * tile_groups, tile_groups)], sem)))
    for cond, cp in pieces:
        pl.when(cond)(cp.start)
    for cond, cp in pieces:
        pl.when(cond)(cp.wait)


def _row_copy_body(*refs, rows, scatter):
    if scatter:
        fill_ref, pos_ref, src_ref, dst_ref, zbuf, sem = refs

        @pl.when(pl.program_id(0) == 0)
        def _():
            _zero_fill(fill_ref, dst_ref, zbuf, sem)
    else:
        pos_ref, src_ref, dst_ref, sem = refs

    def copy(j, k):
        p = pos_ref[j * ROW_COPY_UNROLL + k]
        tok = (j, pl.ds(k, 1))
        slot = (lax.shift_right_logical(p, 3), pl.ds(p & (SUBLANES - 1), 1))
        if scatter:
            return pltpu.make_async_copy(src_ref.at[tok], dst_ref.at[slot], sem)
        return pltpu.make_async_copy(src_ref.at[slot], dst_ref.at[tok], sem)

    def start(j, _):
        for k in range(ROW_COPY_UNROLL):
            copy(j, k).start(priority=k % 2)
        return 0

    def wait(j, _):
        for k in range(ROW_COPY_UNROLL):
            copy(j, k).wait()
        return 0

    lax.fori_loop(0, rows // ROW_COPY_UNROLL, start, 0)
    lax.fori_loop(0, rows // ROW_COPY_UNROLL, wait, 0)


def _row_copy(pos, src, *, dst_rows, fill=None, name):
    assert ROW_COPY_UNROLL == SUBLANES
    T = pos.shape[0]
    rows = 4096
    width = src.shape[1]
    scatter = fill is not None
    hbm = pl.BlockSpec(memory_space=pl.ANY)
    tok = pl.BlockSpec((rows // SUBLANES, SUBLANES, width), lambda i, *_: (i, 0, 0))
    idx = pl.BlockSpec((rows,), lambda i, *_: (i,), memory_space=pltpu.SMEM)
    scratch = [pltpu.SemaphoreType.DMA(())]
    if scatter:
        scratch = [pltpu.VMEM((MOE_TILE // SUBLANES, SUBLANES, width), src.dtype)] + scratch
    grid_spec = pltpu.PrefetchScalarGridSpec(
        num_scalar_prefetch=1 if scatter else 0,
        grid=(T // rows,),
        in_specs=[idx, tok if scatter else hbm],
        out_specs=hbm if scatter else tok,
        scratch_shapes=scratch,
    )
    out = pl.pallas_call(
        functools.partial(_row_copy_body, rows=rows, scatter=scatter),
        grid_spec=grid_spec,
        out_shape=jax.ShapeDtypeStruct((dst_rows // SUBLANES, SUBLANES, width), src.dtype),
        compiler_params=pltpu.CompilerParams(dimension_semantics=("arbitrary",)),
        name=name,
    )(*(((fill,) if scatter else ()) + (pos, src.reshape(src.shape[0] // SUBLANES, SUBLANES, width))))
    return out.reshape(dst_rows, width)


def _ffn_body(ea_ref, eb_ref, na_ref, nb_ref, nt_ref, xs_ref, g_ref, fg_ref, wg_hbm, wu_hbm, wd_hbm, o_ref,
              sga, sua, sda, sgb, sub, sdb, ga, ua, da, gb, ub, db, sem):
    D = g_ref.shape[1]
    i = pl.program_id(0)
    valid = i < nt_ref[0]
    prev = jnp.maximum(i - 1, 0)
    slots = ((0, ea_ref, na_ref, (sga, sua, sda), (ga, ua, da)), (1, eb_ref, nb_ref, (sgb, sub, sdb), (gb, ub, db)))

    def fetch(e, stage, s):
        return [pltpu.make_async_copy(w.at[e], buf, sem.at[s])
                for w, buf in zip((wg_hbm, wu_hbm, wd_hbm), stage)]

    for s, e_ref, n_ref, stage, work in slots:
        @pl.when(i == 0)
        def _(s=s, e_ref=e_ref, stage=stage):
            for c in fetch(e_ref[0], stage, s):
                c.start()

        @pl.when(valid & ((i == 0) | (e_ref[i] != e_ref[prev])))
        def _(s=s, e_ref=e_ref, n_ref=n_ref, stage=stage, work=work):
            for c in fetch(e_ref[i], stage, s):
                c.wait()
            for src, dst in zip(stage, work):
                dst[...] = src[...].astype(BF16)

            @pl.when(n_ref[i] >= 0)
            def _():
                for c in fetch(n_ref[i], stage, s):
                    c.start()

    @pl.when(valid)
    def _():
        x = xs_ref[:, :D]
        meta = xs_ref[:, D:]
        lower_in_a = ea_ref[i] < eb_ref[i]
        g_lo, g_hi = meta[:, META_WA:META_WA + 1], meta[:, META_WB:META_WB + 1]
        gates = (jnp.where(lower_in_a, g_lo, g_hi), jnp.where(lower_in_a, g_hi, g_lo))
        h = (_rms(x) * g_ref[...]).astype(BF16)
        y = x
        for (wg, wu, wd), gate in zip(((ga, ua, da), (gb, ub, db)), gates):
            a = jax.nn.silu(_dot(h, wg[...])) * _dot(h, wu[...]) * gate
            y = y + _dot(a.astype(BF16), wd[...])
        o_ref[...] = _rms(y) * fg_ref[...]

    @pl.when(jnp.logical_not(valid))
    def _():
        o_ref[...] = jnp.zeros_like(o_ref)


def _ffn(xs, D, tile_ea, tile_eb, n_tiles, moe_g, final_g, w_gate, w_up, w_down):
    tm = MOE_TILE
    P = xs.shape[0]
    nt_max = P // tm
    De = w_gate.shape[2]

    def next_expert(e):
        j = jnp.arange(nt_max, dtype=jnp.int32)
        later = (j[None, :] > j[:, None]) & (j[None, :] < n_tiles[0]) & (e[None, :] != e[:, None])
        first = jnp.min(jnp.where(later, j[None, :], nt_max), axis=1)
        return jnp.where(first < nt_max, e[jnp.minimum(first, nt_max - 1)], -1).astype(jnp.int32)

    const = lambda i, ea, eb, na, nb, nt: (0, 0)
    hbm = pl.BlockSpec(memory_space=pl.ANY)
    expert = lambda dt: [pltpu.VMEM((D, De), dt), pltpu.VMEM((D, De), dt), pltpu.VMEM((De, D), dt)]
    grid_spec = pltpu.PrefetchScalarGridSpec(
        num_scalar_prefetch=5,
        grid=(nt_max,),
        in_specs=[pl.BlockSpec((tm, D + LANES),
                               lambda i, ea, eb, na, nb, nt: (jnp.maximum(jnp.minimum(i, nt[0] - 1), 0), 0)),
                  pl.BlockSpec((1, D), const), pl.BlockSpec((1, D), const), hbm, hbm, hbm],
        out_specs=pl.BlockSpec((tm, D), lambda i, ea, eb, na, nb, nt: (i, 0)),
        scratch_shapes=expert(F32) * 2 + expert(BF16) * 2 + [pltpu.SemaphoreType.DMA((2,))],
    )
    return pl.pallas_call(
        _ffn_body,
        grid_spec=grid_spec,
        out_shape=jax.ShapeDtypeStruct((P, D), F32),
        compiler_params=_cparams("arbitrary"),
        name="ffn",
    )(tile_ea, tile_eb, next_expert(tile_ea), next_expert(tile_eb), n_tiles, xs,
      moe_g.astype(F32)[None, :], final_g.astype(F32)[None, :],
      w_gate.astype(F32), w_up.astype(F32), w_down.astype(F32))


def _moe(x2a, D, moe_g, grp_w, grp_b, exp_w, exp_b, w_gate, w_up, w_down, final_g):
    T = x2a.shape[0]
    tm = MOE_TILE
    x2a, record, counts = _route(x2a, D, moe_g, grp_w, grp_b, exp_w, exp_b)
    pos = record[META_POS].astype(jnp.int32)
    cnt = counts[:N_CLASSES, 0].astype(jnp.int32)
    ntile_c = (cnt + tm - 1) // tm
    tile_end = jnp.cumsum(ntile_c)
    nt_max = T // tm + N_CLASSES
    n_tiles = tile_end[-1:]
    tile_id = jnp.arange(nt_max, dtype=jnp.int32)
    tile_cls = jnp.minimum(jnp.sum((tile_end[None, :] <= tile_id[:, None]).astype(jnp.int32), axis=1),
                           N_CLASSES - 1)
    tile_cls = jnp.where(tile_id < n_tiles[0], tile_cls, tile_cls[jnp.maximum(n_tiles[0] - 1, 0)])
    slot_a = jnp.asarray([a for a, _ in PAIR_SLOTS], jnp.int32)
    slot_b = jnp.asarray([b for _, b in PAIR_SLOTS], jnp.int32)
    grp = tile_cls // PAIRS_PER_GROUP
    tile_ea = grp * EXPERTS_PER_GROUP + slot_a[tile_cls % PAIRS_PER_GROUP]
    tile_eb = grp * EXPERTS_PER_GROUP + slot_b[tile_cls % PAIRS_PER_GROUP]
    seg_start = (tile_end - ntile_c) * tm
    fill = jnp.concatenate([jnp.stack([seg_start + cnt, ntile_c * tm - cnt], axis=1).reshape(-1), n_tiles])
    xs = _row_copy(pos, x2a, dst_rows=nt_max * tm, fill=fill.astype(jnp.int32), name="dispatch")
    ys = _ffn(xs, D, tile_ea, tile_eb, n_tiles.astype(jnp.int32), moe_g, final_g, w_gate, w_up, w_down)
    return _row_copy(pos, ys, dst_rows=T, name="unsort")


def kernel(x, mem, mix_norm_g, w_in, q_norm_g, k_norm_g, ssm_lambda_re, ssm_lambda_im, ssm_log_step, ssm_b_re, ssm_b_im, ssm_c_re, ssm_c_im, ssm_d, ssm_glu_w, ssm_glu_b, att_out_g, ssm_out_g, w_out, xa_norm_g, mem_norm_g, xa_wq, xa_wk, xa_wv, xa_wo, moe_norm_g, router_grp_w, router_grp_b, router_exp_w, router_exp_b, exp_w_gate, exp_w_up, exp_w_down, final_norm_g):
    B, S, D = x.shape
    assert mix_norm_g.shape[0] == 1, "single layer"
    x2d = x.reshape(B * S, D)
    q, kd, vd, u = _inproj(x2d, S, mix_norm_g[0], w_in[0], q_norm_g[0], k_norm_g[0])
    y_att = _attention(q, kd, vd, B, S)
    z = _s5(u, B, S, ssm_lambda_re[0], ssm_lambda_im[0], ssm_log_step[0], ssm_b_re[0], ssm_b_im[0],
            ssm_c_re[0], ssm_c_im[0], ssm_d[0])
    kmem, vmem = _memkv(mem, mem_norm_g[0], xa_wk[0], xa_wv[0])
    x2a = _mid(x2d, S, y_att, z, ssm_glu_w[0], ssm_glu_b[0], att_out_g[0], ssm_out_g[0], w_out[0],
               xa_norm_g[0], xa_wq[0], xa_wo[0], kmem, vmem)
    out = _moe(x2a, D, moe_norm_g[0], router_grp_w[0], router_grp_b[0], router_exp_w[0],
               router_exp_b[0], exp_w_gate[0], exp_w_up[0], exp_w_down[0], final_norm_g)
    return out.reshape(B, S, D)
```

```python
import functools

import numpy as np
import jax
import jax.numpy as jnp
from jax import lax
from jax.experimental import pallas as pl
from jax.experimental.pallas import tpu as pltpu

F32 = jnp.float32
BF16 = jnp.bfloat16

EPS = 1e-6
GRID_W = 64
ROPE_THETA = 10000.0
ATT_HEADS = 8
ATT_KV_HEADS = 2
HEAD_DIM = 64
D_ATT = ATT_HEADS * HEAD_DIM
D_KV = ATT_KV_HEADS * HEAD_DIM
SSM_GROUP = 16
SSM_STATE = 64
XA_HEADS = 4
N_EXPERT_GROUPS = 4
EXPERTS_PER_GROUP = 4
N_EXPERTS = N_EXPERT_GROUPS * EXPERTS_PER_GROUP
PAIRS_PER_GROUP = 6
N_CLASSES = N_EXPERT_GROUPS * PAIRS_PER_GROUP

LANES = 128
SUBLANES = 8
CHUNK = 16
VMEM_LIMIT = 56 * 1024 * 1024

META_CLS, META_RANK, META_WA, META_WB, META_POS = 0, 1, 2, 3, 4


def _cparams(*sem):
    return pltpu.CompilerParams(dimension_semantics=sem, vmem_limit_bytes=VMEM_LIMIT)


def _rms(x):
    return x * lax.rsqrt(jnp.mean(x * x, axis=-1, keepdims=True) + EPS)


def _dot(a, b):
    return jnp.dot(a, b, preferred_element_type=F32)


def _dot_nt(a, b):
    return lax.dot_general(a, b, (((1,), (1,)), ((), ())), preferred_element_type=F32)


def _inproj_body(x_ref, g_ref, w_ref, qkg_ref, ones_ref, cos_ref, sin_ref,
                 q_ref, k_ref, v_ref, u_ref, us):
    n_qk = q_ref.shape[1] + k_ref.shape[1]
    h = _rms(x_ref[...]) * g_ref[...]
    proj = _dot(h.astype(BF16), w_ref[...])
    qk = proj[:, :n_qk]
    ssq = _dot((qk * qk).astype(BF16), ones_ref[...])
    qk = qk * lax.rsqrt(ssq * (1.0 / HEAD_DIM) + EPS) * qkg_ref[...]
    cos = cos_ref[...]
    sin = sin_ref[...]
    lane = lax.broadcasted_iota(jnp.int32, cos.shape, 1)
    first = (lane % 32) < 16
    nq = q_ref.shape[1] // LANES
    for c in range(n_qk // LANES):
        xc = qk[:, c * LANES:(c + 1) * LANES]
        partner = jnp.where(first, pltpu.roll(xc, LANES - 16, 1), pltpu.roll(xc, 16, 1))
        rot = xc * cos + partner * sin
        if c < nq:
            q_ref[:, c * LANES:(c + 1) * LANES] = (rot * (HEAD_DIM ** -0.5)).astype(BF16)
        else:
            k_ref[:, (c - nq) * LANES:(c - nq + 1) * LANES] = rot.astype(BF16)
    nv = n_qk + v_ref.shape[1]
    v_ref[...] = proj[:, n_qk:nv].astype(BF16)
    rows = u_ref.shape[1]
    W = CHUNK * SSM_GROUP
    for j in range(u_ref.shape[0]):
        us[j] = proj[:, nv + j * LANES:nv + (j + 1) * LANES]
        steps = [us[j, pl.ds(t, rows, stride=CHUNK), :] for t in range(CHUNK)]
        for h in range(W // LANES):
            per_group = _piece_transpose(steps[h * GROUPS_PER_SET:(h + 1) * GROUPS_PER_SET])
            for g in range(GROUPS_PER_SET):
                u_ref[j, :, g * W + h * LANES:g * W + (h + 1) * LANES] = per_group[g].astype(BF16)


def _rope_tables(S):
    t = np.arange(S)
    freqs = (ROPE_THETA ** (-np.arange(16, dtype=np.float32) / 16)).astype(np.float32)
    halves = []
    for pos in (t // GRID_W, t % GRID_W):
        ang = pos.astype(np.float32)[:, None] * freqs[None, :]
        halves.append((np.cos(ang), np.sin(ang)))
    cos_h = np.concatenate([c for c, _ in halves for _ in range(2)], axis=1)
    sin_h = np.concatenate([sg * s for _, s in halves for sg in (-1.0, 1.0)], axis=1)
    return (jnp.asarray(np.concatenate([cos_h, cos_h], axis=1), F32),
            jnp.asarray(np.concatenate([sin_h, sin_h], axis=1), F32))


def _inproj(x2d, S, mix_g, w_in, q_g, k_g):
    T, D = x2d.shape
    tm = 1024
    d_ssm = w_in.shape[1] - D_ATT - 2 * D_KV
    wq = w_in[:, :D_ATT]
    wk = w_in[:, D_ATT:D_ATT + D_KV]
    wv = w_in[:, D_ATT + D_KV:D_ATT + 2 * D_KV]
    wu = w_in[:, D_ATT + 2 * D_KV:]
    dup = lambda w: jnp.concatenate([w[:, :64], w[:, :64], w[:, 64:], w[:, 64:]], axis=1)
    w_ext = jnp.concatenate([wq, dup(wk), dup(wv), wu], axis=1).astype(BF16)
    n_qk = D_ATT + 2 * D_KV
    qkg = jnp.concatenate([jnp.tile(q_g, ATT_HEADS), jnp.tile(k_g, 2 * ATT_KV_HEADS)])[None, :]
    hid = np.arange(n_qk) // HEAD_DIM
    ones = jnp.asarray((hid[:, None] == hid[None, :]).astype(np.float32), dtype=BF16)
    cos_t, sin_t = _rope_tables(S)
    sb = S // tm
    const = lambda i: (0, 0)
    return pl.pallas_call(
        _inproj_body,
        grid=(T // tm,),
        in_specs=[
            pl.BlockSpec((tm, D), lambda i: (i, 0)),
            pl.BlockSpec((1, D), const),
            pl.BlockSpec(w_ext.shape, const),
            pl.BlockSpec((1, n_qk), const),
            pl.BlockSpec((n_qk, n_qk), const),
            pl.BlockSpec((tm, LANES), lambda i: (i % sb, 0)),
            pl.BlockSpec((tm, LANES), lambda i: (i % sb, 0)),
        ],
        out_specs=[
            pl.BlockSpec((tm, D_ATT), lambda i: (i, 0)),
            pl.BlockSpec((tm, 2 * D_KV), lambda i: (i, 0)),
            pl.BlockSpec((tm, 2 * D_KV), lambda i: (i, 0)),
            pl.BlockSpec((d_ssm // LANES, tm // CHUNK, CHUNK * LANES), lambda i: (0, i, 0)),
        ],
        out_shape=[
            jax.ShapeDtypeStruct((T, D_ATT), BF16),
            jax.ShapeDtypeStruct((T, 2 * D_KV), BF16),
            jax.ShapeDtypeStruct((T, 2 * D_KV), BF16),
            jax.ShapeDtypeStruct((d_ssm // LANES, T // CHUNK, CHUNK * LANES), BF16),
        ],
        scratch_shapes=[pltpu.VMEM((d_ssm // LANES, tm, LANES), F32)],
        compiler_params=_cparams("arbitrary"),
        name="inproj",
    )(x2d, mix_g[None, :], w_ext, qkg, ones, cos_t, sin_t)


ATT_KEY_CHUNK = 256
ATT_ROW_BLOCK = 512


def _attn_body(q_ref, k_ref, v_ref, o_ref, klo, khi, vlo, vhi, s_scr, e_scr):
    @pl.when(pl.program_id(2) == 0)
    def _():
        lane = lax.broadcasted_iota(jnp.int32, k_ref.shape, 1)
        lo = lane < HEAD_DIM
        kd = k_ref[...]
        vd = v_ref[...]
        zero = jnp.zeros_like(kd)
        klo[...] = jnp.where(lo, kd, zero)
        khi[...] = jnp.where(lo, zero, kd)
        vlo[...] = jnp.where(lo, vd, (lane == HEAD_DIM).astype(BF16))
        vhi[...] = jnp.where(lo, (lane == 0).astype(BF16), vd)

    S = klo.shape[0]
    rb = s_scr.shape[1]
    units = [(r0, p, km, vm) for r0 in range(0, q_ref.shape[0], rb)
             for p in range(2) for km, vm in ((klo, vlo), (khi, vhi))]

    def scores(i):
        r0, p, km, _ = units[i]
        qp = q_ref[r0:r0 + rb, p * LANES:(p + 1) * LANES]
        m = None
        for c in range(0, S, ATT_KEY_CHUNK):
            s = _dot_nt(qp, km[c:c + ATT_KEY_CHUNK, :])
            s_scr[i % 2, :, c:c + ATT_KEY_CHUNK] = s
            for l0 in range(0, ATT_KEY_CHUNK, LANES):
                m = s[:, l0:l0 + LANES] if m is None else jnp.maximum(m, s[:, l0:l0 + LANES])
        return jnp.max(m, axis=-1, keepdims=True)

    def values(i, m):
        vm = units[i][3]
        for c in range(0, S, ATT_KEY_CHUNK):
            e_scr[i % 2, :, c:c + ATT_KEY_CHUNK] = jnp.exp(s_scr[i % 2, :, c:c + ATT_KEY_CHUNK] - m).astype(BF16)
        acc = _dot(e_scr[i % 2], vm[...])
        ones_lane = HEAD_DIM if vm is vlo else 0
        return acc * (1.0 / acc[:, ones_lane:ones_lane + 1])

    lo = lax.broadcasted_iota(jnp.int32, (rb, LANES), 1) < HEAD_DIM
    m_prev = scores(0)
    first = None
    for i, (r0, p, _, vm) in enumerate(units):
        m_next = scores(i + 1) if i + 1 < len(units) else None
        o = values(i, m_prev)
        m_prev = m_next
        if vm is vlo:
            first = o
        else:
            o_ref[r0:r0 + rb, p * LANES:(p + 1) * LANES] = jnp.where(lo, first, o).astype(BF16)


def _attention(q, kd, vd, B, S):
    tq = 2 * ATT_ROW_BLOCK
    q3 = q.reshape(B, S, D_ATT)
    k3 = kd.reshape(B, S, 2 * D_KV)
    v3 = vd.reshape(B, S, 2 * D_KV)
    gw = D_ATT // ATT_KV_HEADS
    out = pl.pallas_call(
        _attn_body,
        grid=(B, ATT_KV_HEADS, S // tq),
        in_specs=[
            pl.BlockSpec((None, tq, gw), lambda b, j, i: (b, i, j)),
            pl.BlockSpec((None, S, LANES), lambda b, j, i: (b, 0, j)),
            pl.BlockSpec((None, S, LANES), lambda b, j, i: (b, 0, j)),
        ],
        out_specs=pl.BlockSpec((None, tq, gw), lambda b, j, i: (b, i, j)),
        out_shape=jax.ShapeDtypeStruct((B, S, D_ATT), BF16),
        scratch_shapes=[pltpu.VMEM((S, LANES), BF16)] * 4
        + [pltpu.VMEM((2, ATT_ROW_BLOCK, S), F32), pltpu.VMEM((2, ATT_ROW_BLOCK, S), BF16)],
        compiler_params=_cparams("arbitrary", "arbitrary", "arbitrary"),
        name="attn",
    )(q3, k3, v3)
    return out.reshape(B * S, D_ATT)


def _dot_tn3(a, b):
    tn = lambda x, y: lax.dot_general(x, y, (((0,), (0,)), ((), ())), preferred_element_type=F32)
    ah, al = _split_bf16(a)
    bh, bl = _split_bf16(b)
    return tn(ah, bh) + tn(al, bh) + tn(ah, bl)


def _s5op_body(lam_ref, br_ref, bi_ref, cr_ref, ci_ref, rep_ref, m_ref, et_ref, da_ref, db_ref):
    W = m_ref.shape[1]
    L = CHUNK
    step = lax.broadcasted_iota(jnp.int32, (1, W), 1) // SSM_GROUP

    def along_steps(a, channel_axis):
        hi = a.astype(BF16)
        r1 = a - hi.astype(F32)
        mid = r1.astype(BF16)
        lo = (r1 - mid.astype(F32)).astype(BF16)
        dims = (((channel_axis,), (0,)), ((), ()))
        rep = lambda x: lax.dot_general(x, rep_ref[...], dims, preferred_element_type=F32)
        return rep(hi) + rep(mid) + rep(lo)

    def direction(d):
        lam = lam_ref[d]
        lr = jnp.minimum(lam[:, 0:1], -1e-4)
        li = lam[:, 1:2]
        dt = jnp.exp(lam[:, 2:3])
        mag = jnp.exp(lr * dt)
        ab_r = mag * jnp.cos(li * dt)
        ab_i = mag * jnp.sin(li * dt)
        den = lr * lr + li * li
        nr = ab_r - 1.0
        f_r = (nr * lr + ab_i * li) / den
        f_i = (ab_i * lr - nr * li) / den
        b_r, b_i = along_steps(br_ref[d], 1), along_steps(bi_ref[d], 1)
        bb_r = f_r * b_r - f_i * b_i
        bb_i = f_r * b_i + f_i * b_r
        cr, ci = along_steps(cr_ref[d], 0), along_steps(ci_ref[d], 0)

        def power(n):
            sq_r, sq_i = ab_r, ab_i
            pr = pi = None
            for k in range(CHUNK.bit_length() - 1):
                bit = ((n >> k) & 1) == 1
                tr = jnp.where(bit, sq_r, 1.0)
                ti = jnp.where(bit, sq_i, 0.0)
                pr, pi = (tr, ti) if pr is None else (pr * tr - pi * ti, pr * ti + pi * tr)
                sq_r, sq_i = sq_r * sq_r - sq_i * sq_i, 2.0 * sq_r * sq_i
            return pr, pi

        def times_lam(p):
            return p[0] * ab_r - p[1] * ab_i, p[0] * ab_i + p[1] * ab_r

        def a_b(p):
            return p[0] * bb_r - p[1] * bb_i, p[0] * bb_i + p[1] * bb_r

        def c_a(p):
            return cr * p[0] - ci * p[1], cr * p[1] + ci * p[0]

        up, down = power(step), power(L - 1 - step)
        return bb_r, bb_i, cr, ci, a_b, c_a, up, down, times_lam

    fbr, fbi, _, _, f_ab, f_ca, f_up, f_down, f_next = direction(0)
    bbr, bbi, bcr, bci, b_ab, b_ca, b_up, b_down, b_next = direction(1)
    etf_r, etf_i = f_ab(f_down)
    etb_r, etb_i = b_ab(b_up)
    et_ref[...] = jnp.concatenate([etf_r, etb_r, etf_i, etb_i], axis=0).astype(BF16)
    df_r, df_i = f_ca(f_next(f_up))
    db_r, db_i = b_ca(b_next(b_down))
    zero = jnp.zeros_like(df_r)
    da_ref[...] = jnp.concatenate([df_r, zero, -df_i, zero], axis=0).astype(BF16)
    db_ref[...] = jnp.concatenate([zero, db_r, zero, -db_i], axis=0).astype(BF16)
    kf_r, kf_i = f_ca(f_up)
    lag0 = step == 0
    lhs_f = jnp.concatenate([fbr, -fbi], axis=0)
    lhs_b = jnp.concatenate([bbr, -bbi], axis=0)
    q_b = _dot_tn3(lhs_b, jnp.concatenate([db_r, db_i], axis=0))
    q_f = _dot_tn3(jnp.concatenate([lhs_f, lhs_b], axis=0),
                   jnp.concatenate([kf_r, kf_i, jnp.where(lag0, bcr, 0.0), jnp.where(lag0, bci, 0.0)],
                                   axis=0))
    z = jnp.concatenate([q_b, q_f], axis=1)
    for j in range(L):
        rows = z[j * SSM_GROUP:(j + 1) * SSM_GROUP, :]
        m_ref[j * SSM_GROUP:(j + 1) * SSM_GROUP, :] = pltpu.roll(
            rows, (2 * W - (L - j) * SSM_GROUP) % (2 * W), 1)[:, :W].astype(BF16)


def _s5_operators(lam_re, lam_im, log_step, b_re, b_im, c_re, c_im):
    _, G, P = lam_re.shape
    W = CHUNK * SSM_GROUP
    lam = jnp.stack([lam_re, lam_im, jnp.broadcast_to(log_step[..., None], lam_re.shape)], axis=-1).astype(F32)
    rep = jnp.asarray(np.arange(SSM_GROUP)[:, None] == np.arange(W)[None, :] % SSM_GROUP, dtype=BF16)
    per_g = lambda a: pl.BlockSpec((2, None) + a.shape[2:], lambda g: (0, g, 0, 0))
    out = pl.BlockSpec((None, W, W), lambda g: (g, 0, 0))
    args = (lam, b_re.astype(F32), b_im.astype(F32), c_re.astype(F32), c_im.astype(F32))
    return pl.pallas_call(
        _s5op_body,
        grid=(G,),
        in_specs=[per_g(a) for a in args] + [pl.BlockSpec(rep.shape, lambda g: (0, 0))],
        out_specs=[out] * 4,
        out_shape=[jax.ShapeDtypeStruct((G, W, W), BF16)] * 4,
        compiler_params=_cparams("arbitrary"),
        name="s5op",
    )(*args, rep)


GROUPS_PER_SET = LANES // SSM_GROUP
S5_SCAN_GROUPS = 4


def _transpose8(blocks, axis, piece):
    n = blocks[0].shape[axis]
    q = lax.broadcasted_iota(jnp.int32, blocks[0].shape, axis) // piece
    v = list(blocks)
    for d in (4, 2, 1):
        keep = (q & d) == 0
        nxt = list(v)
        for a in range(8):
            if a & d == 0:
                b = a + d
                nxt[a] = jnp.where(keep, v[a], pltpu.roll(v[b], d * piece, axis))
                nxt[b] = jnp.where(keep, pltpu.roll(v[a], n - d * piece, axis), v[b])
        v = nxt
    return v


def _piece_transpose(blocks):
    return _transpose8(blocks, 1, SSM_GROUP)


def _s5_body(u_ref, m_ref, et_ref, da_ref, db_ref, a_ref, d_ref, z_ref,
             xr_n, xi_n, xr_t, xi_t, har_s, hai_s, hbr_s, hbi_s, *, nb):
    R = u_ref.shape[0]
    nchunk = R // nb
    W = m_ref.shape[1]
    half = W // 2
    group = lambda g: slice(g * W, (g + 1) * W)
    fwd = lax.broadcasted_iota(jnp.int32, (nb, half), 1) < SSM_STATE

    def swap_row_order(src_a, src_b, dst_a, dst_b, to_chunk_major):
        def blocks(cb, _):
            batch_major = [pl.ds(pl.multiple_of(p * nchunk + cb * nb, nb), nb) for p in range(nb)]
            chunk_major = [pl.ds(pl.multiple_of((cb * nb + j) * nb, nb), nb) for j in range(nb)]
            rows_in, rows_out = (batch_major, chunk_major) if to_chunk_major else (chunk_major, batch_major)
            for k in range(S5_SCAN_GROUPS):
                for src, dst in ((src_a, dst_a), (src_b, dst_b)):
                    out = _transpose8([src[k, r, :] for r in rows_in], 0, 1)
                    for j in range(nb):
                        dst[k, rows_out[j], :] = out[j]
            return 0
        lax.fori_loop(0, nchunk // nb, blocks, 0)

    for g0 in range(0, GROUPS_PER_SET, S5_SCAN_GROUPS):
        coef = []
        for k in range(S5_SCAN_GROUPS):
            g = g0 + k
            x = _dot_nt(u_ref[:, group(g)], et_ref[g])
            xr_n[k] = x[:, :half]
            xi_n[k] = x[:, half:]
            lr = jnp.minimum(a_ref[g, 0:1, :], -1e-4)
            dt = jnp.exp(a_ref[g, 2:3, :])
            mag = jnp.exp(lr * dt * CHUNK)
            ang = a_ref[g, 1:2, :] * dt * CHUNK
            coef.append((jnp.broadcast_to(mag * jnp.cos(ang), (nb, half)),
                         jnp.broadcast_to(mag * jnp.sin(ang), (nb, half))))
        swap_row_order(xr_n, xi_n, xr_t, xi_t, True)

        def step(c, carry):
            rf = pl.ds(pl.multiple_of(c * nb, nb), nb)
            rb = pl.ds(pl.multiple_of((nchunk - 1 - c) * nb, nb), nb)
            new = []
            for k in range(S5_SCAN_GROUPS):
                sr, si = carry[2 * k], carry[2 * k + 1]
                ar, ai = coef[k]
                har_s[k, rf, :] = sr
                hai_s[k, rf, :] = si
                hbr_s[k, rb, :] = sr
                hbi_s[k, rb, :] = si
                xr = jnp.where(fwd, xr_t[k, rf, :], xr_t[k, rb, :])
                xi = jnp.where(fwd, xi_t[k, rf, :], xi_t[k, rb, :])
                new += [sr * ar - si * ai + xr, sr * ai + si * ar + xi]
            return tuple(new)

        zero = jnp.zeros((nb, half), F32)
        lax.fori_loop(0, nchunk, step, (zero,) * (2 * S5_SCAN_GROUPS))
        for k in range(S5_SCAN_GROUPS):
            g = g0 + k
            ha = jnp.concatenate([har_s[k], hai_s[k]], axis=1).astype(BF16)
            hb = jnp.concatenate([hbr_s[k], hbi_s[k]], axis=1).astype(BF16)
            y_t = _dot(ha, da_ref[g]) + _dot(hb, db_ref[g])
            xr_n[k] = y_t[:, :half]
            xi_n[k] = y_t[:, half:]
        swap_row_order(xr_n, xi_n, xr_t, xi_t, False)
        for k in range(S5_SCAN_GROUPS):
            g = g0 + k
            u = u_ref[:, group(g)]
            y = _dot(u, m_ref[g]) + jnp.concatenate([xr_t[k], xi_t[k]], axis=1) + d_ref[g] * u.astype(F32)
            z_ref[:, group(g)] = jax.nn.gelu(y).astype(BF16)


def _s5(u_set, B, S, lam_re, lam_im, log_step, b_re, b_im, c_re, c_im, ssm_d):
    n_set, R, WS = u_set.shape
    G = n_set * GROUPS_PER_SET
    W = CHUNK * SSM_GROUP
    assert R == (S // CHUNK) * B and WS == CHUNK * LANES
    M, ET, DA, DB = _s5_operators(lam_re, lam_im, log_step, b_re, b_im, c_re, c_im)
    both = lambda a: jnp.concatenate([a[0], a[1]], axis=-1)
    lam = jnp.stack([both(lam_re), both(lam_im),
                     both(jnp.broadcast_to(log_step[..., None], lam_re.shape))], axis=1).astype(F32)
    dvec = jnp.tile(ssm_d.astype(F32).reshape(G, 1, SSM_GROUP), (1, CHUNK, 1)).reshape(G, 1, W)
    per_set = pl.BlockSpec((None, R, WS), lambda s: (s, 0, 0))
    blk = lambda *sh: pl.BlockSpec((GROUPS_PER_SET,) + sh, lambda s: (s,) + (0,) * len(sh))
    return pl.pallas_call(
        functools.partial(_s5_body, nb=B),
        grid=(n_set,),
        in_specs=[per_set, blk(W, W), blk(W, W), blk(W, W), blk(W, W), blk(3, W // 2), blk(1, W)],
        out_specs=per_set,
        out_shape=jax.ShapeDtypeStruct((n_set, R, WS), BF16),
        scratch_shapes=[pltpu.VMEM((S5_SCAN_GROUPS, R, W // 2), F32)] * 8,
        compiler_params=_cparams("arbitrary"),
        name="s5",
    )(u_set, M, ET, DA, DB, lam, dvec)


def _memkv_body(m_ref, g_ref, wk_ref, wv_ref, k_ref, v_ref):
    h = (_rms(m_ref[...]) * g_ref[...]).astype(BF16)
    k_ref[...] = _dot(h, wk_ref[...]).astype(BF16)
    v_ref[...] = _dot(h, wv_ref[...]).astype(BF16)


def _memkv(mem, g, wk, wv):
    B, M, D = mem.shape
    const = lambda b: (0, 0)
    row = pl.BlockSpec((None, M, D), lambda b: (b, 0, 0))
    return pl.pallas_call(
        _memkv_body,
        grid=(B,),
        in_specs=[row, pl.BlockSpec((1, D), const), pl.BlockSpec((D, D), const),
                  pl.BlockSpec((D, D), const)],
        out_specs=[row, row],
        out_shape=[jax.ShapeDtypeStruct((B, M, D), BF16)] * 2,
        compiler_params=_cparams("arbitrary"),
        name="memkv",
    )(mem, g[None, :], wk.astype(BF16), wv.astype(BF16))


MID_ROW_BLOCK = 512


def _mid_body(x_ref, ya_ref, z_ref, gw_ref, gb_ref, ag_ref, sg_ref, wo_ref, xg_ref,
              wq_ref, xwo_ref, k_ref, v_ref, o_ref, zs):
    D = x_ref.shape[1]
    dh = D // XA_HEADS
    n_set = z_ref.shape[0]
    crows = MID_ROW_BLOCK // CHUNK

    def glu(r0, st):
        c0 = r0 // CHUNK
        W = CHUNK * SSM_GROUP
        for j in range(n_set):
            for h in range(W // LANES):
                steps = _piece_transpose([z_ref[j, c0:c0 + crows, g * W + h * LANES:g * W + (h + 1) * LANES].astype(F32)
                                          for g in range(GROUPS_PER_SET)])
                for k in range(GROUPS_PER_SET):
                    zs[j, pl.ds(r0 + h * GROUPS_PER_SET + k, crows, stride=CHUNK), :] = steps[k]
        z = jnp.concatenate([zs[j, r0:r0 + MID_ROW_BLOCK, :] for j in range(n_set)], axis=1).astype(BF16)
        st['s'] = z.astype(F32) * jax.nn.sigmoid(_dot(z, gw_ref[...]) + gb_ref[...])

    def out_proj(r0, st):
        rows = slice(r0, r0 + MID_ROW_BLOCK)
        ya = _rms(ya_ref[rows, :].astype(F32)) * ag_ref[...]
        ys = _rms(st.pop('s')) * sg_ref[...]
        y = jnp.concatenate([ya, ys], axis=1).astype(BF16)
        st['x1'] = x_ref[rows, :] + _dot(y, wo_ref[...])

    def query(r0, st):
        hq = (_rms(st['x1']) * xg_ref[...]).astype(BF16)
        st['q'] = (_dot(hq, wq_ref[...]) * (dh ** -0.5)).astype(BF16)

    def attend(r0, st):
        q = st.pop('q')
        outs = []
        for h in range(XA_HEADS):
            sl = slice(h * dh, (h + 1) * dh)
            sc = _dot_nt(q[:, sl], k_ref[:, sl])
            e = jnp.exp(sc - jnp.max(sc, axis=-1, keepdims=True))
            inv = 1.0 / jnp.sum(e, axis=-1, keepdims=True)
            outs.append((_dot(e.astype(BF16), v_ref[:, sl]) * inv).astype(BF16))
        st['o'] = jnp.concatenate(outs, axis=1)

    def finish(r0, st):
        rows = slice(r0, r0 + MID_ROW_BLOCK)
        o_ref[rows, :D] = st.pop('x1') + _dot(st.pop('o'), xwo_ref[...])
        o_ref[rows, D:] = jnp.zeros((MID_ROW_BLOCK, o_ref.shape[1] - D), F32)

    blocks = [(r0, {}) for r0 in range(0, x_ref.shape[0], MID_ROW_BLOCK)]
    for stage in (glu, out_proj, query, attend, finish):
        for r0, st in blocks:
            stage(r0, st)


def _mid(x2d, S, y_att, z, glu_w, glu_b, att_g, ssm_g, w_out, xa_g, wq, wo, kmem, vmem):
    T, D = x2d.shape
    M = kmem.shape[1]
    tm = 2 * MID_ROW_BLOCK
    sb = S // tm
    const = lambda i: (0, 0)
    tok = lambda w: pl.BlockSpec((tm, w), lambda i: (i, 0))
    full = lambda a: pl.BlockSpec(a.shape, const)
    vec = lambda a: a.astype(F32)[None, :]
    mem = pl.BlockSpec((None, M, D), lambda i: (i // sb, 0, 0))
    args = [x2d, y_att, z, glu_w.astype(BF16), vec(glu_b), vec(att_g), vec(ssm_g), w_out.astype(BF16),
            vec(xa_g), wq.astype(BF16), wo.astype(BF16), kmem, vmem]
    zspec = pl.BlockSpec((z.shape[0], tm // CHUNK, CHUNK * LANES), lambda i: (0, i, 0))
    specs = [tok(D), tok(y_att.shape[1]), zspec] + [full(a) for a in args[3:11]] + [mem, mem]
    return pl.pallas_call(
        _mid_body,
        grid=(T // tm,),
        in_specs=specs,
        out_specs=pl.BlockSpec((tm, D + LANES), lambda i: (i, 0)),
        out_shape=jax.ShapeDtypeStruct((T, D + LANES), F32),
        scratch_shapes=[pltpu.VMEM((z.shape[0], tm, LANES), F32)],
        compiler_params=_cparams("arbitrary"),
        name="mid",
    )(*args)


def _split_bf16(a):
    hi = a.astype(BF16)
    return hi, (a - hi.astype(F32)).astype(BF16)


ROUTE_ROWS = 32
PAIR_SLOTS = ((0, 1), (2, 1), (2, 0), (3, 0), (3, 1), (3, 2))
MOE_TILE = 256


def _route_body(x_ref, g_ref, whi_ref, wlo_ref, b_ref, tri_ref, meta_ref, mt_ref, cnt_ref, carry):
    i = pl.program_id(0)
    tm, D = x_ref.shape[0], g_ref.shape[1]

    @pl.when(i == 0)
    def _():
        carry[...] = jnp.zeros_like(carry)

    h = _rms(x_ref[:, :D]) * g_ref[...]
    hhi, hlo = _split_bf16(h)
    logits = (_dot_nt(whi_ref[...], hhi) + _dot_nt(whi_ref[...], hlo) + _dot_nt(wlo_ref[...], hhi)
              + b_ref[:, 0:1])
    ninf = jnp.float32(-jnp.inf)
    cmax = lambda v: jnp.max(v, axis=0, keepdims=True)
    cmin = lambda v: jnp.min(v, axis=0, keepdims=True)
    gl = logits[N_EXPERTS:N_EXPERTS + N_EXPERT_GROUPS]
    grow = lax.broadcasted_iota(jnp.int32, gl.shape, 0)
    gmax = cmax(gl)
    g_sel = cmin(jnp.where(gl == gmax, grow, N_EXPERT_GROUPS))
    g_w = 1.0 / jnp.sum(jnp.exp(gl - gmax), axis=0, keepdims=True)
    el = logits[:N_EXPERTS]
    erow = lax.broadcasted_iota(jnp.int32, el.shape, 0)
    el = jnp.where((erow >> 2) == g_sel, el, ninf)
    v1 = cmax(el)
    i1 = cmin(jnp.where(el == v1, erow, N_EXPERTS))
    el2 = jnp.where(erow == i1, ninf, el)
    v2 = cmax(el2)
    i2 = cmin(jnp.where(el2 == v2, erow, N_EXPERTS))
    t = jnp.exp(v2 - v1)
    w1 = g_w / (1.0 + t)
    w2 = g_w * t / (1.0 + t)
    first_low = i1 < i2
    ia = jnp.minimum(i1, i2) & 3
    ib = jnp.maximum(i1, i2) & 3
    wa = jnp.where(first_low, w1, w2)
    wb = jnp.where(first_low, w2, w1)
    pair = ((ia * (7 - ia)) >> 1) + ib - ia - 1
    order = jnp.where(pair == 3, 1, jnp.where(pair == 1, 2, jnp.where(pair == 2, 3, pair)))
    cls = g_sel * PAIRS_PER_GROUP + order
    crow = lax.broadcasted_iota(jnp.int32, (ROUTE_ROWS, tm), 0)
    onehot = (crow == cls).astype(F32)
    before = _dot(onehot.astype(BF16), tri_ref[...])
    rank = jnp.sum(onehot * (before + carry[:, 0:1]), axis=0, keepdims=True)
    carry[...] = carry[...] + jnp.sum(onehot, axis=1, keepdims=True)
    cnt_ref[...] = carry[...]

    def record(rows):
        r = lax.broadcasted_iota(jnp.int32, (rows, tm), 0)
        return jnp.where(r == META_CLS, cls.astype(F32),
                         jnp.where(r == META_RANK, rank,
                                   jnp.where(r == META_WA, wa, jnp.where(r == META_WB, wb, 0.0))))

    meta_ref[...] = record(LANES).T
    mt_ref[:, pl.ds(pl.multiple_of(i * tm, tm), tm)] = record(mt_ref.shape[0])

    @pl.when(i == pl.num_programs(0) - 1)
    def _():
        ntile = jnp.floor((carry[...] + (MOE_TILE - 1.0)) * (1.0 / MOE_TILE))
        r = lax.broadcasted_iota(jnp.int32, (ROUTE_ROWS, ROUTE_ROWS), 0)
        c = lax.broadcasted_iota(jnp.int32, (ROUTE_ROWS, ROUTE_ROWS), 1)
        off = _dot((c < r).astype(BF16), ntile.astype(BF16)) * MOE_TILE
        cls_all = mt_ref[META_CLS:META_CLS + 1, :]
        pos = mt_ref[META_RANK:META_RANK + 1, :]
        for k in range(N_CLASSES):
            pos = pos + jnp.where(cls_all == k, off[k:k + 1, 0:1], 0.0)
        mt_ref[META_POS:META_POS + 1, :] = pos


def _route(x2a, D, moe_g, grp_w, grp_b, exp_w, exp_b):
    T = x2a.shape[0]
    tm = 1024
    pad = ROUTE_ROWS - N_EXPERT_GROUPS - N_EXPERTS
    w = jnp.concatenate([exp_w, grp_w, jnp.zeros((D, pad), F32)], axis=1).astype(F32).T
    b = jnp.concatenate([exp_b, grp_b, jnp.zeros((pad,), F32)]).astype(F32)
    whi, wlo = _split_bf16(w)
    tri = jnp.asarray(np.arange(tm)[:, None] < np.arange(tm)[None, :], dtype=BF16)
    const = lambda i: (0, 0)
    ncol = D // LANES
    return pl.pallas_call(
        _route_body,
        grid=(T // tm,),
        in_specs=[pl.BlockSpec((tm, D + LANES), lambda i: (i, 0)),
                  pl.BlockSpec((1, D), const), pl.BlockSpec((ROUTE_ROWS, D), const),
                  pl.BlockSpec((ROUTE_ROWS, D), const), pl.BlockSpec((ROUTE_ROWS, LANES), const),
                  pl.BlockSpec((tm, tm), const)],
        out_specs=[pl.BlockSpec((tm, LANES), lambda i: (i, ncol)),
                   pl.BlockSpec((SUBLANES, T), const),
                   pl.BlockSpec((ROUTE_ROWS, LANES), const)],
        out_shape=[jax.ShapeDtypeStruct(x2a.shape, F32), jax.ShapeDtypeStruct((SUBLANES, T), F32),
                   jax.ShapeDtypeStruct((ROUTE_ROWS, LANES), F32)],
        scratch_shapes=[pltpu.VMEM((ROUTE_ROWS, LANES), F32)],
        input_output_aliases={0: 0},
        compiler_params=_cparams("arbitrary"),
        name="route",
    )(x2a, moe_g.astype(F32)[None, :], whi, wlo, jnp.broadcast_to(b[:, None], (ROUTE_ROWS, LANES)), tri)


ROW_COPY_UNROLL = 8


def _zero_fill(fill_ref, dst_ref, zbuf, sem):
    tile_groups = MOE_TILE // SUBLANES
    zbuf[...] = jnp.zeros_like(zbuf)
    pieces = []
    for c in range(N_CLASSES):
        s, n = fill_ref[2 * c], fill_ref[2 * c + 1]
        lead = jnp.minimum(n, (-s) & (SUBLANES - 1))
        for j in range(SUBLANES - 1):
            r = s + j
            dst = dst_ref.at[lax.shift_right_logical(r, 3), pl.ds(r & (SUBLANES - 1), 1)]
            pieces.append((j < lead, pltpu.make_async_copy(zbuf.at[0, pl.ds(0, 1)], dst, sem)))
        g0 = lax.shift_right_logical(s + lead, 3)
        groups = lax.shift_right_logical(n - lead, 3)
        k = tile_groups // 2
        while k >= 1:
            pieces.append(((groups & k) != 0,
                           pltpu.make_async_copy(zbuf.at[pl.ds(0, k)], dst_ref.at[pl.ds(g0, k)], sem)))
            g0 = g0 + (groups & k)
            k //= 2
    first_unused = fill_ref[2 * N_CLASSES]
    for t in range(N_CLASSES):
        tile = first_unused + t
        pieces.append((tile < dst_ref.shape[0] // tile_groups,
                       pltpu.make_async_copy(zbuf, dst_ref.at[pl.ds(tile * tile_groups, tile_groups)], sem)))
    for cond, cp in pieces:
        pl.when(cond)(cp.start)
    for cond, cp in pieces:
        pl.when(cond)(cp.wait)


def _row_copy_body(*refs, rows, scatter):
    if scatter:
        fill_ref, pos_ref, src_ref, dst_ref, zbuf, sem = refs

        @pl.when(pl.program_id(0) == 0)
        def _():
            _zero_fill(fill_ref, dst_ref, zbuf, sem)
    else:
        pos_ref, src_ref, dst_ref, sem = refs

    def copy(j, k):
        p = pos_ref[j * ROW_COPY_UNROLL + k]
        tok = (j, pl.ds(k, 1))
        slot = (lax.shift_right_logical(p, 3), pl.ds(p & (SUBLANES - 1), 1))
        if scatter:
            return pltpu.make_async_copy(src_ref.at[tok], dst_ref.at[slot], sem)
        return pltpu.make_async_copy(src_ref.at[slot], dst_ref.at[tok], sem)

    def start(j, _):
        for k in range(ROW_COPY_UNROLL):
            copy(j, k).start(priority=k % 2)
        return 0

    def wait(j, _):
        for k in range(ROW_COPY_UNROLL):
            copy(j, k).wait()
        return 0

    lax.fori_loop(0, rows // ROW_COPY_UNROLL, start, 0)
    lax.fori_loop(0, rows // ROW_COPY_UNROLL, wait, 0)


def _row_copy(pos, src, *, dst_rows, fill=None, name):
    assert ROW_COPY_UNROLL == SUBLANES
    T = pos.shape[0]
    rows = 1024
    width = src.shape[1]
    scatter = fill is not None
    hbm = pl.BlockSpec(memory_space=pl.ANY)
    tok = pl.BlockSpec((rows // SUBLANES, SUBLANES, width), lambda i, *_: (i, 0, 0))
    idx = pl.BlockSpec((rows,), lambda i, *_: (i,), memory_space=pltpu.SMEM)
    scratch = [pltpu.SemaphoreType.DMA(())]
    if scatter:
        scratch = [pltpu.VMEM((MOE_TILE // SUBLANES, SUBLANES, width), src.dtype)] + scratch
    grid_spec = pltpu.PrefetchScalarGridSpec(
        num_scalar_prefetch=1 if scatter else 0,
        grid=(T // rows,),
        in_specs=[idx, tok if scatter else hbm],
        out_specs=hbm if scatter else tok,
        scratch_shapes=scratch,
    )
    out = pl.pallas_call(
        functools.partial(_row_copy_body, rows=rows, scatter=scatter),
        grid_spec=grid_spec,
        out_shape=jax.ShapeDtypeStruct((dst_rows // SUBLANES, SUBLANES, width), src.dtype),
        compiler_params=pltpu.CompilerParams(dimension_semantics=("arbitrary",)),
        name=name,
    )(*(((fill,) if scatter else ()) + (pos, src.reshape(src.shape[0] // SUBLANES, SUBLANES, width))))
    return out.reshape(dst_rows, width)


def _ffn_body(ea_ref, eb_ref, na_ref, nb_ref, nt_ref, xs_ref, g_ref, fg_ref, wg_hbm, wu_hbm, wd_hbm, o_ref,
              sga, sua, sda, sgb, sub, sdb, ga, ua, da, gb, ub, db, sem):
    D = g_ref.shape[1]
    i = pl.program_id(0)
    valid = i < nt_ref[0]
    prev = jnp.maximum(i - 1, 0)
    slots = ((0, ea_ref, na_ref, (sga, sua, sda), (ga, ua, da)), (1, eb_ref, nb_ref, (sgb, sub, sdb), (gb, ub, db)))

    def fetch(e, stage, s):
        return [pltpu.make_async_copy(w.at[e], buf, sem.at[s])
                for w, buf in zip((wg_hbm, wu_hbm, wd_hbm), stage)]

    for s, e_ref, n_ref, stage, work in slots:
        @pl.when(i == 0)
        def _(s=s, e_ref=e_ref, stage=stage):
            for c in fetch(e_ref[0], stage, s):
                c.start()

        @pl.when(valid & ((i == 0) | (e_ref[i] != e_ref[prev])))
        def _(s=s, e_ref=e_ref, n_ref=n_ref, stage=stage, work=work):
            for c in fetch(e_ref[i], stage, s):
                c.wait()
            for src, dst in zip(stage, work):
                dst[...] = src[...].astype(BF16)

            @pl.when(n_ref[i] >= 0)
            def _():
                for c in fetch(n_ref[i], stage, s):
                    c.start()

    @pl.when(valid)
    def _():
        x = xs_ref[:, :D]
        meta = xs_ref[:, D:]
        lower_in_a = ea_ref[i] < eb_ref[i]
        g_lo, g_hi = meta[:, META_WA:META_WA + 1], meta[:, META_WB:META_WB + 1]
        gates = (jnp.where(lower_in_a, g_lo, g_hi), jnp.where(lower_in_a, g_hi, g_lo))
        h = (_rms(x) * g_ref[...]).astype(BF16)
        y = x
        for (wg, wu, wd), gate in zip(((ga, ua, da), (gb, ub, db)), gates):
            a = jax.nn.silu(_dot(h, wg[...])) * _dot(h, wu[...]) * gate
            y = y + _dot(a.astype(BF16), wd[...])
        o_ref[...] = _rms(y) * fg_ref[...]

    @pl.when(jnp.logical_not(valid))
    def _():
        o_ref[...] = jnp.zeros_like(o_ref)


def _ffn(xs, D, tile_ea, tile_eb, n_tiles, moe_g, final_g, w_gate, w_up, w_down):
    tm = MOE_TILE
    P = xs.shape[0]
    nt_max = P // tm
    De = w_gate.shape[2]

    def next_expert(e):
        j = jnp.arange(nt_max, dtype=jnp.int32)
        later = (j[None, :] > j[:, None]) & (j[None, :] < n_tiles[0]) & (e[None, :] != e[:, None])
        first = jnp.min(jnp.where(later, j[None, :], nt_max), axis=1)
        return jnp.where(first < nt_max, e[jnp.minimum(first, nt_max - 1)], -1).astype(jnp.int32)

    const = lambda i, ea, eb, na, nb, nt: (0, 0)
    hbm = pl.BlockSpec(memory_space=pl.ANY)
    expert = lambda dt: [pltpu.VMEM((D, De), dt), pltpu.VMEM((D, De), dt), pltpu.VMEM((De, D), dt)]
    grid_spec = pltpu.PrefetchScalarGridSpec(
        num_scalar_prefetch=5,
        grid=(nt_max,),
        in_specs=[pl.BlockSpec((tm, D + LANES),
                               lambda i, ea, eb, na, nb, nt: (jnp.maximum(jnp.minimum(i, nt[0] - 1), 0), 0)),
                  pl.BlockSpec((1, D), const), pl.BlockSpec((1, D), const), hbm, hbm, hbm],
        out_specs=pl.BlockSpec((tm, D), lambda i, ea, eb, na, nb, nt: (i, 0)),
        scratch_shapes=expert(F32) * 2 + expert(BF16) * 2 + [pltpu.SemaphoreType.DMA((2,))],
    )
    return pl.pallas_call(
        _ffn_body,
        grid_spec=grid_spec,
        out_shape=jax.ShapeDtypeStruct((P, D), F32),
        compiler_params=_cparams("arbitrary"),
        name="ffn",
    )(tile_ea, tile_eb, next_expert(tile_ea), next_expert(tile_eb), n_tiles, xs,
      moe_g.astype(F32)[None, :], final_g.astype(F32)[None, :],
      w_gate.astype(F32), w_up.astype(F32), w_down.astype(F32))


def _moe(x2a, D, moe_g, grp_w, grp_b, exp_w, exp_b, w_gate, w_up, w_down, final_g):
    T = x2a.shape[0]
    tm = MOE_TILE
    x2a, record, counts = _route(x2a, D, moe_g, grp_w, grp_b, exp_w, exp_b)
    pos = record[META_POS].astype(jnp.int32)
    cnt = counts[:N_CLASSES, 0].astype(jnp.int32)
    ntile_c = (cnt + tm - 1) // tm
    tile_end = jnp.cumsum(ntile_c)
    nt_max = T // tm + N_CLASSES
    n_tiles = tile_end[-1:]
    tile_id = jnp.arange(nt_max, dtype=jnp.int32)
    tile_cls = jnp.minimum(jnp.sum((tile_end[None, :] <= tile_id[:, None]).astype(jnp.int32), axis=1),
                           N_CLASSES - 1)
    tile_cls = jnp.where(tile_id < n_tiles[0], tile_cls, tile_cls[jnp.maximum(n_tiles[0] - 1, 0)])
    slot_a = jnp.asarray([a for a, _ in PAIR_SLOTS], jnp.int32)
    slot_b = jnp.asarray([b for _, b in PAIR_SLOTS], jnp.int32)
    grp = tile_cls // PAIRS_PER_GROUP
    tile_ea = grp * EXPERTS_PER_GROUP + slot_a[tile_cls % PAIRS_PER_GROUP]
    tile_eb = grp * EXPERTS_PER_GROUP + slot_b[tile_cls % PAIRS_PER_GROUP]
    seg_start = (tile_end - ntile_c) * tm
    fill = jnp.concatenate([jnp.stack([seg_start + cnt, ntile_c * tm - cnt], axis=1).reshape(-1), n_tiles])
    xs = _row_copy(pos, x2a, dst_rows=nt_max * tm, fill=fill.astype(jnp.int32), name="dispatch")
    ys = _ffn(xs, D, tile_ea, tile_eb, n_tiles.astype(jnp.int32), moe_g, final_g, w_gate, w_up, w_down)
    return _row_copy(pos, ys, dst_rows=T, name="unsort")


def kernel(x, mem, mix_norm_g, w_in, q_norm_g, k_norm_g, ssm_lambda_re, ssm_lambda_im, ssm_log_step, ssm_b_re, ssm_b_im, ssm_c_re, ssm_c_im, ssm_d, ssm_glu_w, ssm_glu_b, att_out_g, ssm_out_g, w_out, xa_norm_g, mem_norm_g, xa_wq, xa_wk, xa_wv, xa_wo, moe_norm_g, router_grp_w, router_grp_b, router_exp_w, router_exp_b, exp_w_gate, exp_w_up, exp_w_down, final_norm_g):
    B, S, D = x.shape
    assert mix_norm_g.shape[0] == 1, "single layer"
    x2d = x.reshape(B * S, D)
    q, kd, vd, u = _inproj(x2d, S, mix_norm_g[0], w_in[0], q_norm_g[0], k_norm_g[0])
    y_att = _attention(q, kd, vd, B, S)
    z = _s5(u, B, S, ssm_lambda_re[0], ssm_lambda_im[0], ssm_log_step[0], ssm_b_re[0], ssm_b_im[0],
            ssm_c_re[0], ssm_c_im[0], ssm_d[0])
    kmem, vmem = _memkv(mem, mem_norm_g[0], xa_wk[0], xa_wv[0])
    x2a = _mid(x2d, S, y_att, z, ssm_glu_w[0], ssm_glu_b[0], att_out_g[0], ssm_out_g[0], w_out[0],
               xa_norm_g[0], xa_wq[0], xa_wo[0], kmem, vmem)
    out = _moe(x2a, D, moe_norm_g[0], router_grp_w[0], router_grp_b[0], router_exp_w[0],
               router_exp_b[0], exp_w_gate[0], exp_w_up[0], exp_w_down[0], final_norm_g)
    return out.reshape(B, S, D)
```
